```python
import functools
import jax, jax.numpy as jnp
from jax import lax
import numpy as np

D_MODEL = 1024
BATCH = 2
SEQ = 16384
DEPTH = 4

GRID_W = 64
CTX_LEN = 256
HEAD_DIM = 64
EPS = 1e-6
HG_HEADS = 4
HG_DK = 64
HG_DV = 64
HG_WIDTH = HG_HEADS * HG_DK
HG_CHUNK = 64
SWA_HEADS = 6
SWA_KV_HEADS = 2
SWA_GROUP = SWA_HEADS // SWA_KV_HEADS
SWA_WINDOW = 128
SWA_BLOCK = 128
ROPE_BASE = 10000.0
NAT_HEADS = 6
NAT_ROWS = 8
NAT_COLS = 16
NAT_QCOLS = 16
NAT_SLAB = 32
MIX_WIDTH = HG_WIDTH + SWA_HEADS * HEAD_DIM + NAT_HEADS * HEAD_DIM
PROJ_SPLITS = (HG_WIDTH,) * 5 + (SWA_HEADS * HEAD_DIM, SWA_KV_HEADS * HEAD_DIM, SWA_KV_HEADS * HEAD_DIM) + (NAT_HEADS * HEAD_DIM,) * 3
PROJ_WIDTH = sum(PROJ_SPLITS)
PEER_HEADS = 8
PEER_NKEYS = 128
PEER_EXPERTS = PEER_NKEYS * PEER_NKEYS
PEER_DK = 256
PEER_TOPK = 16
PEER_BLOCK = 128

kernel_name = "hybrid_hgrn2_swa_natten_peer_dit"


def rms_norm(x, w):
    xf = x.astype(jnp.float32)
    y = xf * lax.rsqrt(jnp.mean(xf * xf, axis=-1, keepdims=True) + EPS)
    return (y * w.astype(jnp.float32)).astype(x.dtype)


def modulate(h, shift, scale):
    return h * (1.0 + scale) + shift


def split_proj(p):
    idx = [int(i) for i in np.cumsum(PROJ_SPLITS)[:-1]]
    return jnp.split(p, idx, axis=-1)


def _heads(a, h):
    return a.reshape(a.shape[0], a.shape[1], h, a.shape[-1] // h)


def joint_softmax(scores, sink=None):
    m = functools.reduce(jnp.maximum, [s.max(axis=-1, keepdims=True) for s in scores])
    if sink is not None:
        m = jnp.maximum(m, sink)
    ps = [jnp.exp(s - m) for s in scores]
    den = ps[0].sum(axis=-1, keepdims=True)
    for p in ps[1:]:
        den = den + p.sum(axis=-1, keepdims=True)
    if sink is not None:
        den = den + jnp.exp(sink - m)
    return [p / den for p in ps]


def axial_rope(x, pos_row, pos_col):
    d = x.shape[-1]
    half = d // 2
    nf = half // 2
    inv_freq = ROPE_BASE ** (-jnp.arange(nf, dtype=jnp.float32) / nf)
    xf = x.astype(jnp.float32)

    def rot(xa, pos):
        ang = pos.astype(jnp.float32)[:, None] * inv_freq
        cos = jnp.cos(ang)[None, :, None, :]
        sin = jnp.sin(ang)[None, :, None, :]
        x1, x2 = xa[..., :nf], xa[..., nf:]
        return jnp.concatenate([x1 * cos - x2 * sin, x2 * cos + x1 * sin], axis=-1)

    out = jnp.concatenate([rot(xf[..., :half], pos_row), rot(xf[..., half:], pos_col)], axis=-1)
    return out.astype(x.dtype)


def gla_chunk_scan(q, k, v, log_f, s0):
    B, L, H, dk = q.shape
    dv = v.shape[-1]
    n = L // HG_CHUNK

    def chunks(a):
        return a.reshape(B, n, HG_CHUNK, H, a.shape[-1]).transpose(1, 0, 3, 2, 4)

    tri = jnp.tril(jnp.ones((HG_CHUNK, HG_CHUNK), dtype=bool))[:, :, None]

    def step(S, inp):
        qc, kc, vc, gc = inp
        b = jnp.cumsum(gc, axis=2)
        b_last = b[:, :, -1:, :]
        decay = jnp.exp(jnp.where(tri, b[:, :, :, None, :] - b[:, :, None, :, :], -jnp.inf))
        scores = jnp.einsum('bhtd,bhsd,bhtsd->bhts', qc, kc, decay)
        o = jnp.einsum('bhts,bhse->bhte', scores, vc) + jnp.einsum('bhtd,bhde->bhte', qc * jnp.exp(b), S)
        S = jnp.exp(b_last)[:, :, 0, :, None] * S + jnp.einsum('bhsd,bhse->bhde', kc * jnp.exp(b_last - b), vc)
        return S, o

    S, o = lax.scan(step, s0, (chunks(q), chunks(k), chunks(v), chunks(log_f)))
    return o.transpose(1, 0, 3, 2, 4).reshape(B, L, H, dv), S


def _hgrn2_gates(z, lb):
    z = z.astype(jnp.float32)
    log_f = jnp.logaddexp(jnp.log(lb), jnp.log1p(-lb) + jax.nn.log_sigmoid(z))
    k = (1.0 - lb) * jax.nn.sigmoid(-z)
    return _heads(log_f, HG_HEADS), _heads(k, HG_HEADS)


def hgrn2_mixer(parts_ctx, parts_lat, lb_fwd, lb_bwd, norm_w, need_ctx):
    def prep(parts):
        q, zf, zb, v, g = parts
        q = _heads(jax.nn.silu(q.astype(jnp.float32)), HG_HEADS)
        v = _heads(v.astype(jnp.float32), HG_HEADS)
        return q, _hgrn2_gates(zf, lb_fwd), _hgrn2_gates(zb, lb_bwd), v, g

    qc, fwd_c, bwd_c, vc, gc = prep(parts_ctx)
    ql, fwd_l, bwd_l, vl, gl = prep(parts_lat)
    s0 = jnp.zeros((qc.shape[0], HG_HEADS, HG_DK, HG_DV), jnp.float32)
    rev = lambda a: jnp.flip(a, axis=1)
    oc_f, sc_f = gla_chunk_scan(qc, fwd_c[1], vc, fwd_c[0], s0)
    ol_f, _ = gla_chunk_scan(ql, fwd_l[1], vl, fwd_l[0], sc_f)
    oc_b, sc_b = gla_chunk_scan(rev(qc), rev(bwd_c[1]), rev(vc), rev(bwd_c[0]), s0)
    ol_b, _ = gla_chunk_scan(rev(ql), rev(bwd_l[1]), rev(vl), rev(bwd_l[0]), sc_b)

    def readout(o, g):
        o = o * lax.rsqrt(jnp.mean(o * o, axis=-1, keepdims=True) + EPS)
        o = o.reshape(o.shape[0], o.shape[1], HG_WIDTH) * norm_w.astype(jnp.float32)
        return (o * jax.nn.silu(g.astype(jnp.float32))).astype(g.dtype)

    out_lat = readout(ol_f + rev(ol_b), gl)
    out_ctx = readout(oc_f + rev(oc_b), gc) if need_ctx else None
    return out_lat, out_ctx


def swa_mixer(parts_ctx, parts_lat, sink, pos_row, pos_col, need_ctx):
    q_c, k_c, v_c = parts_ctx
    q_l, k_l, v_l = parts_lat
    B, L = q_l.shape[:2]
    C = q_c.shape[1]
    scale = HEAD_DIM ** -0.5
    ql = axial_rope(_heads(q_l, SWA_HEADS), pos_row, pos_col)
    kl = axial_rope(_heads(k_l, SWA_KV_HEADS), pos_row, pos_col)
    vl = _heads(v_l, SWA_KV_HEADS)
    qc = _heads(q_c, SWA_HEADS).reshape(B, C, SWA_KV_HEADS, SWA_GROUP, HEAD_DIM)
    kc = _heads(k_c, SWA_KV_HEADS)
    vc = _heads(v_c, SWA_KV_HEADS)
    sink_g = sink.astype(jnp.float32).reshape(SWA_KV_HEADS, SWA_GROUP)[None, :, :, None, None]
    pad = ((0, 0), (SWA_BLOCK, SWA_BLOCK), (0, 0), (0, 0))
    kp, vp = jnp.pad(kl, pad), jnp.pad(vl, pad)
    offs_q = jnp.arange(SWA_BLOCK)
    offs_k = jnp.arange(3 * SWA_BLOCK) - SWA_BLOCK

    def block(i):
        start = i * SWA_BLOCK
        qb = lax.dynamic_slice_in_dim(ql, start, SWA_BLOCK, axis=1).reshape(B, SWA_BLOCK, SWA_KV_HEADS, SWA_GROUP, HEAD_DIM)
        kb = lax.dynamic_slice_in_dim(kp, start, 3 * SWA_BLOCK, axis=1)
        vb = lax.dynamic_slice_in_dim(vp, start, 3 * SWA_BLOCK, axis=1)
        qpos = start + offs_q
        kpos = start + offs_k
        mask = (jnp.abs(qpos[:, None] - kpos[None, :]) <= SWA_WINDOW) & (kpos >= 0)[None, :] & (kpos < L)[None, :]
        s_loc = jnp.einsum('bqkgd,bskd->bkgqs', qb, kb).astype(jnp.float32) * scale
        s_loc = jnp.where(mask, s_loc, -jnp.inf)
        s_ctx = jnp.einsum('bqkgd,bskd->bkgqs', qb, kc).astype(jnp.float32) * scale
        p_loc, p_ctx = joint_softmax([s_loc, s_ctx], sink_g)
        o = jnp.einsum('bkgqs,bskd->bqkgd', p_loc.astype(vb.dtype), vb) + jnp.einsum('bkgqs,bskd->bqkgd', p_ctx.astype(vc.dtype), vc)
        return o.reshape(B, SWA_BLOCK, SWA_HEADS * HEAD_DIM)

    o_lat = lax.map(block, jnp.arange(L // SWA_BLOCK))
    o_lat = o_lat.transpose(1, 0, 2, 3).reshape(B, L, SWA_HEADS * HEAD_DIM).astype(q_l.dtype)
    o_ctx = None
    if need_ctx:
        s = jnp.einsum('bqkgd,bskd->bkgqs', qc, kc).astype(jnp.float32) * scale
        (p,) = joint_softmax([s], sink_g)
        o_ctx = jnp.einsum('bkgqs,bskd->bqkgd', p.astype(vc.dtype), vc).reshape(B, C, SWA_HEADS * HEAD_DIM).astype(q_c.dtype)
    return o_lat, o_ctx


def natten_mixer(parts_ctx, parts_lat, rpb, need_ctx):
    q_c, k_c, v_c = parts_ctx
    q_l, k_l, v_l = parts_lat
    B, L = q_l.shape[:2]
    C = q_c.shape[1]
    rows = L // GRID_W
    wr = min(NAT_ROWS, rows)
    scale = HEAD_DIM ** -0.5
    qg = _heads(q_l, NAT_HEADS).reshape(B, rows, GRID_W, NAT_HEADS, HEAD_DIM)
    kg = _heads(k_l, NAT_HEADS).reshape(B, rows, GRID_W, NAT_HEADS, HEAD_DIM)
    vg = _heads(v_l, NAT_HEADS).reshape(B, rows, GRID_W, NAT_HEADS, HEAD_DIM)
    qc, kc, vc = _heads(q_c, NAT_HEADS), _heads(k_c, NAT_HEADS), _heads(v_c, NAT_HEADS)
    n_cb = GRID_W // NAT_QCOLS
    qcol = np.arange(GRID_W).reshape(n_cb, NAT_QCOLS)
    slab_start = np.clip(np.arange(n_cb) * NAT_QCOLS - NAT_COLS // 2, 0, GRID_W - NAT_SLAB)
    slab_cols = slab_start[:, None] + np.arange(NAT_SLAB)
    win_start = np.clip(qcol - NAT_COLS // 2, 0, GRID_W - NAT_COLS)
    col_mask = (slab_cols[:, None, :] >= win_start[..., None]) & (slab_cols[:, None, :] < win_start[..., None] + NAT_COLS)
    col_idx = slab_cols[:, None, :] - qcol[..., None] + NAT_COLS - 1
    col_mask_b = jnp.asarray(col_mask)[:, :, None, :]
    rpb = rpb.astype(jnp.float32)

    def row_block(r):
        rs = jnp.clip(r - wr // 2, 0, rows - wr)
        q_r = lax.dynamic_index_in_dim(qg, r, axis=1, keepdims=False).reshape(B, n_cb, NAT_QCOLS, NAT_HEADS, HEAD_DIM)
        k_slab = lax.dynamic_slice_in_dim(kg, rs, wr, axis=1)[:, :, slab_cols]
        v_slab = lax.dynamic_slice_in_dim(vg, rs, wr, axis=1)[:, :, slab_cols]
        s = jnp.einsum('bjqhd,brjkhd->bhjqrk', q_r, k_slab).astype(jnp.float32) * scale
        row_idx = rs + jnp.arange(wr) - r + NAT_ROWS - 1
        bias = rpb[:, row_idx[None, None, :, None], col_idx[:, :, None, :]]
        s = jnp.where(col_mask_b, s + bias[None], -jnp.inf)
        s = s.reshape(B, NAT_HEADS, n_cb, NAT_QCOLS, wr * NAT_SLAB)
        s_ctx = jnp.einsum('bjqhd,bshd->bhjqs', q_r, kc).astype(jnp.float32) * scale
        p_loc, p_ctx = joint_softmax([s, s_ctx])
        p_loc = p_loc.reshape(B, NAT_HEADS, n_cb, NAT_QCOLS, wr, NAT_SLAB)
        o = jnp.einsum('bhjqrk,brjkhd->bjqhd', p_loc.astype(v_slab.dtype), v_slab) + jnp.einsum('bhjqs,bshd->bjqhd', p_ctx.astype(vc.dtype), vc)
        return o.reshape(B, GRID_W, NAT_HEADS * HEAD_DIM)

    o_lat = lax.map(row_block, jnp.arange(rows))
    o_lat = o_lat.transpose(1, 0, 2, 3).reshape(B, L, NAT_HEADS * HEAD_DIM).astype(q_l.dtype)
    o_ctx = None
    if need_ctx:
        s = jnp.einsum('bqhd,bshd->bhqs', qc, kc).astype(jnp.float32) * scale
        (p,) = joint_softmax([s])
        o_ctx = jnp.einsum('bhqs,bshd->bqhd', p.astype(vc.dtype), vc).reshape(B, C, NAT_HEADS * HEAD_DIM).astype(q_c.dtype)
    return o_lat, o_ctx


def peer_ffn(h, wq, subkeys, u_tab, v_tab):
    T, D = h.shape

    def block(hb):
        q = (hb @ wq).reshape(PEER_BLOCK, PEER_HEADS, 2, PEER_DK // 2)
        s = jnp.einsum('thpd,hpnd->thpn', q, subkeys).astype(jnp.float32)
        top_s, top_i = lax.top_k(s, PEER_TOPK)
        cand_s = (top_s[:, :, 0, :, None] + top_s[:, :, 1, None, :]).reshape(PEER_BLOCK, PEER_HEADS, PEER_TOPK * PEER_TOPK)
        cand_i = (top_i[:, :, 0, :, None] * PEER_NKEYS + top_i[:, :, 1, None, :]).reshape(PEER_BLOCK, PEER_HEADS, PEER_TOPK * PEER_TOPK)
        best_s, pos = lax.top_k(cand_s, PEER_TOPK)
        idx = jnp.take_along_axis(cand_i, pos, axis=-1)
        g = jax.nn.softmax(best_s, axis=-1)
        a = jax.nn.gelu(jnp.einsum('td,thkd->thk', hb, u_tab[idx]).astype(jnp.float32), approximate=False)
        return jnp.einsum('thk,thkd->td', (g * a).astype(hb.dtype), v_tab[idx])

    return lax.map(block, h.reshape(T // PEER_BLOCK, PEER_BLOCK, D)).reshape(T, D)


def setup_inputs(seed: int = 0) -> dict:
    key = jax.random.key(seed)
    ks = jax.random.split(key, 19)
    D = D_MODEL

    def n(k, shape, s):
        return jax.random.normal(k, shape, jnp.float32) * s

    return {
        "x": n(ks[0], (BATCH, SEQ, D), 1.0),
        "c": n(ks[1], (BATCH, D), 1.0),
        "ctx": n(ks[2], (BATCH, CTX_LEN, D), 1.0),
        "c_ctx": n(ks[3], (D,), 1.0),
        "w_mod": n(ks[4], (DEPTH, D, 6 * D), 0.5 * D ** -0.5),
        "b_mod": n(ks[5], (DEPTH, 6 * D), 0.01),
        "norm_mix_w": 1.0 + n(ks[6], (DEPTH, D), 0.02),
        "norm_ffn_w": 1.0 + n(ks[7], (DEPTH, D), 0.02),
        "w_in": n(ks[8], (DEPTH, D, PROJ_WIDTH), D ** -0.5),
        "hgrn_lb_logits": n(ks[9], (2, DEPTH, HG_WIDTH), 0.5),
        "hgrn_norm_w": 1.0 + n(ks[10], (DEPTH, HG_WIDTH), 0.02),
        "swa_sink": n(ks[11], (DEPTH, SWA_HEADS), 1.0),
        "nat_rpb": n(ks[12], (DEPTH, NAT_HEADS, 2 * NAT_ROWS - 1, 2 * NAT_COLS - 1), 0.5),
        "w_out": n(ks[13], (DEPTH, MIX_WIDTH, D), MIX_WIDTH ** -0.5),
        "peer_wq": n(ks[14], (DEPTH, D, PEER_HEADS * PEER_DK), D ** -0.5),
        "peer_subkeys": n(ks[15], (DEPTH, PEER_HEADS, 2, PEER_NKEYS, PEER_DK // 2), (PEER_DK // 2) ** -0.5),
        "peer_u": n(ks[16], (DEPTH, PEER_EXPERTS, D), D ** -0.5),
        "peer_v": n(ks[17], (DEPTH, PEER_EXPERTS, D), 0.5),
        "final_norm_w": 1.0 + n(ks[18], (D,), 0.02),
    }


def reference(x, c, ctx, c_ctx, w_mod, b_mod, norm_mix_w, norm_ffn_w, w_in, hgrn_lb_logits, hgrn_norm_w,
              swa_sink, nat_rpb, w_out, peer_wq, peer_subkeys, peer_u, peer_v, final_norm_w):
    B, L, D = x.shape
    C = ctx.shape[1]
    pos = jnp.arange(L)
    pos_row, pos_col = pos // GRID_W, pos % GRID_W
    lb = jnp.cumsum(jax.nn.softmax(hgrn_lb_logits.astype(jnp.float32), axis=1), axis=1)
    lb = lb - lb[:, :1]
    xc = ctx
    s_c = jax.nn.silu(c)
    s_cc = jax.nn.silu(c_ctx)
    for l in range(DEPTH):
        last = l == DEPTH - 1
        need_ctx = not last
        mod = jnp.split(s_c @ w_mod[l] + b_mod[l], 6, axis=-1)
        mod_c = jnp.split(s_cc @ w_mod[l] + b_mod[l], 6, axis=-1)
        h = modulate(rms_norm(x, norm_mix_w[l]), mod[0][:, None], mod[1][:, None])
        hc = modulate(rms_norm(xc, norm_mix_w[l]), mod_c[0], mod_c[1])
        pl = split_proj(h @ w_in[l])
        pc = split_proj(hc @ w_in[l])
        a_l, a_c = hgrn2_mixer(pc[:5], pl[:5], lb[0, l], lb[1, l], hgrn_norm_w[l], need_ctx)
        b_l, b_c = swa_mixer(pc[5:8], pl[5:8], swa_sink[l], pos_row, pos_col, need_ctx)
        n_l, n_c = natten_mixer(pc[8:], pl[8:], nat_rpb[l], need_ctx)
        x = x + mod[2][:, None] * (jnp.concatenate([a_l, b_l, n_l], axis=-1) @ w_out[l])
        if need_ctx:
            xc = xc + mod_c[2] * (jnp.concatenate([a_c, b_c, n_c], axis=-1) @ w_out[l])
        h = modulate(rms_norm(x, norm_ffn_w[l]), mod[3][:, None], mod[4][:, None])
        if need_ctx:
            hc = modulate(rms_norm(xc, norm_ffn_w[l]), mod_c[3], mod_c[4])
            tokens = jnp.concatenate([hc.reshape(B * C, D), h.reshape(B * L, D)], axis=0)
            y_all = peer_ffn(tokens, peer_wq[l], peer_subkeys[l], peer_u[l], peer_v[l])
            xc = xc + mod_c[5] * y_all[:B * C].reshape(B, C, D)
            y = y_all[B * C:].reshape(B, L, D)
        else:
            y = peer_ffn(h.reshape(B * L, D), peer_wq[l], peer_subkeys[l], peer_u[l], peer_v[l]).reshape(B, L, D)
        x = x + mod[5][:, None] * y
    return rms_norm(x, final_norm_w)
```

```python
import functools

import numpy as np
import jax
import jax.numpy as jnp
from jax import lax
from jax.experimental import pallas as pl
from jax.experimental.pallas import tpu as pltpu

f32 = jnp.float32
bf16 = jnp.bfloat16
i32 = jnp.int32

LANE = 128
VMEM_LIMIT_BYTES = 56 * 1024 * 1024

GRID_W = 64
HEAD_DIM = 64
EPS = 1e-6
HG_HEADS = 4
HG_WIDTH = 256
HG_SUB = 16
SWA_HEADS = 6
SWA_KV_HEADS = 2
SWA_WINDOW = 128
SWA_BLOCK = 128
ROPE_BASE = 10000.0
NAT_HEADS = 6
NAT_ROWS = 8
NAT_COLS = 16
SWA_Q = SWA_HEADS * HEAD_DIM
SWA_KV = SWA_KV_HEADS * HEAD_DIM
NAT_W = NAT_HEADS * HEAD_DIM
PROJ_WIDTH = 5 * HG_WIDTH + SWA_Q + 2 * SWA_KV + 3 * NAT_W
PEER_HEADS = 8
PEER_NKEYS = 128
PEER_DK = 256
PEER_TOPK = 16
PEER_SLOTS = PEER_HEADS * PEER_TOPK

TM = 256
PEER_TB = 128
PEER_NBUF = 4
NEG = -1e30

SWA_HEAD_ORDER = (0, 3, 1, 4, 2, 5)

_NT = (((1,), (1,)), ((), ()))
_TN = (((0,), (0,)), ((), ()))


def _silu(x):
    return x * (1.0 / (1.0 + jnp.exp(-x)))


def _split2(x):
    hi = x.astype(bf16)
    lo = (x - hi.astype(f32)).astype(bf16)
    return hi, lo


def _mods_kernel(sc_ref, w_ref, b_ref, o_ref):
    s = _silu(sc_ref[...])
    o_ref[0] = jnp.dot(s.astype(bf16), w_ref[0].astype(bf16), preferred_element_type=f32) + b_ref[0]


def _mods(c, c_ctx, w_mod, b_mod):
    depth, d, d6 = w_mod.shape
    nb = c.shape[0]
    rows = jnp.zeros((8, d), f32).at[:nb].set(c).at[nb].set(c_ctx)
    tn = d6 // 4
    out = pl.pallas_call(
        _mods_kernel,
        grid=(depth, 4),
        in_specs=[pl.BlockSpec((8, d), lambda l, j: (0, 0)),
                  pl.BlockSpec((1, d, tn), lambda l, j: (l, 0, j)),
                  pl.BlockSpec((1, 1, tn), lambda l, j: (l, 0, j))],
        out_specs=pl.BlockSpec((1, 8, tn), lambda l, j: (l, 0, j)),
        out_shape=jax.ShapeDtypeStruct((depth, 8, d6), f32),
        compiler_params=pltpu.CompilerParams(dimension_semantics=("arbitrary", "arbitrary"),
                                             vmem_limit_bytes=VMEM_LIMIT_BYTES),
        name="adaln_mods",
    )(rows, w_mod, b_mod.reshape(depth, 1, d6))
    lat = out[:, :nb].reshape(depth, nb, 6, d)
    ctx = jnp.broadcast_to(out[:, nb].reshape(depth, 1, 6, d), (depth, nb, 6, d))
    return jnp.stack([ctx, lat], axis=2)


def _norm_mod(x, nw, shift, scale):
    y = x * lax.rsqrt(jnp.mean(x * x, axis=-1, keepdims=True) + EPS) * nw
    return y * (1.0 + scale) + shift


def _rope(t, cos, sins):
    lane = lax.broadcasted_iota(i32, t.shape, 1)
    sw = jnp.where((lane % 32) < 16, pltpu.roll(t, LANE - 16, 1), pltpu.roll(t, 16, 1))
    return t * cos + sw * sins


def _hgrn_gates(z, log_lb, log1m_lb, one_m_lb):
    log_sig = jnp.minimum(z, 0.0) - jnp.log1p(jnp.exp(-jnp.abs(z)))
    c = log1m_lb + log_sig
    m = jnp.maximum(log_lb, c)
    log_f = m + jnp.log1p(jnp.exp(-jnp.abs(log_lb - c)))
    k = one_m_lb * (1.0 / (1.0 + jnp.exp(z)))
    return log_f, k


def _cumsum16(tri, log_f):
    h1 = log_f.astype(bf16)
    r1 = log_f - h1.astype(f32)
    h2 = r1.astype(bf16)
    h3 = (r1 - h2.astype(f32)).astype(bf16)
    return (jnp.dot(tri, h1, preferred_element_type=f32) + jnp.dot(tri, h2, preferred_element_type=f32)
            + jnp.dot(tri, h3, preferred_element_type=f32))


def _proj_kernel(x_ref, nw_ref, mod_ref, w_ref, cs_ref, gp_ref, tri_ref,
                 hq_ref, cum_ref, kk_ref, hv_ref, sg_ref, sq_ref, sk_ref, sv_ref, nq_ref, nk_ref, nv_ref):
    h = _norm_mod(x_ref[0], nw_ref[...], mod_ref[0, 0, 0:1, :], mod_ref[0, 0, 1:2, :])
    p = jnp.dot(h.astype(bf16), w_ref[...], preferred_element_type=f32)
    w = HG_WIDTH
    hq_ref[0] = _silu(p[:, 0:w])
    for d in range(2):
        log_f, k = _hgrn_gates(p[:, (1 + d) * w:(2 + d) * w], gp_ref[3 * d:3 * d + 1, :],
                               gp_ref[3 * d + 1:3 * d + 2, :], gp_ref[3 * d + 2:3 * d + 3, :])
        cum_ref[d, 0] = _cumsum16(tri_ref[d], log_f)
        kk_ref[d, 0] = k
    hv_ref[0] = p[:, 3 * w:4 * w]
    sg_ref[0] = _silu(p[:, 4 * w:5 * w])
    cos, sins = cs_ref[0], cs_ref[1]
    o = 5 * w
    scale = HEAD_DIM ** -0.5
    for j in range(SWA_Q // LANE):
        sq_ref[0, :, j * LANE:(j + 1) * LANE] = (_rope(p[:, o + j * LANE:o + (j + 1) * LANE], cos, sins) * scale).astype(bf16)
    o += SWA_Q
    sk_ref[0] = _rope(p[:, o:o + SWA_KV], cos, sins).astype(bf16)
    o += SWA_KV
    sv_ref[0] = p[:, o:o + SWA_KV].astype(bf16)
    o += SWA_KV
    nq_ref[0] = (p[:, o:o + NAT_W] * scale).astype(bf16)
    o += NAT_W
    nk_ref[0] = p[:, o:o + NAT_W].astype(bf16)
    o += NAT_W
    nv_ref[0] = p[:, o:o + NAT_W].astype(bf16)


def _proj(xall, nw, mod, w_in, cs, gp, tri):
    nb, ta, d = xall.shape
    nt = ta // TM
    tok = lambda width, dt: jax.ShapeDtypeStruct((nb, ta, width), dt)
    tok2 = jax.ShapeDtypeStruct((2, nb, ta, HG_WIDTH), f32)
    bs = lambda width: pl.BlockSpec((1, TM, width), lambda b, i: (b, i, 0))
    bs2 = pl.BlockSpec((2, 1, TM, HG_WIDTH), lambda b, i: (0, b, i, 0))
    return pl.pallas_call(
        _proj_kernel,
        grid=(nb, nt),
        in_specs=[pl.BlockSpec((1, TM, d), lambda b, i: (b, i, 0)),
                  pl.BlockSpec((1, d), lambda b, i: (0, 0)),
                  pl.BlockSpec((1, 1, 6, d), lambda b, i: (b, jnp.minimum(i, 1), 0, 0)),
                  pl.BlockSpec((d, PROJ_WIDTH), lambda b, i: (0, 0)),
                  pl.BlockSpec((2, TM, LANE), lambda b, i: (0, i, 0)),
                  pl.BlockSpec((8, HG_WIDTH), lambda b, i: (0, 0)),
                  pl.BlockSpec((2, TM, TM), lambda b, i: (0, 0, 0))],
        out_specs=[bs(HG_WIDTH), bs2, bs2, bs(HG_WIDTH), bs(HG_WIDTH), bs(SWA_Q), bs(SWA_KV), bs(SWA_KV),
                   bs(NAT_W), bs(NAT_W), bs(NAT_W)],
        out_shape=[tok(HG_WIDTH, f32), tok2, tok2, tok(HG_WIDTH, f32), tok(HG_WIDTH, f32), tok(SWA_Q, bf16),
                   tok(SWA_KV, bf16), tok(SWA_KV, bf16), tok(NAT_W, bf16), tok(NAT_W, bf16), tok(NAT_W, bf16)],
        compiler_params=pltpu.CompilerParams(dimension_semantics=("arbitrary", "arbitrary"),
                                             vmem_limit_bytes=VMEM_LIMIT_BYTES),
        name="norm_in_proj",
    )(xall, nw, mod, w_in, cs, gp, tri)


def _hgrn_kernel(q_ref, cum_ref, k_ref, v_ref, bd_ref, o_ref, st_ref):
    d = pl.program_id(1)
    fwd = d == 0

    @pl.when(pl.program_id(2) == 0)
    def _():
        st_ref[...] = jnp.zeros_like(st_ref)

    bd = bd_ref[...]
    bd_mask = bd > 0
    sgn = jnp.where(fwd, 1, -1)
    t_sgn = lax.broadcasted_iota(i32, (HG_SUB, 1), 0) * sgn
    nsteps = TM // HG_SUB

    def step(i, carry):
        sc = jnp.where(fwd, i, nsteps - 1 - i)
        r0 = pl.multiple_of(sc * HG_SUB, HG_SUB)
        q = q_ref[0, pl.ds(r0, HG_SUB), :]
        b = cum_ref[0, 0, pl.ds(r0, HG_SUB), :]
        k = k_ref[0, 0, pl.ds(r0, HG_SUB), :]
        v = v_ref[0, pl.ds(r0, HG_SUB), :]
        b_end = jnp.where(fwd, b[HG_SUB - 1:HG_SUB, :], b[0:1, :])
        st = st_ref[...]
        o = lax.dot_general((q * jnp.exp(b)).astype(bf16), st.astype(bf16), _NT, preferred_element_type=f32)
        rows = []
        for s in range(HG_SUB):
            valid = t_sgn >= s * sgn
            decay = jnp.exp(jnp.minimum(b - b[s:s + 1, :], 0.0))
            rows.append(jnp.where(valid, decay * q * k[s:s + 1, :], 0.0))
        hi, lo = _split2(jnp.concatenate(rows, axis=0))
        pe = jnp.dot(hi, bd, preferred_element_type=f32) + jnp.dot(lo, bd, preferred_element_type=f32)
        for s in range(HG_SUB):
            o = o + pe[s * HG_SUB:(s + 1) * HG_SUB, :] * v[s:s + 1, :]
        o_ref[0, 0, pl.ds(r0, HG_SUB), :] = o
        kd = (k * jnp.exp(b_end - b)).astype(bf16)
        kv_t = lax.dot_general(v.astype(bf16), kd, _TN, preferred_element_type=f32)
        st_ref[...] = st * jnp.exp(b_end) + jnp.where(bd_mask, kv_t, 0.0)
        return carry

    lax.fori_loop(0, nsteps, step, 0)


def _hgrn_scan(hq, cum, kk, hv, bd):
    nb, ta, w = hq.shape
    nt = ta // TM

    def blk(d, j):
        return jnp.where(d == 0, j, jnp.where(j == 0, 0, nt - j))

    return pl.pallas_call(
        _hgrn_kernel,
        grid=(nb, 2, nt),
        in_specs=[pl.BlockSpec((1, TM, w), lambda b, d, j: (b, blk(d, j), 0)),
                  pl.BlockSpec((1, 1, TM, w), lambda b, d, j: (d, b, blk(d, j), 0)),
                  pl.BlockSpec((1, 1, TM, w), lambda b, d, j: (d, b, blk(d, j), 0)),
                  pl.BlockSpec((1, TM, w), lambda b, d, j: (b, blk(d, j), 0)),
                  pl.BlockSpec((w, w), lambda b, d, j: (0, 0))],
        out_specs=pl.BlockSpec((1, 1, TM, w), lambda b, d, j: (d, b, blk(d, j), 0)),
        out_shape=jax.ShapeDtypeStruct((2, nb, ta, w), f32),
        scratch_shapes=[pltpu.VMEM((w, w), f32)],
        compiler_params=pltpu.CompilerParams(dimension_semantics=("arbitrary", "arbitrary", "arbitrary")),
        name="hgrn_scan",
    )(hq, cum, kk, hv, bd)


def _pair_rows(q2):
    lane = lax.broadcasted_iota(i32, q2.shape, 1)
    zero = jnp.zeros_like(q2)
    return jnp.concatenate([jnp.where(lane < HEAD_DIM, q2, zero), jnp.where(lane >= HEAD_DIM, q2, zero)], axis=0)


def _pair_merge(o):
    m = o.shape[0] // 2
    lane = lax.broadcasted_iota(i32, (m, LANE), 1)
    return jnp.where(lane < HEAD_DIM, o[:m], o[m:])


def _joint_attention(s_loc, s_ctx, v_loc, v_ctx, sink=None):
    m = jnp.maximum(jnp.max(s_loc, axis=-1, keepdims=True), jnp.max(s_ctx, axis=-1, keepdims=True))
    if sink is not None:
        m = jnp.maximum(m, sink)
    p_loc = jnp.exp(s_loc - m)
    p_ctx = jnp.exp(s_ctx - m)
    den = jnp.sum(p_loc, axis=-1, keepdims=True) + jnp.sum(p_ctx, axis=-1, keepdims=True)
    if sink is not None:
        den = den + jnp.exp(sink - m)
    o = (jnp.dot(p_loc.astype(bf16), v_loc, preferred_element_type=f32)
         + jnp.dot(p_ctx.astype(bf16), v_ctx, preferred_element_type=f32))
    return o / den


def _swa_kernel(q_ref, kp_ref, kc_ref, kn_ref, vp_ref, vc_ref, vn_ref, kx_ref, vx_ref, sink_ref, o_ref, *, n_ctx_blocks, seq):
    g = pl.program_id(1)
    i = g - n_ctx_blocks
    k_loc = jnp.concatenate([kp_ref[0], kc_ref[0], kn_ref[0]], axis=0)
    v_loc = jnp.concatenate([vp_ref[0], vc_ref[0], vn_ref[0]], axis=0)
    row = lax.broadcasted_iota(i32, (SWA_BLOCK, 3 * SWA_BLOCK), 0)
    col = lax.broadcasted_iota(i32, (SWA_BLOCK, 3 * SWA_BLOCK), 1)
    rel = col - SWA_BLOCK - row
    kpos = (i - 1) * SWA_BLOCK + col
    ok = (jnp.abs(rel) <= SWA_WINDOW) & (kpos >= 0) & (kpos < seq) & (i >= 0)
    pen = jnp.where(ok, 0.0, NEG)
    pen = jnp.concatenate([pen, pen], axis=0)
    for j in range(SWA_Q // LANE):
        q = _pair_rows(q_ref[0, :, j * LANE:(j + 1) * LANE])
        s_loc = lax.dot_general(q, k_loc, _NT, preferred_element_type=f32) + pen
        s_ctx = lax.dot_general(q, kx_ref[0], _NT, preferred_element_type=f32)
        sink = jnp.concatenate([jnp.broadcast_to(sink_ref[2 * j:2 * j + 1, 0:1], (SWA_BLOCK, 1)),
                                jnp.broadcast_to(sink_ref[2 * j + 1:2 * j + 2, 0:1], (SWA_BLOCK, 1))], axis=0)
        o = _joint_attention(s_loc, s_ctx, v_loc, vx_ref[0], sink)
        o_ref[0, :, j * LANE:(j + 1) * LANE] = _pair_merge(o).astype(bf16)


def _swa(sq, sk, sv, sink_rows, n_ctx):
    nb, ta, _ = sq.shape
    ncb = n_ctx // SWA_BLOCK
    nblk = ta // SWA_BLOCK
    nlat = nblk - ncb
    prev = lambda b, g: (b, jnp.clip(g - ncb - 1, 0, nlat - 1) + ncb, 0)
    cur = lambda b, g: (b, g, 0)
    nxt = lambda b, g: (b, jnp.clip(g - ncb + 1, 0, nlat - 1) + ncb, 0)
    kvs = lambda f: pl.BlockSpec((1, SWA_BLOCK, SWA_KV), f)
    ctx = pl.BlockSpec((1, n_ctx, SWA_KV), lambda b, g: (b, 0, 0))
    return pl.pallas_call(
        functools.partial(_swa_kernel, n_ctx_blocks=ncb, seq=ta - n_ctx),
        grid=(nb, nblk),
        in_specs=[pl.BlockSpec((1, SWA_BLOCK, SWA_Q), cur), kvs(prev), kvs(cur), kvs(nxt), kvs(prev), kvs(cur), kvs(nxt),
                  ctx, ctx, pl.BlockSpec((8, LANE), lambda b, g: (0, 0))],
        out_specs=pl.BlockSpec((1, SWA_BLOCK, SWA_Q), cur),
        out_shape=jax.ShapeDtypeStruct((nb, ta, SWA_Q), bf16),
        compiler_params=pltpu.CompilerParams(dimension_semantics=("arbitrary", "arbitrary")),
        name="swa_attention",
    )(sq, sk, sk, sk, sv, sv, sv, sk, sv, sink_rows)


def _nat_kernel(q_ref, k_ref, v_ref, bias_ref, o_ref, *, n_ctx, n_rows):
    g = pl.program_id(1)
    is_ctx = g == 0
    ctx_pen = jnp.where(is_ctx, NEG, 0.0)
    rows_per_tile = TM // GRID_W
    nwin = NAT_ROWS * GRID_W
    for rr in range(rows_per_tile):
        r = jnp.maximum((g - 1) * rows_per_tile + rr, 0)
        rs = jnp.clip(r - NAT_ROWS // 2, 0, n_rows - NAT_ROWS)
        dr0 = rs - r + NAT_ROWS - 1
        start = pl.multiple_of(n_ctx + rs * GRID_W, GRID_W)
        for p in range(NAT_W // LANE):
            ls = slice(p * LANE, (p + 1) * LANE)
            q = _pair_rows(q_ref[0, rr * GRID_W:(rr + 1) * GRID_W, ls])
            s_loc = (lax.dot_general(q, k_ref[0, pl.ds(start, nwin), ls], _NT, preferred_element_type=f32)
                     + bias_ref[p, dr0] + ctx_pen)
            s_ctx = lax.dot_general(q, k_ref[0, 0:n_ctx, ls], _NT, preferred_element_type=f32)
            o = _joint_attention(s_loc, s_ctx, v_ref[0, pl.ds(start, nwin), ls], v_ref[0, 0:n_ctx, ls])
            o_ref[0, rr * GRID_W:(rr + 1) * GRID_W, ls] = _pair_merge(o).astype(bf16)


def _nat(nq, nk, nv, bias, n_ctx):
    nb, ta, w = nq.shape
    nt = ta // TM
    whole = pl.BlockSpec((1, ta, w), lambda b, g: (b, 0, 0), pipeline_mode=pl.Buffered(1))
    return pl.pallas_call(
        functools.partial(_nat_kernel, n_ctx=n_ctx, n_rows=(ta - n_ctx) // GRID_W),
        grid=(nb, nt),
        in_specs=[pl.BlockSpec((1, TM, w), lambda b, g: (b, g, 0)), whole, whole,
                  pl.BlockSpec(bias.shape, lambda b, g: (0, 0, 0, 0), pipeline_mode=pl.Buffered(1))],
        out_specs=pl.BlockSpec((1, TM, w), lambda b, g: (b, g, 0)),
        out_shape=jax.ShapeDtypeStruct((nb, ta, w), bf16),
        compiler_params=pltpu.CompilerParams(dimension_semantics=("arbitrary", "arbitrary"),
                                             vmem_limit_bytes=VMEM_LIMIT_BYTES),
        name="nat_attention",
    )(nq, nk, nv, bias)


def _nat_bias_tables(rpb):
    c = np.arange(GRID_W)[:, None]
    kc = np.arange(GRID_W)[None, :]
    ws = np.clip(c - NAT_COLS // 2, 0, GRID_W - NAT_COLS)
    valid = (kc >= ws) & (kc < ws + NAT_COLS)
    col_idx = np.clip(kc - c + NAT_COLS - 1, 0, 2 * NAT_COLS - 2)
    dr = np.arange(NAT_ROWS)[:, None] + np.arange(NAT_ROWS)[None, :]
    t = rpb.astype(f32)[:, dr][:, :, :, col_idx]
    t = jnp.where(jnp.asarray(valid)[None, None, None], t, NEG)
    t = t.transpose(0, 1, 3, 2, 4).reshape(NAT_HEADS // 2, 2, NAT_ROWS, GRID_W, NAT_ROWS * GRID_W)
    return t.transpose(0, 2, 1, 3, 4).reshape(NAT_HEADS // 2, NAT_ROWS, 2 * GRID_W, NAT_ROWS * GRID_W)


def _outproj_kernel(o_ref, sg_ref, hnw_ref, bd_ref, so_ref, no_ref, wa_ref, wb_ref, wn_ref, mod_ref, x_ref, y_ref):
    oo = o_ref[0, 0] + o_ref[1, 0]
    hi, lo = _split2(oo * oo)
    bd = bd_ref[...]
    msq = (jnp.dot(hi, bd, preferred_element_type=f32) + jnp.dot(lo, bd, preferred_element_type=f32)) * (1.0 / HEAD_DIM)
    a = oo * lax.rsqrt(msq + EPS) * hnw_ref[...] * sg_ref[0]
    y = (jnp.dot(a.astype(bf16), wa_ref[...], preferred_element_type=f32)
         + jnp.dot(so_ref[0], wb_ref[...], preferred_element_type=f32)
         + jnp.dot(no_ref[0], wn_ref[...], preferred_element_type=f32))
    y_ref[0] = x_ref[0] + mod_ref[0, 0, 2:3, :] * y


def _outproj(o, sg, hnw, bd, so, no, wa, wb, wn, mod, xall):
    nb, ta, d = xall.shape
    nt = ta // TM
    bs = lambda width: pl.BlockSpec((1, TM, width), lambda b, i: (b, i, 0))
    full = lambda a: pl.BlockSpec(a.shape, lambda b, i: (0,) * a.ndim)
    return pl.pallas_call(
        _outproj_kernel,
        grid=(nb, nt),
        in_specs=[pl.BlockSpec((2, 1, TM, HG_WIDTH), lambda b, i: (0, b, i, 0)), bs(HG_WIDTH), full(hnw), full(bd),
                  bs(SWA_Q), bs(NAT_W), full(wa), full(wb), full(wn),
                  pl.BlockSpec((1, 1, 6, d), lambda b, i: (b, jnp.minimum(i, 1), 0, 0)), bs(d)],
        out_specs=bs(d),
        out_shape=jax.ShapeDtypeStruct(xall.shape, f32),
        compiler_params=pltpu.CompilerParams(dimension_semantics=("arbitrary", "arbitrary"),
                                             vmem_limit_bytes=VMEM_LIMIT_BYTES),
        name="out_proj",
    )(o, sg, hnw, bd, so, no, wa, wb, wn, mod, xall)


def _top16_rows(vals, n):
    iota = lax.broadcasted_iota(i32, vals.shape, 0)
    best, pos = [], []
    for _ in range(PEER_TOPK):
        m = jnp.max(vals, axis=0, keepdims=True)
        idx = jnp.min(jnp.where(vals == m, iota, n), axis=0, keepdims=True)
        best.append(m)
        pos.append(idx)
        vals = jnp.where(iota == idx, -jnp.inf, vals)
    return jnp.concatenate(best, axis=0), jnp.concatenate(pos, axis=0)


def _route_kernel(x_ref, nw_ref, mod_ref, wq_ref, sub_ref, h_ref, idx_ref, g_ref):
    h = _norm_mod(x_ref[0], nw_ref[...], mod_ref[0, 0, 3:4, :], mod_ref[0, 0, 4:5, :])
    h_ref[0] = h
    q = jnp.dot(h.astype(bf16), wq_ref[...], preferred_element_type=f32).astype(bf16)
    half = PEER_DK // 2
    ids, gates = [], []
    for hh in range(PEER_HEADS):
        top_s, top_i = [], []
        for p in range(2):
            c0 = (hh * 2 + p) * half
            s = lax.dot_general(sub_ref[hh, p], q[:, c0:c0 + half], _NT, preferred_element_type=f32)
            ts, ti = _top16_rows(s, PEER_NKEYS)
            top_s.append(ts)
            top_i.append(ti)
        cand_s = jnp.concatenate([top_s[0][a:a + 1, :] + top_s[1] for a in range(PEER_TOPK)], axis=0)
        cand_i = jnp.concatenate([top_i[0][a:a + 1, :] * PEER_NKEYS + top_i[1] for a in range(PEER_TOPK)], axis=0)
        best, pos = _top16_rows(cand_s, PEER_TOPK * PEER_TOPK)
        iota = lax.broadcasted_iota(i32, cand_i.shape, 0)
        ids.append(jnp.concatenate(
            [jnp.max(jnp.where(iota == pos[r:r + 1, :], cand_i, -1), axis=0, keepdims=True) for r in range(PEER_TOPK)], axis=0))
        e = jnp.exp(best - best[0:1, :])
        gates.append(e / jnp.sum(e, axis=0, keepdims=True))
    idx_ref[0] = jnp.concatenate(ids, axis=0).T
    g_ref[0] = jnp.concatenate(gates, axis=0).T


def _route(xall, nw, mod, wq, sub):
    nb, ta, d = xall.shape
    nt = ta // TM
    full = lambda a: pl.BlockSpec(a.shape, lambda b, i: (0,) * a.ndim)
    return pl.pallas_call(
        _route_kernel,
        grid=(nb, nt),
        in_specs=[pl.BlockSpec((1, TM, d), lambda b, i: (b, i, 0)), full(nw),
                  pl.BlockSpec((1, 1, 6, d), lambda b, i: (b, jnp.minimum(i, 1), 0, 0)), full(wq), full(sub)],
        out_specs=[pl.BlockSpec((1, TM, d), lambda b, i: (b, i, 0)),
                   pl.BlockSpec((1, TM, PEER_SLOTS), lambda b, i: (b, i, 0)),
                   pl.BlockSpec((1, TM, PEER_SLOTS), lambda b, i: (b, i, 0))],
        out_shape=[jax.ShapeDtypeStruct(xall.shape, f32), jax.ShapeDtypeStruct((nb, ta, PEER_SLOTS), i32),
                   jax.ShapeDtypeStruct((nb, ta, PEER_SLOTS), f32)],
        compiler_params=pltpu.CompilerParams(dimension_semantics=("arbitrary", "arbitrary"),
                                             vmem_limit_bytes=VMEM_LIMIT_BYTES),
        name="peer_route",
    )(xall, nw, mod, wq, sub)


def _expert_kernel(idx_ref, h_ref, g_ref, x_ref, mod_ref, u_hbm, v_hbm, o_ref, ubuf, vbuf, sem):
    d = h_ref.shape[-1]

    def row_copy(tab, e, buf, slot, k, s):
        return pltpu.make_async_copy(tab.at[pl.ds(e, 1), :], buf.at[slot, pl.ds(k, 1), :], s)

    def issue(t, slot):
        for k in range(PEER_SLOTS):
            e = idx_ref[t, k]
            row_copy(u_hbm, e, ubuf, slot, k, sem.at[0, slot]).start()
            row_copy(v_hbm, e, vbuf, slot, k, sem.at[1, slot]).start()

    def wait(slot):
        pltpu.make_async_copy(u_hbm.at[pl.ds(0, PEER_SLOTS), :], ubuf.at[slot], sem.at[0, slot]).wait()
        pltpu.make_async_copy(v_hbm.at[pl.ds(0, PEER_SLOTS), :], vbuf.at[slot], sem.at[1, slot]).wait()

    for t in range(PEER_NBUF - 1):
        issue(t, t)

    def body(t, carry):
        slot = lax.rem(t, PEER_NBUF)
        ahead = t + PEER_NBUF - 1

        @pl.when(ahead < PEER_TB)
        def _():
            issue(ahead, lax.rem(ahead, PEER_NBUF))

        wait(slot)
        h8 = jnp.broadcast_to(h_ref[pl.ds(t, 1), :], (8, d)).astype(bf16)
        a = lax.dot_general(h8, ubuf[slot].astype(bf16), _NT, preferred_element_type=f32)[0:1, :]
        w = g_ref[pl.ds(t, 1), :] * (0.5 * a * (1.0 + lax.erf(a * (2.0 ** -0.5))))
        w8 = jnp.broadcast_to(w, (8, PEER_SLOTS)).astype(bf16)
        y = jnp.dot(w8, vbuf[slot].astype(bf16), preferred_element_type=f32)
        o_ref[pl.ds(t, 1), :] = y[0:1, :]
        return carry

    lax.fori_loop(0, PEER_TB, body, 0)
    o_ref[...] = x_ref[...] + mod_ref[0, 0, 5:6, :] * o_ref[...]


def _experts(idx, h, g, xflat, mod, u_tab, v_tab, ta, n_ctx):
    n, d = xflat.shape
    bpb = ta // PEER_TB
    cb = n_ctx // PEER_TB
    tokb = lambda width: pl.BlockSpec((PEER_TB, width), lambda i: (i, 0))
    return pl.pallas_call(
        _expert_kernel,
        grid=(n // PEER_TB,),
        in_specs=[pl.BlockSpec((PEER_TB, PEER_SLOTS), lambda i: (i, 0), memory_space=pltpu.SMEM),
                  tokb(d), tokb(PEER_SLOTS), tokb(d),
                  pl.BlockSpec((1, 1, 6, d), lambda i: (i // bpb, jnp.where(i % bpb < cb, 0, 1), 0, 0)),
                  pl.BlockSpec(memory_space=pl.ANY), pl.BlockSpec(memory_space=pl.ANY)],
        out_specs=tokb(d),
        out_shape=jax.ShapeDtypeStruct((n, d), f32),
        scratch_shapes=[pltpu.VMEM((PEER_NBUF, PEER_SLOTS, d), f32), pltpu.VMEM((PEER_NBUF, PEER_SLOTS, d), f32),
                        pltpu.SemaphoreType.DMA((2, PEER_NBUF))],
        compiler_params=pltpu.CompilerParams(dimension_semantics=("arbitrary",), vmem_limit_bytes=VMEM_LIMIT_BYTES),
        name="peer_experts",
    )(idx, h, g, xflat, mod, u_tab, v_tab)


def _final_norm_kernel(x_ref, w_ref, o_ref):
    x = x_ref[0]
    o_ref[0] = x * lax.rsqrt(jnp.mean(x * x, axis=-1, keepdims=True) + EPS) * w_ref[...]


def _final_norm(xall, w, n_ctx):
    nb, ta, d = xall.shape
    off = n_ctx // TM
    return pl.pallas_call(
        _final_norm_kernel,
        grid=(nb, (ta - n_ctx) // TM),
        in_specs=[pl.BlockSpec((1, TM, d), lambda b, i: (b, i + off, 0)), pl.BlockSpec((1, d), lambda b, i: (0, 0))],
        out_specs=pl.BlockSpec((1, TM, d), lambda b, i: (b, i, 0)),
        out_shape=jax.ShapeDtypeStruct((nb, ta - n_ctx, d), f32),
        compiler_params=pltpu.CompilerParams(dimension_semantics=("arbitrary", "arbitrary")),
        name="final_norm",
    )(xall, w.reshape(1, d))


def _rope_tables(n_ctx, seq):
    nf = HEAD_DIM // 4
    pos = np.arange(seq)
    inv = ROPE_BASE ** (-np.arange(nf, dtype=np.float32) / nf)
    ar = (pos // GRID_W).astype(np.float32)[:, None] * inv
    ac = (pos % GRID_W).astype(np.float32)[:, None] * inv
    cos = np.concatenate([np.cos(ar), np.cos(ar), np.cos(ac), np.cos(ac)], axis=-1)
    sin = np.concatenate([-np.sin(ar), np.sin(ar), -np.sin(ac), np.sin(ac)], axis=-1)
    cos = np.concatenate([np.ones((n_ctx, HEAD_DIM), np.float32), cos.astype(np.float32)], axis=0)
    sin = np.concatenate([np.zeros((n_ctx, HEAD_DIM), np.float32), sin.astype(np.float32)], axis=0)
    return jnp.asarray(np.stack([np.tile(cos, (1, 2)), np.tile(sin, (1, 2))]))


def _tri_tables():
    r = np.arange(TM)[:, None]
    c = np.arange(TM)[None, :]
    same = (r // HG_SUB) == (c // HG_SUB)
    return jnp.asarray(np.stack([same & (c <= r), same & (c >= r)]).astype(np.float32), dtype=bf16)


def _block_diag_ones():
    r = np.arange(HG_WIDTH)
    return jnp.asarray((r[:, None] // HEAD_DIM == r[None, :] // HEAD_DIM).astype(np.float32), dtype=bf16)


def _swa_perm():
    return np.concatenate([np.arange(h * HEAD_DIM, (h + 1) * HEAD_DIM) for h in SWA_HEAD_ORDER])


def kernel(x, c, ctx, c_ctx, w_mod, b_mod, norm_mix_w, norm_ffn_w, w_in, hgrn_lb_logits, hgrn_norm_w, swa_sink, nat_rpb,
           w_out, peer_wq, peer_subkeys, peer_u, peer_v, final_norm_w):
    nb, seq, d = x.shape
    n_ctx = ctx.shape[1]
    depth = w_in.shape[0]
    ta = n_ctx + seq
    assert n_ctx == TM and seq % TM == 0 and seq % (GRID_W * NAT_ROWS) == 0 and d % LANE == 0
    assert w_in.shape[-1] == PROJ_WIDTH and peer_wq.shape[-1] == PEER_HEADS * PEER_DK

    mods = _mods(c, c_ctx, w_mod, b_mod)
    cs = _rope_tables(n_ctx, seq)
    tri = _tri_tables()
    bd = _block_diag_ones()
    perm = _swa_perm()
    q0 = 5 * HG_WIDTH
    col_perm = np.concatenate([np.arange(q0), q0 + perm, np.arange(q0 + SWA_Q, PROJ_WIDTH)])
    w_in_b = w_in[:, :, col_perm].astype(bf16)
    w_out_b = w_out.astype(bf16)
    wa = w_out_b[:, :HG_WIDTH]
    wb = w_out_b[:, HG_WIDTH:HG_WIDTH + SWA_Q][:, perm]
    wn = w_out_b[:, HG_WIDTH + SWA_Q:]
    wq_b = peer_wq.astype(bf16)
    sub_b = peer_subkeys.astype(bf16)
    lb = jnp.cumsum(jax.nn.softmax(hgrn_lb_logits.astype(f32), axis=1), axis=1)
    lb = lb - lb[:, :1]
    gp = jnp.stack([jnp.log(lb[0]), jnp.log1p(-lb[0]), 1.0 - lb[0], jnp.log(lb[1]), jnp.log1p(-lb[1]), 1.0 - lb[1],
                    jnp.zeros_like(lb[0]), jnp.zeros_like(lb[0])], axis=1)
    sink_rows = jnp.zeros((depth, 8, LANE), f32).at[:, :SWA_HEADS].set(
        jnp.broadcast_to(swa_sink.astype(f32)[:, list(SWA_HEAD_ORDER), None], (depth, SWA_HEADS, LANE)))

    xall = jnp.concatenate([ctx, x], axis=1)
    for l in range(depth):
        mod = mods[l]
        hq, cum, kk, hv, sg, sq, sk, sv, nq, nk, nv = _proj(xall, norm_mix_w[l].reshape(1, d), mod, w_in_b[l], cs, gp[l], tri)
        o = _hgrn_scan(hq, cum, kk, hv, bd)
        so = _swa(sq, sk, sv, sink_rows[l], n_ctx)
        no = _nat(nq, nk, nv, _nat_bias_tables(nat_rpb[l]), n_ctx)
        xall = _outproj(o, sg, hgrn_norm_w[l].reshape(1, HG_WIDTH), bd, so, no, wa[l], wb[l], wn[l], mod, xall)
        h2, idx, g = _route(xall, norm_ffn_w[l].reshape(1, d), mod, wq_b[l], sub_b[l])
        xall = _experts(idx.reshape(nb * ta, PEER_SLOTS), h2.reshape(nb * ta, d), g.reshape(nb * ta, PEER_SLOTS),
                        xall.reshape(nb * ta, d), mod, peer_u[l], peer_v[l], ta, n_ctx).reshape(nb, ta, d)
    return _final_norm(xall, final_norm_w, n_ctx)
```

```python
import functools

import numpy as np
import jax
import jax.numpy as jnp
from jax import lax
from jax.experimental import pallas as pl
from jax.experimental.pallas import tpu as pltpu

f32 = jnp.float32
bf16 = jnp.bfloat16
i32 = jnp.int32

LANE = 128
VMEM_LIMIT_BYTES = 56 * 1024 * 1024

GRID_W = 64
HEAD_DIM = 64
EPS = 1e-6
HG_HEADS = 4
HG_WIDTH = 256
HG_SUB = 16
SWA_HEADS = 6
SWA_KV_HEADS = 2
SWA_WINDOW = 128
SWA_BLOCK = 128
ROPE_BASE = 10000.0
NAT_HEADS = 6
NAT_ROWS = 8
NAT_COLS = 16
SWA_Q = SWA_HEADS * HEAD_DIM
SWA_KV = SWA_KV_HEADS * HEAD_DIM
NAT_W = NAT_HEADS * HEAD_DIM
PROJ_WIDTH = 5 * HG_WIDTH + SWA_Q + 2 * SWA_KV + 3 * NAT_W
PEER_HEADS = 8
PEER_NKEYS = 128
PEER_DK = 256
PEER_TOPK = 16
PEER_SLOTS = PEER_HEADS * PEER_TOPK

TM = 256
PEER_TB = 128
PEER_NBUF = 8
PEER_AHEAD = PEER_NBUF - 1
NEG = -1e30

SWA_HEAD_ORDER = (0, 3, 1, 4, 2, 5)

_NT = (((1,), (1,)), ((), ()))
_TN = (((0,), (0,)), ((), ()))


def _silu(x):
    return x * (1.0 / (1.0 + jnp.exp(-x)))


def _split2(x):
    hi = x.astype(bf16)
    lo = (x - hi.astype(f32)).astype(bf16)
    return hi, lo


def _mods_kernel(sc_ref, w_ref, b_ref, o_ref):
    s = _silu(sc_ref[...])
    o_ref[0] = jnp.dot(s.astype(bf16), w_ref[0].astype(bf16), preferred_element_type=f32) + b_ref[0]


def _mods(c, c_ctx, w_mod, b_mod):
    depth, d, d6 = w_mod.shape
    nb = c.shape[0]
    rows = jnp.zeros((8, d), f32).at[:nb].set(c).at[nb].set(c_ctx)
    tn = d6 // 4
    out = pl.pallas_call(
        _mods_kernel,
        grid=(depth, 4),
        in_specs=[pl.BlockSpec((8, d), lambda l, j: (0, 0)),
                  pl.BlockSpec((1, d, tn), lambda l, j: (l, 0, j)),
                  pl.BlockSpec((1, 1, tn), lambda l, j: (l, 0, j))],
        out_specs=pl.BlockSpec((1, 8, tn), lambda l, j: (l, 0, j)),
        out_shape=jax.ShapeDtypeStruct((depth, 8, d6), f32),
        compiler_params=pltpu.CompilerParams(dimension_semantics=("arbitrary", "arbitrary"),
                                             vmem_limit_bytes=VMEM_LIMIT_BYTES),
        name="adaln_mods",
    )(rows, w_mod, b_mod.reshape(depth, 1, d6))
    lat = out[:, :nb].reshape(depth, nb, 6, d)
    ctx = jnp.broadcast_to(out[:, nb].reshape(depth, 1, 6, d), (depth, nb, 6, d))
    return jnp.stack([ctx, lat], axis=2)


def _norm_mod(x, nw, shift, scale):
    y = x * lax.rsqrt(jnp.mean(x * x, axis=-1, keepdims=True) + EPS) * nw
    return y * (1.0 + scale) + shift


def _rope(t, cos, sins):
    lane = lax.broadcasted_iota(i32, t.shape, 1)
    sw = jnp.where((lane % 32) < 16, pltpu.roll(t, LANE - 16, 1), pltpu.roll(t, 16, 1))
    return t * cos + sw * sins


def _hgrn_gates(z, log_lb, log1m_lb, one_m_lb):
    log_sig = jnp.minimum(z, 0.0) - jnp.log1p(jnp.exp(-jnp.abs(z)))
    c = log1m_lb + log_sig
    m = jnp.maximum(log_lb, c)
    log_f = m + jnp.log1p(jnp.exp(-jnp.abs(log_lb - c)))
    k = one_m_lb * (1.0 / (1.0 + jnp.exp(z)))
    return log_f, k


def _cumsum16(tri, log_f):
    h1 = log_f.astype(bf16)
    r1 = log_f - h1.astype(f32)
    h2 = r1.astype(bf16)
    h3 = (r1 - h2.astype(f32)).astype(bf16)
    return (jnp.dot(tri, h1, preferred_element_type=f32) + jnp.dot(tri, h2, preferred_element_type=f32)
            + jnp.dot(tri, h3, preferred_element_type=f32))


def _proj_kernel(x_ref, nw_ref, mod_ref, w_ref, cs_ref, gp_ref, tri_ref,
                 hq_ref, cum_ref, kk_ref, hv_ref, sg_ref, sq_ref, sk_ref, sv_ref, nq_ref, nk_ref, nv_ref):
    h = _norm_mod(x_ref[0], nw_ref[...], mod_ref[0, 0, 0:1, :], mod_ref[0, 0, 1:2, :])
    p = jnp.dot(h.astype(bf16), w_ref[...], preferred_element_type=f32)
    w = HG_WIDTH
    hq_ref[0] = _silu(p[:, 0:w])
    for d in range(2):
        log_f, k = _hgrn_gates(p[:, (1 + d) * w:(2 + d) * w], gp_ref[3 * d:3 * d + 1, :],
                               gp_ref[3 * d + 1:3 * d + 2, :], gp_ref[3 * d + 2:3 * d + 3, :])
        cum_ref[d, 0] = _cumsum16(tri_ref[d], log_f)
        kk_ref[d, 0] = k
    hv_ref[0] = p[:, 3 * w:4 * w]
    sg_ref[0] = _silu(p[:, 4 * w:5 * w])
    cos, sins = cs_ref[0], cs_ref[1]
    o = 5 * w
    scale = HEAD_DIM ** -0.5
    for j in range(SWA_Q // LANE):
        sq_ref[0, :, j * LANE:(j + 1) * LANE] = (_rope(p[:, o + j * LANE:o + (j + 1) * LANE], cos, sins) * scale).astype(bf16)
    o += SWA_Q
    sk_ref[0] = _rope(p[:, o:o + SWA_KV], cos, sins).astype(bf16)
    o += SWA_KV
    sv_ref[0] = p[:, o:o + SWA_KV].astype(bf16)
    o += SWA_KV
    nq_ref[0] = (p[:, o:o + NAT_W] * scale).astype(bf16)
    o += NAT_W
    nk_ref[0] = p[:, o:o + NAT_W].astype(bf16)
    o += NAT_W
    nv_ref[0] = p[:, o:o + NAT_W].astype(bf16)


def _proj(xall, nw, mod, w_in, cs, gp, tri):
    nb, ta, d = xall.shape
    nt = ta // TM
    tok = lambda width, dt: jax.ShapeDtypeStruct((nb, ta, width), dt)
    tok2 = jax.ShapeDtypeStruct((2, nb, ta, HG_WIDTH), f32)
    bs = lambda width: pl.BlockSpec((1, TM, width), lambda b, i: (b, i, 0))
    bs2 = pl.BlockSpec((2, 1, TM, HG_WIDTH), lambda b, i: (0, b, i, 0))
    return pl.pallas_call(
        _proj_kernel,
        grid=(nb, nt),
        in_specs=[pl.BlockSpec((1, TM, d), lambda b, i: (b, i, 0)),
                  pl.BlockSpec((1, d), lambda b, i: (0, 0)),
                  pl.BlockSpec((1, 1, 6, d), lambda b, i: (b, jnp.minimum(i, 1), 0, 0)),
                  pl.BlockSpec((d, PROJ_WIDTH), lambda b, i: (0, 0)),
                  pl.BlockSpec((2, TM, LANE), lambda b, i: (0, i, 0)),
                  pl.BlockSpec((8, HG_WIDTH), lambda b, i: (0, 0)),
                  pl.BlockSpec((2, TM, TM), lambda b, i: (0, 0, 0))],
        out_specs=[bs(HG_WIDTH), bs2, bs2, bs(HG_WIDTH), bs(HG_WIDTH), bs(SWA_Q), bs(SWA_KV), bs(SWA_KV),
                   bs(NAT_W), bs(NAT_W), bs(NAT_W)],
        out_shape=[tok(HG_WIDTH, f32), tok2, tok2, tok(HG_WIDTH, f32), tok(HG_WIDTH, f32), tok(SWA_Q, bf16),
                   tok(SWA_KV, bf16), tok(SWA_KV, bf16), tok(NAT_W, bf16), tok(NAT_W, bf16), tok(NAT_W, bf16)],
        compiler_params=pltpu.CompilerParams(dimension_semantics=("arbitrary", "arbitrary"),
                                             vmem_limit_bytes=VMEM_LIMIT_BYTES),
        name="norm_in_proj",
    )(xall, nw, mod, w_in, cs, gp, tri)


def _hgrn_kernel(q_ref, cum_ref, k_ref, v_ref, bd_ref, o_ref, st_ref):
    d = pl.program_id(1)
    fwd = d == 0

    @pl.when(pl.program_id(2) == 0)
    def _():
        st_ref[...] = jnp.zeros_like(st_ref)

    bd = bd_ref[...]
    bd_mask = bd > 0
    sgn = jnp.where(fwd, 1, -1)
    t_sgn = lax.broadcasted_iota(i32, (HG_SUB, 1), 0) * sgn
    nsteps = TM // HG_SUB

    def step(i, carry):
        sc = jnp.where(fwd, i, nsteps - 1 - i)
        r0 = pl.multiple_of(sc * HG_SUB, HG_SUB)
        q = q_ref[0, pl.ds(r0, HG_SUB), :]
        b = cum_ref[0, 0, pl.ds(r0, HG_SUB), :]
        k = k_ref[0, 0, pl.ds(r0, HG_SUB), :]
        v = v_ref[0, pl.ds(r0, HG_SUB), :]
        b_end = jnp.where(fwd, b[HG_SUB - 1:HG_SUB, :], b[0:1, :])
        st = st_ref[...]
        o = lax.dot_general((q * jnp.exp(b)).astype(bf16), st.astype(bf16), _NT, preferred_element_type=f32)
        rows = []
        for s in range(HG_SUB):
            valid = t_sgn >= s * sgn
            decay = jnp.exp(jnp.minimum(b - b[s:s + 1, :], 0.0))
            rows.append(jnp.where(valid, decay * q * k[s:s + 1, :], 0.0))
        hi, lo = _split2(jnp.concatenate(rows, axis=0))
        pe = jnp.dot(hi, bd, preferred_element_type=f32) + jnp.dot(lo, bd, preferred_element_type=f32)
        for s in range(HG_SUB):
            o = o + pe[s * HG_SUB:(s + 1) * HG_SUB, :] * v[s:s + 1, :]
        o_ref[0, 0, pl.ds(r0, HG_SUB), :] = o
        kd = (k * jnp.exp(b_end - b)).astype(bf16)
        kv_t = lax.dot_general(v.astype(bf16), kd, _TN, preferred_element_type=f32)
        st_ref[...] = st * jnp.exp(b_end) + jnp.where(bd_mask, kv_t, 0.0)
        return carry

    lax.fori_loop(0, nsteps, step, 0)


def _hgrn_scan(hq, cum, kk, hv, bd):
    nb, ta, w = hq.shape
    nt = ta // TM

    def blk(d, j):
        return jnp.where(d == 0, j, jnp.where(j == 0, 0, nt - j))

    return pl.pallas_call(
        _hgrn_kernel,
        grid=(nb, 2, nt),
        in_specs=[pl.BlockSpec((1, TM, w), lambda b, d, j: (b, blk(d, j), 0)),
                  pl.BlockSpec((1, 1, TM, w), lambda b, d, j: (d, b, blk(d, j), 0)),
                  pl.BlockSpec((1, 1, TM, w), lambda b, d, j: (d, b, blk(d, j), 0)),
                  pl.BlockSpec((1, TM, w), lambda b, d, j: (b, blk(d, j), 0)),
                  pl.BlockSpec((w, w), lambda b, d, j: (0, 0))],
        out_specs=pl.BlockSpec((1, 1, TM, w), lambda b, d, j: (d, b, blk(d, j), 0)),
        out_shape=jax.ShapeDtypeStruct((2, nb, ta, w), f32),
        scratch_shapes=[pltpu.VMEM((w, w), f32)],
        compiler_params=pltpu.CompilerParams(dimension_semantics=("arbitrary", "arbitrary", "arbitrary")),
        name="hgrn_scan",
    )(hq, cum, kk, hv, bd)


def _pair_rows(q2):
    lane = lax.broadcasted_iota(i32, q2.shape, 1)
    zero = jnp.zeros_like(q2)
    return jnp.concatenate([jnp.where(lane < HEAD_DIM, q2, zero), jnp.where(lane >= HEAD_DIM, q2, zero)], axis=0)


def _pair_merge(o):
    m = o.shape[0] // 2
    lane = lax.broadcasted_iota(i32, (m, LANE), 1)
    return jnp.where(lane < HEAD_DIM, o[:m], o[m:])


def _joint_attention(s_loc, s_ctx, v_loc, v_ctx, sink=None):
    m = jnp.maximum(jnp.max(s_loc, axis=-1, keepdims=True), jnp.max(s_ctx, axis=-1, keepdims=True))
    if sink is not None:
        m = jnp.maximum(m, sink)
    p_loc = jnp.exp(s_loc - m)
    p_ctx = jnp.exp(s_ctx - m)
    den = jnp.sum(p_loc, axis=-1, keepdims=True) + jnp.sum(p_ctx, axis=-1, keepdims=True)
    if sink is not None:
        den = den + jnp.exp(sink - m)
    o = (jnp.dot(p_loc.astype(bf16), v_loc, preferred_element_type=f32)
         + jnp.dot(p_ctx.astype(bf16), v_ctx, preferred_element_type=f32))
    return o / den


def _swa_kernel(q_ref, kp_ref, kc_ref, kn_ref, vp_ref, vc_ref, vn_ref, kx_ref, vx_ref, sink_ref, o_ref, *, n_ctx_blocks, seq):
    g = pl.program_id(1)
    i = g - n_ctx_blocks
    k_loc = jnp.concatenate([kp_ref[0], kc_ref[0], kn_ref[0]], axis=0)
    v_loc = jnp.concatenate([vp_ref[0], vc_ref[0], vn_ref[0]], axis=0)
    row = lax.broadcasted_iota(i32, (SWA_BLOCK, 3 * SWA_BLOCK), 0)
    col = lax.broadcasted_iota(i32, (SWA_BLOCK, 3 * SWA_BLOCK), 1)
    rel = col - SWA_BLOCK - row
    kpos = (i - 1) * SWA_BLOCK + col
    ok = (jnp.abs(rel) <= SWA_WINDOW) & (kpos >= 0) & (kpos < seq) & (i >= 0)
    pen = jnp.where(ok, 0.0, NEG)
    pen = jnp.concatenate([pen, pen], axis=0)
    for j in range(SWA_Q // LANE):
        q = _pair_rows(q_ref[0, :, j * LANE:(j + 1) * LANE])
        s_loc = lax.dot_general(q, k_loc, _NT, preferred_element_type=f32) + pen
        s_ctx = lax.dot_general(q, kx_ref[0], _NT, preferred_element_type=f32)
        sink = jnp.concatenate([jnp.broadcast_to(sink_ref[2 * j:2 * j + 1, 0:1], (SWA_BLOCK, 1)),
                                jnp.broadcast_to(sink_ref[2 * j + 1:2 * j + 2, 0:1], (SWA_BLOCK, 1))], axis=0)
        o = _joint_attention(s_loc, s_ctx, v_loc, vx_ref[0], sink)
        o_ref[0, :, j * LANE:(j + 1) * LANE] = _pair_merge(o).astype(bf16)


def _swa(sq, sk, sv, sink_rows, n_ctx):
    nb, ta, _ = sq.shape
    ncb = n_ctx // SWA_BLOCK
    nblk = ta // SWA_BLOCK
    nlat = nblk - ncb
    prev = lambda b, g: (b, jnp.clip(g - ncb - 1, 0, nlat - 1) + ncb, 0)
    cur = lambda b, g: (b, g, 0)
    nxt = lambda b, g: (b, jnp.clip(g - ncb + 1, 0, nlat - 1) + ncb, 0)
    kvs = lambda f: pl.BlockSpec((1, SWA_BLOCK, SWA_KV), f)
    ctx = pl.BlockSpec((1, n_ctx, SWA_KV), lambda b, g: (b, 0, 0))
    return pl.pallas_call(
        functools.partial(_swa_kernel, n_ctx_blocks=ncb, seq=ta - n_ctx),
        grid=(nb, nblk),
        in_specs=[pl.BlockSpec((1, SWA_BLOCK, SWA_Q), cur), kvs(prev), kvs(cur), kvs(nxt), kvs(prev), kvs(cur), kvs(nxt),
                  ctx, ctx, pl.BlockSpec((8, LANE), lambda b, g: (0, 0))],
        out_specs=pl.BlockSpec((1, SWA_BLOCK, SWA_Q), cur),
        out_shape=jax.ShapeDtypeStruct((nb, ta, SWA_Q), bf16),
        compiler_params=pltpu.CompilerParams(dimension_semantics=("arbitrary", "arbitrary")),
        name="swa_attention",
    )(sq, sk, sk, sk, sv, sv, sv, sk, sv, sink_rows)


def _nat_kernel(q_ref, k_ref, v_ref, bias_ref, o_ref, *, n_ctx, n_rows):
    g = pl.program_id(1)
    is_ctx = g == 0
    ctx_pen = jnp.where(is_ctx, NEG, 0.0)
    rows_per_tile = TM // GRID_W
    nwin = NAT_ROWS * GRID_W
    for rr in range(rows_per_tile):
        r = jnp.maximum((g - 1) * rows_per_tile + rr, 0)
        rs = jnp.clip(r - NAT_ROWS // 2, 0, n_rows - NAT_ROWS)
        dr0 = rs - r + NAT_ROWS - 1
        start = pl.multiple_of(n_ctx + rs * GRID_W, GRID_W)
        for p in range(NAT_W // LANE):
            ls = slice(p * LANE, (p + 1) * LANE)
            q = _pair_rows(q_ref[0, rr * GRID_W:(rr + 1) * GRID_W, ls])
            s_loc = (lax.dot_general(q, k_ref[0, pl.ds(start, nwin), ls], _NT, preferred_element_type=f32)
                     + bias_ref[p, dr0] + ctx_pen)
            s_ctx = lax.dot_general(q, k_ref[0, 0:n_ctx, ls], _NT, preferred_element_type=f32)
            o = _joint_attention(s_loc, s_ctx, v_ref[0, pl.ds(start, nwin), ls], v_ref[0, 0:n_ctx, ls])
            o_ref[0, rr * GRID_W:(rr + 1) * GRID_W, ls] = _pair_merge(o).astype(bf16)


def _nat(nq, nk, nv, bias, n_ctx):
    nb, ta, w = nq.shape
    nt = ta // TM
    whole = pl.BlockSpec((1, ta, w), lambda b, g: (b, 0, 0), pipeline_mode=pl.Buffered(1))
    return pl.pallas_call(
        functools.partial(_nat_kernel, n_ctx=n_ctx, n_rows=(ta - n_ctx) // GRID_W),
        grid=(nb, nt),
        in_specs=[pl.BlockSpec((1, TM, w), lambda b, g: (b, g, 0)), whole, whole,
                  pl.BlockSpec(bias.shape, lambda b, g: (0, 0, 0, 0), pipeline_mode=pl.Buffered(1))],
        out_specs=pl.BlockSpec((1, TM, w), lambda b, g: (b, g, 0)),
        out_shape=jax.ShapeDtypeStruct((nb, ta, w), bf16),
        compiler_params=pltpu.CompilerParams(dimension_semantics=("arbitrary", "arbitrary"),
                                             vmem_limit_bytes=VMEM_LIMIT_BYTES),
        name="nat_attention",
    )(nq, nk, nv, bias)


def _nat_bias_tables(rpb):
    c = np.arange(GRID_W)[:, None]
    kc = np.arange(GRID_W)[None, :]
    ws = np.clip(c - NAT_COLS // 2, 0, GRID_W - NAT_COLS)
    valid = (kc >= ws) & (kc < ws + NAT_COLS)
    col_idx = np.clip(kc - c + NAT_COLS - 1, 0, 2 * NAT_COLS - 2)
    dr = np.arange(NAT_ROWS)[:, None] + np.arange(NAT_ROWS)[None, :]
    t = rpb.astype(f32)[:, dr][:, :, :, col_idx]
    t = jnp.where(jnp.asarray(valid)[None, None, None], t, NEG)
    t = t.transpose(0, 1, 3, 2, 4).reshape(NAT_HEADS // 2, 2, NAT_ROWS, GRID_W, NAT_ROWS * GRID_W)
    return t.transpose(0, 2, 1, 3, 4).reshape(NAT_HEADS // 2, NAT_ROWS, 2 * GRID_W, NAT_ROWS * GRID_W)


def _outproj_kernel(o_ref, sg_ref, hnw_ref, bd_ref, so_ref, no_ref, wa_ref, wb_ref, wn_ref, mod_ref, x_ref, y_ref):
    oo = o_ref[0, 0] + o_ref[1, 0]
    hi, lo = _split2(oo * oo)
    bd = bd_ref[...]
    msq = (jnp.dot(hi, bd, preferred_element_type=f32) + jnp.dot(lo, bd, preferred_element_type=f32)) * (1.0 / HEAD_DIM)
    a = oo * lax.rsqrt(msq + EPS) * hnw_ref[...] * sg_ref[0]
    y = (jnp.dot(a.astype(bf16), wa_ref[...], preferred_element_type=f32)
         + jnp.dot(so_ref[0], wb_ref[...], preferred_element_type=f32)
         + jnp.dot(no_ref[0], wn_ref[...], preferred_element_type=f32))
    y_ref[0] = x_ref[0] + mod_ref[0, 0, 2:3, :] * y


def _outproj(o, sg, hnw, bd, so, no, wa, wb, wn, mod, xall):
    nb, ta, d = xall.shape
    nt = ta // TM
    bs = lambda width: pl.BlockSpec((1, TM, width), lambda b, i: (b, i, 0))
    full = lambda a: pl.BlockSpec(a.shape, lambda b, i: (0,) * a.ndim)
    return pl.pallas_call(
        _outproj_kernel,
        grid=(nb, nt),
        in_specs=[pl.BlockSpec((2, 1, TM, HG_WIDTH), lambda b, i: (0, b, i, 0)), bs(HG_WIDTH), full(hnw), full(bd),
                  bs(SWA_Q), bs(NAT_W), full(wa), full(wb), full(wn),
                  pl.BlockSpec((1, 1, 6, d), lambda b, i: (b, jnp.minimum(i, 1), 0, 0)), bs(d)],
        out_specs=bs(d),
        out_shape=jax.ShapeDtypeStruct(xall.shape, f32),
        compiler_params=pltpu.CompilerParams(dimension_semantics=("arbitrary", "arbitrary"),
                                             vmem_limit_bytes=VMEM_LIMIT_BYTES),
        name="out_proj",
    )(o, sg, hnw, bd, so, no, wa, wb, wn, mod, xall)


def _top16_rows(vals, n):
    iota = lax.broadcasted_iota(i32, vals.shape, 0)
    best, pos = [], []
    for _ in range(PEER_TOPK):
        m = jnp.max(vals, axis=0, keepdims=True)
        idx = jnp.min(jnp.where(vals == m, iota, n), axis=0, keepdims=True)
        best.append(m)
        pos.append(idx)
        vals = jnp.where(iota == idx, -jnp.inf, vals)
    return jnp.concatenate(best, axis=0), jnp.concatenate(pos, axis=0)


def _route_kernel(x_ref, nw_ref, mod_ref, wq_ref, sub_ref, h_ref, idx_ref, g_ref):
    h = _norm_mod(x_ref[0], nw_ref[...], mod_ref[0, 0, 3:4, :], mod_ref[0, 0, 4:5, :])
    h_ref[0] = h
    q = jnp.dot(h.astype(bf16), wq_ref[...], preferred_element_type=f32).astype(bf16)
    half = PEER_DK // 2
    ids, gates = [], []
    for hh in range(PEER_HEADS):
        top_s, top_i = [], []
        for p in range(2):
            c0 = (hh * 2 + p) * half
            s = lax.dot_general(sub_ref[hh, p], q[:, c0:c0 + half], _NT, preferred_element_type=f32)
            ts, ti = _top16_rows(s, PEER_NKEYS)
            top_s.append(ts)
            top_i.append(ti)
        cand_s = jnp.concatenate([top_s[0][a:a + 1, :] + top_s[1] for a in range(PEER_TOPK)], axis=0)
        cand_i = jnp.concatenate([top_i[0][a:a + 1, :] * PEER_NKEYS + top_i[1] for a in range(PEER_TOPK)], axis=0)
        best, pos = _top16_rows(cand_s, PEER_TOPK * PEER_TOPK)
        iota = lax.broadcasted_iota(i32, cand_i.shape, 0)
        ids.append(jnp.concatenate(
            [jnp.max(jnp.where(iota == pos[r:r + 1, :], cand_i, -1), axis=0, keepdims=True) for r in range(PEER_TOPK)], axis=0))
        e = jnp.exp(best - best[0:1, :])
        gates.append(e / jnp.sum(e, axis=0, keepdims=True))
    idx_ref[0] = jnp.concatenate(ids, axis=0).T
    g_ref[0] = jnp.concatenate(gates, axis=0).T


def _route(xall, nw, mod, wq, sub):
    nb, ta, d = xall.shape
    nt = ta // TM
    full = lambda a: pl.BlockSpec(a.shape, lambda b, i: (0,) * a.ndim)
    return pl.pallas_call(
        _route_kernel,
        grid=(nb, nt),
        in_specs=[pl.BlockSpec((1, TM, d), lambda b, i: (b, i, 0)), full(nw),
                  pl.BlockSpec((1, 1, 6, d), lambda b, i: (b, jnp.minimum(i, 1), 0, 0)), full(wq), full(sub)],
        out_specs=[pl.BlockSpec((1, TM, d), lambda b, i: (b, i, 0)),
                   pl.BlockSpec((1, TM, PEER_SLOTS), lambda b, i: (b, i, 0)),
                   pl.BlockSpec((1, TM, PEER_SLOTS), lambda b, i: (b, i, 0))],
        out_shape=[jax.ShapeDtypeStruct(xall.shape, f32), jax.ShapeDtypeStruct((nb, ta, PEER_SLOTS), i32),
                   jax.ShapeDtypeStruct((nb, ta, PEER_SLOTS), f32)],
        compiler_params=pltpu.CompilerParams(dimension_semantics=("arbitrary", "arbitrary"),
                                             vmem_limit_bytes=VMEM_LIMIT_BYTES),
        name="peer_route",
    )(xall, nw, mod, wq, sub)


def _expert_kernel(idx_ref, h_ref, g_ref, x_ref, mod_ref, tab_hbm, o_ref, buf, sem):
    d = h_ref.shape[-1]
    nct = d // LANE
    groups = PEER_SLOTS // 8

    per_piece = PEER_SLOTS // nct

    def issue(t, slot, piece):
        for k in range(piece * per_piece, (piece + 1) * per_piece):
            pltpu.make_async_copy(tab_hbm.at[idx_ref[t, k]], buf.at[slot, k // 8, :, pl.ds(k % 8, 1), :], sem.at[slot]).start()

    def wait(slot):
        pltpu.make_async_copy(buf.at[slot], buf.at[slot], sem.at[slot]).wait()

    def col(slot, c):
        return jnp.concatenate([buf[slot, kg, c] for kg in range(groups)], axis=0).astype(bf16)

    def step(t, w8, do_issue, do_u, do_v):
        nslot = lax.rem(t + 1, PEER_NBUF)
        slot = lax.rem(t, PEER_NBUF)
        islot = lax.rem(t + PEER_AHEAD, PEER_NBUF)
        if do_u:
            wait(nslot)
            h8 = jnp.broadcast_to(h_ref[pl.ds(t + 1, 1), :], (8, d)).astype(bf16)
            gate = g_ref[pl.ds(t + 1, 1), :]
            a = jnp.zeros((8, PEER_SLOTS), f32)
        ys = []
        for j in range(nct):
            for c in ((2 * j, 2 * j + 1) if 2 * j < nct else ()):
                if do_u:
                    a = a + lax.dot_general(h8[:, c * LANE:(c + 1) * LANE], col(nslot, c), _NT, preferred_element_type=f32)
                if do_v:
                    ys.append(jnp.dot(w8, col(slot, nct + c), preferred_element_type=f32)[0:1, :])
            if do_issue:
                issue(t + PEER_AHEAD, islot, j)
        if do_v:
            o_ref[pl.ds(t, 1), :] = jnp.concatenate(ys, axis=1)
        if do_u:
            a1 = a[0:1, :]
            w = gate * (0.5 * a1 * (1.0 + lax.erf(a1 * (2.0 ** -0.5))))
            w8 = jnp.broadcast_to(w, (8, PEER_SLOTS)).astype(bf16)
        return w8

    def prologue(t, carry):
        for j in range(nct):
            issue(t, t, j)
        return carry

    lax.fori_loop(0, PEER_AHEAD, prologue, 0)
    w8 = step(-1, jnp.zeros((8, PEER_SLOTS), bf16), False, True, False)
    w8 = lax.fori_loop(0, PEER_TB - PEER_AHEAD, lambda t, w: step(t, w, True, True, True), w8)
    w8 = lax.fori_loop(PEER_TB - PEER_AHEAD, PEER_TB - 1, lambda t, w: step(t, w, False, True, True), w8)
    step(PEER_TB - 1, w8, False, False, True)
    o_ref[...] = x_ref[...] + mod_ref[0, 0, 5:6, :] * o_ref[...]


def _experts(idx, h, g, xflat, mod, tab, ta, n_ctx):
    n, d = xflat.shape
    bpb = ta // PEER_TB
    cb = n_ctx // PEER_TB
    tokb = lambda width: pl.BlockSpec((PEER_TB, width), lambda i: (i, 0))
    return pl.pallas_call(
        _expert_kernel,
        grid=(n // PEER_TB,),
        in_specs=[pl.BlockSpec((PEER_TB, PEER_SLOTS), lambda i: (i, 0), memory_space=pltpu.SMEM),
                  tokb(d), tokb(PEER_SLOTS), tokb(d),
                  pl.BlockSpec((1, 1, 6, d), lambda i: (i // bpb, jnp.where(i % bpb < cb, 0, 1), 0, 0)),
                  pl.BlockSpec(memory_space=pl.ANY)],
        out_specs=tokb(d),
        out_shape=jax.ShapeDtypeStruct((n, d), f32),
        scratch_shapes=[pltpu.VMEM((PEER_NBUF, PEER_SLOTS // 8, 2 * (d // LANE), 8, LANE), f32),
                        pltpu.SemaphoreType.DMA((PEER_NBUF,))],
        compiler_params=pltpu.CompilerParams(dimension_semantics=("arbitrary",), vmem_limit_bytes=VMEM_LIMIT_BYTES),
        name="peer_experts",
    )(idx, h, g, xflat, mod, tab)


def _expert_table(u, v):
    depth, e, d = u.shape
    return jnp.concatenate([u.reshape(depth, e, d // LANE, 1, LANE), v.reshape(depth, e, d // LANE, 1, LANE)], axis=2)


def _final_norm_kernel(x_ref, w_ref, o_ref):
    x = x_ref[0]
    o_ref[0] = x * lax.rsqrt(jnp.mean(x * x, axis=-1, keepdims=True) + EPS) * w_ref[...]


def _final_norm(xall, w, n_ctx):
    nb, ta, d = xall.shape
    off = n_ctx // TM
    return pl.pallas_call(
        _final_norm_kernel,
        grid=(nb, (ta - n_ctx) // TM),
        in_specs=[pl.BlockSpec((1, TM, d), lambda b, i: (b, i + off, 0)), pl.BlockSpec((1, d), lambda b, i: (0, 0))],
        out_specs=pl.BlockSpec((1, TM, d), lambda b, i: (b, i, 0)),
        out_shape=jax.ShapeDtypeStruct((nb, ta - n_ctx, d), f32),
        compiler_params=pltpu.CompilerParams(dimension_semantics=("arbitrary", "arbitrary")),
        name="final_norm",
    )(xall, w.reshape(1, d))


def _rope_tables(n_ctx, seq):
    nf = HEAD_DIM // 4
    pos = np.arange(seq)
    inv = ROPE_BASE ** (-np.arange(nf, dtype=np.float32) / nf)
    ar = (pos // GRID_W).astype(np.float32)[:, None] * inv
    ac = (pos % GRID_W).astype(np.float32)[:, None] * inv
    cos = np.concatenate([np.cos(ar), np.cos(ar), np.cos(ac), np.cos(ac)], axis=-1)
    sin = np.concatenate([-np.sin(ar), np.sin(ar), -np.sin(ac), np.sin(ac)], axis=-1)
    cos = np.concatenate([np.ones((n_ctx, HEAD_DIM), np.float32), cos.astype(np.float32)], axis=0)
    sin = np.concatenate([np.zeros((n_ctx, HEAD_DIM), np.float32), sin.astype(np.float32)], axis=0)
    return jnp.asarray(np.stack([np.tile(cos, (1, 2)), np.tile(sin, (1, 2))]))


def _tri_tables():
    r = np.arange(TM)[:, None]
    c = np.arange(TM)[None, :]
    same = (r // HG_SUB) == (c // HG_SUB)
    return jnp.asarray(np.stack([same & (c <= r), same & (c >= r)]).astype(np.float32), dtype=bf16)


def _block_diag_ones():
    r = np.arange(HG_WIDTH)
    return jnp.asarray((r[:, None] // HEAD_DIM == r[None, :] // HEAD_DIM).astype(np.float32), dtype=bf16)


def _swa_perm():
    return np.concatenate([np.arange(h * HEAD_DIM, (h + 1) * HEAD_DIM) for h in SWA_HEAD_ORDER])


def kernel(x, c, ctx, c_ctx, w_mod, b_mod, norm_mix_w, norm_ffn_w, w_in, hgrn_lb_logits, hgrn_norm_w, swa_sink, nat_rpb,
           w_out, peer_wq, peer_subkeys, peer_u, peer_v, final_norm_w):
    nb, seq, d = x.shape
    n_ctx = ctx.shape[1]
    depth = w_in.shape[0]
    ta = n_ctx + seq
    assert n_ctx == TM and seq % TM == 0 and seq % (GRID_W * NAT_ROWS) == 0 and d % LANE == 0
    assert w_in.shape[-1] == PROJ_WIDTH and peer_wq.shape[-1] == PEER_HEADS * PEER_DK

    mods = _mods(c, c_ctx, w_mod, b_mod)
    cs = _rope_tables(n_ctx, seq)
    tri = _tri_tables()
    bd = _block_diag_ones()
    perm = _swa_perm()
    q0 = 5 * HG_WIDTH
    col_perm = np.concatenate([np.arange(q0), q0 + perm, np.arange(q0 + SWA_Q, PROJ_WIDTH)])
    w_in_b = w_in[:, :, col_perm].astype(bf16)
    w_out_b = w_out.astype(bf16)
    wa = w_out_b[:, :HG_WIDTH]
    wb = w_out_b[:, HG_WIDTH:HG_WIDTH + SWA_Q][:, perm]
    wn = w_out_b[:, HG_WIDTH + SWA_Q:]
    wq_b = peer_wq.astype(bf16)
    sub_b = peer_subkeys.astype(bf16)
    lb = jnp.cumsum(jax.nn.softmax(hgrn_lb_logits.astype(f32), axis=1), axis=1)
    lb = lb - lb[:, :1]
    gp = jnp.stack([jnp.log(lb[0]), jnp.log1p(-lb[0]), 1.0 - lb[0], jnp.log(lb[1]), jnp.log1p(-lb[1]), 1.0 - lb[1],
                    jnp.zeros_like(lb[0]), jnp.zeros_like(lb[0])], axis=1)
    sink_rows = jnp.zeros((depth, 8, LANE), f32).at[:, :SWA_HEADS].set(
        jnp.broadcast_to(swa_sink.astype(f32)[:, list(SWA_HEAD_ORDER), None], (depth, SWA_HEADS, LANE)))

    tab = _expert_table(peer_u, peer_v)

    xall = jnp.concatenate([ctx, x], axis=1)
    for l in range(depth):
        mod = mods[l]
        hq, cum, kk, hv, sg, sq, sk, sv, nq, nk, nv = _proj(xall, norm_mix_w[l].reshape(1, d), mod, w_in_b[l], cs, gp[l], tri)
        o = _hgrn_scan(hq, cum, kk, hv, bd)
        so = _swa(sq, sk, sv, sink_rows[l], n_ctx)
        no = _nat(nq, nk, nv, _nat_bias_tables(nat_rpb[l]), n_ctx)
        xall = _outproj(o, sg, hgrn_norm_w[l].reshape(1, HG_WIDTH), bd, so, no, wa[l], wb[l], wn[l], mod, xall)
        h2, idx, g = _route(xall, norm_ffn_w[l].reshape(1, d), mod, wq_b[l], sub_b[l])
        xall = _experts(idx.reshape(nb * ta, PEER_SLOTS), h2.reshape(nb * ta, d), g.reshape(nb * ta, PEER_SLOTS),
                        xall.reshape(nb * ta, d), mod, tab[l], ta, n_ctx).reshape(nb, ta, d)
    return _final_norm(xall, final_norm_w, n_ctx)
```

```python
import functools

import numpy as np
import jax
import jax.numpy as jnp
from jax import lax
from jax.experimental import pallas as pl
from jax.experimental.pallas import tpu as pltpu

f32 = jnp.float32
bf16 = jnp.bfloat16
i32 = jnp.int32

LANE = 128
VMEM_LIMIT_BYTES = 56 * 1024 * 1024

GRID_W = 64
HEAD_DIM = 64
EPS = 1e-6
HG_HEADS = 4
HG_WIDTH = 256
HG_SUB = 16
SWA_HEADS = 6
SWA_KV_HEADS = 2
SWA_WINDOW = 128
SWA_BLOCK = 128
ROPE_BASE = 10000.0
NAT_HEADS = 6
NAT_ROWS = 8
NAT_COLS = 16
SWA_Q = SWA_HEADS * HEAD_DIM
SWA_KV = SWA_KV_HEADS * HEAD_DIM
NAT_W = NAT_HEADS * HEAD_DIM
PROJ_WIDTH = 5 * HG_WIDTH + SWA_Q + 2 * SWA_KV + 3 * NAT_W
PEER_HEADS = 8
PEER_NKEYS = 128
PEER_DK = 256
PEER_TOPK = 16
PEER_SLOTS = PEER_HEADS * PEER_TOPK

TM = 256
PEER_TB = 128
PEER_NBUF = 8
PEER_AHEAD = PEER_NBUF - 1
NEG = -1e30

SWA_HEAD_ORDER = (0, 3, 1, 4, 2, 5)

_NT = (((1,), (1,)), ((), ()))
_TN = (((0,), (0,)), ((), ()))


def _silu(x):
    return x * (1.0 / (1.0 + jnp.exp(-x)))


def _split2(x):
    hi = x.astype(bf16)
    lo = (x - hi.astype(f32)).astype(bf16)
    return hi, lo


def _mods_kernel(sc_ref, w_ref, b_ref, o_ref):
    s = _silu(sc_ref[...])
    o_ref[0] = jnp.dot(s.astype(bf16), w_ref[0].astype(bf16), preferred_element_type=f32) + b_ref[0]


def _mods(c, c_ctx, w_mod, b_mod):
    depth, d, d6 = w_mod.shape
    nb = c.shape[0]
    rows = jnp.zeros((8, d), f32).at[:nb].set(c).at[nb].set(c_ctx)
    tn = d6 // 4
    out = pl.pallas_call(
        _mods_kernel,
        grid=(depth, 4),
        in_specs=[pl.BlockSpec((8, d), lambda l, j: (0, 0)),
                  pl.BlockSpec((1, d, tn), lambda l, j: (l, 0, j)),
                  pl.BlockSpec((1, 1, tn), lambda l, j: (l, 0, j))],
        out_specs=pl.BlockSpec((1, 8, tn), lambda l, j: (l, 0, j)),
        out_shape=jax.ShapeDtypeStruct((depth, 8, d6), f32),
        compiler_params=pltpu.CompilerParams(dimension_semantics=("arbitrary", "arbitrary"),
                                             vmem_limit_bytes=VMEM_LIMIT_BYTES),
        name="adaln_mods",
    )(rows, w_mod, b_mod.reshape(depth, 1, d6))
    lat = out[:, :nb].reshape(depth, nb, 6, d)
    ctx = jnp.broadcast_to(out[:, nb].reshape(depth, 1, 6, d), (depth, nb, 6, d))
    return jnp.stack([ctx, lat], axis=2)


def _norm_mod(x, nw, shift, scale):
    y = x * lax.rsqrt(jnp.mean(x * x, axis=-1, keepdims=True) + EPS) * nw
    return y * (1.0 + scale) + shift


def _rope(t, cos, sins):
    lane = lax.broadcasted_iota(i32, t.shape, 1)
    sw = jnp.where((lane % 32) < 16, pltpu.roll(t, LANE - 16, 1), pltpu.roll(t, 16, 1))
    return t * cos + sw * sins


def _hgrn_gates(z, log_lb, log1m_lb, one_m_lb):
    log_sig = jnp.minimum(z, 0.0) - jnp.log1p(jnp.exp(-jnp.abs(z)))
    c = log1m_lb + log_sig
    m = jnp.maximum(log_lb, c)
    log_f = m + jnp.log1p(jnp.exp(-jnp.abs(log_lb - c)))
    k = one_m_lb * (1.0 / (1.0 + jnp.exp(z)))
    return log_f, k


def _cumsum16(tri, log_f):
    h1 = log_f.astype(bf16)
    r1 = log_f - h1.astype(f32)
    h2 = r1.astype(bf16)
    h3 = (r1 - h2.astype(f32)).astype(bf16)
    return (jnp.dot(tri, h1, preferred_element_type=f32) + jnp.dot(tri, h2, preferred_element_type=f32)
            + jnp.dot(tri, h3, preferred_element_type=f32))


def _proj_kernel(x_ref, nw_ref, mod_ref, w_ref, cs_ref, gp_ref, tri_ref,
                 hq_ref, cum_ref, kk_ref, hv_ref, sg_ref, sq_ref, sk_ref, sv_ref, nq_ref, nk_ref, nv_ref):
    h = _norm_mod(x_ref[0], nw_ref[...], mod_ref[0, 0, 0:1, :], mod_ref[0, 0, 1:2, :])
    p = jnp.dot(h.astype(bf16), w_ref[...], preferred_element_type=f32)
    w = HG_WIDTH
    hq_ref[0] = _silu(p[:, 0:w])
    for d in range(2):
        log_f, k = _hgrn_gates(p[:, (1 + d) * w:(2 + d) * w], gp_ref[3 * d:3 * d + 1, :],
                               gp_ref[3 * d + 1:3 * d + 2, :], gp_ref[3 * d + 2:3 * d + 3, :])
        cum_ref[d, 0] = _cumsum16(tri_ref[d], log_f)
        kk_ref[d, 0] = k
    hv_ref[0] = p[:, 3 * w:4 * w]
    sg_ref[0] = _silu(p[:, 4 * w:5 * w])
    cos, sins = cs_ref[0], cs_ref[1]
    o = 5 * w
    scale = HEAD_DIM ** -0.5
    for j in range(SWA_Q // LANE):
        sq_ref[0, :, j * LANE:(j + 1) * LANE] = (_rope(p[:, o + j * LANE:o + (j + 1) * LANE], cos, sins) * scale).astype(bf16)
    o += SWA_Q
    sk_ref[0] = _rope(p[:, o:o + SWA_KV], cos, sins).astype(bf16)
    o += SWA_KV
    sv_ref[0] = p[:, o:o + SWA_KV].astype(bf16)
    o += SWA_KV
    nq_ref[0] = (p[:, o:o + NAT_W] * scale).astype(bf16)
    o += NAT_W
    nk_ref[0] = p[:, o:o + NAT_W].astype(bf16)
    o += NAT_W
    nv_ref[0] = p[:, o:o + NAT_W].astype(bf16)


def _proj(xall, nw, mod, w_in, cs, gp, tri):
    nb, ta, d = xall.shape
    nt = ta // TM
    tok = lambda width, dt: jax.ShapeDtypeStruct((nb, ta, width), dt)
    tok2 = jax.ShapeDtypeStruct((2, nb, ta, HG_WIDTH), f32)
    bs = lambda width: pl.BlockSpec((1, TM, width), lambda b, i: (b, i, 0))
    bs2 = pl.BlockSpec((2, 1, TM, HG_WIDTH), lambda b, i: (0, b, i, 0))
    return pl.pallas_call(
        _proj_kernel,
        grid=(nb, nt),
        in_specs=[pl.BlockSpec((1, TM, d), lambda b, i: (b, i, 0)),
                  pl.BlockSpec((1, d), lambda b, i: (0, 0)),
                  pl.BlockSpec((1, 1, 6, d), lambda b, i: (b, jnp.minimum(i, 1), 0, 0)),
                  pl.BlockSpec((d, PROJ_WIDTH), lambda b, i: (0, 0)),
                  pl.BlockSpec((2, TM, LANE), lambda b, i: (0, i, 0)),
                  pl.BlockSpec((8, HG_WIDTH), lambda b, i: (0, 0)),
                  pl.BlockSpec((2, TM, TM), lambda b, i: (0, 0, 0))],
        out_specs=[bs(HG_WIDTH), bs2, bs2, bs(HG_WIDTH), bs(HG_WIDTH), bs(SWA_Q), bs(SWA_KV), bs(SWA_KV),
                   bs(NAT_W), bs(NAT_W), bs(NAT_W)],
        out_shape=[tok(HG_WIDTH, f32), tok2, tok2, tok(HG_WIDTH, f32), tok(HG_WIDTH, f32), tok(SWA_Q, bf16),
                   tok(SWA_KV, bf16), tok(SWA_KV, bf16), tok(NAT_W, bf16), tok(NAT_W, bf16), tok(NAT_W, bf16)],
        compiler_params=pltpu.CompilerParams(dimension_semantics=("arbitrary", "arbitrary"),
                                             vmem_limit_bytes=VMEM_LIMIT_BYTES),
        name="norm_in_proj",
    )(xall, nw, mod, w_in, cs, gp, tri)


def _hgrn_kernel(q_ref, cum_ref, k_ref, v_ref, bd_ref, o_ref, st_ref):
    d = pl.program_id(1)
    fwd = d == 0

    @pl.when(pl.program_id(2) == 0)
    def _():
        st_ref[...] = jnp.zeros_like(st_ref)

    bd = bd_ref[...]
    bd_mask = bd > 0
    sgn = jnp.where(fwd, 1, -1)
    t_sgn = lax.broadcasted_iota(i32, (HG_SUB, 1), 0) * sgn
    nsteps = TM // HG_SUB

    def step(i, carry):
        sc = jnp.where(fwd, i, nsteps - 1 - i)
        r0 = pl.multiple_of(sc * HG_SUB, HG_SUB)
        q = q_ref[0, pl.ds(r0, HG_SUB), :]
        b = cum_ref[0, 0, pl.ds(r0, HG_SUB), :]
        k = k_ref[0, 0, pl.ds(r0, HG_SUB), :]
        v = v_ref[0, pl.ds(r0, HG_SUB), :]
        b_end = jnp.where(fwd, b[HG_SUB - 1:HG_SUB, :], b[0:1, :])
        st = st_ref[...]
        o = lax.dot_general((q * jnp.exp(b)).astype(bf16), st.astype(bf16), _NT, preferred_element_type=f32)
        rows = []
        for s in range(HG_SUB):
            valid = t_sgn >= s * sgn
            decay = jnp.exp(jnp.minimum(b - b[s:s + 1, :], 0.0))
            rows.append(jnp.where(valid, decay * q * k[s:s + 1, :], 0.0))
        hi, lo = _split2(jnp.concatenate(rows, axis=0))
        pe = jnp.dot(hi, bd, preferred_element_type=f32) + jnp.dot(lo, bd, preferred_element_type=f32)
        for s in range(HG_SUB):
            o = o + pe[s * HG_SUB:(s + 1) * HG_SUB, :] * v[s:s + 1, :]
        o_ref[0, 0, pl.ds(r0, HG_SUB), :] = o
        kd = (k * jnp.exp(b_end - b)).astype(bf16)
        kv_t = lax.dot_general(v.astype(bf16), kd, _TN, preferred_element_type=f32)
        st_ref[...] = st * jnp.exp(b_end) + jnp.where(bd_mask, kv_t, 0.0)
        return carry

    lax.fori_loop(0, nsteps, step, 0, unroll=2)


def _hgrn_scan(hq, cum, kk, hv, bd):
    nb, ta, w = hq.shape
    nt = ta // TM

    def blk(d, j):
        return jnp.where(d == 0, j, jnp.where(j == 0, 0, nt - j))

    return pl.pallas_call(
        _hgrn_kernel,
        grid=(nb, 2, nt),
        in_specs=[pl.BlockSpec((1, TM, w), lambda b, d, j: (b, blk(d, j), 0)),
                  pl.BlockSpec((1, 1, TM, w), lambda b, d, j: (d, b, blk(d, j), 0)),
                  pl.BlockSpec((1, 1, TM, w), lambda b, d, j: (d, b, blk(d, j), 0)),
                  pl.BlockSpec((1, TM, w), lambda b, d, j: (b, blk(d, j), 0)),
                  pl.BlockSpec((w, w), lambda b, d, j: (0, 0))],
        out_specs=pl.BlockSpec((1, 1, TM, w), lambda b, d, j: (d, b, blk(d, j), 0)),
        out_shape=jax.ShapeDtypeStruct((2, nb, ta, w), f32),
        scratch_shapes=[pltpu.VMEM((w, w), f32)],
        compiler_params=pltpu.CompilerParams(dimension_semantics=("arbitrary", "arbitrary", "arbitrary")),
        name="hgrn_scan",
    )(hq, cum, kk, hv, bd)


def _pair_rows(q2):
    lane = lax.broadcasted_iota(i32, q2.shape, 1)
    zero = jnp.zeros_like(q2)
    return jnp.concatenate([jnp.where(lane < HEAD_DIM, q2, zero), jnp.where(lane >= HEAD_DIM, q2, zero)], axis=0)


def _pair_merge(o):
    m = o.shape[0] // 2
    lane = lax.broadcasted_iota(i32, (m, LANE), 1)
    return jnp.where(lane < HEAD_DIM, o[:m], o[m:])


def _joint_attention(s_loc, s_ctx, v_loc, v_ctx, sink=None):
    m = jnp.maximum(jnp.max(s_loc, axis=-1, keepdims=True), jnp.max(s_ctx, axis=-1, keepdims=True))
    if sink is not None:
        m = jnp.maximum(m, sink)
    p_loc = jnp.exp(s_loc - m)
    p_ctx = jnp.exp(s_ctx - m)
    den = jnp.sum(p_loc, axis=-1, keepdims=True) + jnp.sum(p_ctx, axis=-1, keepdims=True)
    if sink is not None:
        den = den + jnp.exp(sink - m)
    o = (jnp.dot(p_loc.astype(bf16), v_loc, preferred_element_type=f32)
         + jnp.dot(p_ctx.astype(bf16), v_ctx, preferred_element_type=f32))
    return o / den


def _swa_kernel(q_ref, kp_ref, kc_ref, kn_ref, vp_ref, vc_ref, vn_ref, kx_ref, vx_ref, sink_ref, o_ref, *, n_ctx_blocks, seq):
    g = pl.program_id(1)
    i = g - n_ctx_blocks
    k_loc = jnp.concatenate([kp_ref[0], kc_ref[0], kn_ref[0]], axis=0)
    v_loc = jnp.concatenate([vp_ref[0], vc_ref[0], vn_ref[0]], axis=0)
    row = lax.broadcasted_iota(i32, (SWA_BLOCK, 3 * SWA_BLOCK), 0)
    col = lax.broadcasted_iota(i32, (SWA_BLOCK, 3 * SWA_BLOCK), 1)
    rel = col - SWA_BLOCK - row
    kpos = (i - 1) * SWA_BLOCK + col
    ok = (jnp.abs(rel) <= SWA_WINDOW) & (kpos >= 0) & (kpos < seq) & (i >= 0)
    pen = jnp.where(ok, 0.0, NEG)
    pen = jnp.concatenate([pen, pen], axis=0)
    for j in range(SWA_Q // LANE):
        q = _pair_rows(q_ref[0, :, j * LANE:(j + 1) * LANE])
        s_loc = lax.dot_general(q, k_loc, _NT, preferred_element_type=f32) + pen
        s_ctx = lax.dot_general(q, kx_ref[0], _NT, preferred_element_type=f32)
        sink = jnp.concatenate([jnp.broadcast_to(sink_ref[2 * j:2 * j + 1, 0:1], (SWA_BLOCK, 1)),
                                jnp.broadcast_to(sink_ref[2 * j + 1:2 * j + 2, 0:1], (SWA_BLOCK, 1))], axis=0)
        o = _joint_attention(s_loc, s_ctx, v_loc, vx_ref[0], sink)
        o_ref[0, :, j * LANE:(j + 1) * LANE] = _pair_merge(o).astype(bf16)


def _swa(sq, sk, sv, sink_rows, n_ctx):
    nb, ta, _ = sq.shape
    ncb = n_ctx // SWA_BLOCK
    nblk = ta // SWA_BLOCK
    nlat = nblk - ncb
    prev = lambda b, g: (b, jnp.clip(g - ncb - 1, 0, nlat - 1) + ncb, 0)
    cur = lambda b, g: (b, g, 0)
    nxt = lambda b, g: (b, jnp.clip(g - ncb + 1, 0, nlat - 1) + ncb, 0)
    kvs = lambda f: pl.BlockSpec((1, SWA_BLOCK, SWA_KV), f)
    ctx = pl.BlockSpec((1, n_ctx, SWA_KV), lambda b, g: (b, 0, 0))
    return pl.pallas_call(
        functools.partial(_swa_kernel, n_ctx_blocks=ncb, seq=ta - n_ctx),
        grid=(nb, nblk),
        in_specs=[pl.BlockSpec((1, SWA_BLOCK, SWA_Q), cur), kvs(prev), kvs(cur), kvs(nxt), kvs(prev), kvs(cur), kvs(nxt),
                  ctx, ctx, pl.BlockSpec((8, LANE), lambda b, g: (0, 0))],
        out_specs=pl.BlockSpec((1, SWA_BLOCK, SWA_Q), cur),
        out_shape=jax.ShapeDtypeStruct((nb, ta, SWA_Q), bf16),
        compiler_params=pltpu.CompilerParams(dimension_semantics=("arbitrary", "arbitrary")),
        name="swa_attention",
    )(sq, sk, sk, sk, sv, sv, sv, sk, sv, sink_rows)


def _nat_kernel(q_ref, k_ref, v_ref, bias_ref, o_ref, *, n_ctx, n_rows):
    g = pl.program_id(1)
    is_ctx = g == 0
    ctx_pen = jnp.where(is_ctx, NEG, 0.0)
    rows_per_tile = TM // GRID_W
    nwin = NAT_ROWS * GRID_W
    for rr in range(rows_per_tile):
        r = jnp.maximum((g - 1) * rows_per_tile + rr, 0)
        rs = jnp.clip(r - NAT_ROWS // 2, 0, n_rows - NAT_ROWS)
        dr0 = rs - r + NAT_ROWS - 1
        start = pl.multiple_of(n_ctx + rs * GRID_W, GRID_W)
        for p in range(NAT_W // LANE):
            ls = slice(p * LANE, (p + 1) * LANE)
            q = _pair_rows(q_ref[0, rr * GRID_W:(rr + 1) * GRID_W, ls])
            s_loc = (lax.dot_general(q, k_ref[0, pl.ds(start, nwin), ls], _NT, preferred_element_type=f32)
                     + bias_ref[p, dr0] + ctx_pen)
            s_ctx = lax.dot_general(q, k_ref[0, 0:n_ctx, ls], _NT, preferred_element_type=f32)
            o = _joint_attention(s_loc, s_ctx, v_ref[0, pl.ds(start, nwin), ls], v_ref[0, 0:n_ctx, ls])
            o_ref[0, rr * GRID_W:(rr + 1) * GRID_W, ls] = _pair_merge(o).astype(bf16)


def _nat(nq, nk, nv, bias, n_ctx):
    nb, ta, w = nq.shape
    nt = ta // TM
    whole = pl.BlockSpec((1, ta, w), lambda b, g: (b, 0, 0), pipeline_mode=pl.Buffered(1))
    return pl.pallas_call(
        functools.partial(_nat_kernel, n_ctx=n_ctx, n_rows=(ta - n_ctx) // GRID_W),
        grid=(nb, nt),
        in_specs=[pl.BlockSpec((1, TM, w), lambda b, g: (b, g, 0)), whole, whole,
                  pl.BlockSpec(bias.shape, lambda b, g: (0, 0, 0, 0), pipeline_mode=pl.Buffered(1))],
        out_specs=pl.BlockSpec((1, TM, w), lambda b, g: (b, g, 0)),
        out_shape=jax.ShapeDtypeStruct((nb, ta, w), bf16),
        compiler_params=pltpu.CompilerParams(dimension_semantics=("arbitrary", "arbitrary"),
                                             vmem_limit_bytes=VMEM_LIMIT_BYTES),
        name="nat_attention",
    )(nq, nk, nv, bias)


def _nat_bias_tables(rpb):
    c = np.arange(GRID_W)[:, None]
    kc = np.arange(GRID_W)[None, :]
    ws = np.clip(c - NAT_COLS // 2, 0, GRID_W - NAT_COLS)
    valid = (kc >= ws) & (kc < ws + NAT_COLS)
    col_idx = np.clip(kc - c + NAT_COLS - 1, 0, 2 * NAT_COLS - 2)
    dr = np.arange(NAT_ROWS)[:, None] + np.arange(NAT_ROWS)[None, :]
    t = rpb.astype(f32)[:, dr][:, :, :, col_idx]
    t = jnp.where(jnp.asarray(valid)[None, None, None], t, NEG)
    t = t.transpose(0, 1, 3, 2, 4).reshape(NAT_HEADS // 2, 2, NAT_ROWS, GRID_W, NAT_ROWS * GRID_W)
    return t.transpose(0, 2, 1, 3, 4).reshape(NAT_HEADS // 2, NAT_ROWS, 2 * GRID_W, NAT_ROWS * GRID_W)


def _outproj_kernel(o_ref, sg_ref, hnw_ref, bd_ref, so_ref, no_ref, wa_ref, wb_ref, wn_ref, mod_ref, x_ref, y_ref):
    oo = o_ref[0, 0] + o_ref[1, 0]
    hi, lo = _split2(oo * oo)
    bd = bd_ref[...]
    msq = (jnp.dot(hi, bd, preferred_element_type=f32) + jnp.dot(lo, bd, preferred_element_type=f32)) * (1.0 / HEAD_DIM)
    a = oo * lax.rsqrt(msq + EPS) * hnw_ref[...] * sg_ref[0]
    y = (jnp.dot(a.astype(bf16), wa_ref[...], preferred_element_type=f32)
         + jnp.dot(so_ref[0], wb_ref[...], preferred_element_type=f32)
         + jnp.dot(no_ref[0], wn_ref[...], preferred_element_type=f32))
    y_ref[0] = x_ref[0] + mod_ref[0, 0, 2:3, :] * y


def _outproj(o, sg, hnw, bd, so, no, wa, wb, wn, mod, xall):
    nb, ta, d = xall.shape
    nt = ta // TM
    bs = lambda width: pl.BlockSpec((1, TM, width), lambda b, i: (b, i, 0))
    full = lambda a: pl.BlockSpec(a.shape, lambda b, i: (0,) * a.ndim)
    return pl.pallas_call(
        _outproj_kernel,
        grid=(nb, nt),
        in_specs=[pl.BlockSpec((2, 1, TM, HG_WIDTH), lambda b, i: (0, b, i, 0)), bs(HG_WIDTH), full(hnw), full(bd),
                  bs(SWA_Q), bs(NAT_W), full(wa), full(wb), full(wn),
                  pl.BlockSpec((1, 1, 6, d), lambda b, i: (b, jnp.minimum(i, 1), 0, 0)), bs(d)],
        out_specs=bs(d),
        out_shape=jax.ShapeDtypeStruct(xall.shape, f32),
        compiler_params=pltpu.CompilerParams(dimension_semantics=("arbitrary", "arbitrary"),
                                             vmem_limit_bytes=VMEM_LIMIT_BYTES),
        name="out_proj",
    )(o, sg, hnw, bd, so, no, wa, wb, wn, mod, xall)


def _top16_rows(vals, payload=None):
    n = vals.shape[0]
    iota = lax.broadcasted_iota(i32, vals.shape, 0)
    best, picked = [], []
    for _ in range(PEER_TOPK):
        m = jnp.max(vals, axis=0, keepdims=True)
        idx = jnp.min(jnp.where(vals == m, iota, n), axis=0, keepdims=True)
        hit = iota == idx
        best.append(m)
        picked.append(idx if payload is None else jnp.max(jnp.where(hit, payload, -1), axis=0, keepdims=True))
        vals = jnp.where(hit, -jnp.inf, vals)
    return jnp.concatenate(best, axis=0), jnp.concatenate(picked, axis=0)


def _product_candidates(first, second, combine):
    half = PEER_TOPK // 2
    rows = [combine(first[0:1, :], second)]
    rows += [combine(first[a:a + 1, :], second[0:half, :]) for a in range(1, half)]
    rows.append(combine(first[half:, :], second[0:1, :]))
    return jnp.concatenate(rows, axis=0)


def _route_kernel(x_ref, nw_ref, mod_ref, wq_ref, sub_ref, h_ref, idx_ref, g_ref):
    h = _norm_mod(x_ref[0], nw_ref[...], mod_ref[0, 0, 3:4, :], mod_ref[0, 0, 4:5, :])
    h_ref[0] = h
    q = jnp.dot(h.astype(bf16), wq_ref[...], preferred_element_type=f32).astype(bf16)
    half = PEER_DK // 2
    ids, gates = [], []
    for hh in range(PEER_HEADS):
        top_s, top_i = [], []
        for p in range(2):
            c0 = (hh * 2 + p) * half
            s = lax.dot_general(sub_ref[hh, p], q[:, c0:c0 + half], _NT, preferred_element_type=f32)
            ts, ti = _top16_rows(s)
            top_s.append(ts)
            top_i.append(ti)
        cand_s = _product_candidates(top_s[0], top_s[1], lambda a, b: a + b)
        cand_i = _product_candidates(top_i[0], top_i[1], lambda a, b: a * PEER_NKEYS + b)
        best, eid = _top16_rows(cand_s, cand_i)
        ids.append(eid)
        e = jnp.exp(best - best[0:1, :])
        gates.append(e / jnp.sum(e, axis=0, keepdims=True))
    idx_ref[0] = jnp.concatenate(ids, axis=0).T
    g_ref[0] = jnp.concatenate(gates, axis=0).T


def _route(xall, nw, mod, wq, sub):
    nb, ta, d = xall.shape
    nt = ta // TM
    full = lambda a: pl.BlockSpec(a.shape, lambda b, i: (0,) * a.ndim)
    return pl.pallas_call(
        _route_kernel,
        grid=(nb, nt),
        in_specs=[pl.BlockSpec((1, TM, d), lambda b, i: (b, i, 0)), full(nw),
                  pl.BlockSpec((1, 1, 6, d), lambda b, i: (b, jnp.minimum(i, 1), 0, 0)), full(wq), full(sub)],
        out_specs=[pl.BlockSpec((1, TM, d), lambda b, i: (b, i, 0)),
                   pl.BlockSpec((1, TM, PEER_SLOTS), lambda b, i: (b, i, 0)),
                   pl.BlockSpec((1, TM, PEER_SLOTS), lambda b, i: (b, i, 0))],
        out_shape=[jax.ShapeDtypeStruct(xall.shape, f32), jax.ShapeDtypeStruct((nb, ta, PEER_SLOTS), i32),
                   jax.ShapeDtypeStruct((nb, ta, PEER_SLOTS), f32)],
        compiler_params=pltpu.CompilerParams(dimension_semantics=("arbitrary", "arbitrary"),
                                             vmem_limit_bytes=VMEM_LIMIT_BYTES),
        name="peer_route",
    )(xall, nw, mod, wq, sub)


def _expert_kernel(idx_ref, h_ref, g_ref, x_ref, mod_ref, tab_hbm, o_ref, buf, sem):
    d = h_ref.shape[-1]
    nct = d // LANE
    groups = PEER_SLOTS // 8

    per_piece = PEER_SLOTS // nct

    def issue(t, slot, piece):
        for k in range(piece * per_piece, (piece + 1) * per_piece):
            pltpu.make_async_copy(tab_hbm.at[idx_ref[t, k]], buf.at[slot, k // 8, :, pl.ds(k % 8, 1), :],
                                  sem.at[slot]).start(priority=k % 2)

    def wait(slot):
        pltpu.make_async_copy(buf.at[slot], buf.at[slot], sem.at[slot]).wait()

    def cols(slot, c):
        return jnp.concatenate([jnp.concatenate([buf[slot, kg, c], buf[slot, kg, c + 1]], axis=1)
                                for kg in range(groups)], axis=0).astype(bf16)

    def step(t, w8, do_issue, do_u, do_v):
        nslot = lax.rem(t + 1, PEER_NBUF)
        slot = lax.rem(t, PEER_NBUF)
        islot = lax.rem(t + PEER_AHEAD, PEER_NBUF)
        if do_u:
            wait(nslot)
            h8 = jnp.broadcast_to(h_ref[pl.ds(t + 1, 1), :], (8, d)).astype(bf16)
            gate = g_ref[pl.ds(t + 1, 1), :]
            a = jnp.zeros((8, PEER_SLOTS), f32)
        ys = []
        for j in range(nct):
            if 2 * j < nct:
                c = 2 * j
                if do_u:
                    a = a + lax.dot_general(h8[:, c * LANE:(c + 2) * LANE], cols(nslot, c), _NT, preferred_element_type=f32)
                if do_v:
                    ys.append(jnp.dot(w8, cols(slot, nct + c), preferred_element_type=f32)[0:1, :])
            if do_issue:
                issue(t + PEER_AHEAD, islot, j)
        if do_v:
            o_ref[pl.ds(t, 1), :] = jnp.concatenate(ys, axis=1)
        if do_u:
            a1 = a[0:1, :]
            w = gate * (0.5 * a1 * (1.0 + lax.erf(a1 * (2.0 ** -0.5))))
            w8 = jnp.broadcast_to(w, (8, PEER_SLOTS)).astype(bf16)
        return w8

    def prologue(t, carry):
        for j in range(nct):
            issue(t, t, j)
        return carry

    lax.fori_loop(0, PEER_AHEAD, prologue, 0)
    w8 = step(-1, jnp.zeros((8, PEER_SLOTS), bf16), False, True, False)
    w8 = lax.fori_loop(0, PEER_TB - PEER_AHEAD, lambda t, w: step(t, w, True, True, True), w8)
    w8 = lax.fori_loop(PEER_TB - PEER_AHEAD, PEER_TB - 1, lambda t, w: step(t, w, False, True, True), w8)
    step(PEER_TB - 1, w8, False, False, True)
    o_ref[...] = x_ref[...] + mod_ref[0, 0, 5:6, :] * o_ref[...]


def _experts(idx, h, g, xflat, mod, tab, ta, n_ctx):
    n, d = xflat.shape
    bpb = ta // PEER_TB
    cb = n_ctx // PEER_TB
    tokb = lambda width: pl.BlockSpec((PEER_TB, width), lambda i: (i, 0))
    return pl.pallas_call(
        _expert_kernel,
        grid=(n // PEER_TB,),
        in_specs=[pl.BlockSpec((PEER_TB, PEER_SLOTS), lambda i: (i, 0), memory_space=pltpu.SMEM),
                  tokb(d), tokb(PEER_SLOTS), tokb(d),
                  pl.BlockSpec((1, 1, 6, d), lambda i: (i // bpb, jnp.where(i % bpb < cb, 0, 1), 0, 0)),
                  pl.BlockSpec(memory_space=pl.ANY)],
        out_specs=tokb(d),
        out_shape=jax.ShapeDtypeStruct((n, d), f32),
        scratch_shapes=[pltpu.VMEM((PEER_NBUF, PEER_SLOTS // 8, 2 * (d // LANE), 8, LANE), f32),
                        pltpu.SemaphoreType.DMA((PEER_NBUF,))],
        compiler_params=pltpu.CompilerParams(dimension_semantics=("arbitrary",), vmem_limit_bytes=VMEM_LIMIT_BYTES),
        name="peer_experts",
    )(idx, h, g, xflat, mod, tab)


def _expert_table(u, v):
    depth, e, d = u.shape
    return jnp.concatenate([u.reshape(depth, e, d // LANE, 1, LANE), v.reshape(depth, e, d // LANE, 1, LANE)], axis=2)


def _final_norm_kernel(x_ref, w_ref, o_ref):
    x = x_ref[0]
    o_ref[0] = x * lax.rsqrt(jnp.mean(x * x, axis=-1, keepdims=True) + EPS) * w_ref[...]


def _final_norm(xall, w, n_ctx):
    nb, ta, d = xall.shape
    off = n_ctx // TM
    return pl.pallas_call(
        _final_norm_kernel,
        grid=(nb, (ta - n_ctx) // TM),
        in_specs=[pl.BlockSpec((1, TM, d), lambda b, i: (b, i + off, 0)), pl.BlockSpec((1, d), lambda b, i: (0, 0))],
        out_specs=pl.BlockSpec((1, TM, d), lambda b, i: (b, i, 0)),
        out_shape=jax.ShapeDtypeStruct((nb, ta - n_ctx, d), f32),
        compiler_params=pltpu.CompilerParams(dimension_semantics=("arbitrary", "arbitrary")),
        name="final_norm",
    )(xall, w.reshape(1, d))


def _rope_tables(n_ctx, seq):
    nf = HEAD_DIM // 4
    pos = np.arange(seq)
    inv = ROPE_BASE ** (-np.arange(nf, dtype=np.float32) / nf)
    ar = (pos // GRID_W).astype(np.float32)[:, None] * inv
    ac = (pos % GRID_W).astype(np.float32)[:, None] * inv
    cos = np.concatenate([np.cos(ar), np.cos(ar), np.cos(ac), np.cos(ac)], axis=-1)
    sin = np.concatenate([-np.sin(ar), np.sin(ar), -np.sin(ac), np.sin(ac)], axis=-1)
    cos = np.concatenate([np.ones((n_ctx, HEAD_DIM), np.float32), cos.astype(np.float32)], axis=0)
    sin = np.concatenate([np.zeros((n_ctx, HEAD_DIM), np.float32), sin.astype(np.float32)], axis=0)
    return jnp.asarray(np.stack([np.tile(cos, (1, 2)), np.tile(sin, (1, 2))]))


def _tri_tables():
    r = np.arange(TM)[:, None]
    c = np.arange(TM)[None, :]
    same = (r // HG_SUB) == (c // HG_SUB)
    return jnp.asarray(np.stack([same & (c <= r), same & (c >= r)]).astype(np.float32), dtype=bf16)


def _block_diag_ones():
    r = np.arange(HG_WIDTH)
    return jnp.asarray((r[:, None] // HEAD_DIM == r[None, :] // HEAD_DIM).astype(np.float32), dtype=bf16)


def _swa_perm():
    return np.concatenate([np.arange(h * HEAD_DIM, (h + 1) * HEAD_DIM) for h in SWA_HEAD_ORDER])


def kernel(x, c, ctx, c_ctx, w_mod, b_mod, norm_mix_w, norm_ffn_w, w_in, hgrn_lb_logits, hgrn_norm_w, swa_sink, nat_rpb,
           w_out, peer_wq, peer_subkeys, peer_u, peer_v, final_norm_w):
    nb, seq, d = x.shape
    n_ctx = ctx.shape[1]
    depth = w_in.shape[0]
    ta = n_ctx + seq
    assert n_ctx == TM and seq % TM == 0 and seq % (GRID_W * NAT_ROWS) == 0 and d % LANE == 0
    assert w_in.shape[-1] == PROJ_WIDTH and peer_wq.shape[-1] == PEER_HEADS * PEER_DK

    mods = _mods(c, c_ctx, w_mod, b_mod)
    cs = _rope_tables(n_ctx, seq)
    tri = _tri_tables()
    bd = _block_diag_ones()
    perm = _swa_perm()
    q0 = 5 * HG_WIDTH
    col_perm = np.concatenate([np.arange(q0), q0 + perm, np.arange(q0 + SWA_Q, PROJ_WIDTH)])
    w_in_b = w_in[:, :, col_perm].astype(bf16)
    w_out_b = w_out.astype(bf16)
    wa = w_out_b[:, :HG_WIDTH]
    wb = w_out_b[:, HG_WIDTH:HG_WIDTH + SWA_Q][:, perm]
    wn = w_out_b[:, HG_WIDTH + SWA_Q:]
    wq_b = peer_wq.astype(bf16)
    sub_b = peer_subkeys.astype(bf16)
    lb = jnp.cumsum(jax.nn.softmax(hgrn_lb_logits.astype(f32), axis=1), axis=1)
    lb = lb - lb[:, :1]
    gp = jnp.stack([jnp.log(lb[0]), jnp.log1p(-lb[0]), 1.0 - lb[0], jnp.log(lb[1]), jnp.log1p(-lb[1]), 1.0 - lb[1],
                    jnp.zeros_like(lb[0]), jnp.zeros_like(lb[0])], axis=1)
    sink_rows = jnp.zeros((depth, 8, LANE), f32).at[:, :SWA_HEADS].set(
        jnp.broadcast_to(swa_sink.astype(f32)[:, list(SWA_HEAD_ORDER), None], (depth, SWA_HEADS, LANE)))

    tab = _expert_table(peer_u, peer_v)

    xall = jnp.concatenate([ctx, x], axis=1)
    for l in range(depth):
        mod = mods[l]
        hq, cum, kk, hv, sg, sq, sk, sv, nq, nk, nv = _proj(xall, norm_mix_w[l].reshape(1, d), mod, w_in_b[l], cs, gp[l], tri)
        o = _hgrn_scan(hq, cum, kk, hv, bd)
        so = _swa(sq, sk, sv, sink_rows[l], n_ctx)
        no = _nat(nq, nk, nv, _nat_bias_tables(nat_rpb[l]), n_ctx)
        xall = _outproj(o, sg, hgrn_norm_w[l].reshape(1, HG_WIDTH), bd, so, no, wa[l], wb[l], wn[l], mod, xall)
        h2, idx, g = _route(xall, norm_ffn_w[l].reshape(1, d), mod, wq_b[l], sub_b[l])
        xall = _experts(idx.reshape(nb * ta, PEER_SLOTS), h2.reshape(nb * ta, d), g.reshape(nb * ta, PEER_SLOTS),
                        xall.reshape(nb * ta, d), mod, tab[l], ta, n_ctx).reshape(nb, ta, d)
    return _final_norm(xall, final_norm_w, n_ctx)
```

```python
import functools

import numpy as np
import jax
import jax.numpy as jnp
from jax import lax
from jax.experimental import pallas as pl
from jax.experimental.pallas import tpu as pltpu

f32 = jnp.float32
bf16 = jnp.bfloat16
i32 = jnp.int32

LANE = 128
VMEM_LIMIT_BYTES = 56 * 1024 * 1024

GRID_W = 64
HEAD_DIM = 64
EPS = 1e-6
HG_HEADS = 4
HG_WIDTH = 256
HG_SUB = 16
SWA_HEADS = 6
SWA_KV_HEADS = 2
SWA_WINDOW = 128
SWA_BLOCK = 128
ROPE_BASE = 10000.0
NAT_HEADS = 6
NAT_ROWS = 8
NAT_COLS = 16
SWA_Q = SWA_HEADS * HEAD_DIM
SWA_KV = SWA_KV_HEADS * HEAD_DIM
NAT_W = NAT_HEADS * HEAD_DIM
PROJ_WIDTH = 5 * HG_WIDTH + SWA_Q + 2 * SWA_KV + 3 * NAT_W
PEER_HEADS = 8
PEER_NKEYS = 128
PEER_DK = 256
PEER_TOPK = 16
PEER_SLOTS = PEER_HEADS * PEER_TOPK

TM = 256
PEER_TB = 128
PEER_NBUF = 8
PEER_AHEAD = PEER_NBUF - 1
NEG = -1e30

SWA_HEAD_ORDER = (0, 3, 1, 4, 2, 5)

_NT = (((1,), (1,)), ((), ()))
_TN = (((0,), (0,)), ((), ()))


def _silu(x):
    return x * (1.0 / (1.0 + jnp.exp(-x)))


def _split2(x):
    hi = x.astype(bf16)
    lo = (x - hi.astype(f32)).astype(bf16)
    return hi, lo


def _mods_kernel(sc_ref, w_ref, b_ref, o_ref):
    s = _silu(sc_ref[...])
    o_ref[0] = jnp.dot(s.astype(bf16), w_ref[0].astype(bf16), preferred_element_type=f32) + b_ref[0]


def _mods(c, c_ctx, w_mod, b_mod):
    depth, d, d6 = w_mod.shape
    nb = c.shape[0]
    rows = jnp.zeros((8, d), f32).at[:nb].set(c).at[nb].set(c_ctx)
    tn = d6 // 4
    out = pl.pallas_call(
        _mods_kernel,
        grid=(depth, 4),
        in_specs=[pl.BlockSpec((8, d), lambda l, j: (0, 0)),
                  pl.BlockSpec((1, d, tn), lambda l, j: (l, 0, j)),
                  pl.BlockSpec((1, 1, tn), lambda l, j: (l, 0, j))],
        out_specs=pl.BlockSpec((1, 8, tn), lambda l, j: (l, 0, j)),
        out_shape=jax.ShapeDtypeStruct((depth, 8, d6), f32),
        compiler_params=pltpu.CompilerParams(dimension_semantics=("arbitrary", "arbitrary"),
                                             vmem_limit_bytes=VMEM_LIMIT_BYTES),
        name="adaln_mods",
    )(rows, w_mod, b_mod.reshape(depth, 1, d6))
    lat = out[:, :nb].reshape(depth, nb, 6, d)
    ctx = jnp.broadcast_to(out[:, nb].reshape(depth, 1, 6, d), (depth, nb, 6, d))
    return jnp.stack([ctx, lat], axis=2)


def _norm_mod(x, nw, shift, scale):
    y = x * lax.rsqrt(jnp.mean(x * x, axis=-1, keepdims=True) + EPS) * nw
    return y * (1.0 + scale) + shift


def _rope(t, cos, sins):
    lane = lax.broadcasted_iota(i32, t.shape, 1)
    sw = jnp.where((lane % 32) < 16, pltpu.roll(t, LANE - 16, 1), pltpu.roll(t, 16, 1))
    return t * cos + sw * sins


def _hgrn_gates(z, log_lb, log1m_lb, one_m_lb):
    log_sig = jnp.minimum(z, 0.0) - jnp.log1p(jnp.exp(-jnp.abs(z)))
    c = log1m_lb + log_sig
    m = jnp.maximum(log_lb, c)
    log_f = m + jnp.log1p(jnp.exp(-jnp.abs(log_lb - c)))
    k = one_m_lb * (1.0 / (1.0 + jnp.exp(z)))
    return log_f, k


def _cumsum16(tri, log_f):
    h1 = log_f.astype(bf16)
    r1 = log_f - h1.astype(f32)
    h2 = r1.astype(bf16)
    h3 = (r1 - h2.astype(f32)).astype(bf16)
    return (jnp.dot(tri, h1, preferred_element_type=f32) + jnp.dot(tri, h2, preferred_element_type=f32)
            + jnp.dot(tri, h3, preferred_element_type=f32))


def _proj_kernel(x_ref, nw_ref, mod_ref, w_ref, cs_ref, gp_ref, tri_ref,
                 hq_ref, cum_ref, kk_ref, hv_ref, sg_ref, sq_ref, sk_ref, sv_ref, nq_ref, nk_ref, nv_ref):
    h = _norm_mod(x_ref[0], nw_ref[...], mod_ref[0, 0, 0:1, :], mod_ref[0, 0, 1:2, :])
    p = jnp.dot(h.astype(bf16), w_ref[...], preferred_element_type=f32)
    w = HG_WIDTH
    hq_ref[0] = _silu(p[:, 0:w])
    for d in range(2):
        log_f, k = _hgrn_gates(p[:, (1 + d) * w:(2 + d) * w], gp_ref[3 * d:3 * d + 1, :],
                               gp_ref[3 * d + 1:3 * d + 2, :], gp_ref[3 * d + 2:3 * d + 3, :])
        cum_ref[d, 0] = _cumsum16(tri_ref[d], log_f)
        kk_ref[d, 0] = k
    hv_ref[0] = p[:, 3 * w:4 * w]
    sg_ref[0] = _silu(p[:, 4 * w:5 * w])
    cos, sins = cs_ref[0], cs_ref[1]
    o = 5 * w
    scale = HEAD_DIM ** -0.5
    for j in range(SWA_Q // LANE):
        sq_ref[0, :, j * LANE:(j + 1) * LANE] = (_rope(p[:, o + j * LANE:o + (j + 1) * LANE], cos, sins) * scale).astype(bf16)
    o += SWA_Q
    sk_ref[0] = _rope(p[:, o:o + SWA_KV], cos, sins).astype(bf16)
    o += SWA_KV
    sv_ref[0] = p[:, o:o + SWA_KV].astype(bf16)
    o += SWA_KV
    nq_ref[0] = (p[:, o:o + NAT_W] * scale).astype(bf16)
    o += NAT_W
    nk_ref[0] = p[:, o:o + NAT_W].astype(bf16)
    o += NAT_W
    nv_ref[0] = p[:, o:o + NAT_W].astype(bf16)


def _proj(xall, nw, mod, w_in, cs, gp, tri):
    nb, ta, d = xall.shape
    nt = ta // TM
    tok = lambda width, dt: jax.ShapeDtypeStruct((nb, ta, width), dt)
    tok2 = jax.ShapeDtypeStruct((2, nb, ta, HG_WIDTH), f32)
    bs = lambda width: pl.BlockSpec((1, TM, width), lambda b, i: (b, i, 0))
    bs2 = pl.BlockSpec((2, 1, TM, HG_WIDTH), lambda b, i: (0, b, i, 0))
    return pl.pallas_call(
        _proj_kernel,
        grid=(nb, nt),
        in_specs=[pl.BlockSpec((1, TM, d), lambda b, i: (b, i, 0)),
                  pl.BlockSpec((1, d), lambda b, i: (0, 0)),
                  pl.BlockSpec((1, 1, 6, d), lambda b, i: (b, jnp.minimum(i, 1), 0, 0)),
                  pl.BlockSpec((d, PROJ_WIDTH), lambda b, i: (0, 0)),
                  pl.BlockSpec((2, TM, LANE), lambda b, i: (0, i, 0)),
                  pl.BlockSpec((8, HG_WIDTH), lambda b, i: (0, 0)),
                  pl.BlockSpec((2, TM, TM), lambda b, i: (0, 0, 0))],
        out_specs=[bs(HG_WIDTH), bs2, bs2, bs(HG_WIDTH), bs(HG_WIDTH), bs(SWA_Q), bs(SWA_KV), bs(SWA_KV),
                   bs(NAT_W), bs(NAT_W), bs(NAT_W)],
        out_shape=[tok(HG_WIDTH, f32), tok2, tok2, tok(HG_WIDTH, f32), tok(HG_WIDTH, f32), tok(SWA_Q, bf16),
                   tok(SWA_KV, bf16), tok(SWA_KV, bf16), tok(NAT_W, bf16), tok(NAT_W, bf16), tok(NAT_W, bf16)],
        compiler_params=pltpu.CompilerParams(dimension_semantics=("arbitrary", "arbitrary"),
                                             vmem_limit_bytes=VMEM_LIMIT_BYTES),
        name="norm_in_proj",
    )(xall, nw, mod, w_in, cs, gp, tri)


def _hgrn_kernel(q_ref, cum_ref, k_ref, v_ref, bd_ref, o_ref, st_ref):
    d = pl.program_id(1)
    fwd = d == 0

    @pl.when(pl.program_id(2) == 0)
    def _():
        st_ref[...] = jnp.zeros_like(st_ref)

    bd = bd_ref[...]
    bd_mask = bd > 0
    sgn = jnp.where(fwd, 1, -1)
    t_sgn = lax.broadcasted_iota(i32, (HG_SUB, 1), 0) * sgn
    nsteps = TM // HG_SUB

    def step(i, carry):
        sc = jnp.where(fwd, i, nsteps - 1 - i)
        r0 = pl.multiple_of(sc * HG_SUB, HG_SUB)
        q = q_ref[0, pl.ds(r0, HG_SUB), :]
        b = cum_ref[0, 0, pl.ds(r0, HG_SUB), :]
        k = k_ref[0, 0, pl.ds(r0, HG_SUB), :]
        v = v_ref[0, pl.ds(r0, HG_SUB), :]
        b_end = jnp.where(fwd, b[HG_SUB - 1:HG_SUB, :], b[0:1, :])
        st = st_ref[...]
        o = lax.dot_general((q * jnp.exp(b)).astype(bf16), st.astype(bf16), _NT, preferred_element_type=f32)
        rows = []
        for s in range(HG_SUB):
            valid = t_sgn >= s * sgn
            decay = jnp.exp(jnp.minimum(b - b[s:s + 1, :], 0.0))
            rows.append(jnp.where(valid, decay * q * k[s:s + 1, :], 0.0))
        hi, lo = _split2(jnp.concatenate(rows, axis=0))
        pe = jnp.dot(hi, bd, preferred_element_type=f32) + jnp.dot(lo, bd, preferred_element_type=f32)
        for s in range(HG_SUB):
            o = o + pe[s * HG_SUB:(s + 1) * HG_SUB, :] * v[s:s + 1, :]
        o_ref[0, 0, pl.ds(r0, HG_SUB), :] = o
        kd = (k * jnp.exp(b_end - b)).astype(bf16)
        kv_t = lax.dot_general(v.astype(bf16), kd, _TN, preferred_element_type=f32)
        st_ref[...] = st * jnp.exp(b_end) + jnp.where(bd_mask, kv_t, 0.0)
        return carry

    lax.fori_loop(0, nsteps, step, 0, unroll=2)


def _hgrn_scan(hq, cum, kk, hv, bd):
    nb, ta, w = hq.shape
    nt = ta // TM

    def blk(d, j):
        return jnp.where(d == 0, j, jnp.where(j == 0, 0, nt - j))

    return pl.pallas_call(
        _hgrn_kernel,
        grid=(nb, 2, nt),
        in_specs=[pl.BlockSpec((1, TM, w), lambda b, d, j: (b, blk(d, j), 0)),
                  pl.BlockSpec((1, 1, TM, w), lambda b, d, j: (d, b, blk(d, j), 0)),
                  pl.BlockSpec((1, 1, TM, w), lambda b, d, j: (d, b, blk(d, j), 0)),
                  pl.BlockSpec((1, TM, w), lambda b, d, j: (b, blk(d, j), 0)),
                  pl.BlockSpec((w, w), lambda b, d, j: (0, 0))],
        out_specs=pl.BlockSpec((1, 1, TM, w), lambda b, d, j: (d, b, blk(d, j), 0)),
        out_shape=jax.ShapeDtypeStruct((2, nb, ta, w), f32),
        scratch_shapes=[pltpu.VMEM((w, w), f32)],
        compiler_params=pltpu.CompilerParams(dimension_semantics=("arbitrary", "arbitrary", "arbitrary")),
        name="hgrn_scan",
    )(hq, cum, kk, hv, bd)


def _pair_rows(q2):
    lane = lax.broadcasted_iota(i32, q2.shape, 1)
    zero = jnp.zeros_like(q2)
    return jnp.concatenate([jnp.where(lane < HEAD_DIM, q2, zero), jnp.where(lane >= HEAD_DIM, q2, zero)], axis=0)


def _pair_merge(o):
    m = o.shape[0] // 2
    lane = lax.broadcasted_iota(i32, (m, LANE), 1)
    return jnp.where(lane < HEAD_DIM, o[:m], o[m:])


def _joint_attention(s_loc, s_ctx, v_loc, v_ctx, sink=None):
    m = jnp.maximum(jnp.max(s_loc, axis=-1, keepdims=True), jnp.max(s_ctx, axis=-1, keepdims=True))
    if sink is not None:
        m = jnp.maximum(m, sink)
    p_loc = jnp.exp(s_loc - m)
    p_ctx = jnp.exp(s_ctx - m)
    den = jnp.sum(p_loc, axis=-1, keepdims=True) + jnp.sum(p_ctx, axis=-1, keepdims=True)
    if sink is not None:
        den = den + jnp.exp(sink - m)
    o = (jnp.dot(p_loc.astype(bf16), v_loc, preferred_element_type=f32)
         + jnp.dot(p_ctx.astype(bf16), v_ctx, preferred_element_type=f32))
    return o / den


def _swa_kernel(q_ref, kp_ref, kc_ref, kn_ref, vp_ref, vc_ref, vn_ref, kx_ref, vx_ref, sink_ref, o_ref, *, n_ctx_blocks, seq):
    g = pl.program_id(1)
    i = g - n_ctx_blocks
    k_loc = jnp.concatenate([kp_ref[0], kc_ref[0], kn_ref[0]], axis=0)
    v_loc = jnp.concatenate([vp_ref[0], vc_ref[0], vn_ref[0]], axis=0)
    row = lax.broadcasted_iota(i32, (SWA_BLOCK, 3 * SWA_BLOCK), 0)
    col = lax.broadcasted_iota(i32, (SWA_BLOCK, 3 * SWA_BLOCK), 1)
    rel = col - SWA_BLOCK - row
    kpos = (i - 1) * SWA_BLOCK + col
    ok = (jnp.abs(rel) <= SWA_WINDOW) & (kpos >= 0) & (kpos < seq) & (i >= 0)
    pen = jnp.where(ok, 0.0, NEG)
    pen = jnp.concatenate([pen, pen], axis=0)
    for j in range(SWA_Q // LANE):
        q = _pair_rows(q_ref[0, :, j * LANE:(j + 1) * LANE])
        s_loc = lax.dot_general(q, k_loc, _NT, preferred_element_type=f32) + pen
        s_ctx = lax.dot_general(q, kx_ref[0], _NT, preferred_element_type=f32)
        sink = jnp.concatenate([jnp.broadcast_to(sink_ref[2 * j:2 * j + 1, 0:1], (SWA_BLOCK, 1)),
                                jnp.broadcast_to(sink_ref[2 * j + 1:2 * j + 2, 0:1], (SWA_BLOCK, 1))], axis=0)
        o = _joint_attention(s_loc, s_ctx, v_loc, vx_ref[0], sink)
        o_ref[0, :, j * LANE:(j + 1) * LANE] = _pair_merge(o).astype(bf16)


def _swa(sq, sk, sv, sink_rows, n_ctx):
    nb, ta, _ = sq.shape
    ncb = n_ctx // SWA_BLOCK
    nblk = ta // SWA_BLOCK
    nlat = nblk - ncb
    prev = lambda b, g: (b, jnp.clip(g - ncb - 1, 0, nlat - 1) + ncb, 0)
    cur = lambda b, g: (b, g, 0)
    nxt = lambda b, g: (b, jnp.clip(g - ncb + 1, 0, nlat - 1) + ncb, 0)
    kvs = lambda f: pl.BlockSpec((1, SWA_BLOCK, SWA_KV), f)
    ctx = pl.BlockSpec((1, n_ctx, SWA_KV), lambda b, g: (b, 0, 0))
    return pl.pallas_call(
        functools.partial(_swa_kernel, n_ctx_blocks=ncb, seq=ta - n_ctx),
        grid=(nb, nblk),
        in_specs=[pl.BlockSpec((1, SWA_BLOCK, SWA_Q), cur), kvs(prev), kvs(cur), kvs(nxt), kvs(prev), kvs(cur), kvs(nxt),
                  ctx, ctx, pl.BlockSpec((8, LANE), lambda b, g: (0, 0))],
        out_specs=pl.BlockSpec((1, SWA_BLOCK, SWA_Q), cur),
        out_shape=jax.ShapeDtypeStruct((nb, ta, SWA_Q), bf16),
        compiler_params=pltpu.CompilerParams(dimension_semantics=("arbitrary", "arbitrary")),
        name="swa_attention",
    )(sq, sk, sk, sk, sv, sv, sv, sk, sv, sink_rows)


def _nat_kernel(q_ref, k_ref, v_ref, bias_ref, o_ref, *, n_ctx, n_rows):
    g = pl.program_id(1)
    is_ctx = g == 0
    ctx_pen = jnp.where(is_ctx, NEG, 0.0)
    rows_per_tile = TM // GRID_W
    nwin = NAT_ROWS * GRID_W
    for rr in range(rows_per_tile):
        r = jnp.maximum((g - 1) * rows_per_tile + rr, 0)
        rs = jnp.clip(r - NAT_ROWS // 2, 0, n_rows - NAT_ROWS)
        dr0 = rs - r + NAT_ROWS - 1
        start = pl.multiple_of(n_ctx + rs * GRID_W, GRID_W)
        for p in range(NAT_W // LANE):
            ls = slice(p * LANE, (p + 1) * LANE)
            q = _pair_rows(q_ref[0, rr * GRID_W:(rr + 1) * GRID_W, ls])
            s_loc = (lax.dot_general(q, k_ref[0, pl.ds(start, nwin), ls], _NT, preferred_element_type=f32)
                     + bias_ref[p, dr0] + ctx_pen)
            s_ctx = lax.dot_general(q, k_ref[0, 0:n_ctx, ls], _NT, preferred_element_type=f32)
            o = _joint_attention(s_loc, s_ctx, v_ref[0, pl.ds(start, nwin), ls], v_ref[0, 0:n_ctx, ls])
            o_ref[0, rr * GRID_W:(rr + 1) * GRID_W, ls] = _pair_merge(o).astype(bf16)


def _nat(nq, nk, nv, bias, n_ctx):
    nb, ta, w = nq.shape
    nt = ta // TM
    whole = pl.BlockSpec((1, ta, w), lambda b, g: (b, 0, 0), pipeline_mode=pl.Buffered(1))
    return pl.pallas_call(
        functools.partial(_nat_kernel, n_ctx=n_ctx, n_rows=(ta - n_ctx) // GRID_W),
        grid=(nb, nt),
        in_specs=[pl.BlockSpec((1, TM, w), lambda b, g: (b, g, 0)), whole, whole,
                  pl.BlockSpec(bias.shape, lambda b, g: (0, 0, 0, 0), pipeline_mode=pl.Buffered(1))],
        out_specs=pl.BlockSpec((1, TM, w), lambda b, g: (b, g, 0)),
        out_shape=jax.ShapeDtypeStruct((nb, ta, w), bf16),
        compiler_params=pltpu.CompilerParams(dimension_semantics=("arbitrary", "arbitrary"),
                                             vmem_limit_bytes=VMEM_LIMIT_BYTES),
        name="nat_attention",
    )(nq, nk, nv, bias)


def _nat_bias_tables(rpb):
    c = np.arange(GRID_W)[:, None]
    kc = np.arange(GRID_W)[None, :]
    ws = np.clip(c - NAT_COLS // 2, 0, GRID_W - NAT_COLS)
    valid = (kc >= ws) & (kc < ws + NAT_COLS)
    col_idx = np.clip(kc - c + NAT_COLS - 1, 0, 2 * NAT_COLS - 2)
    dr = np.arange(NAT_ROWS)[:, None] + np.arange(NAT_ROWS)[None, :]
    t = rpb.astype(f32)[:, dr][:, :, :, col_idx]
    t = jnp.where(jnp.asarray(valid)[None, None, None], t, NEG)
    t = t.transpose(0, 1, 3, 2, 4).reshape(NAT_HEADS // 2, 2, NAT_ROWS, GRID_W, NAT_ROWS * GRID_W)
    return t.transpose(0, 2, 1, 3, 4).reshape(NAT_HEADS // 2, NAT_ROWS, 2 * GRID_W, NAT_ROWS * GRID_W)


def _outproj_kernel(o_ref, sg_ref, hnw_ref, bd_ref, so_ref, no_ref, wa_ref, wb_ref, wn_ref, mod_ref, x_ref, y_ref):
    oo = o_ref[0, 0] + o_ref[1, 0]
    hi, lo = _split2(oo * oo)
    bd = bd_ref[...]
    msq = (jnp.dot(hi, bd, preferred_element_type=f32) + jnp.dot(lo, bd, preferred_element_type=f32)) * (1.0 / HEAD_DIM)
    a = oo * lax.rsqrt(msq + EPS) * hnw_ref[...] * sg_ref[0]
    y = (jnp.dot(a.astype(bf16), wa_ref[...], preferred_element_type=f32)
         + jnp.dot(so_ref[0], wb_ref[...], preferred_element_type=f32)
         + jnp.dot(no_ref[0], wn_ref[...], preferred_element_type=f32))
    y_ref[0] = x_ref[0] + mod_ref[0, 0, 2:3, :] * y


def _outproj(o, sg, hnw, bd, so, no, wa, wb, wn, mod, xall):
    nb, ta, d = xall.shape
    nt = ta // TM
    bs = lambda width: pl.BlockSpec((1, TM, width), lambda b, i: (b, i, 0))
    full = lambda a: pl.BlockSpec(a.shape, lambda b, i: (0,) * a.ndim)
    return pl.pallas_call(
        _outproj_kernel,
        grid=(nb, nt),
        in_specs=[pl.BlockSpec((2, 1, TM, HG_WIDTH), lambda b, i: (0, b, i, 0)), bs(HG_WIDTH), full(hnw), full(bd),
                  bs(SWA_Q), bs(NAT_W), full(wa), full(wb), full(wn),
                  pl.BlockSpec((1, 1, 6, d), lambda b, i: (b, jnp.minimum(i, 1), 0, 0)), bs(d)],
        out_specs=bs(d),
        out_shape=jax.ShapeDtypeStruct(xall.shape, f32),
        compiler_params=pltpu.CompilerParams(dimension_semantics=("arbitrary", "arbitrary"),
                                             vmem_limit_bytes=VMEM_LIMIT_BYTES),
        name="out_proj",
    )(o, sg, hnw, bd, so, no, wa, wb, wn, mod, xall)


def _top16_rows(vals, payload=None):
    n = vals.shape[0]
    iota = lax.broadcasted_iota(i32, vals.shape, 0)
    best, picked = [], []
    for _ in range(PEER_TOPK):
        m = jnp.max(vals, axis=0, keepdims=True)
        idx = jnp.min(jnp.where(vals == m, iota, n), axis=0, keepdims=True)
        hit = iota == idx
        best.append(m)
        picked.append(idx if payload is None else jnp.max(jnp.where(hit, payload, -1), axis=0, keepdims=True))
        vals = jnp.where(hit, -jnp.inf, vals)
    return jnp.concatenate(best, axis=0), jnp.concatenate(picked, axis=0)


def _product_candidates(first, second, combine):
    half = PEER_TOPK // 2
    rows = [combine(first[0:1, :], second)]
    rows += [combine(first[a:a + 1, :], second[0:half, :]) for a in range(1, half)]
    rows.append(combine(first[half:, :], second[0:1, :]))
    return jnp.concatenate(rows, axis=0)


def _route_kernel(x_ref, nw_ref, mod_ref, wq_ref, sub_ref, h_ref, idx_ref, g_ref):
    h = _norm_mod(x_ref[0], nw_ref[...], mod_ref[0, 0, 3:4, :], mod_ref[0, 0, 4:5, :])
    h_ref[0] = h
    q = jnp.dot(h.astype(bf16), wq_ref[...], preferred_element_type=f32).astype(bf16)
    half = PEER_DK // 2
    ids, gates = [], []
    for hh in range(PEER_HEADS):
        top_s, top_i = [], []
        for p in range(2):
            c0 = (hh * 2 + p) * half
            s = lax.dot_general(sub_ref[hh, p], q[:, c0:c0 + half], _NT, preferred_element_type=f32)
            ts, ti = _top16_rows(s)
            top_s.append(ts)
            top_i.append(ti)
        cand_s = _product_candidates(top_s[0], top_s[1], lambda a, b: a + b)
        cand_i = _product_candidates(top_i[0], top_i[1], lambda a, b: a * PEER_NKEYS + b)
        best, eid = _top16_rows(cand_s, cand_i)
        ids.append(eid)
        e = jnp.exp(best - best[0:1, :])
        gates.append(e / jnp.sum(e, axis=0, keepdims=True))
    idx_ref[0] = jnp.concatenate(ids, axis=0).T
    g_ref[0] = jnp.concatenate(gates, axis=0).T


def _route(xall, nw, mod, wq, sub):
    nb, ta, d = xall.shape
    nt = ta // TM
    full = lambda a: pl.BlockSpec(a.shape, lambda b, i: (0,) * a.ndim)
    return pl.pallas_call(
        _route_kernel,
        grid=(nb, nt),
        in_specs=[pl.BlockSpec((1, TM, d), lambda b, i: (b, i, 0)), full(nw),
                  pl.BlockSpec((1, 1, 6, d), lambda b, i: (b, jnp.minimum(i, 1), 0, 0)), full(wq), full(sub)],
        out_specs=[pl.BlockSpec((1, TM, d), lambda b, i: (b, i, 0)),
                   pl.BlockSpec((1, TM, PEER_SLOTS), lambda b, i: (b, i, 0)),
                   pl.BlockSpec((1, TM, PEER_SLOTS), lambda b, i: (b, i, 0))],
        out_shape=[jax.ShapeDtypeStruct(xall.shape, f32), jax.ShapeDtypeStruct((nb, ta, PEER_SLOTS), i32),
                   jax.ShapeDtypeStruct((nb, ta, PEER_SLOTS), f32)],
        compiler_params=pltpu.CompilerParams(dimension_semantics=("arbitrary", "arbitrary"),
                                             vmem_limit_bytes=VMEM_LIMIT_BYTES),
        name="peer_route",
    )(xall, nw, mod, wq, sub)


def _expert_kernel(idx_ref, h_ref, g_ref, x_ref, mod_ref, tab_hbm, o_ref, buf, sem):
    d = h_ref.shape[-1]
    nct = d // LANE
    groups = PEER_SLOTS // 8

    per_piece = PEER_SLOTS // nct

    def issue(t, slot, piece):
        for k in range(piece * per_piece, (piece + 1) * per_piece):
            pltpu.make_async_copy(tab_hbm.at[idx_ref[t, k]], buf.at[slot, k // 8, :, pl.ds(k % 8, 1), :],
                                  sem.at[slot]).start(priority=k % 2)

    def wait(slot):
        pltpu.make_async_copy(buf.at[slot], buf.at[slot], sem.at[slot]).wait()

    def cols(slot, c, high):
        words = jnp.concatenate([jnp.concatenate([buf[slot, kg, c], buf[slot, kg, c + 1]], axis=1)
                                 for kg in range(groups)], axis=0)
        bits = (words & jnp.uint32(0xFFFF0000)) if high else (words << 16)
        return lax.bitcast_convert_type(bits, f32).astype(bf16)

    def step(t, w8, do_issue, do_u, do_v):
        nslot = lax.rem(t + 1, PEER_NBUF)
        slot = lax.rem(t, PEER_NBUF)
        islot = lax.rem(t + PEER_AHEAD, PEER_NBUF)
        if do_u:
            wait(nslot)
            h8 = jnp.broadcast_to(h_ref[pl.ds(t + 1, 1), :], (8, d)).astype(bf16)
            gate = g_ref[pl.ds(t + 1, 1), :]
            a = jnp.zeros((8, PEER_SLOTS), f32)
        ys = []
        for j in range(nct):
            if 2 * j < nct:
                c = 2 * j
                if do_u:
                    a = a + lax.dot_general(h8[:, c * LANE:(c + 2) * LANE], cols(nslot, c, True), _NT, preferred_element_type=f32)
                if do_v:
                    ys.append(jnp.dot(w8, cols(slot, c, False), preferred_element_type=f32)[0:1, :])
            if do_issue:
                issue(t + PEER_AHEAD, islot, j)
        if do_v:
            o_ref[pl.ds(t, 1), :] = jnp.concatenate(ys, axis=1)
        if do_u:
            a1 = a[0:1, :]
            w = gate * (0.5 * a1 * (1.0 + lax.erf(a1 * (2.0 ** -0.5))))
            w8 = jnp.broadcast_to(w, (8, PEER_SLOTS)).astype(bf16)
        return w8

    def prologue(t, carry):
        for j in range(nct):
            issue(t, t, j)
        return carry

    lax.fori_loop(0, PEER_AHEAD, prologue, 0)
    w8 = step(-1, jnp.zeros((8, PEER_SLOTS), bf16), False, True, False)
    w8 = lax.fori_loop(0, PEER_TB - PEER_AHEAD, lambda t, w: step(t, w, True, True, True), w8)
    w8 = lax.fori_loop(PEER_TB - PEER_AHEAD, PEER_TB - 1, lambda t, w: step(t, w, False, True, True), w8)
    step(PEER_TB - 1, w8, False, False, True)
    o_ref[...] = x_ref[...] + mod_ref[0, 0, 5:6, :] * o_ref[...]


def _experts(idx, h, g, xflat, mod, tab, ta, n_ctx):
    n, d = xflat.shape
    bpb = ta // PEER_TB
    cb = n_ctx // PEER_TB
    tokb = lambda width: pl.BlockSpec((PEER_TB, width), lambda i: (i, 0))
    return pl.pallas_call(
        _expert_kernel,
        grid=(n // PEER_TB,),
        in_specs=[pl.BlockSpec((PEER_TB, PEER_SLOTS), lambda i: (i, 0), memory_space=pltpu.SMEM),
                  tokb(d), tokb(PEER_SLOTS), tokb(d),
                  pl.BlockSpec((1, 1, 6, d), lambda i: (i // bpb, jnp.where(i % bpb < cb, 0, 1), 0, 0)),
                  pl.BlockSpec(memory_space=pl.ANY)],
        out_specs=tokb(d),
        out_shape=jax.ShapeDtypeStruct((n, d), f32),
        scratch_shapes=[pltpu.VMEM((PEER_NBUF, PEER_SLOTS // 8, d // LANE, 8, LANE), jnp.uint32),
                        pltpu.SemaphoreType.DMA((PEER_NBUF,))],
        compiler_params=pltpu.CompilerParams(dimension_semantics=("arbitrary",), vmem_limit_bytes=VMEM_LIMIT_BYTES),
        name="peer_experts",
    )(idx, h, g, xflat, mod, tab)


def _expert_table(u, v):
    depth, e, d = u.shape
    half = lambda a: lax.bitcast_convert_type(a.astype(bf16), jnp.uint16).astype(jnp.uint32)
    return ((half(u) << 16) | half(v)).reshape(depth, e, d // LANE, 1, LANE)


def _final_norm_kernel(x_ref, w_ref, o_ref):
    x = x_ref[0]
    o_ref[0] = x * lax.rsqrt(jnp.mean(x * x, axis=-1, keepdims=True) + EPS) * w_ref[...]


def _final_norm(xall, w, n_ctx):
    nb, ta, d = xall.shape
    off = n_ctx // TM
    return pl.pallas_call(
        _final_norm_kernel,
        grid=(nb, (ta - n_ctx) // TM),
        in_specs=[pl.BlockSpec((1, TM, d), lambda b, i: (b, i + off, 0)), pl.BlockSpec((1, d), lambda b, i: (0, 0))],
        out_specs=pl.BlockSpec((1, TM, d), lambda b, i: (b, i, 0)),
        out_shape=jax.ShapeDtypeStruct((nb, ta - n_ctx, d), f32),
        compiler_params=pltpu.CompilerParams(dimension_semantics=("arbitrary", "arbitrary")),
        name="final_norm",
    )(xall, w.reshape(1, d))


def _rope_tables(n_ctx, seq):
    nf = HEAD_DIM // 4
    pos = np.arange(seq)
    inv = ROPE_BASE ** (-np.arange(nf, dtype=np.float32) / nf)
    ar = (pos // GRID_W).astype(np.float32)[:, None] * inv
    ac = (pos % GRID_W).astype(np.float32)[:, None] * inv
    cos = np.concatenate([np.cos(ar), np.cos(ar), np.cos(ac), np.cos(ac)], axis=-1)
    sin = np.concatenate([-np.sin(ar), np.sin(ar), -np.sin(ac), np.sin(ac)], axis=-1)
    cos = np.concatenate([np.ones((n_ctx, HEAD_DIM), np.float32), cos.astype(np.float32)], axis=0)
    sin = np.concatenate([np.zeros((n_ctx, HEAD_DIM), np.float32), sin.astype(np.float32)], axis=0)
    return jnp.asarray(np.stack([np.tile(cos, (1, 2)), np.tile(sin, (1, 2))]))


def _tri_tables():
    r = np.arange(TM)[:, None]
    c = np.arange(TM)[None, :]
    same = (r // HG_SUB) == (c // HG_SUB)
    return jnp.asarray(np.stack([same & (c <= r), same & (c >= r)]).astype(np.float32), dtype=bf16)


def _block_diag_ones():
    r = np.arange(HG_WIDTH)
    return jnp.asarray((r[:, None] // HEAD_DIM == r[None, :] // HEAD_DIM).astype(np.float32), dtype=bf16)


def _swa_perm():
    return np.concatenate([np.arange(h * HEAD_DIM, (h + 1) * HEAD_DIM) for h in SWA_HEAD_ORDER])


def kernel(x, c, ctx, c_ctx, w_mod, b_mod, norm_mix_w, norm_ffn_w, w_in, hgrn_lb_logits, hgrn_norm_w, swa_sink, nat_rpb,
           w_out, peer_wq, peer_subkeys, peer_u, peer_v, final_norm_w):
    nb, seq, d = x.shape
    n_ctx = ctx.shape[1]
    depth = w_in.shape[0]
    ta = n_ctx + seq
    assert n_ctx == TM and seq % TM == 0 and seq % (GRID_W * NAT_ROWS) == 0 and d % LANE == 0
    assert w_in.shape[-1] == PROJ_WIDTH and peer_wq.shape[-1] == PEER_HEADS * PEER_DK

    mods = _mods(c, c_ctx, w_mod, b_mod)
    cs = _rope_tables(n_ctx, seq)
    tri = _tri_tables()
    bd = _block_diag_ones()
    perm = _swa_perm()
    q0 = 5 * HG_WIDTH
    col_perm = np.concatenate([np.arange(q0), q0 + perm, np.arange(q0 + SWA_Q, PROJ_WIDTH)])
    w_in_b = w_in[:, :, col_perm].astype(bf16)
    w_out_b = w_out.astype(bf16)
    wa = w_out_b[:, :HG_WIDTH]
    wb = w_out_b[:, HG_WIDTH:HG_WIDTH + SWA_Q][:, perm]
    wn = w_out_b[:, HG_WIDTH + SWA_Q:]
    wq_b = peer_wq.astype(bf16)
    sub_b = peer_subkeys.astype(bf16)
    lb = jnp.cumsum(jax.nn.softmax(hgrn_lb_logits.astype(f32), axis=1), axis=1)
    lb = lb - lb[:, :1]
    gp = jnp.stack([jnp.log(lb[0]), jnp.log1p(-lb[0]), 1.0 - lb[0], jnp.log(lb[1]), jnp.log1p(-lb[1]), 1.0 - lb[1],
                    jnp.zeros_like(lb[0]), jnp.zeros_like(lb[0])], axis=1)
    sink_rows = jnp.zeros((depth, 8, LANE), f32).at[:, :SWA_HEADS].set(
        jnp.broadcast_to(swa_sink.astype(f32)[:, list(SWA_HEAD_ORDER), None], (depth, SWA_HEADS, LANE)))

    tab = _expert_table(peer_u, peer_v)

    xall = jnp.concatenate([ctx, x], axis=1)
    for l in range(depth):
        mod = mods[l]
        hq, cum, kk, hv, sg, sq, sk, sv, nq, nk, nv = _proj(xall, norm_mix_w[l].reshape(1, d), mod, w_in_b[l], cs, gp[l], tri)
        o = _hgrn_scan(hq, cum, kk, hv, bd)
        so = _swa(sq, sk, sv, sink_rows[l], n_ctx)
        no = _nat(nq, nk, nv, _nat_bias_tables(nat_rpb[l]), n_ctx)
        xall = _outproj(o, sg, hgrn_norm_w[l].reshape(1, HG_WIDTH), bd, so, no, wa[l], wb[l], wn[l], mod, xall)
        h2, idx, g = _route(xall, norm_ffn_w[l].reshape(1, d), mod, wq_b[l], sub_b[l])
        xall = _experts(idx.reshape(nb * ta, PEER_SLOTS), h2.reshape(nb * ta, d), g.reshape(nb * ta, PEER_SLOTS),
                        xall.reshape(nb * ta, d), mod, tab[l], ta, n_ctx).reshape(nb, ta, d)
    return _final_norm(xall, final_norm_w, n_ctx)
```

```python
import functools

import numpy as np
import jax
import jax.numpy as jnp
from jax import lax
from jax.experimental import pallas as pl
from jax.experimental.pallas import tpu as pltpu
from jax.experimental.pallas import tpu_sc as plsc

f32 = jnp.float32
bf16 = jnp.bfloat16
i32 = jnp.int32

LANE = 128
VMEM_LIMIT_BYTES = 56 * 1024 * 1024

GRID_W = 64
HEAD_DIM = 64
EPS = 1e-6
HG_HEADS = 4
HG_WIDTH = 256
HG_SUB = 16
SWA_HEADS = 6
SWA_KV_HEADS = 2
SWA_WINDOW = 128
SWA_BLOCK = 128
ROPE_BASE = 10000.0
NAT_HEADS = 6
NAT_ROWS = 8
NAT_COLS = 16
SWA_Q = SWA_HEADS * HEAD_DIM
SWA_KV = SWA_KV_HEADS * HEAD_DIM
NAT_W = NAT_HEADS * HEAD_DIM
PROJ_WIDTH = 5 * HG_WIDTH + SWA_Q + 2 * SWA_KV + 3 * NAT_W
PEER_HEADS = 8
PEER_NKEYS = 128
PEER_DK = 256
PEER_TOPK = 16
PEER_SLOTS = PEER_HEADS * PEER_TOPK

TM = 256
PEER_TB = 128
PEER_NBUF = 8
PEER_AHEAD = PEER_NBUF - 1
PEER_SC_BLOCKS = 78
PEER_DENSE_TB = 16
SC_CHUNK = 32
NEG = -1e30

SWA_HEAD_ORDER = (0, 3, 1, 4, 2, 5)

_NT = (((1,), (1,)), ((), ()))
_TN = (((0,), (0,)), ((), ()))


def _silu(x):
    return x * (1.0 / (1.0 + jnp.exp(-x)))


def _split2(x):
    hi = x.astype(bf16)
    lo = (x - hi.astype(f32)).astype(bf16)
    return hi, lo


def _mods_kernel(sc_ref, w_ref, b_ref, o_ref):
    s = _silu(sc_ref[...])
    o_ref[0] = jnp.dot(s.astype(bf16), w_ref[0].astype(bf16), preferred_element_type=f32) + b_ref[0]


def _mods(c, c_ctx, w_mod, b_mod):
    depth, d, d6 = w_mod.shape
    nb = c.shape[0]
    rows = jnp.zeros((8, d), f32).at[:nb].set(c).at[nb].set(c_ctx)
    tn = d6 // 4
    out = pl.pallas_call(
        _mods_kernel,
        grid=(depth, 4),
        in_specs=[pl.BlockSpec((8, d), lambda l, j: (0, 0)),
                  pl.BlockSpec((1, d, tn), lambda l, j: (l, 0, j)),
                  pl.BlockSpec((1, 1, tn), lambda l, j: (l, 0, j))],
        out_specs=pl.BlockSpec((1, 8, tn), lambda l, j: (l, 0, j)),
        out_shape=jax.ShapeDtypeStruct((depth, 8, d6), f32),
        compiler_params=pltpu.CompilerParams(dimension_semantics=("arbitrary", "arbitrary"),
                                             vmem_limit_bytes=VMEM_LIMIT_BYTES),
        name="adaln_mods",
    )(rows, w_mod, b_mod.reshape(depth, 1, d6))
    lat = out[:, :nb].reshape(depth, nb, 6, d)
    ctx = jnp.broadcast_to(out[:, nb].reshape(depth, 1, 6, d), (depth, nb, 6, d))
    return jnp.stack([ctx, lat], axis=2)


def _norm_mod(x, nw, shift, scale):
    y = x * lax.rsqrt(jnp.mean(x * x, axis=-1, keepdims=True) + EPS) * nw
    return y * (1.0 + scale) + shift


def _rope(t, cos, sins):
    lane = lax.broadcasted_iota(i32, t.shape, 1)
    sw = jnp.where((lane % 32) < 16, pltpu.roll(t, LANE - 16, 1), pltpu.roll(t, 16, 1))
    return t * cos + sw * sins


def _hgrn_gates(z, log_lb, log1m_lb, one_m_lb):
    log_sig = jnp.minimum(z, 0.0) - jnp.log1p(jnp.exp(-jnp.abs(z)))
    c = log1m_lb + log_sig
    m = jnp.maximum(log_lb, c)
    log_f = m + jnp.log1p(jnp.exp(-jnp.abs(log_lb - c)))
    k = one_m_lb * (1.0 / (1.0 + jnp.exp(z)))
    return log_f, k


def _cumsum16(tri, log_f):
    h1 = log_f.astype(bf16)
    r1 = log_f - h1.astype(f32)
    h2 = r1.astype(bf16)
    h3 = (r1 - h2.astype(f32)).astype(bf16)
    return (jnp.dot(tri, h1, preferred_element_type=f32) + jnp.dot(tri, h2, preferred_element_type=f32)
            + jnp.dot(tri, h3, preferred_element_type=f32))


def _proj_kernel(x_ref, nw_ref, mod_ref, w_ref, cs_ref, gp_ref, tri_ref,
                 hq_ref, cum_ref, kk_ref, hv_ref, sg_ref, sq_ref, sk_ref, sv_ref, nq_ref, nk_ref, nv_ref):
    h = _norm_mod(x_ref[0], nw_ref[...], mod_ref[0, 0, 0:1, :], mod_ref[0, 0, 1:2, :])
    p = jnp.dot(h.astype(bf16), w_ref[...], preferred_element_type=f32)
    w = HG_WIDTH
    hq_ref[0] = _silu(p[:, 0:w])
    for d in range(2):
        log_f, k = _hgrn_gates(p[:, (1 + d) * w:(2 + d) * w], gp_ref[3 * d:3 * d + 1, :],
                               gp_ref[3 * d + 1:3 * d + 2, :], gp_ref[3 * d + 2:3 * d + 3, :])
        cum_ref[d, 0] = _cumsum16(tri_ref[d], log_f)
        kk_ref[d, 0] = k
    hv_ref[0] = p[:, 3 * w:4 * w]
    sg_ref[0] = _silu(p[:, 4 * w:5 * w])
    cos, sins = cs_ref[0], cs_ref[1]
    o = 5 * w
    scale = HEAD_DIM ** -0.5
    for j in range(SWA_Q // LANE):
        sq_ref[0, :, j * LANE:(j + 1) * LANE] = (_rope(p[:, o + j * LANE:o + (j + 1) * LANE], cos, sins) * scale).astype(bf16)
    o += SWA_Q
    sk_ref[0] = _rope(p[:, o:o + SWA_KV], cos, sins).astype(bf16)
    o += SWA_KV
    sv_ref[0] = p[:, o:o + SWA_KV].astype(bf16)
    o += SWA_KV
    nq_ref[0] = (p[:, o:o + NAT_W] * scale).astype(bf16)
    o += NAT_W
    nk_ref[0] = p[:, o:o + NAT_W].astype(bf16)
    o += NAT_W
    nv_ref[0] = p[:, o:o + NAT_W].astype(bf16)


def _proj(xall, nw, mod, w_in, cs, gp, tri):
    nb, ta, d = xall.shape
    nt = ta // TM
    tok = lambda width, dt: jax.ShapeDtypeStruct((nb, ta, width), dt)
    tok2 = jax.ShapeDtypeStruct((2, nb, ta, HG_WIDTH), f32)
    bs = lambda width: pl.BlockSpec((1, TM, width), lambda b, i: (b, i, 0))
    bs2 = pl.BlockSpec((2, 1, TM, HG_WIDTH), lambda b, i: (0, b, i, 0))
    return pl.pallas_call(
        _proj_kernel,
        grid=(nb, nt),
        in_specs=[pl.BlockSpec((1, TM, d), lambda b, i: (b, i, 0)),
                  pl.BlockSpec((1, d), lambda b, i: (0, 0)),
                  pl.BlockSpec((1, 1, 6, d), lambda b, i: (b, jnp.minimum(i, 1), 0, 0)),
                  pl.BlockSpec((d, PROJ_WIDTH), lambda b, i: (0, 0)),
                  pl.BlockSpec((2, TM, LANE), lambda b, i: (0, i, 0)),
                  pl.BlockSpec((8, HG_WIDTH), lambda b, i: (0, 0)),
                  pl.BlockSpec((2, TM, TM), lambda b, i: (0, 0, 0))],
        out_specs=[bs(HG_WIDTH), bs2, bs2, bs(HG_WIDTH), bs(HG_WIDTH), bs(SWA_Q), bs(SWA_KV), bs(SWA_KV),
                   bs(NAT_W), bs(NAT_W), bs(NAT_W)],
        out_shape=[tok(HG_WIDTH, f32), tok2, tok2, tok(HG_WIDTH, f32), tok(HG_WIDTH, f32), tok(SWA_Q, bf16),
                   tok(SWA_KV, bf16), tok(SWA_KV, bf16), tok(NAT_W, bf16), tok(NAT_W, bf16), tok(NAT_W, bf16)],
        compiler_params=pltpu.CompilerParams(dimension_semantics=("arbitrary", "arbitrary"),
                                             vmem_limit_bytes=VMEM_LIMIT_BYTES),
        name="norm_in_proj",
    )(xall, nw, mod, w_in, cs, gp, tri)


def _hgrn_kernel(q_ref, cum_ref, k_ref, v_ref, bd_ref, o_ref, st_ref):
    d = pl.program_id(1)
    fwd = d == 0

    @pl.when(pl.program_id(2) == 0)
    def _():
        st_ref[...] = jnp.zeros_like(st_ref)

    bd = bd_ref[...]
    bd_mask = bd > 0
    sgn = jnp.where(fwd, 1, -1)
    t_sgn = lax.broadcasted_iota(i32, (HG_SUB, 1), 0) * sgn
    nsteps = TM // HG_SUB

    def step(i, carry):
        sc = jnp.where(fwd, i, nsteps - 1 - i)
        r0 = pl.multiple_of(sc * HG_SUB, HG_SUB)
        q = q_ref[0, pl.ds(r0, HG_SUB), :]
        b = cum_ref[0, 0, pl.ds(r0, HG_SUB), :]
        k = k_ref[0, 0, pl.ds(r0, HG_SUB), :]
        v = v_ref[0, pl.ds(r0, HG_SUB), :]
        b_end = jnp.where(fwd, b[HG_SUB - 1:HG_SUB, :], b[0:1, :])
        st = st_ref[...]
        o = lax.dot_general((q * jnp.exp(b)).astype(bf16), st.astype(bf16), _NT, preferred_element_type=f32)
        rows = []
        for s in range(HG_SUB):
            valid = t_sgn >= s * sgn
            decay = jnp.exp(jnp.minimum(b - b[s:s + 1, :], 0.0))
            rows.append(jnp.where(valid, decay * q * k[s:s + 1, :], 0.0))
        hi, lo = _split2(jnp.concatenate(rows, axis=0))
        pe = jnp.dot(hi, bd, preferred_element_type=f32) + jnp.dot(lo, bd, preferred_element_type=f32)
        for s in range(HG_SUB):
            o = o + pe[s * HG_SUB:(s + 1) * HG_SUB, :] * v[s:s + 1, :]
        o_ref[0, 0, pl.ds(r0, HG_SUB), :] = o
        kd = (k * jnp.exp(b_end - b)).astype(bf16)
        kv_t = lax.dot_general(v.astype(bf16), kd, _TN, preferred_element_type=f32)
        st_ref[...] = st * jnp.exp(b_end) + jnp.where(bd_mask, kv_t, 0.0)
        return carry

    lax.fori_loop(0, nsteps, step, 0, unroll=2)


def _hgrn_scan(hq, cum, kk, hv, bd):
    nb, ta, w = hq.shape
    nt = ta // TM

    def blk(d, j):
        return jnp.where(d == 0, j, jnp.where(j == 0, 0, nt - j))

    return pl.pallas_call(
        _hgrn_kernel,
        grid=(nb, 2, nt),
        in_specs=[pl.BlockSpec((1, TM, w), lambda b, d, j: (b, blk(d, j), 0)),
                  pl.BlockSpec((1, 1, TM, w), lambda b, d, j: (d, b, blk(d, j), 0)),
                  pl.BlockSpec((1, 1, TM, w), lambda b, d, j: (d, b, blk(d, j), 0)),
                  pl.BlockSpec((1, TM, w), lambda b, d, j: (b, blk(d, j), 0)),
                  pl.BlockSpec((w, w), lambda b, d, j: (0, 0))],
        out_specs=pl.BlockSpec((1, 1, TM, w), lambda b, d, j: (d, b, blk(d, j), 0)),
        out_shape=jax.ShapeDtypeStruct((2, nb, ta, w), f32),
        scratch_shapes=[pltpu.VMEM((w, w), f32)],
        compiler_params=pltpu.CompilerParams(dimension_semantics=("arbitrary", "arbitrary", "arbitrary")),
        name="hgrn_scan",
    )(hq, cum, kk, hv, bd)


def _pair_rows(q2):
    lane = lax.broadcasted_iota(i32, q2.shape, 1)
    zero = jnp.zeros_like(q2)
    return jnp.concatenate([jnp.where(lane < HEAD_DIM, q2, zero), jnp.where(lane >= HEAD_DIM, q2, zero)], axis=0)


def _pair_merge(o):
    m = o.shape[0] // 2
    lane = lax.broadcasted_iota(i32, (m, LANE), 1)
    return jnp.where(lane < HEAD_DIM, o[:m], o[m:])


def _joint_attention(s_loc, s_ctx, v_loc, v_ctx, sink=None):
    m = jnp.maximum(jnp.max(s_loc, axis=-1, keepdims=True), jnp.max(s_ctx, axis=-1, keepdims=True))
    if sink is not None:
        m = jnp.maximum(m, sink)
    p_loc = jnp.exp(s_loc - m)
    p_ctx = jnp.exp(s_ctx - m)
    den = jnp.sum(p_loc, axis=-1, keepdims=True) + jnp.sum(p_ctx, axis=-1, keepdims=True)
    if sink is not None:
        den = den + jnp.exp(sink - m)
    o = (jnp.dot(p_loc.astype(bf16), v_loc, preferred_element_type=f32)
         + jnp.dot(p_ctx.astype(bf16), v_ctx, preferred_element_type=f32))
    return o / den


def _swa_kernel(q_ref, kp_ref, kc_ref, kn_ref, vp_ref, vc_ref, vn_ref, kx_ref, vx_ref, sink_ref, o_ref, *, n_ctx_blocks, seq):
    g = pl.program_id(1)
    i = g - n_ctx_blocks
    k_loc = jnp.concatenate([kp_ref[0], kc_ref[0], kn_ref[0]], axis=0)
    v_loc = jnp.concatenate([vp_ref[0], vc_ref[0], vn_ref[0]], axis=0)
    row = lax.broadcasted_iota(i32, (SWA_BLOCK, 3 * SWA_BLOCK), 0)
    col = lax.broadcasted_iota(i32, (SWA_BLOCK, 3 * SWA_BLOCK), 1)
    rel = col - SWA_BLOCK - row
    kpos = (i - 1) * SWA_BLOCK + col
    ok = (jnp.abs(rel) <= SWA_WINDOW) & (kpos >= 0) & (kpos < seq) & (i >= 0)
    pen = jnp.where(ok, 0.0, NEG)
    pen = jnp.concatenate([pen, pen], axis=0)
    for j in range(SWA_Q // LANE):
        q = _pair_rows(q_ref[0, :, j * LANE:(j + 1) * LANE])
        s_loc = lax.dot_general(q, k_loc, _NT, preferred_element_type=f32) + pen
        s_ctx = lax.dot_general(q, kx_ref[0], _NT, preferred_element_type=f32)
        sink = jnp.concatenate([jnp.broadcast_to(sink_ref[2 * j:2 * j + 1, 0:1], (SWA_BLOCK, 1)),
                                jnp.broadcast_to(sink_ref[2 * j + 1:2 * j + 2, 0:1], (SWA_BLOCK, 1))], axis=0)
        o = _joint_attention(s_loc, s_ctx, v_loc, vx_ref[0], sink)
        o_ref[0, :, j * LANE:(j + 1) * LANE] = _pair_merge(o).astype(bf16)


def _swa(sq, sk, sv, sink_rows, n_ctx):
    nb, ta, _ = sq.shape
    ncb = n_ctx // SWA_BLOCK
    nblk = ta // SWA_BLOCK
    nlat = nblk - ncb
    prev = lambda b, g: (b, jnp.clip(g - ncb - 1, 0, nlat - 1) + ncb, 0)
    cur = lambda b, g: (b, g, 0)
    nxt = lambda b, g: (b, jnp.clip(g - ncb + 1, 0, nlat - 1) + ncb, 0)
    kvs = lambda f: pl.BlockSpec((1, SWA_BLOCK, SWA_KV), f)
    ctx = pl.BlockSpec((1, n_ctx, SWA_KV), lambda b, g: (b, 0, 0))
    return pl.pallas_call(
        functools.partial(_swa_kernel, n_ctx_blocks=ncb, seq=ta - n_ctx),
        grid=(nb, nblk),
        in_specs=[pl.BlockSpec((1, SWA_BLOCK, SWA_Q), cur), kvs(prev), kvs(cur), kvs(nxt), kvs(prev), kvs(cur), kvs(nxt),
                  ctx, ctx, pl.BlockSpec((8, LANE), lambda b, g: (0, 0))],
        out_specs=pl.BlockSpec((1, SWA_BLOCK, SWA_Q), cur),
        out_shape=jax.ShapeDtypeStruct((nb, ta, SWA_Q), bf16),
        compiler_params=pltpu.CompilerParams(dimension_semantics=("arbitrary", "arbitrary")),
        name="swa_attention",
    )(sq, sk, sk, sk, sv, sv, sv, sk, sv, sink_rows)


def _nat_kernel(q_ref, k_ref, v_ref, bias_ref, o_ref, *, n_ctx, n_rows):
    g = pl.program_id(1)
    is_ctx = g == 0
    ctx_pen = jnp.where(is_ctx, NEG, 0.0)
    rows_per_tile = TM // GRID_W
    nwin = NAT_ROWS * GRID_W
    for rr in range(rows_per_tile):
        r = jnp.maximum((g - 1) * rows_per_tile + rr, 0)
        rs = jnp.clip(r - NAT_ROWS // 2, 0, n_rows - NAT_ROWS)
        dr0 = rs - r + NAT_ROWS - 1
        start = pl.multiple_of(n_ctx + rs * GRID_W, GRID_W)
        for p in range(NAT_W // LANE):
            ls = slice(p * LANE, (p + 1) * LANE)
            q = _pair_rows(q_ref[0, rr * GRID_W:(rr + 1) * GRID_W, ls])
            s_loc = (lax.dot_general(q, k_ref[0, pl.ds(start, nwin), ls], _NT, preferred_element_type=f32)
                     + bias_ref[p, dr0] + ctx_pen)
            s_ctx = lax.dot_general(q, k_ref[0, 0:n_ctx, ls], _NT, preferred_element_type=f32)
            o = _joint_attention(s_loc, s_ctx, v_ref[0, pl.ds(start, nwin), ls], v_ref[0, 0:n_ctx, ls])
            o_ref[0, rr * GRID_W:(rr + 1) * GRID_W, ls] = _pair_merge(o).astype(bf16)


def _nat(nq, nk, nv, bias, n_ctx):
    nb, ta, w = nq.shape
    nt = ta // TM
    whole = pl.BlockSpec((1, ta, w), lambda b, g: (b, 0, 0), pipeline_mode=pl.Buffered(1))
    return pl.pallas_call(
        functools.partial(_nat_kernel, n_ctx=n_ctx, n_rows=(ta - n_ctx) // GRID_W),
        grid=(nb, nt),
        in_specs=[pl.BlockSpec((1, TM, w), lambda b, g: (b, g, 0)), whole, whole,
                  pl.BlockSpec(bias.shape, lambda b, g: (0, 0, 0, 0), pipeline_mode=pl.Buffered(1))],
        out_specs=pl.BlockSpec((1, TM, w), lambda b, g: (b, g, 0)),
        out_shape=jax.ShapeDtypeStruct((nb, ta, w), bf16),
        compiler_params=pltpu.CompilerParams(dimension_semantics=("arbitrary", "arbitrary"),
                                             vmem_limit_bytes=VMEM_LIMIT_BYTES),
        name="nat_attention",
    )(nq, nk, nv, bias)


def _nat_bias_tables(rpb):
    c = np.arange(GRID_W)[:, None]
    kc = np.arange(GRID_W)[None, :]
    ws = np.clip(c - NAT_COLS // 2, 0, GRID_W - NAT_COLS)
    valid = (kc >= ws) & (kc < ws + NAT_COLS)
    col_idx = np.clip(kc - c + NAT_COLS - 1, 0, 2 * NAT_COLS - 2)
    dr = np.arange(NAT_ROWS)[:, None] + np.arange(NAT_ROWS)[None, :]
    t = rpb.astype(f32)[:, dr][:, :, :, col_idx]
    t = jnp.where(jnp.asarray(valid)[None, None, None], t, NEG)
    t = t.transpose(0, 1, 3, 2, 4).reshape(NAT_HEADS // 2, 2, NAT_ROWS, GRID_W, NAT_ROWS * GRID_W)
    return t.transpose(0, 2, 1, 3, 4).reshape(NAT_HEADS // 2, NAT_ROWS, 2 * GRID_W, NAT_ROWS * GRID_W)


def _outproj_kernel(o_ref, sg_ref, hnw_ref, bd_ref, so_ref, no_ref, wa_ref, wb_ref, wn_ref, mod_ref, x_ref, y_ref):
    oo = o_ref[0, 0] + o_ref[1, 0]
    hi, lo = _split2(oo * oo)
    bd = bd_ref[...]
    msq = (jnp.dot(hi, bd, preferred_element_type=f32) + jnp.dot(lo, bd, preferred_element_type=f32)) * (1.0 / HEAD_DIM)
    a = oo * lax.rsqrt(msq + EPS) * hnw_ref[...] * sg_ref[0]
    y = (jnp.dot(a.astype(bf16), wa_ref[...], preferred_element_type=f32)
         + jnp.dot(so_ref[0], wb_ref[...], preferred_element_type=f32)
         + jnp.dot(no_ref[0], wn_ref[...], preferred_element_type=f32))
    y_ref[0] = x_ref[0] + mod_ref[0, 0, 2:3, :] * y


def _outproj(o, sg, hnw, bd, so, no, wa, wb, wn, mod, xall):
    nb, ta, d = xall.shape
    nt = ta // TM
    bs = lambda width: pl.BlockSpec((1, TM, width), lambda b, i: (b, i, 0))
    full = lambda a: pl.BlockSpec(a.shape, lambda b, i: (0,) * a.ndim)
    return pl.pallas_call(
        _outproj_kernel,
        grid=(nb, nt),
        in_specs=[pl.BlockSpec((2, 1, TM, HG_WIDTH), lambda b, i: (0, b, i, 0)), bs(HG_WIDTH), full(hnw), full(bd),
                  bs(SWA_Q), bs(NAT_W), full(wa), full(wb), full(wn),
                  pl.BlockSpec((1, 1, 6, d), lambda b, i: (b, jnp.minimum(i, 1), 0, 0)), bs(d)],
        out_specs=bs(d),
        out_shape=jax.ShapeDtypeStruct(xall.shape, f32),
        compiler_params=pltpu.CompilerParams(dimension_semantics=("arbitrary", "arbitrary"),
                                             vmem_limit_bytes=VMEM_LIMIT_BYTES),
        name="out_proj",
    )(o, sg, hnw, bd, so, no, wa, wb, wn, mod, xall)


def _top16_rows(vals, payload=None):
    n = vals.shape[0]
    iota = lax.broadcasted_iota(i32, vals.shape, 0)
    best, picked = [], []
    for _ in range(PEER_TOPK):
        m = jnp.max(vals, axis=0, keepdims=True)
        idx = jnp.min(jnp.where(vals == m, iota, n), axis=0, keepdims=True)
        hit = iota == idx
        best.append(m)
        picked.append(idx if payload is None else jnp.max(jnp.where(hit, payload, -1), axis=0, keepdims=True))
        vals = jnp.where(hit, -jnp.inf, vals)
    return jnp.concatenate(best, axis=0), jnp.concatenate(picked, axis=0)


def _product_candidates(first, second, combine):
    half = PEER_TOPK // 2
    rows = [combine(first[0:1, :], second)]
    rows += [combine(first[a:a + 1, :], second[0:half, :]) for a in range(1, half)]
    rows.append(combine(first[half:, :], second[0:1, :]))
    return jnp.concatenate(rows, axis=0)


def _route_kernel(x_ref, nw_ref, mod_ref, wq_ref, sub_ref, h_ref, idx_ref, g_ref):
    h = _norm_mod(x_ref[0], nw_ref[...], mod_ref[0, 0, 3:4, :], mod_ref[0, 0, 4:5, :])
    h_ref[0] = h
    q = jnp.dot(h.astype(bf16), wq_ref[...], preferred_element_type=f32).astype(bf16)
    half = PEER_DK // 2
    ids, gates = [], []
    for hh in range(PEER_HEADS):
        top_s, top_i = [], []
        for p in range(2):
            c0 = (hh * 2 + p) * half
            s = lax.dot_general(sub_ref[hh, p], q[:, c0:c0 + half], _NT, preferred_element_type=f32)
            ts, ti = _top16_rows(s)
            top_s.append(ts)
            top_i.append(ti)
        cand_s = _product_candidates(top_s[0], top_s[1], lambda a, b: a + b)
        cand_i = _product_candidates(top_i[0], top_i[1], lambda a, b: a * PEER_NKEYS + b)
        best, eid = _top16_rows(cand_s, cand_i)
        ids.append(eid)
        e = jnp.exp(best - best[0:1, :])
        gates.append(e / jnp.sum(e, axis=0, keepdims=True))
    idx_ref[0] = jnp.concatenate(ids, axis=0).T
    g_ref[0] = jnp.concatenate(gates, axis=0).T


def _route(xall, nw, mod, wq, sub):
    nb, ta, d = xall.shape
    nt = ta // TM
    full = lambda a: pl.BlockSpec(a.shape, lambda b, i: (0,) * a.ndim)
    return pl.pallas_call(
        _route_kernel,
        grid=(nb, nt),
        in_specs=[pl.BlockSpec((1, TM, d), lambda b, i: (b, i, 0)), full(nw),
                  pl.BlockSpec((1, 1, 6, d), lambda b, i: (b, jnp.minimum(i, 1), 0, 0)), full(wq), full(sub)],
        out_specs=[pl.BlockSpec((1, TM, d), lambda b, i: (b, i, 0)),
                   pl.BlockSpec((1, TM, PEER_SLOTS), lambda b, i: (b, i, 0)),
                   pl.BlockSpec((1, TM, PEER_SLOTS), lambda b, i: (b, i, 0))],
        out_shape=[jax.ShapeDtypeStruct(xall.shape, f32), jax.ShapeDtypeStruct((nb, ta, PEER_SLOTS), i32),
                   jax.ShapeDtypeStruct((nb, ta, PEER_SLOTS), f32)],
        compiler_params=pltpu.CompilerParams(dimension_semantics=("arbitrary", "arbitrary"),
                                             vmem_limit_bytes=VMEM_LIMIT_BYTES),
        name="peer_route",
    )(xall, nw, mod, wq, sub)


def _expert_kernel(idx_ref, h_ref, g_ref, x_ref, mod_ref, tab_hbm, o_ref, buf, sem):
    d = h_ref.shape[-1]
    nct = d // LANE
    groups = PEER_SLOTS // 8

    per_piece = PEER_SLOTS // nct

    def issue(t, slot, piece):
        for k in range(piece * per_piece, (piece + 1) * per_piece):
            pltpu.make_async_copy(tab_hbm.at[idx_ref[t, k]], buf.at[slot, k // 8, :, pl.ds(k % 8, 1), :],
                                  sem.at[slot]).start(priority=k % 2)

    def wait(slot):
        pltpu.make_async_copy(buf.at[slot], buf.at[slot], sem.at[slot]).wait()

    def cols(slot, c, high):
        words = jnp.concatenate([jnp.concatenate([buf[slot, kg, c], buf[slot, kg, c + 1]], axis=1)
                                 for kg in range(groups)], axis=0)
        bits = (words & jnp.uint32(0xFFFF0000)) if high else (words << 16)
        return lax.bitcast_convert_type(bits, f32).astype(bf16)

    def step(t, w8, do_issue, do_u, do_v):
        nslot = lax.rem(t + 1, PEER_NBUF)
        slot = lax.rem(t, PEER_NBUF)
        islot = lax.rem(t + PEER_AHEAD, PEER_NBUF)
        if do_u:
            wait(nslot)
            h8 = jnp.broadcast_to(h_ref[pl.ds(t + 1, 1), :], (8, d)).astype(bf16)
            gate = g_ref[pl.ds(t + 1, 1), :]
            a = jnp.zeros((8, PEER_SLOTS), f32)
        ys = []
        for j in range(nct):
            if 2 * j < nct:
                c = 2 * j
                if do_u:
                    a = a + lax.dot_general(h8[:, c * LANE:(c + 2) * LANE], cols(nslot, c, True), _NT, preferred_element_type=f32)
                if do_v:
                    ys.append(jnp.dot(w8, cols(slot, c, False), preferred_element_type=f32)[0:1, :])
            if do_issue:
                issue(t + PEER_AHEAD, islot, j)
        if do_v:
            o_ref[pl.ds(t, 1), :] = jnp.concatenate(ys, axis=1)
        if do_u:
            a1 = a[0:1, :]
            w = gate * (0.5 * a1 * (1.0 + lax.erf(a1 * (2.0 ** -0.5))))
            w8 = jnp.broadcast_to(w, (8, PEER_SLOTS)).astype(bf16)
        return w8

    def prologue(t, carry):
        for j in range(nct):
            issue(t, t, j)
        return carry

    lax.fori_loop(0, PEER_AHEAD, prologue, 0)
    w8 = step(-1, jnp.zeros((8, PEER_SLOTS), bf16), False, True, False)
    w8 = lax.fori_loop(0, PEER_TB - PEER_AHEAD, lambda t, w: step(t, w, True, True, True), w8)
    w8 = lax.fori_loop(PEER_TB - PEER_AHEAD, PEER_TB - 1, lambda t, w: step(t, w, False, True, True), w8)
    step(PEER_TB - 1, w8, False, False, True)
    o_ref[...] = x_ref[...] + mod_ref[0, 0, 5:6, :] * o_ref[...]


def _mod_index(tokens_per_step, ta, n_ctx, first_step=0):
    spb = ta // tokens_per_step
    cs = n_ctx // tokens_per_step
    return lambda i: ((i + first_step) // spb, jnp.where((i + first_step) % spb < cs, 0, 1), 0, 0)


def _experts(idx, h, g, xflat, mod, tab, ta, n_ctx, nblocks):
    n, d = xflat.shape
    tokb = lambda width: pl.BlockSpec((PEER_TB, width), lambda i: (i, 0))
    return pl.pallas_call(
        _expert_kernel,
        grid=(nblocks,),
        in_specs=[pl.BlockSpec((PEER_TB, PEER_SLOTS), lambda i: (i, 0), memory_space=pltpu.SMEM),
                  tokb(d), tokb(PEER_SLOTS), tokb(d),
                  pl.BlockSpec((1, 1, 6, d), _mod_index(PEER_TB, ta, n_ctx)),
                  pl.BlockSpec(memory_space=pl.ANY)],
        out_specs=tokb(d),
        out_shape=jax.ShapeDtypeStruct((n, d), f32),
        input_output_aliases={3: 0},
        scratch_shapes=[pltpu.VMEM((PEER_NBUF, PEER_SLOTS // 8, d // LANE, 8, LANE), jnp.uint32),
                        pltpu.SemaphoreType.DMA((PEER_NBUF,))],
        compiler_params=pltpu.CompilerParams(dimension_semantics=("arbitrary",), vmem_limit_bytes=VMEM_LIMIT_BYTES),
        name="peer_experts",
    )(idx, h, g, xflat, mod, tab)


def _sc_gather(table, idx):
    info = plsc.get_sparse_core_info()
    nc = info.num_cores
    nw = nc * info.num_subcores
    r = idx.shape[0]
    per_w = r // nw
    nch = per_w // SC_CHUNK
    assert per_w * nw == r and nch * SC_CHUNK == per_w and nch % 2 == 0
    rows_shape = (SC_CHUNK,) + table.shape[1:]
    mesh = plsc.VectorSubcoreMesh(core_axis_name="c", subcore_axis_name="s")

    @functools.partial(
        pl.kernel, mesh=mesh, out_type=jax.ShapeDtypeStruct((r,) + table.shape[1:], table.dtype),
        scratch_types=[pltpu.VMEM((2, SC_CHUNK), i32), pltpu.VMEM((2,) + rows_shape, table.dtype),
                       pltpu.SemaphoreType.DMA((2,)), pltpu.SemaphoreType.DMA((2,))])
    def gather_kernel(table_hbm, idx_hbm, out_hbm, idx_v, rows_v, gsem, wsem):
        base = (lax.axis_index("s") * nc + lax.axis_index("c")) * per_w

        def rows_of(j):
            return pl.ds(pl.multiple_of(base + j * SC_CHUNK, 8), SC_CHUNK)

        def gather(j, b):
            pltpu.sync_copy(idx_hbm.at[rows_of(j)], idx_v.at[b])
            return pltpu.make_async_copy(table_hbm.at[idx_v.at[b]], rows_v.at[b], gsem.at[b])

        def write(j, b):
            return pltpu.make_async_copy(rows_v.at[b], out_hbm.at[rows_of(j)], wsem.at[b])

        gather(0, 0).start()

        @pl.loop(0, nch, step=2)
        def _(j0):
            for b in (0, 1):
                j = j0 + b

                @pl.when(j >= 1)
                def _():
                    write(j - 1, 1 - b).wait()

                @pl.when(j + 1 < nch)
                def _():
                    gather(j + 1, 1 - b).start()

                pltpu.make_async_copy(table_hbm.at[idx_v.at[b]], rows_v.at[b], gsem.at[b]).wait()
                write(j, b).start()

        write(nch - 1, 1).wait()

    return gather_kernel(table, idx)


def _expert_dense_kernel(st_ref, h_ref, g_ref, x_ref, mod_ref, o_ref):
    d = h_ref.shape[-1]
    nct = d // LANE
    groups = PEER_SLOTS // 8
    rows_per_token = PEER_SLOTS * nct

    def cols(t, c, high):
        base = t * rows_per_token
        tile = lambda kg, cc: st_ref[pl.ds(base + kg * 8 * nct + cc, 8, stride=nct), :]
        words = jnp.concatenate([jnp.concatenate([tile(kg, c), tile(kg, c + 1)], axis=1) for kg in range(groups)], axis=0)
        bits = (words & jnp.uint32(0xFFFF0000)) if high else (words << 16)
        return lax.bitcast_convert_type(bits, f32).astype(bf16)

    def u_stage(t):
        h8 = jnp.broadcast_to(h_ref[pl.ds(t, 1), :], (8, d)).astype(bf16)
        a = jnp.zeros((8, PEER_SLOTS), f32)
        for c in range(0, nct, 2):
            a = a + lax.dot_general(h8[:, c * LANE:(c + 2) * LANE], cols(t, c, True), _NT, preferred_element_type=f32)
        a1 = a[0:1, :]
        w = g_ref[pl.ds(t, 1), :] * (0.5 * a1 * (1.0 + lax.erf(a1 * (2.0 ** -0.5))))
        return jnp.broadcast_to(w, (8, PEER_SLOTS)).astype(bf16)

    def v_stage(t, w8):
        o_ref[pl.ds(t, 1), :] = jnp.concatenate(
            [jnp.dot(w8, cols(t, c, False), preferred_element_type=f32)[0:1, :] for c in range(0, nct, 2)], axis=1)

    def body(t, w8):
        w_next = u_stage(t + 1)
        v_stage(t, w8)
        return w_next

    w8 = lax.fori_loop(0, PEER_DENSE_TB - 1, body, u_stage(0), unroll=3)
    v_stage(PEER_DENSE_TB - 1, w8)
    o_ref[...] = x_ref[...] + mod_ref[0, 0, 5:6, :] * o_ref[...]


def _experts_dense(staged, h, g, xflat, mod, ta, n_ctx, first_block):
    n, d = xflat.shape
    rows_per_token = PEER_SLOTS * (d // LANE)
    steps = staged.shape[0] // (PEER_DENSE_TB * rows_per_token)
    off = first_block * PEER_TB // PEER_DENSE_TB
    tokb = lambda width: pl.BlockSpec((PEER_DENSE_TB, width), lambda i: (i + off, 0))
    return pl.pallas_call(
        _expert_dense_kernel,
        grid=(steps,),
        in_specs=[pl.BlockSpec((PEER_DENSE_TB * rows_per_token, LANE), lambda i: (i, 0)),
                  tokb(d), tokb(PEER_SLOTS), tokb(d),
                  pl.BlockSpec((1, 1, 6, d), _mod_index(PEER_DENSE_TB, ta, n_ctx, off))],
        out_specs=tokb(d),
        out_shape=jax.ShapeDtypeStruct((n, d), f32),
        input_output_aliases={3: 0},
        compiler_params=pltpu.CompilerParams(dimension_semantics=("arbitrary",), vmem_limit_bytes=VMEM_LIMIT_BYTES),
        name="peer_experts_dense",
    )(staged, h, g, xflat, mod)


def _expert_table(u, v):
    half = lambda a: lax.bitcast_convert_type(a.astype(bf16), jnp.uint16).astype(jnp.uint32)
    return (half(u) << 16) | half(v)


def _final_norm_kernel(x_ref, w_ref, o_ref):
    x = x_ref[0]
    o_ref[0] = x * lax.rsqrt(jnp.mean(x * x, axis=-1, keepdims=True) + EPS) * w_ref[...]


def _final_norm(xall, w, n_ctx):
    nb, ta, d = xall.shape
    off = n_ctx // TM
    return pl.pallas_call(
        _final_norm_kernel,
        grid=(nb, (ta - n_ctx) // TM),
        in_specs=[pl.BlockSpec((1, TM, d), lambda b, i: (b, i + off, 0)), pl.BlockSpec((1, d), lambda b, i: (0, 0))],
        out_specs=pl.BlockSpec((1, TM, d), lambda b, i: (b, i, 0)),
        out_shape=jax.ShapeDtypeStruct((nb, ta - n_ctx, d), f32),
        compiler_params=pltpu.CompilerParams(dimension_semantics=("arbitrary", "arbitrary")),
        name="final_norm",
    )(xall, w.reshape(1, d))


def _rope_tables(n_ctx, seq):
    nf = HEAD_DIM // 4
    pos = np.arange(seq)
    inv = ROPE_BASE ** (-np.arange(nf, dtype=np.float32) / nf)
    ar = (pos // GRID_W).astype(np.float32)[:, None] * inv
    ac = (pos % GRID_W).astype(np.float32)[:, None] * inv
    cos = np.concatenate([np.cos(ar), np.cos(ar), np.cos(ac), np.cos(ac)], axis=-1)
    sin = np.concatenate([-np.sin(ar), np.sin(ar), -np.sin(ac), np.sin(ac)], axis=-1)
    cos = np.concatenate([np.ones((n_ctx, HEAD_DIM), np.float32), cos.astype(np.float32)], axis=0)
    sin = np.concatenate([np.zeros((n_ctx, HEAD_DIM), np.float32), sin.astype(np.float32)], axis=0)
    return jnp.asarray(np.stack([np.tile(cos, (1, 2)), np.tile(sin, (1, 2))]))


def _tri_tables():
    r = np.arange(TM)[:, None]
    c = np.arange(TM)[None, :]
    same = (r // HG_SUB) == (c // HG_SUB)
    return jnp.asarray(np.stack([same & (c <= r), same & (c >= r)]).astype(np.float32), dtype=bf16)


def _block_diag_ones():
    r = np.arange(HG_WIDTH)
    return jnp.asarray((r[:, None] // HEAD_DIM == r[None, :] // HEAD_DIM).astype(np.float32), dtype=bf16)


def _swa_perm():
    return np.concatenate([np.arange(h * HEAD_DIM, (h + 1) * HEAD_DIM) for h in SWA_HEAD_ORDER])


def kernel(x, c, ctx, c_ctx, w_mod, b_mod, norm_mix_w, norm_ffn_w, w_in, hgrn_lb_logits, hgrn_norm_w, swa_sink, nat_rpb,
           w_out, peer_wq, peer_subkeys, peer_u, peer_v, final_norm_w):
    nb, seq, d = x.shape
    n_ctx = ctx.shape[1]
    depth = w_in.shape[0]
    ta = n_ctx + seq
    assert n_ctx == TM and seq % TM == 0 and seq % (GRID_W * NAT_ROWS) == 0 and d % LANE == 0
    assert w_in.shape[-1] == PROJ_WIDTH and peer_wq.shape[-1] == PEER_HEADS * PEER_DK

    mods = _mods(c, c_ctx, w_mod, b_mod)
    cs = _rope_tables(n_ctx, seq)
    tri = _tri_tables()
    bd = _block_diag_ones()
    perm = _swa_perm()
    q0 = 5 * HG_WIDTH
    col_perm = np.concatenate([np.arange(q0), q0 + perm, np.arange(q0 + SWA_Q, PROJ_WIDTH)])
    w_in_b = w_in[:, :, col_perm].astype(bf16)
    w_out_b = w_out.astype(bf16)
    wa = w_out_b[:, :HG_WIDTH]
    wb = w_out_b[:, HG_WIDTH:HG_WIDTH + SWA_Q][:, perm]
    wn = w_out_b[:, HG_WIDTH + SWA_Q:]
    wq_b = peer_wq.astype(bf16)
    sub_b = peer_subkeys.astype(bf16)
    lb = jnp.cumsum(jax.nn.softmax(hgrn_lb_logits.astype(f32), axis=1), axis=1)
    lb = lb - lb[:, :1]
    gp = jnp.stack([jnp.log(lb[0]), jnp.log1p(-lb[0]), 1.0 - lb[0], jnp.log(lb[1]), jnp.log1p(-lb[1]), 1.0 - lb[1],
                    jnp.zeros_like(lb[0]), jnp.zeros_like(lb[0])], axis=1)
    sink_rows = jnp.zeros((depth, 8, LANE), f32).at[:, :SWA_HEADS].set(
        jnp.broadcast_to(swa_sink.astype(f32)[:, list(SWA_HEAD_ORDER), None], (depth, SWA_HEADS, LANE)))

    words = _expert_table(peer_u, peer_v)
    n_exp = words.shape[1]
    tab = words.reshape(depth, n_exp, d // LANE, 1, LANE)
    tab_sc = words.reshape(depth, n_exp, d // LANE, LANE)
    n_tc = nb * ta // PEER_TB - PEER_SC_BLOCKS

    xall = jnp.concatenate([ctx, x], axis=1)
    for l in range(depth):
        mod = mods[l]
        hq, cum, kk, hv, sg, sq, sk, sv, nq, nk, nv = _proj(xall, norm_mix_w[l].reshape(1, d), mod, w_in_b[l], cs, gp[l], tri)
        o = _hgrn_scan(hq, cum, kk, hv, bd)
        so = _swa(sq, sk, sv, sink_rows[l], n_ctx)
        no = _nat(nq, nk, nv, _nat_bias_tables(nat_rpb[l]), n_ctx)
        xall = _outproj(o, sg, hgrn_norm_w[l].reshape(1, HG_WIDTH), bd, so, no, wa[l], wb[l], wn[l], mod, xall)
        h2, idx, g = _route(xall, norm_ffn_w[l].reshape(1, d), mod, wq_b[l], sub_b[l])
        idx, h2, g = idx.reshape(nb * ta, PEER_SLOTS), h2.reshape(nb * ta, d), g.reshape(nb * ta, PEER_SLOTS)
        staged = _sc_gather(tab_sc[l], idx[n_tc * PEER_TB:].reshape(-1))
        xf = _experts(idx, h2, g, xall.reshape(nb * ta, d), mod, tab[l], ta, n_ctx, n_tc)
        xf = _experts_dense(staged.reshape(-1, LANE), h2, g, xf, mod, ta, n_ctx, n_tc)
        xall = xf.reshape(nb, ta, d)
    return _final_norm(xall, final_norm_w, n_ctx)
```

```python
import functools

import numpy as np
import jax
import jax.numpy as jnp
from jax import lax
from jax.experimental import pallas as pl
from jax.experimental.pallas import tpu as pltpu
from jax.experimental.pallas import tpu_sc as plsc

f32 = jnp.float32
bf16 = jnp.bfloat16
i32 = jnp.int32

LANE = 128
VMEM_LIMIT_BYTES = 56 * 1024 * 1024

GRID_W = 64
HEAD_DIM = 64
EPS = 1e-6
HG_HEADS = 4
HG_WIDTH = 256
HG_SUB = 16
SWA_HEADS = 6
SWA_KV_HEADS = 2
SWA_WINDOW = 128
SWA_BLOCK = 128
ROPE_BASE = 10000.0
NAT_HEADS = 6
NAT_ROWS = 8
NAT_COLS = 16
SWA_Q = SWA_HEADS * HEAD_DIM
SWA_KV = SWA_KV_HEADS * HEAD_DIM
NAT_W = NAT_HEADS * HEAD_DIM
PROJ_WIDTH = 5 * HG_WIDTH + SWA_Q + 2 * SWA_KV + 3 * NAT_W
PEER_HEADS = 8
PEER_NKEYS = 128
PEER_DK = 256
PEER_TOPK = 16
PEER_SLOTS = PEER_HEADS * PEER_TOPK

TM = 256
PEER_TB = 128
PEER_NBUF = 8
PEER_AHEAD = PEER_NBUF - 1
PEER_SC_BLOCKS = 120
PEER_DENSE_TB = 16
SC_CHUNK = 32
NEG = -1e30

SWA_HEAD_ORDER = (0, 3, 1, 4, 2, 5)

_NT = (((1,), (1,)), ((), ()))
_TN = (((0,), (0,)), ((), ()))


def _silu(x):
    return x * (1.0 / (1.0 + jnp.exp(-x)))


def _split2(x):
    hi = x.astype(bf16)
    lo = (x - hi.astype(f32)).astype(bf16)
    return hi, lo


def _mods_kernel(sc_ref, w_ref, b_ref, o_ref):
    s = _silu(sc_ref[...])
    o_ref[0] = jnp.dot(s.astype(bf16), w_ref[0].astype(bf16), preferred_element_type=f32) + b_ref[0]


def _mods(c, c_ctx, w_mod, b_mod):
    depth, d, d6 = w_mod.shape
    nb = c.shape[0]
    rows = jnp.zeros((8, d), f32).at[:nb].set(c).at[nb].set(c_ctx)
    tn = d6 // 4
    out = pl.pallas_call(
        _mods_kernel,
        grid=(depth, 4),
        in_specs=[pl.BlockSpec((8, d), lambda l, j: (0, 0)),
                  pl.BlockSpec((1, d, tn), lambda l, j: (l, 0, j)),
                  pl.BlockSpec((1, 1, tn), lambda l, j: (l, 0, j))],
        out_specs=pl.BlockSpec((1, 8, tn), lambda l, j: (l, 0, j)),
        out_shape=jax.ShapeDtypeStruct((depth, 8, d6), f32),
        compiler_params=pltpu.CompilerParams(dimension_semantics=("arbitrary", "arbitrary"),
                                             vmem_limit_bytes=VMEM_LIMIT_BYTES),
        name="adaln_mods",
    )(rows, w_mod, b_mod.reshape(depth, 1, d6))
    lat = out[:, :nb].reshape(depth, nb, 6, d)
    ctx = jnp.broadcast_to(out[:, nb].reshape(depth, 1, 6, d), (depth, nb, 6, d))
    return jnp.stack([ctx, lat], axis=2)


def _norm_mod(x, nw, shift, scale):
    y = x * lax.rsqrt(jnp.mean(x * x, axis=-1, keepdims=True) + EPS) * nw
    return y * (1.0 + scale) + shift


def _rope(t, cos, sins):
    lane = lax.broadcasted_iota(i32, t.shape, 1)
    sw = jnp.where((lane % 32) < 16, pltpu.roll(t, LANE - 16, 1), pltpu.roll(t, 16, 1))
    return t * cos + sw * sins


def _hgrn_gates(z, log_lb, log1m_lb, one_m_lb):
    log_sig = jnp.minimum(z, 0.0) - jnp.log1p(jnp.exp(-jnp.abs(z)))
    c = log1m_lb + log_sig
    m = jnp.maximum(log_lb, c)
    log_f = m + jnp.log1p(jnp.exp(-jnp.abs(log_lb - c)))
    k = one_m_lb * (1.0 / (1.0 + jnp.exp(z)))
    return log_f, k


def _cumsum16(tri, log_f):
    h1 = log_f.astype(bf16)
    r1 = log_f - h1.astype(f32)
    h2 = r1.astype(bf16)
    h3 = (r1 - h2.astype(f32)).astype(bf16)
    return (jnp.dot(tri, h1, preferred_element_type=f32) + jnp.dot(tri, h2, preferred_element_type=f32)
            + jnp.dot(tri, h3, preferred_element_type=f32))


def _proj_kernel(x_ref, nw_ref, mod_ref, w_ref, cs_ref, gp_ref, tri_ref,
                 hq_ref, cum_ref, kk_ref, hv_ref, sg_ref, sq_ref, sk_ref, sv_ref, nq_ref, nk_ref, nv_ref):
    h = _norm_mod(x_ref[0], nw_ref[...], mod_ref[0, 0, 0:1, :], mod_ref[0, 0, 1:2, :])
    p = jnp.dot(h.astype(bf16), w_ref[...], preferred_element_type=f32)
    w = HG_WIDTH
    hq_ref[0] = _silu(p[:, 0:w])
    for d in range(2):
        log_f, k = _hgrn_gates(p[:, (1 + d) * w:(2 + d) * w], gp_ref[3 * d:3 * d + 1, :],
                               gp_ref[3 * d + 1:3 * d + 2, :], gp_ref[3 * d + 2:3 * d + 3, :])
        cum_ref[d, 0] = _cumsum16(tri_ref[d], log_f)
        kk_ref[d, 0] = k
    hv_ref[0] = p[:, 3 * w:4 * w]
    sg_ref[0] = _silu(p[:, 4 * w:5 * w])
    cos, sins = cs_ref[0], cs_ref[1]
    o = 5 * w
    scale = HEAD_DIM ** -0.5
    for j in range(SWA_Q // LANE):
        sq_ref[0, :, j * LANE:(j + 1) * LANE] = (_rope(p[:, o + j * LANE:o + (j + 1) * LANE], cos, sins) * scale).astype(bf16)
    o += SWA_Q
    sk_ref[0] = _rope(p[:, o:o + SWA_KV], cos, sins).astype(bf16)
    o += SWA_KV
    sv_ref[0] = p[:, o:o + SWA_KV].astype(bf16)
    o += SWA_KV
    nq_ref[0] = (p[:, o:o + NAT_W] * scale).astype(bf16)
    o += NAT_W
    nk_ref[0] = p[:, o:o + NAT_W].astype(bf16)
    o += NAT_W
    nv_ref[0] = p[:, o:o + NAT_W].astype(bf16)


def _proj(xall, nw, mod, w_in, cs, gp, tri):
    nb, ta, d = xall.shape
    nt = ta // TM
    tok = lambda width, dt: jax.ShapeDtypeStruct((nb, ta, width), dt)
    tok2 = jax.ShapeDtypeStruct((2, nb, ta, HG_WIDTH), f32)
    bs = lambda width: pl.BlockSpec((1, TM, width), lambda b, i: (b, i, 0))
    bs2 = pl.BlockSpec((2, 1, TM, HG_WIDTH), lambda b, i: (0, b, i, 0))
    return pl.pallas_call(
        _proj_kernel,
        grid=(nb, nt),
        in_specs=[pl.BlockSpec((1, TM, d), lambda b, i: (b, i, 0)),
                  pl.BlockSpec((1, d), lambda b, i: (0, 0)),
                  pl.BlockSpec((1, 1, 6, d), lambda b, i: (b, jnp.minimum(i, 1), 0, 0)),
                  pl.BlockSpec((d, PROJ_WIDTH), lambda b, i: (0, 0)),
                  pl.BlockSpec((2, TM, LANE), lambda b, i: (0, i, 0)),
                  pl.BlockSpec((8, HG_WIDTH), lambda b, i: (0, 0)),
                  pl.BlockSpec((2, TM, TM), lambda b, i: (0, 0, 0))],
        out_specs=[bs(HG_WIDTH), bs2, bs2, bs(HG_WIDTH), bs(HG_WIDTH), bs(SWA_Q), bs(SWA_KV), bs(SWA_KV),
                   bs(NAT_W), bs(NAT_W), bs(NAT_W)],
        out_shape=[tok(HG_WIDTH, f32), tok2, tok2, tok(HG_WIDTH, f32), tok(HG_WIDTH, f32), tok(SWA_Q, bf16),
                   tok(SWA_KV, bf16), tok(SWA_KV, bf16), tok(NAT_W, bf16), tok(NAT_W, bf16), tok(NAT_W, bf16)],
        compiler_params=pltpu.CompilerParams(dimension_semantics=("arbitrary", "arbitrary"),
                                             vmem_limit_bytes=VMEM_LIMIT_BYTES),
        name="norm_in_proj",
    )(xall, nw, mod, w_in, cs, gp, tri)


def _hgrn_kernel(q_ref, cum_ref, k_ref, v_ref, bd_ref, o_ref, st_ref):
    d = pl.program_id(1)
    fwd = d == 0

    @pl.when(pl.program_id(2) == 0)
    def _():
        st_ref[...] = jnp.zeros_like(st_ref)

    bd = bd_ref[...]
    bd_mask = bd > 0
    sgn = jnp.where(fwd, 1, -1)
    t_sgn = lax.broadcasted_iota(i32, (HG_SUB, 1), 0) * sgn
    nsteps = TM // HG_SUB

    def step(i, carry):
        sc = jnp.where(fwd, i, nsteps - 1 - i)
        r0 = pl.multiple_of(sc * HG_SUB, HG_SUB)
        q = q_ref[0, pl.ds(r0, HG_SUB), :]
        b = cum_ref[0, 0, pl.ds(r0, HG_SUB), :]
        k = k_ref[0, 0, pl.ds(r0, HG_SUB), :]
        v = v_ref[0, pl.ds(r0, HG_SUB), :]
        b_end = jnp.where(fwd, b[HG_SUB - 1:HG_SUB, :], b[0:1, :])
        st = st_ref[...]
        o = lax.dot_general((q * jnp.exp(b)).astype(bf16), st.astype(bf16), _NT, preferred_element_type=f32)
        rows = []
        for s in range(HG_SUB):
            valid = t_sgn >= s * sgn
            decay = jnp.exp(jnp.minimum(b - b[s:s + 1, :], 0.0))
            rows.append(jnp.where(valid, decay * q * k[s:s + 1, :], 0.0))
        hi, lo = _split2(jnp.concatenate(rows, axis=0))
        pe = jnp.dot(hi, bd, preferred_element_type=f32) + jnp.dot(lo, bd, preferred_element_type=f32)
        for s in range(HG_SUB):
            o = o + pe[s * HG_SUB:(s + 1) * HG_SUB, :] * v[s:s + 1, :]
        o_ref[0, 0, pl.ds(r0, HG_SUB), :] = o
        kd = (k * jnp.exp(b_end - b)).astype(bf16)
        kv_t = lax.dot_general(v.astype(bf16), kd, _TN, preferred_element_type=f32)
        st_ref[...] = st * jnp.exp(b_end) + jnp.where(bd_mask, kv_t, 0.0)
        return carry

    lax.fori_loop(0, nsteps, step, 0, unroll=2)


def _hgrn_scan(hq, cum, kk, hv, bd):
    nb, ta, w = hq.shape
    nt = ta // TM

    def blk(d, j):
        return jnp.where(d == 0, j, jnp.where(j == 0, 0, nt - j))

    return pl.pallas_call(
        _hgrn_kernel,
        grid=(nb, 2, nt),
        in_specs=[pl.BlockSpec((1, TM, w), lambda b, d, j: (b, blk(d, j), 0)),
                  pl.BlockSpec((1, 1, TM, w), lambda b, d, j: (d, b, blk(d, j), 0)),
                  pl.BlockSpec((1, 1, TM, w), lambda b, d, j: (d, b, blk(d, j), 0)),
                  pl.BlockSpec((1, TM, w), lambda b, d, j: (b, blk(d, j), 0)),
                  pl.BlockSpec((w, w), lambda b, d, j: (0, 0))],
        out_specs=pl.BlockSpec((1, 1, TM, w), lambda b, d, j: (d, b, blk(d, j), 0)),
        out_shape=jax.ShapeDtypeStruct((2, nb, ta, w), f32),
        scratch_shapes=[pltpu.VMEM((w, w), f32)],
        compiler_params=pltpu.CompilerParams(dimension_semantics=("arbitrary", "arbitrary", "arbitrary")),
        name="hgrn_scan",
    )(hq, cum, kk, hv, bd)


def _pair_rows(q2):
    lane = lax.broadcasted_iota(i32, q2.shape, 1)
    zero = jnp.zeros_like(q2)
    return jnp.concatenate([jnp.where(lane < HEAD_DIM, q2, zero), jnp.where(lane >= HEAD_DIM, q2, zero)], axis=0)


def _pair_merge(o):
    m = o.shape[0] // 2
    lane = lax.broadcasted_iota(i32, (m, LANE), 1)
    return jnp.where(lane < HEAD_DIM, o[:m], o[m:])


def _joint_attention(s_loc, s_ctx, v_loc, v_ctx, sink=None):
    m = jnp.maximum(jnp.max(s_loc, axis=-1, keepdims=True), jnp.max(s_ctx, axis=-1, keepdims=True))
    if sink is not None:
        m = jnp.maximum(m, sink)
    p_loc = jnp.exp(s_loc - m)
    p_ctx = jnp.exp(s_ctx - m)
    den = jnp.sum(p_loc, axis=-1, keepdims=True) + jnp.sum(p_ctx, axis=-1, keepdims=True)
    if sink is not None:
        den = den + jnp.exp(sink - m)
    o = (jnp.dot(p_loc.astype(bf16), v_loc, preferred_element_type=f32)
         + jnp.dot(p_ctx.astype(bf16), v_ctx, preferred_element_type=f32))
    return o / den


def _swa_kernel(q_ref, kp_ref, kc_ref, kn_ref, vp_ref, vc_ref, vn_ref, kx_ref, vx_ref, sink_ref, o_ref, *, n_ctx_blocks, seq):
    g = pl.program_id(1)
    i = g - n_ctx_blocks
    k_loc = jnp.concatenate([kp_ref[0], kc_ref[0], kn_ref[0]], axis=0)
    v_loc = jnp.concatenate([vp_ref[0], vc_ref[0], vn_ref[0]], axis=0)
    row = lax.broadcasted_iota(i32, (SWA_BLOCK, 3 * SWA_BLOCK), 0)
    col = lax.broadcasted_iota(i32, (SWA_BLOCK, 3 * SWA_BLOCK), 1)
    rel = col - SWA_BLOCK - row
    kpos = (i - 1) * SWA_BLOCK + col
    ok = (jnp.abs(rel) <= SWA_WINDOW) & (kpos >= 0) & (kpos < seq) & (i >= 0)
    pen = jnp.where(ok, 0.0, NEG)
    pen = jnp.concatenate([pen, pen], axis=0)
    for j in range(SWA_Q // LANE):
        q = _pair_rows(q_ref[0, :, j * LANE:(j + 1) * LANE])
        s_loc = lax.dot_general(q, k_loc, _NT, preferred_element_type=f32) + pen
        s_ctx = lax.dot_general(q, kx_ref[0], _NT, preferred_element_type=f32)
        sink = jnp.concatenate([jnp.broadcast_to(sink_ref[2 * j:2 * j + 1, 0:1], (SWA_BLOCK, 1)),
                                jnp.broadcast_to(sink_ref[2 * j + 1:2 * j + 2, 0:1], (SWA_BLOCK, 1))], axis=0)
        o = _joint_attention(s_loc, s_ctx, v_loc, vx_ref[0], sink)
        o_ref[0, :, j * LANE:(j + 1) * LANE] = _pair_merge(o).astype(bf16)


def _swa(sq, sk, sv, sink_rows, n_ctx):
    nb, ta, _ = sq.shape
    ncb = n_ctx // SWA_BLOCK
    nblk = ta // SWA_BLOCK
    nlat = nblk - ncb
    prev = lambda b, g: (b, jnp.clip(g - ncb - 1, 0, nlat - 1) + ncb, 0)
    cur = lambda b, g: (b, g, 0)
    nxt = lambda b, g: (b, jnp.clip(g - ncb + 1, 0, nlat - 1) + ncb, 0)
    kvs = lambda f: pl.BlockSpec((1, SWA_BLOCK, SWA_KV), f)
    ctx = pl.BlockSpec((1, n_ctx, SWA_KV), lambda b, g: (b, 0, 0))
    return pl.pallas_call(
        functools.partial(_swa_kernel, n_ctx_blocks=ncb, seq=ta - n_ctx),
        grid=(nb, nblk),
        in_specs=[pl.BlockSpec((1, SWA_BLOCK, SWA_Q), cur), kvs(prev), kvs(cur), kvs(nxt), kvs(prev), kvs(cur), kvs(nxt),
                  ctx, ctx, pl.BlockSpec((8, LANE), lambda b, g: (0, 0))],
        out_specs=pl.BlockSpec((1, SWA_BLOCK, SWA_Q), cur),
        out_shape=jax.ShapeDtypeStruct((nb, ta, SWA_Q), bf16),
        compiler_params=pltpu.CompilerParams(dimension_semantics=("arbitrary", "arbitrary")),
        name="swa_attention",
    )(sq, sk, sk, sk, sv, sv, sv, sk, sv, sink_rows)


def _nat_kernel(q_ref, k_ref, v_ref, bias_ref, o_ref, *, n_ctx, n_rows):
    g = pl.program_id(1)
    is_ctx = g == 0
    ctx_pen = jnp.where(is_ctx, NEG, 0.0)
    rows_per_tile = TM // GRID_W
    nwin = NAT_ROWS * GRID_W
    for rr in range(rows_per_tile):
        r = jnp.maximum((g - 1) * rows_per_tile + rr, 0)
        rs = jnp.clip(r - NAT_ROWS // 2, 0, n_rows - NAT_ROWS)
        dr0 = rs - r + NAT_ROWS - 1
        start = pl.multiple_of(n_ctx + rs * GRID_W, GRID_W)
        for p in range(NAT_W // LANE):
            ls = slice(p * LANE, (p + 1) * LANE)
            q = _pair_rows(q_ref[0, rr * GRID_W:(rr + 1) * GRID_W, ls])
            s_loc = (lax.dot_general(q, k_ref[0, pl.ds(start, nwin), ls], _NT, preferred_element_type=f32)
                     + bias_ref[p, dr0] + ctx_pen)
            s_ctx = lax.dot_general(q, k_ref[0, 0:n_ctx, ls], _NT, preferred_element_type=f32)
            o = _joint_attention(s_loc, s_ctx, v_ref[0, pl.ds(start, nwin), ls], v_ref[0, 0:n_ctx, ls])
            o_ref[0, rr * GRID_W:(rr + 1) * GRID_W, ls] = _pair_merge(o).astype(bf16)


def _nat(nq, nk, nv, bias, n_ctx):
    nb, ta, w = nq.shape
    nt = ta // TM
    whole = pl.BlockSpec((1, ta, w), lambda b, g: (b, 0, 0), pipeline_mode=pl.Buffered(1))
    return pl.pallas_call(
        functools.partial(_nat_kernel, n_ctx=n_ctx, n_rows=(ta - n_ctx) // GRID_W),
        grid=(nb, nt),
        in_specs=[pl.BlockSpec((1, TM, w), lambda b, g: (b, g, 0)), whole, whole,
                  pl.BlockSpec(bias.shape, lambda b, g: (0, 0, 0, 0), pipeline_mode=pl.Buffered(1))],
        out_specs=pl.BlockSpec((1, TM, w), lambda b, g: (b, g, 0)),
        out_shape=jax.ShapeDtypeStruct((nb, ta, w), bf16),
        compiler_params=pltpu.CompilerParams(dimension_semantics=("arbitrary", "arbitrary"),
                                             vmem_limit_bytes=VMEM_LIMIT_BYTES),
        name="nat_attention",
    )(nq, nk, nv, bias)


def _nat_bias_tables(rpb):
    c = np.arange(GRID_W)[:, None]
    kc = np.arange(GRID_W)[None, :]
    ws = np.clip(c - NAT_COLS // 2, 0, GRID_W - NAT_COLS)
    valid = (kc >= ws) & (kc < ws + NAT_COLS)
    col_idx = np.clip(kc - c + NAT_COLS - 1, 0, 2 * NAT_COLS - 2)
    dr = np.arange(NAT_ROWS)[:, None] + np.arange(NAT_ROWS)[None, :]
    t = rpb.astype(f32)[:, dr][:, :, :, col_idx]
    t = jnp.where(jnp.asarray(valid)[None, None, None], t, NEG)
    t = t.transpose(0, 1, 3, 2, 4).reshape(NAT_HEADS // 2, 2, NAT_ROWS, GRID_W, NAT_ROWS * GRID_W)
    return t.transpose(0, 2, 1, 3, 4).reshape(NAT_HEADS // 2, NAT_ROWS, 2 * GRID_W, NAT_ROWS * GRID_W)


def _outproj_kernel(o_ref, sg_ref, hnw_ref, bd_ref, so_ref, no_ref, wa_ref, wb_ref, wn_ref, mod_ref, x_ref, y_ref):
    oo = o_ref[0, 0] + o_ref[1, 0]
    hi, lo = _split2(oo * oo)
    bd = bd_ref[...]
    msq = (jnp.dot(hi, bd, preferred_element_type=f32) + jnp.dot(lo, bd, preferred_element_type=f32)) * (1.0 / HEAD_DIM)
    a = oo * lax.rsqrt(msq + EPS) * hnw_ref[...] * sg_ref[0]
    y = (jnp.dot(a.astype(bf16), wa_ref[...], preferred_element_type=f32)
         + jnp.dot(so_ref[0], wb_ref[...], preferred_element_type=f32)
         + jnp.dot(no_ref[0], wn_ref[...], preferred_element_type=f32))
    y_ref[0] = x_ref[0] + mod_ref[0, 0, 2:3, :] * y


def _outproj(o, sg, hnw, bd, so, no, wa, wb, wn, mod, xall):
    nb, ta, d = xall.shape
    nt = ta // TM
    bs = lambda width: pl.BlockSpec((1, TM, width), lambda b, i: (b, i, 0))
    full = lambda a: pl.BlockSpec(a.shape, lambda b, i: (0,) * a.ndim)
    return pl.pallas_call(
        _outproj_kernel,
        grid=(nb, nt),
        in_specs=[pl.BlockSpec((2, 1, TM, HG_WIDTH), lambda b, i: (0, b, i, 0)), bs(HG_WIDTH), full(hnw), full(bd),
                  bs(SWA_Q), bs(NAT_W), full(wa), full(wb), full(wn),
                  pl.BlockSpec((1, 1, 6, d), lambda b, i: (b, jnp.minimum(i, 1), 0, 0)), bs(d)],
        out_specs=bs(d),
        out_shape=jax.ShapeDtypeStruct(xall.shape, f32),
        compiler_params=pltpu.CompilerParams(dimension_semantics=("arbitrary", "arbitrary"),
                                             vmem_limit_bytes=VMEM_LIMIT_BYTES),
        name="out_proj",
    )(o, sg, hnw, bd, so, no, wa, wb, wn, mod, xall)


def _top16_rows(vals, payload=None):
    n = vals.shape[0]
    iota = lax.broadcasted_iota(i32, vals.shape, 0)
    best, picked = [], []
    for _ in range(PEER_TOPK):
        m = jnp.max(vals, axis=0, keepdims=True)
        idx = jnp.min(jnp.where(vals == m, iota, n), axis=0, keepdims=True)
        hit = iota == idx
        best.append(m)
        picked.append(idx if payload is None else jnp.max(jnp.where(hit, payload, -1), axis=0, keepdims=True))
        vals = jnp.where(hit, -jnp.inf, vals)
    return jnp.concatenate(best, axis=0), jnp.concatenate(picked, axis=0)


def _product_candidates(first, second, combine):
    half = PEER_TOPK // 2
    rows = [combine(first[0:1, :], second)]
    rows += [combine(first[a:a + 1, :], second[0:half, :]) for a in range(1, half)]
    rows.append(combine(first[half:, :], second[0:1, :]))
    return jnp.concatenate(rows, axis=0)


def _route_kernel(x_ref, nw_ref, mod_ref, wq_ref, sub_ref, h_ref, idx_ref, g_ref):
    h = _norm_mod(x_ref[0], nw_ref[...], mod_ref[0, 0, 3:4, :], mod_ref[0, 0, 4:5, :])
    h_ref[0] = h
    q = jnp.dot(h.astype(bf16), wq_ref[...], preferred_element_type=f32).astype(bf16)
    half = PEER_DK // 2
    ids, gates = [], []
    for hh in range(PEER_HEADS):
        top_s, top_i = [], []
        for p in range(2):
            c0 = (hh * 2 + p) * half
            s = lax.dot_general(sub_ref[hh, p], q[:, c0:c0 + half], _NT, preferred_element_type=f32)
            ts, ti = _top16_rows(s)
            top_s.append(ts)
            top_i.append(ti)
        cand_s = _product_candidates(top_s[0], top_s[1], lambda a, b: a + b)
        cand_i = _product_candidates(top_i[0], top_i[1], lambda a, b: a * PEER_NKEYS + b)
        best, eid = _top16_rows(cand_s, cand_i)
        ids.append(eid)
        e = jnp.exp(best - best[0:1, :])
        gates.append(e / jnp.sum(e, axis=0, keepdims=True))
    idx_ref[0] = jnp.concatenate(ids, axis=0).T
    g_ref[0] = jnp.concatenate(gates, axis=0).T


def _route(xall, nw, mod, wq, sub):
    nb, ta, d = xall.shape
    nt = ta // TM
    full = lambda a: pl.BlockSpec(a.shape, lambda b, i: (0,) * a.ndim)
    return pl.pallas_call(
        _route_kernel,
        grid=(nb, nt),
        in_specs=[pl.BlockSpec((1, TM, d), lambda b, i: (b, i, 0)), full(nw),
                  pl.BlockSpec((1, 1, 6, d), lambda b, i: (b, jnp.minimum(i, 1), 0, 0)), full(wq), full(sub)],
        out_specs=[pl.BlockSpec((1, TM, d), lambda b, i: (b, i, 0)),
                   pl.BlockSpec((1, TM, PEER_SLOTS), lambda b, i: (b, i, 0)),
                   pl.BlockSpec((1, TM, PEER_SLOTS), lambda b, i: (b, i, 0))],
        out_shape=[jax.ShapeDtypeStruct(xall.shape, f32), jax.ShapeDtypeStruct((nb, ta, PEER_SLOTS), i32),
                   jax.ShapeDtypeStruct((nb, ta, PEER_SLOTS), f32)],
        compiler_params=pltpu.CompilerParams(dimension_semantics=("arbitrary", "arbitrary"),
                                             vmem_limit_bytes=VMEM_LIMIT_BYTES),
        name="peer_route",
    )(xall, nw, mod, wq, sub)


def _expert_kernel(idx_ref, h_ref, g_ref, x_ref, mod_ref, tab_hbm, o_ref, buf, sem):
    d = h_ref.shape[-1]
    nct = d // LANE
    groups = PEER_SLOTS // 8

    per_piece = PEER_SLOTS // nct

    def issue(t, slot, piece):
        for k in range(piece * per_piece, (piece + 1) * per_piece):
            pltpu.make_async_copy(tab_hbm.at[idx_ref[t, k]], buf.at[slot, k // 8, :, pl.ds(k % 8, 1), :],
                                  sem.at[slot]).start(priority=k % 2)

    def wait(slot):
        pltpu.make_async_copy(buf.at[slot], buf.at[slot], sem.at[slot]).wait()

    def cols(slot, c, high):
        words = jnp.concatenate([jnp.concatenate([buf[slot, kg, c], buf[slot, kg, c + 1]], axis=1)
                                 for kg in range(groups)], axis=0)
        bits = (words & jnp.uint32(0xFFFF0000)) if high else (words << 16)
        return lax.bitcast_convert_type(bits, f32).astype(bf16)

    def step(t, w8, do_issue, do_u, do_v):
        nslot = lax.rem(t + 1, PEER_NBUF)
        slot = lax.rem(t, PEER_NBUF)
        islot = lax.rem(t + PEER_AHEAD, PEER_NBUF)
        if do_u:
            wait(nslot)
            h8 = jnp.broadcast_to(h_ref[pl.ds(t + 1, 1), :], (8, d)).astype(bf16)
            gate = g_ref[pl.ds(t + 1, 1), :]
            a = jnp.zeros((8, PEER_SLOTS), f32)
        ys = []
        for j in range(nct):
            if 2 * j < nct:
                c = 2 * j
                if do_u:
                    a = a + lax.dot_general(h8[:, c * LANE:(c + 2) * LANE], cols(nslot, c, True), _NT, preferred_element_type=f32)
                if do_v:
                    ys.append(jnp.dot(w8, cols(slot, c, False), preferred_element_type=f32)[0:1, :])
            if do_issue:
                issue(t + PEER_AHEAD, islot, j)
        if do_v:
            o_ref[pl.ds(t, 1), :] = jnp.concatenate(ys, axis=1)
        if do_u:
            a1 = a[0:1, :]
            w = gate * (0.5 * a1 * (1.0 + lax.erf(a1 * (2.0 ** -0.5))))
            w8 = jnp.broadcast_to(w, (8, PEER_SLOTS)).astype(bf16)
        return w8

    def prologue(t, carry):
        for j in range(nct):
            issue(t, t, j)
        return carry

    lax.fori_loop(0, PEER_AHEAD, prologue, 0)
    w8 = step(-1, jnp.zeros((8, PEER_SLOTS), bf16), False, True, False)
    w8 = lax.fori_loop(0, PEER_TB - PEER_AHEAD, lambda t, w: step(t, w, True, True, True), w8)
    w8 = lax.fori_loop(PEER_TB - PEER_AHEAD, PEER_TB - 1, lambda t, w: step(t, w, False, True, True), w8)
    step(PEER_TB - 1, w8, False, False, True)
    o_ref[...] = x_ref[...] + mod_ref[0, 0, 5:6, :] * o_ref[...]


def _mod_index(tokens_per_step, ta, n_ctx, first_step=0):
    spb = ta // tokens_per_step
    cs = n_ctx // tokens_per_step
    return lambda i: ((i + first_step) // spb, jnp.where((i + first_step) % spb < cs, 0, 1), 0, 0)


def _experts(idx, h, g, xflat, mod, tab, ta, n_ctx, nblocks):
    n, d = xflat.shape
    tokb = lambda width: pl.BlockSpec((PEER_TB, width), lambda i: (i, 0))
    return pl.pallas_call(
        _expert_kernel,
        grid=(nblocks,),
        in_specs=[pl.BlockSpec((PEER_TB, PEER_SLOTS), lambda i: (i, 0), memory_space=pltpu.SMEM),
                  tokb(d), tokb(PEER_SLOTS), tokb(d),
                  pl.BlockSpec((1, 1, 6, d), _mod_index(PEER_TB, ta, n_ctx)),
                  pl.BlockSpec(memory_space=pl.ANY)],
        out_specs=tokb(d),
        out_shape=jax.ShapeDtypeStruct((n, d), f32),
        input_output_aliases={3: 0},
        scratch_shapes=[pltpu.VMEM((PEER_NBUF, PEER_SLOTS // 8, d // LANE, 8, LANE), jnp.uint32),
                        pltpu.SemaphoreType.DMA((PEER_NBUF,))],
        compiler_params=pltpu.CompilerParams(dimension_semantics=("arbitrary",), vmem_limit_bytes=VMEM_LIMIT_BYTES),
        name="peer_experts",
    )(idx, h, g, xflat, mod, tab)


def _sc_gather(table, idx):
    info = plsc.get_sparse_core_info()
    nc = info.num_cores
    nw = nc * info.num_subcores
    r = idx.shape[0]
    per_w = r // nw
    nch = per_w // SC_CHUNK
    assert per_w * nw == r and nch * SC_CHUNK == per_w and nch % 2 == 0
    rows_shape = (SC_CHUNK,) + table.shape[1:]
    mesh = plsc.VectorSubcoreMesh(core_axis_name="c", subcore_axis_name="s")

    @functools.partial(
        pl.kernel, mesh=mesh, out_type=jax.ShapeDtypeStruct((r,) + table.shape[1:], table.dtype),
        scratch_types=[pltpu.VMEM((2, SC_CHUNK), i32), pltpu.VMEM((2,) + rows_shape, table.dtype),
                       pltpu.SemaphoreType.DMA((2,)), pltpu.SemaphoreType.DMA((2,))])
    def gather_kernel(table_hbm, idx_hbm, out_hbm, idx_v, rows_v, gsem, wsem):
        base = (lax.axis_index("s") * nc + lax.axis_index("c")) * per_w

        def rows_of(j):
            return pl.ds(pl.multiple_of(base + j * SC_CHUNK, 8), SC_CHUNK)

        def gather(j, b):
            pltpu.sync_copy(idx_hbm.at[rows_of(j)], idx_v.at[b])
            return pltpu.make_async_copy(table_hbm.at[idx_v.at[b]], rows_v.at[b], gsem.at[b])

        def write(j, b):
            return pltpu.make_async_copy(rows_v.at[b], out_hbm.at[rows_of(j)], wsem.at[b])

        gather(0, 0).start()

        @pl.loop(0, nch, step=2)
        def _(j0):
            for b in (0, 1):
                j = j0 + b

                @pl.when(j >= 1)
                def _():
                    write(j - 1, 1 - b).wait()

                @pl.when(j + 1 < nch)
                def _():
                    gather(j + 1, 1 - b).start()

                pltpu.make_async_copy(table_hbm.at[idx_v.at[b]], rows_v.at[b], gsem.at[b]).wait()
                write(j, b).start()

        write(nch - 1, 1).wait()

    return gather_kernel(table, idx)


def _expert_dense_kernel(st_ref, h_ref, g_ref, x_ref, mod_ref, o_ref):
    d = h_ref.shape[-1]
    nct = d // LANE
    groups = PEER_SLOTS // 8
    rows_per_token = PEER_SLOTS * nct

    def cols(t, c, high):
        base = t * rows_per_token
        tile = lambda kg, cc: st_ref[pl.ds(base + kg * 8 * nct + cc, 8, stride=nct), :]
        words = jnp.concatenate([jnp.concatenate([tile(kg, c), tile(kg, c + 1)], axis=1) for kg in range(groups)], axis=0)
        bits = (words & jnp.uint32(0xFFFF0000)) if high else (words << 16)
        return lax.bitcast_convert_type(bits, f32).astype(bf16)

    def u_stage(t):
        h8 = jnp.broadcast_to(h_ref[pl.ds(t, 1), :], (8, d)).astype(bf16)
        a = jnp.zeros((8, PEER_SLOTS), f32)
        for c in range(0, nct, 2):
            a = a + lax.dot_general(h8[:, c * LANE:(c + 2) * LANE], cols(t, c, True), _NT, preferred_element_type=f32)
        a1 = a[0:1, :]
        w = g_ref[pl.ds(t, 1), :] * (0.5 * a1 * (1.0 + lax.erf(a1 * (2.0 ** -0.5))))
        return jnp.broadcast_to(w, (8, PEER_SLOTS)).astype(bf16)

    def v_stage(t, w8):
        o_ref[pl.ds(t, 1), :] = jnp.concatenate(
            [jnp.dot(w8, cols(t, c, False), preferred_element_type=f32)[0:1, :] for c in range(0, nct, 2)], axis=1)

    def body(t, w8):
        w_next = u_stage(t + 1)
        v_stage(t, w8)
        return w_next

    w8 = lax.fori_loop(0, PEER_DENSE_TB - 1, body, u_stage(0), unroll=5)
    v_stage(PEER_DENSE_TB - 1, w8)
    o_ref[...] = x_ref[...] + mod_ref[0, 0, 5:6, :] * o_ref[...]


def _experts_dense(staged, h, g, xflat, mod, ta, n_ctx, first_block):
    n, d = xflat.shape
    rows_per_token = PEER_SLOTS * (d // LANE)
    steps = staged.shape[0] // (PEER_DENSE_TB * rows_per_token)
    off = first_block * PEER_TB // PEER_DENSE_TB
    tokb = lambda width: pl.BlockSpec((PEER_DENSE_TB, width), lambda i: (i + off, 0))
    return pl.pallas_call(
        _expert_dense_kernel,
        grid=(steps,),
        in_specs=[pl.BlockSpec((PEER_DENSE_TB * rows_per_token, LANE), lambda i: (i, 0)),
                  tokb(d), tokb(PEER_SLOTS), tokb(d),
                  pl.BlockSpec((1, 1, 6, d), _mod_index(PEER_DENSE_TB, ta, n_ctx, off))],
        out_specs=tokb(d),
        out_shape=jax.ShapeDtypeStruct((n, d), f32),
        input_output_aliases={3: 0},
        compiler_params=pltpu.CompilerParams(dimension_semantics=("arbitrary",), vmem_limit_bytes=VMEM_LIMIT_BYTES),
        name="peer_experts_dense",
    )(staged, h, g, xflat, mod)


def _expert_table(u, v):
    half = lambda a: lax.bitcast_convert_type(a.astype(bf16), jnp.uint16).astype(jnp.uint32)
    return (half(u) << 16) | half(v)


def _final_norm_kernel(x_ref, w_ref, o_ref):
    x = x_ref[0]
    o_ref[0] = x * lax.rsqrt(jnp.mean(x * x, axis=-1, keepdims=True) + EPS) * w_ref[...]


def _final_norm(xall, w, n_ctx):
    nb, ta, d = xall.shape
    off = n_ctx // TM
    return pl.pallas_call(
        _final_norm_kernel,
        grid=(nb, (ta - n_ctx) // TM),
        in_specs=[pl.BlockSpec((1, TM, d), lambda b, i: (b, i + off, 0)), pl.BlockSpec((1, d), lambda b, i: (0, 0))],
        out_specs=pl.BlockSpec((1, TM, d), lambda b, i: (b, i, 0)),
        out_shape=jax.ShapeDtypeStruct((nb, ta - n_ctx, d), f32),
        compiler_params=pltpu.CompilerParams(dimension_semantics=("arbitrary", "arbitrary")),
        name="final_norm",
    )(xall, w.reshape(1, d))


def _rope_tables(n_ctx, seq):
    nf = HEAD_DIM // 4
    pos = np.arange(seq)
    inv = ROPE_BASE ** (-np.arange(nf, dtype=np.float32) / nf)
    ar = (pos // GRID_W).astype(np.float32)[:, None] * inv
    ac = (pos % GRID_W).astype(np.float32)[:, None] * inv
    cos = np.concatenate([np.cos(ar), np.cos(ar), np.cos(ac), np.cos(ac)], axis=-1)
    sin = np.concatenate([-np.sin(ar), np.sin(ar), -np.sin(ac), np.sin(ac)], axis=-1)
    cos = np.concatenate([np.ones((n_ctx, HEAD_DIM), np.float32), cos.astype(np.float32)], axis=0)
    sin = np.concatenate([np.zeros((n_ctx, HEAD_DIM), np.float32), sin.astype(np.float32)], axis=0)
    return jnp.asarray(np.stack([np.tile(cos, (1, 2)), np.tile(sin, (1, 2))]))


def _tri_tables():
    r = np.arange(TM)[:, None]
    c = np.arange(TM)[None, :]
    same = (r // HG_SUB) == (c // HG_SUB)
    return jnp.asarray(np.stack([same & (c <= r), same & (c >= r)]).astype(np.float32), dtype=bf16)


def _block_diag_ones():
    r = np.arange(HG_WIDTH)
    return jnp.asarray((r[:, None] // HEAD_DIM == r[None, :] // HEAD_DIM).astype(np.float32), dtype=bf16)


def _swa_perm():
    return np.concatenate([np.arange(h * HEAD_DIM, (h + 1) * HEAD_DIM) for h in SWA_HEAD_ORDER])


def kernel(x, c, ctx, c_ctx, w_mod, b_mod, norm_mix_w, norm_ffn_w, w_in, hgrn_lb_logits, hgrn_norm_w, swa_sink, nat_rpb,
           w_out, peer_wq, peer_subkeys, peer_u, peer_v, final_norm_w):
    nb, seq, d = x.shape
    n_ctx = ctx.shape[1]
    depth = w_in.shape[0]
    ta = n_ctx + seq
    assert n_ctx == TM and seq % TM == 0 and seq % (GRID_W * NAT_ROWS) == 0 and d % LANE == 0
    assert w_in.shape[-1] == PROJ_WIDTH and peer_wq.shape[-1] == PEER_HEADS * PEER_DK

    mods = _mods(c, c_ctx, w_mod, b_mod)
    cs = _rope_tables(n_ctx, seq)
    tri = _tri_tables()
    bd = _block_diag_ones()
    perm = _swa_perm()
    q0 = 5 * HG_WIDTH
    col_perm = np.concatenate([np.arange(q0), q0 + perm, np.arange(q0 + SWA_Q, PROJ_WIDTH)])
    w_in_b = w_in[:, :, col_perm].astype(bf16)
    w_out_b = w_out.astype(bf16)
    wa = w_out_b[:, :HG_WIDTH]
    wb = w_out_b[:, HG_WIDTH:HG_WIDTH + SWA_Q][:, perm]
    wn = w_out_b[:, HG_WIDTH + SWA_Q:]
    wq_b = peer_wq.astype(bf16)
    sub_b = peer_subkeys.astype(bf16)
    lb = jnp.cumsum(jax.nn.softmax(hgrn_lb_logits.astype(f32), axis=1), axis=1)
    lb = lb - lb[:, :1]
    gp = jnp.stack([jnp.log(lb[0]), jnp.log1p(-lb[0]), 1.0 - lb[0], jnp.log(lb[1]), jnp.log1p(-lb[1]), 1.0 - lb[1],
                    jnp.zeros_like(lb[0]), jnp.zeros_like(lb[0])], axis=1)
    sink_rows = jnp.zeros((depth, 8, LANE), f32).at[:, :SWA_HEADS].set(
        jnp.broadcast_to(swa_sink.astype(f32)[:, list(SWA_HEAD_ORDER), None], (depth, SWA_HEADS, LANE)))

    words = _expert_table(peer_u, peer_v)
    n_exp = words.shape[1]
    tab = words.reshape(depth, n_exp, d // LANE, 1, LANE)
    tab_sc = words.reshape(depth, n_exp, d // LANE, LANE)
    n_tc = nb * ta // PEER_TB - PEER_SC_BLOCKS

    xall = jnp.concatenate([ctx, x], axis=1)
    for l in range(depth):
        mod = mods[l]
        hq, cum, kk, hv, sg, sq, sk, sv, nq, nk, nv = _proj(xall, norm_mix_w[l].reshape(1, d), mod, w_in_b[l], cs, gp[l], tri)
        o = _hgrn_scan(hq, cum, kk, hv, bd)
        so = _swa(sq, sk, sv, sink_rows[l], n_ctx)
        no = _nat(nq, nk, nv, _nat_bias_tables(nat_rpb[l]), n_ctx)
        xall = _outproj(o, sg, hgrn_norm_w[l].reshape(1, HG_WIDTH), bd, so, no, wa[l], wb[l], wn[l], mod, xall)
        h2, idx, g = _route(xall, norm_ffn_w[l].reshape(1, d), mod, wq_b[l], sub_b[l])
        idx, h2, g = idx.reshape(nb * ta, PEER_SLOTS), h2.reshape(nb * ta, d), g.reshape(nb * ta, PEER_SLOTS)
        staged = _sc_gather(tab_sc[l], idx[n_tc * PEER_TB:].reshape(-1))
        xf = _experts(idx, h2, g, xall.reshape(nb * ta, d), mod, tab[l], ta, n_ctx, n_tc)
        xf = _experts_dense(staged.reshape(-1, LANE), h2, g, xf, mod, ta, n_ctx, n_tc)
        xall = xf.reshape(nb, ta, d)
    return _final_norm(xall, final_norm_w, n_ctx)
```

```python
import functools

import numpy as np
import jax
import jax.numpy as jnp
from jax import lax
from jax.experimental import pallas as pl
from jax.experimental.pallas import tpu as pltpu
from jax.experimental.pallas import tpu_sc as plsc

f32 = jnp.float32
bf16 = jnp.bfloat16
i32 = jnp.int32

LANE = 128
VMEM_LIMIT_BYTES = 56 * 1024 * 1024

GRID_W = 64
HEAD_DIM = 64
EPS = 1e-6
HG_HEADS = 4
HG_WIDTH = 256
HG_SUB = 16
SWA_HEADS = 6
SWA_KV_HEADS = 2
SWA_WINDOW = 128
SWA_BLOCK = 128
ROPE_BASE = 10000.0
NAT_HEADS = 6
NAT_ROWS = 8
NAT_COLS = 16
SWA_Q = SWA_HEADS * HEAD_DIM
SWA_KV = SWA_KV_HEADS * HEAD_DIM
NAT_W = NAT_HEADS * HEAD_DIM
PROJ_WIDTH = 5 * HG_WIDTH + SWA_Q + 2 * SWA_KV + 3 * NAT_W
PEER_HEADS = 8
PEER_NKEYS = 128
PEER_DK = 256
PEER_TOPK = 16
PEER_SLOTS = PEER_HEADS * PEER_TOPK

TM = 256
PEER_TB = 128
PEER_NBUF = 16
PEER_AHEAD = PEER_NBUF - 1
PEER_SC_BLOCKS = 132
PEER_DENSE_TB = 16
SC_CHUNK = 32
NEG = -1e30

SWA_HEAD_ORDER = (0, 3, 1, 4, 2, 5)

_NT = (((1,), (1,)), ((), ()))
_TN = (((0,), (0,)), ((), ()))


def _silu(x):
    return x * (1.0 / (1.0 + jnp.exp(-x)))


def _split2(x):
    hi = x.astype(bf16)
    lo = (x - hi.astype(f32)).astype(bf16)
    return hi, lo


def _mods_kernel(sc_ref, w_ref, b_ref, o_ref):
    s = _silu(sc_ref[...])
    o_ref[0] = jnp.dot(s.astype(bf16), w_ref[0].astype(bf16), preferred_element_type=f32) + b_ref[0]


def _mods(c, c_ctx, w_mod, b_mod):
    depth, d, d6 = w_mod.shape
    nb = c.shape[0]
    rows = jnp.zeros((8, d), f32).at[:nb].set(c).at[nb].set(c_ctx)
    tn = d6 // 4
    out = pl.pallas_call(
        _mods_kernel,
        grid=(depth, 4),
        in_specs=[pl.BlockSpec((8, d), lambda l, j: (0, 0)),
                  pl.BlockSpec((1, d, tn), lambda l, j: (l, 0, j)),
                  pl.BlockSpec((1, 1, tn), lambda l, j: (l, 0, j))],
        out_specs=pl.BlockSpec((1, 8, tn), lambda l, j: (l, 0, j)),
        out_shape=jax.ShapeDtypeStruct((depth, 8, d6), f32),
        compiler_params=pltpu.CompilerParams(dimension_semantics=("arbitrary", "arbitrary"),
                                             vmem_limit_bytes=VMEM_LIMIT_BYTES),
        name="adaln_mods",
    )(rows, w_mod, b_mod.reshape(depth, 1, d6))
    lat = out[:, :nb].reshape(depth, nb, 6, d)
    ctx = jnp.broadcast_to(out[:, nb].reshape(depth, 1, 6, d), (depth, nb, 6, d))
    return jnp.stack([ctx, lat], axis=2)


def _norm_mod(x, nw, shift, scale):
    y = x * lax.rsqrt(jnp.mean(x * x, axis=-1, keepdims=True) + EPS) * nw
    return y * (1.0 + scale) + shift


def _rope(t, cos, sins):
    lane = lax.broadcasted_iota(i32, t.shape, 1)
    sw = jnp.where((lane % 32) < 16, pltpu.roll(t, LANE - 16, 1), pltpu.roll(t, 16, 1))
    return t * cos + sw * sins


def _hgrn_gates(z, log_lb, log1m_lb, one_m_lb):
    log_sig = jnp.minimum(z, 0.0) - jnp.log1p(jnp.exp(-jnp.abs(z)))
    c = log1m_lb + log_sig
    m = jnp.maximum(log_lb, c)
    log_f = m + jnp.log1p(jnp.exp(-jnp.abs(log_lb - c)))
    k = one_m_lb * (1.0 / (1.0 + jnp.exp(z)))
    return log_f, k


def _cumsum16(tri, log_f):
    h1 = log_f.astype(bf16)
    r1 = log_f - h1.astype(f32)
    h2 = r1.astype(bf16)
    h3 = (r1 - h2.astype(f32)).astype(bf16)
    return (jnp.dot(tri, h1, preferred_element_type=f32) + jnp.dot(tri, h2, preferred_element_type=f32)
            + jnp.dot(tri, h3, preferred_element_type=f32))


def _proj_kernel(x_ref, nw_ref, mod_ref, w_ref, cs_ref, gp_ref, tri_ref,
                 hq_ref, cum_ref, kk_ref, hv_ref, sg_ref, sq_ref, sk_ref, sv_ref, nq_ref, nk_ref, nv_ref):
    h = _norm_mod(x_ref[0], nw_ref[...], mod_ref[0, 0, 0:1, :], mod_ref[0, 0, 1:2, :])
    p = jnp.dot(h.astype(bf16), w_ref[...], preferred_element_type=f32)
    w = HG_WIDTH
    hq_ref[0] = _silu(p[:, 0:w])
    for d in range(2):
        log_f, k = _hgrn_gates(p[:, (1 + d) * w:(2 + d) * w], gp_ref[3 * d:3 * d + 1, :],
                               gp_ref[3 * d + 1:3 * d + 2, :], gp_ref[3 * d + 2:3 * d + 3, :])
        cum_ref[d, 0] = _cumsum16(tri_ref[d], log_f)
        kk_ref[d, 0] = k
    hv_ref[0] = p[:, 3 * w:4 * w]
    sg_ref[0] = _silu(p[:, 4 * w:5 * w])
    cos, sins = cs_ref[0], cs_ref[1]
    o = 5 * w
    scale = HEAD_DIM ** -0.5
    for j in range(SWA_Q // LANE):
        sq_ref[0, :, j * LANE:(j + 1) * LANE] = (_rope(p[:, o + j * LANE:o + (j + 1) * LANE], cos, sins) * scale).astype(bf16)
    o += SWA_Q
    sk_ref[0] = _rope(p[:, o:o + SWA_KV], cos, sins).astype(bf16)
    o += SWA_KV
    sv_ref[0] = p[:, o:o + SWA_KV].astype(bf16)
    o += SWA_KV
    nq_ref[0] = (p[:, o:o + NAT_W] * scale).astype(bf16)
    o += NAT_W
    nk_ref[0] = p[:, o:o + NAT_W].astype(bf16)
    o += NAT_W
    nv_ref[0] = p[:, o:o + NAT_W].astype(bf16)


def _proj(xall, nw, mod, w_in, cs, gp, tri):
    nb, ta, d = xall.shape
    nt = ta // TM
    tok = lambda width, dt: jax.ShapeDtypeStruct((nb, ta, width), dt)
    tok2 = jax.ShapeDtypeStruct((2, nb, ta, HG_WIDTH), f32)
    bs = lambda width: pl.BlockSpec((1, TM, width), lambda b, i: (b, i, 0))
    bs2 = pl.BlockSpec((2, 1, TM, HG_WIDTH), lambda b, i: (0, b, i, 0))
    return pl.pallas_call(
        _proj_kernel,
        grid=(nb, nt),
        in_specs=[pl.BlockSpec((1, TM, d), lambda b, i: (b, i, 0)),
                  pl.BlockSpec((1, d), lambda b, i: (0, 0)),
                  pl.BlockSpec((1, 1, 6, d), lambda b, i: (b, jnp.minimum(i, 1), 0, 0)),
                  pl.BlockSpec((d, PROJ_WIDTH), lambda b, i: (0, 0)),
                  pl.BlockSpec((2, TM, LANE), lambda b, i: (0, i, 0)),
                  pl.BlockSpec((8, HG_WIDTH), lambda b, i: (0, 0)),
                  pl.BlockSpec((2, TM, TM), lambda b, i: (0, 0, 0))],
        out_specs=[bs(HG_WIDTH), bs2, bs2, bs(HG_WIDTH), bs(HG_WIDTH), bs(SWA_Q), bs(SWA_KV), bs(SWA_KV),
                   bs(NAT_W), bs(NAT_W), bs(NAT_W)],
        out_shape=[tok(HG_WIDTH, f32), tok2, tok2, tok(HG_WIDTH, f32), tok(HG_WIDTH, f32), tok(SWA_Q, bf16),
                   tok(SWA_KV, bf16), tok(SWA_KV, bf16), tok(NAT_W, bf16), tok(NAT_W, bf16), tok(NAT_W, bf16)],
        compiler_params=pltpu.CompilerParams(dimension_semantics=("arbitrary", "arbitrary"),
                                             vmem_limit_bytes=VMEM_LIMIT_BYTES),
        name="norm_in_proj",
    )(xall, nw, mod, w_in, cs, gp, tri)


def _hgrn_kernel(q_ref, cum_ref, k_ref, v_ref, bd_ref, o_ref, st_ref):
    d = pl.program_id(1)
    fwd = d == 0

    @pl.when(pl.program_id(2) == 0)
    def _():
        st_ref[...] = jnp.zeros_like(st_ref)

    bd = bd_ref[...]
    bd_mask = bd > 0
    sgn = jnp.where(fwd, 1, -1)
    t_sgn = lax.broadcasted_iota(i32, (HG_SUB, 1), 0) * sgn
    nsteps = TM // HG_SUB

    def step(i, carry):
        sc = jnp.where(fwd, i, nsteps - 1 - i)
        r0 = pl.multiple_of(sc * HG_SUB, HG_SUB)
        q = q_ref[0, pl.ds(r0, HG_SUB), :]
        b = cum_ref[0, 0, pl.ds(r0, HG_SUB), :]
        k = k_ref[0, 0, pl.ds(r0, HG_SUB), :]
        v = v_ref[0, pl.ds(r0, HG_SUB), :]
        b_end = jnp.where(fwd, b[HG_SUB - 1:HG_SUB, :], b[0:1, :])
        st = st_ref[...]
        o = lax.dot_general((q * jnp.exp(b)).astype(bf16), st.astype(bf16), _NT, preferred_element_type=f32)
        rows = []
        for s in range(HG_SUB):
            valid = t_sgn >= s * sgn
            decay = jnp.exp(jnp.minimum(b - b[s:s + 1, :], 0.0))
            rows.append(jnp.where(valid, decay * q * k[s:s + 1, :], 0.0))
        hi, lo = _split2(jnp.concatenate(rows, axis=0))
        pe = jnp.dot(hi, bd, preferred_element_type=f32) + jnp.dot(lo, bd, preferred_element_type=f32)
        for s in range(HG_SUB):
            o = o + pe[s * HG_SUB:(s + 1) * HG_SUB, :] * v[s:s + 1, :]
        o_ref[0, 0, pl.ds(r0, HG_SUB), :] = o
        kd = (k * jnp.exp(b_end - b)).astype(bf16)
        kv_t = lax.dot_general(v.astype(bf16), kd, _TN, preferred_element_type=f32)
        st_ref[...] = st * jnp.exp(b_end) + jnp.where(bd_mask, kv_t, 0.0)
        return carry

    lax.fori_loop(0, nsteps, step, 0, unroll=2)


def _hgrn_scan(hq, cum, kk, hv, bd):
    nb, ta, w = hq.shape
    nt = ta // TM

    def blk(d, j):
        return jnp.where(d == 0, j, jnp.where(j == 0, 0, nt - j))

    return pl.pallas_call(
        _hgrn_kernel,
        grid=(nb, 2, nt),
        in_specs=[pl.BlockSpec((1, TM, w), lambda b, d, j: (b, blk(d, j), 0)),
                  pl.BlockSpec((1, 1, TM, w), lambda b, d, j: (d, b, blk(d, j), 0)),
                  pl.BlockSpec((1, 1, TM, w), lambda b, d, j: (d, b, blk(d, j), 0)),
                  pl.BlockSpec((1, TM, w), lambda b, d, j: (b, blk(d, j), 0)),
                  pl.BlockSpec((w, w), lambda b, d, j: (0, 0))],
        out_specs=pl.BlockSpec((1, 1, TM, w), lambda b, d, j: (d, b, blk(d, j), 0)),
        out_shape=jax.ShapeDtypeStruct((2, nb, ta, w), f32),
        scratch_shapes=[pltpu.VMEM((w, w), f32)],
        compiler_params=pltpu.CompilerParams(dimension_semantics=("arbitrary", "arbitrary", "arbitrary")),
        name="hgrn_scan",
    )(hq, cum, kk, hv, bd)


def _pair_rows(q2):
    lane = lax.broadcasted_iota(i32, q2.shape, 1)
    zero = jnp.zeros_like(q2)
    return jnp.concatenate([jnp.where(lane < HEAD_DIM, q2, zero), jnp.where(lane >= HEAD_DIM, q2, zero)], axis=0)


def _pair_merge(o):
    m = o.shape[0] // 2
    lane = lax.broadcasted_iota(i32, (m, LANE), 1)
    return jnp.where(lane < HEAD_DIM, o[:m], o[m:])


def _joint_attention(s_loc, s_ctx, v_loc, v_ctx, sink=None):
    m = jnp.maximum(jnp.max(s_loc, axis=-1, keepdims=True), jnp.max(s_ctx, axis=-1, keepdims=True))
    if sink is not None:
        m = jnp.maximum(m, sink)
    p_loc = jnp.exp(s_loc - m)
    p_ctx = jnp.exp(s_ctx - m)
    den = jnp.sum(p_loc, axis=-1, keepdims=True) + jnp.sum(p_ctx, axis=-1, keepdims=True)
    if sink is not None:
        den = den + jnp.exp(sink - m)
    o = (jnp.dot(p_loc.astype(bf16), v_loc, preferred_element_type=f32)
         + jnp.dot(p_ctx.astype(bf16), v_ctx, preferred_element_type=f32))
    return o / den


def _swa_kernel(q_ref, kp_ref, kc_ref, kn_ref, vp_ref, vc_ref, vn_ref, kx_ref, vx_ref, sink_ref, o_ref, *, n_ctx_blocks, seq):
    g = pl.program_id(1)
    i = g - n_ctx_blocks
    k_loc = jnp.concatenate([kp_ref[0], kc_ref[0], kn_ref[0]], axis=0)
    v_loc = jnp.concatenate([vp_ref[0], vc_ref[0], vn_ref[0]], axis=0)
    row = lax.broadcasted_iota(i32, (SWA_BLOCK, 3 * SWA_BLOCK), 0)
    col = lax.broadcasted_iota(i32, (SWA_BLOCK, 3 * SWA_BLOCK), 1)
    rel = col - SWA_BLOCK - row
    kpos = (i - 1) * SWA_BLOCK + col
    ok = (jnp.abs(rel) <= SWA_WINDOW) & (kpos >= 0) & (kpos < seq) & (i >= 0)
    pen = jnp.where(ok, 0.0, NEG)
    pen = jnp.concatenate([pen, pen], axis=0)
    for j in range(SWA_Q // LANE):
        q = _pair_rows(q_ref[0, :, j * LANE:(j + 1) * LANE])
        s_loc = lax.dot_general(q, k_loc, _NT, preferred_element_type=f32) + pen
        s_ctx = lax.dot_general(q, kx_ref[0], _NT, preferred_element_type=f32)
        sink = jnp.concatenate([jnp.broadcast_to(sink_ref[2 * j:2 * j + 1, 0:1], (SWA_BLOCK, 1)),
                                jnp.broadcast_to(sink_ref[2 * j + 1:2 * j + 2, 0:1], (SWA_BLOCK, 1))], axis=0)
        o = _joint_attention(s_loc, s_ctx, v_loc, vx_ref[0], sink)
        o_ref[0, :, j * LANE:(j + 1) * LANE] = _pair_merge(o).astype(bf16)


def _swa(sq, sk, sv, sink_rows, n_ctx):
    nb, ta, _ = sq.shape
    ncb = n_ctx // SWA_BLOCK
    nblk = ta // SWA_BLOCK
    nlat = nblk - ncb
    prev = lambda b, g: (b, jnp.clip(g - ncb - 1, 0, nlat - 1) + ncb, 0)
    cur = lambda b, g: (b, g, 0)
    nxt = lambda b, g: (b, jnp.clip(g - ncb + 1, 0, nlat - 1) + ncb, 0)
    kvs = lambda f: pl.BlockSpec((1, SWA_BLOCK, SWA_KV), f)
    ctx = pl.BlockSpec((1, n_ctx, SWA_KV), lambda b, g: (b, 0, 0))
    return pl.pallas_call(
        functools.partial(_swa_kernel, n_ctx_blocks=ncb, seq=ta - n_ctx),
        grid=(nb, nblk),
        in_specs=[pl.BlockSpec((1, SWA_BLOCK, SWA_Q), cur), kvs(prev), kvs(cur), kvs(nxt), kvs(prev), kvs(cur), kvs(nxt),
                  ctx, ctx, pl.BlockSpec((8, LANE), lambda b, g: (0, 0))],
        out_specs=pl.BlockSpec((1, SWA_BLOCK, SWA_Q), cur),
        out_shape=jax.ShapeDtypeStruct((nb, ta, SWA_Q), bf16),
        compiler_params=pltpu.CompilerParams(dimension_semantics=("arbitrary", "arbitrary")),
        name="swa_attention",
    )(sq, sk, sk, sk, sv, sv, sv, sk, sv, sink_rows)


def _nat_kernel(q_ref, k_ref, v_ref, bias_ref, o_ref, *, n_ctx, n_rows):
    g = pl.program_id(1)
    is_ctx = g == 0
    ctx_pen = jnp.where(is_ctx, NEG, 0.0)
    rows_per_tile = TM // GRID_W
    nwin = NAT_ROWS * GRID_W
    for rr in range(rows_per_tile):
        r = jnp.maximum((g - 1) * rows_per_tile + rr, 0)
        rs = jnp.clip(r - NAT_ROWS // 2, 0, n_rows - NAT_ROWS)
        dr0 = rs - r + NAT_ROWS - 1
        start = pl.multiple_of(n_ctx + rs * GRID_W, GRID_W)
        for p in range(NAT_W // LANE):
            ls = slice(p * LANE, (p + 1) * LANE)
            q = _pair_rows(q_ref[0, rr * GRID_W:(rr + 1) * GRID_W, ls])
            s_loc = (lax.dot_general(q, k_ref[0, pl.ds(start, nwin), ls], _NT, preferred_element_type=f32)
                     + bias_ref[p, dr0] + ctx_pen)
            s_ctx = lax.dot_general(q, k_ref[0, 0:n_ctx, ls], _NT, preferred_element_type=f32)
            o = _joint_attention(s_loc, s_ctx, v_ref[0, pl.ds(start, nwin), ls], v_ref[0, 0:n_ctx, ls])
            o_ref[0, rr * GRID_W:(rr + 1) * GRID_W, ls] = _pair_merge(o).astype(bf16)


def _nat(nq, nk, nv, bias, n_ctx):
    nb, ta, w = nq.shape
    nt = ta // TM
    whole = pl.BlockSpec((1, ta, w), lambda b, g: (b, 0, 0), pipeline_mode=pl.Buffered(1))
    return pl.pallas_call(
        functools.partial(_nat_kernel, n_ctx=n_ctx, n_rows=(ta - n_ctx) // GRID_W),
        grid=(nb, nt),
        in_specs=[pl.BlockSpec((1, TM, w), lambda b, g: (b, g, 0)), whole, whole,
                  pl.BlockSpec(bias.shape, lambda b, g: (0, 0, 0, 0), pipeline_mode=pl.Buffered(1))],
        out_specs=pl.BlockSpec((1, TM, w), lambda b, g: (b, g, 0)),
        out_shape=jax.ShapeDtypeStruct((nb, ta, w), bf16),
        compiler_params=pltpu.CompilerParams(dimension_semantics=("arbitrary", "arbitrary"),
                                             vmem_limit_bytes=VMEM_LIMIT_BYTES),
        name="nat_attention",
    )(nq, nk, nv, bias)


def _nat_bias_tables(rpb):
    c = np.arange(GRID_W)[:, None]
    kc = np.arange(GRID_W)[None, :]
    ws = np.clip(c - NAT_COLS // 2, 0, GRID_W - NAT_COLS)
    valid = (kc >= ws) & (kc < ws + NAT_COLS)
    col_idx = np.clip(kc - c + NAT_COLS - 1, 0, 2 * NAT_COLS - 2)
    dr = np.arange(NAT_ROWS)[:, None] + np.arange(NAT_ROWS)[None, :]
    t = rpb.astype(f32)[:, dr][:, :, :, col_idx]
    t = jnp.where(jnp.asarray(valid)[None, None, None], t, NEG)
    t = t.transpose(0, 1, 3, 2, 4).reshape(NAT_HEADS // 2, 2, NAT_ROWS, GRID_W, NAT_ROWS * GRID_W)
    return t.transpose(0, 2, 1, 3, 4).reshape(NAT_HEADS // 2, NAT_ROWS, 2 * GRID_W, NAT_ROWS * GRID_W)


def _outproj_kernel(o_ref, sg_ref, hnw_ref, bd_ref, so_ref, no_ref, wa_ref, wb_ref, wn_ref, mod_ref, x_ref, y_ref):
    oo = o_ref[0, 0] + o_ref[1, 0]
    hi, lo = _split2(oo * oo)
    bd = bd_ref[...]
    msq = (jnp.dot(hi, bd, preferred_element_type=f32) + jnp.dot(lo, bd, preferred_element_type=f32)) * (1.0 / HEAD_DIM)
    a = oo * lax.rsqrt(msq + EPS) * hnw_ref[...] * sg_ref[0]
    y = (jnp.dot(a.astype(bf16), wa_ref[...], preferred_element_type=f32)
         + jnp.dot(so_ref[0], wb_ref[...], preferred_element_type=f32)
         + jnp.dot(no_ref[0], wn_ref[...], preferred_element_type=f32))
    y_ref[0] = x_ref[0] + mod_ref[0, 0, 2:3, :] * y


def _outproj(o, sg, hnw, bd, so, no, wa, wb, wn, mod, xall):
    nb, ta, d = xall.shape
    nt = ta // TM
    bs = lambda width: pl.BlockSpec((1, TM, width), lambda b, i: (b, i, 0))
    full = lambda a: pl.BlockSpec(a.shape, lambda b, i: (0,) * a.ndim)
    return pl.pallas_call(
        _outproj_kernel,
        grid=(nb, nt),
        in_specs=[pl.BlockSpec((2, 1, TM, HG_WIDTH), lambda b, i: (0, b, i, 0)), bs(HG_WIDTH), full(hnw), full(bd),
                  bs(SWA_Q), bs(NAT_W), full(wa), full(wb), full(wn),
                  pl.BlockSpec((1, 1, 6, d), lambda b, i: (b, jnp.minimum(i, 1), 0, 0)), bs(d)],
        out_specs=bs(d),
        out_shape=jax.ShapeDtypeStruct(xall.shape, f32),
        compiler_params=pltpu.CompilerParams(dimension_semantics=("arbitrary", "arbitrary"),
                                             vmem_limit_bytes=VMEM_LIMIT_BYTES),
        name="out_proj",
    )(o, sg, hnw, bd, so, no, wa, wb, wn, mod, xall)


def _top16_rows(vals, payload=None):
    n = vals.shape[0]
    iota = lax.broadcasted_iota(i32, vals.shape, 0)
    best, picked = [], []
    for _ in range(PEER_TOPK):
        m = jnp.max(vals, axis=0, keepdims=True)
        idx = jnp.min(jnp.where(vals == m, iota, n), axis=0, keepdims=True)
        hit = iota == idx
        best.append(m)
        picked.append(idx if payload is None else jnp.max(jnp.where(hit, payload, -1), axis=0, keepdims=True))
        vals = jnp.where(hit, -jnp.inf, vals)
    return jnp.concatenate(best, axis=0), jnp.concatenate(picked, axis=0)


def _product_candidates(first, second, combine):
    half = PEER_TOPK // 2
    rows = [combine(first[0:1, :], second)]
    rows += [combine(first[a:a + 1, :], second[0:half, :]) for a in range(1, half)]
    rows.append(combine(first[half:, :], second[0:1, :]))
    return jnp.concatenate(rows, axis=0)


def _route_kernel(x_ref, nw_ref, mod_ref, wq_ref, sub_ref, h_ref, idx_ref, g_ref):
    h = _norm_mod(x_ref[0], nw_ref[...], mod_ref[0, 0, 3:4, :], mod_ref[0, 0, 4:5, :])
    h_ref[0] = h
    q = jnp.dot(h.astype(bf16), wq_ref[...], preferred_element_type=f32).astype(bf16)
    half = PEER_DK // 2
    ids, gates = [], []
    for hh in range(PEER_HEADS):
        top_s, top_i = [], []
        for p in range(2):
            c0 = (hh * 2 + p) * half
            s = lax.dot_general(sub_ref[hh, p], q[:, c0:c0 + half], _NT, preferred_element_type=f32)
            ts, ti = _top16_rows(s)
            top_s.append(ts)
            top_i.append(ti)
        cand_s = _product_candidates(top_s[0], top_s[1], lambda a, b: a + b)
        cand_i = _product_candidates(top_i[0], top_i[1], lambda a, b: a * PEER_NKEYS + b)
        best, eid = _top16_rows(cand_s, cand_i)
        ids.append(eid)
        e = jnp.exp(best - best[0:1, :])
        gates.append(e / jnp.sum(e, axis=0, keepdims=True))
    idx_ref[0] = jnp.concatenate(ids, axis=0).T
    g_ref[0] = jnp.concatenate(gates, axis=0).T


def _route(xall, nw, mod, wq, sub):
    nb, ta, d = xall.shape
    nt = ta // TM
    full = lambda a: pl.BlockSpec(a.shape, lambda b, i: (0,) * a.ndim)
    return pl.pallas_call(
        _route_kernel,
        grid=(nb, nt),
        in_specs=[pl.BlockSpec((1, TM, d), lambda b, i: (b, i, 0)), full(nw),
                  pl.BlockSpec((1, 1, 6, d), lambda b, i: (b, jnp.minimum(i, 1), 0, 0)), full(wq), full(sub)],
        out_specs=[pl.BlockSpec((1, TM, d), lambda b, i: (b, i, 0)),
                   pl.BlockSpec((1, TM, PEER_SLOTS), lambda b, i: (b, i, 0)),
                   pl.BlockSpec((1, TM, PEER_SLOTS), lambda b, i: (b, i, 0))],
        out_shape=[jax.ShapeDtypeStruct(xall.shape, f32), jax.ShapeDtypeStruct((nb, ta, PEER_SLOTS), i32),
                   jax.ShapeDtypeStruct((nb, ta, PEER_SLOTS), f32)],
        compiler_params=pltpu.CompilerParams(dimension_semantics=("arbitrary", "arbitrary"),
                                             vmem_limit_bytes=VMEM_LIMIT_BYTES),
        name="peer_route",
    )(xall, nw, mod, wq, sub)


def _expert_kernel(idx_ref, idx_next_ref, h_ref, g_ref, x_ref, mod_ref, tab_hbm, o_ref, buf, sem):
    d = h_ref.shape[-1]
    nct = d // LANE
    groups = PEER_SLOTS // 8

    per_piece = PEER_SLOTS // nct

    def issue(ids_ref, t, slot, piece):
        for k in range(piece * per_piece, (piece + 1) * per_piece):
            pltpu.make_async_copy(tab_hbm.at[ids_ref[t, k]], buf.at[slot, k // 8, :, pl.ds(k % 8, 1), :],
                                  sem.at[slot]).start(priority=k % 2)

    def wait(slot):
        pltpu.make_async_copy(buf.at[slot], buf.at[slot], sem.at[slot]).wait()

    def cols(slot, c, high):
        words = jnp.concatenate([jnp.concatenate([buf[slot, kg, c], buf[slot, kg, c + 1]], axis=1)
                                 for kg in range(groups)], axis=0)
        bits = (words & jnp.uint32(0xFFFF0000)) if high else (words << 16)
        return lax.bitcast_convert_type(bits, f32).astype(bf16)

    not_last = pl.program_id(0) < pl.num_programs(0) - 1

    def step(t, w8, do_issue, do_u, do_v):
        nslot = lax.rem(t + 1, PEER_NBUF)
        slot = lax.rem(t, PEER_NBUF)
        islot = lax.rem(t + PEER_AHEAD, PEER_NBUF)
        if do_issue == "next":
            @pl.when(not_last)
            def _():
                for j in range(nct):
                    issue(idx_next_ref, t + PEER_AHEAD - PEER_TB, islot, j)
        if do_u:
            wait(nslot)
            h8 = jnp.broadcast_to(h_ref[pl.ds(t + 1, 1), :], (8, d)).astype(bf16)
            gate = g_ref[pl.ds(t + 1, 1), :]
            a = jnp.zeros((8, PEER_SLOTS), f32)
        ys = []
        for j in range(nct):
            if 2 * j < nct:
                c = 2 * j
                if do_u:
                    a = a + lax.dot_general(h8[:, c * LANE:(c + 2) * LANE], cols(nslot, c, True), _NT, preferred_element_type=f32)
                if do_v:
                    ys.append(jnp.dot(w8, cols(slot, c, False), preferred_element_type=f32)[0:1, :])
            if do_issue is True:
                issue(idx_ref, t + PEER_AHEAD, islot, j)
        if do_v:
            o_ref[pl.ds(t, 1), :] = jnp.concatenate(ys, axis=1)
        if do_u:
            a1 = a[0:1, :]
            w = gate * (0.5 * a1 * (1.0 + lax.erf(a1 * (2.0 ** -0.5))))
            w8 = jnp.broadcast_to(w, (8, PEER_SLOTS)).astype(bf16)
        return w8

    @pl.when(pl.program_id(0) == 0)
    def _():
        def prologue(t, carry):
            for j in range(nct):
                issue(idx_ref, t, t, j)
            return carry

        lax.fori_loop(0, PEER_AHEAD, prologue, 0)

    w8 = step(-1, jnp.zeros((8, PEER_SLOTS), bf16), False, True, False)
    w8 = lax.fori_loop(0, PEER_TB - PEER_AHEAD, lambda t, w: step(t, w, True, True, True), w8)
    w8 = lax.fori_loop(PEER_TB - PEER_AHEAD, PEER_TB - 1, lambda t, w: step(t, w, "next", True, True), w8)
    step(PEER_TB - 1, w8, "next", False, True)
    o_ref[...] = x_ref[...] + mod_ref[0, 0, 5:6, :] * o_ref[...]


def _mod_index(tokens_per_step, ta, n_ctx, first_step=0):
    spb = ta // tokens_per_step
    cs = n_ctx // tokens_per_step
    return lambda i: ((i + first_step) // spb, jnp.where((i + first_step) % spb < cs, 0, 1), 0, 0)


def _experts(idx, h, g, xflat, mod, tab, ta, n_ctx, nblocks):
    n, d = xflat.shape
    tokb = lambda width: pl.BlockSpec((PEER_TB, width), lambda i: (i, 0))
    return pl.pallas_call(
        _expert_kernel,
        grid=(nblocks,),
        in_specs=[pl.BlockSpec((PEER_TB, PEER_SLOTS), lambda i: (i, 0), memory_space=pltpu.SMEM),
                  pl.BlockSpec((PEER_TB, PEER_SLOTS), lambda i: (jnp.minimum(i + 1, nblocks - 1), 0), memory_space=pltpu.SMEM),
                  tokb(d), tokb(PEER_SLOTS), tokb(d),
                  pl.BlockSpec((1, 1, 6, d), _mod_index(PEER_TB, ta, n_ctx)),
                  pl.BlockSpec(memory_space=pl.ANY)],
        out_specs=tokb(d),
        out_shape=jax.ShapeDtypeStruct((n, d), f32),
        input_output_aliases={4: 0},
        scratch_shapes=[pltpu.VMEM((PEER_NBUF, PEER_SLOTS // 8, d // LANE, 8, LANE), jnp.uint32),
                        pltpu.SemaphoreType.DMA((PEER_NBUF,))],
        compiler_params=pltpu.CompilerParams(dimension_semantics=("arbitrary",), vmem_limit_bytes=VMEM_LIMIT_BYTES),
        name="peer_experts",
    )(idx, idx, h, g, xflat, mod, tab)


def _sc_gather(table, idx):
    info = plsc.get_sparse_core_info()
    nc = info.num_cores
    nw = nc * info.num_subcores
    r = idx.shape[0]
    per_w = r // nw
    nch = per_w // SC_CHUNK
    assert per_w * nw == r and nch * SC_CHUNK == per_w and nch % 2 == 0
    rows_shape = (SC_CHUNK,) + table.shape[1:]
    mesh = plsc.VectorSubcoreMesh(core_axis_name="c", subcore_axis_name="s")

    @functools.partial(
        pl.kernel, mesh=mesh, out_type=jax.ShapeDtypeStruct((r,) + table.shape[1:], table.dtype),
        scratch_types=[pltpu.VMEM((2, SC_CHUNK), i32), pltpu.VMEM((2,) + rows_shape, table.dtype),
                       pltpu.SemaphoreType.DMA((2,)), pltpu.SemaphoreType.DMA((2,))])
    def gather_kernel(table_hbm, idx_hbm, out_hbm, idx_v, rows_v, gsem, wsem):
        base = (lax.axis_index("s") * nc + lax.axis_index("c")) * per_w

        def rows_of(j):
            return pl.ds(pl.multiple_of(base + j * SC_CHUNK, 8), SC_CHUNK)

        def gather(j, b):
            pltpu.sync_copy(idx_hbm.at[rows_of(j)], idx_v.at[b])
            return pltpu.make_async_copy(table_hbm.at[idx_v.at[b]], rows_v.at[b], gsem.at[b])

        def write(j, b):
            return pltpu.make_async_copy(rows_v.at[b], out_hbm.at[rows_of(j)], wsem.at[b])

        gather(0, 0).start()

        @pl.loop(0, nch, step=2)
        def _(j0):
            for b in (0, 1):
                j = j0 + b

                @pl.when(j >= 1)
                def _():
                    write(j - 1, 1 - b).wait()

                @pl.when(j + 1 < nch)
                def _():
                    gather(j + 1, 1 - b).start()

                pltpu.make_async_copy(table_hbm.at[idx_v.at[b]], rows_v.at[b], gsem.at[b]).wait()
                write(j, b).start()

        write(nch - 1, 1).wait()

    return gather_kernel(table, idx)


def _expert_dense_kernel(st_ref, h_ref, g_ref, x_ref, mod_ref, o_ref):
    d = h_ref.shape[-1]
    nct = d // LANE
    groups = PEER_SLOTS // 8
    rows_per_token = PEER_SLOTS * nct

    def cols(t, c, high):
        base = t * rows_per_token
        tile = lambda kg, cc: st_ref[pl.ds(base + kg * 8 * nct + cc, 8, stride=nct), :]
        words = jnp.concatenate([jnp.concatenate([tile(kg, c), tile(kg, c + 1)], axis=1) for kg in range(groups)], axis=0)
        bits = (words & jnp.uint32(0xFFFF0000)) if high else (words << 16)
        return lax.bitcast_convert_type(bits, f32).astype(bf16)

    def u_stage(t):
        h8 = jnp.broadcast_to(h_ref[pl.ds(t, 1), :], (8, d)).astype(bf16)
        a = jnp.zeros((8, PEER_SLOTS), f32)
        for c in range(0, nct, 2):
            a = a + lax.dot_general(h8[:, c * LANE:(c + 2) * LANE], cols(t, c, True), _NT, preferred_element_type=f32)
        a1 = a[0:1, :]
        w = g_ref[pl.ds(t, 1), :] * (0.5 * a1 * (1.0 + lax.erf(a1 * (2.0 ** -0.5))))
        return jnp.broadcast_to(w, (8, PEER_SLOTS)).astype(bf16)

    def v_stage(t, w8):
        o_ref[pl.ds(t, 1), :] = jnp.concatenate(
            [jnp.dot(w8, cols(t, c, False), preferred_element_type=f32)[0:1, :] for c in range(0, nct, 2)], axis=1)

    def body(t, w8):
        w_next = u_stage(t + 1)
        v_stage(t, w8)
        return w_next

    w8 = lax.fori_loop(0, PEER_DENSE_TB - 1, body, u_stage(0), unroll=5)
    v_stage(PEER_DENSE_TB - 1, w8)
    o_ref[...] = x_ref[...] + mod_ref[0, 0, 5:6, :] * o_ref[...]


def _experts_dense(staged, h, g, xflat, mod, ta, n_ctx, first_block):
    n, d = xflat.shape
    rows_per_token = PEER_SLOTS * (d // LANE)
    steps = staged.shape[0] // (PEER_DENSE_TB * rows_per_token)
    off = first_block * PEER_TB // PEER_DENSE_TB
    tokb = lambda width: pl.BlockSpec((PEER_DENSE_TB, width), lambda i: (i + off, 0))
    return pl.pallas_call(
        _expert_dense_kernel,
        grid=(steps,),
        in_specs=[pl.BlockSpec((PEER_DENSE_TB * rows_per_token, LANE), lambda i: (i, 0)),
                  tokb(d), tokb(PEER_SLOTS), tokb(d),
                  pl.BlockSpec((1, 1, 6, d), _mod_index(PEER_DENSE_TB, ta, n_ctx, off))],
        out_specs=tokb(d),
        out_shape=jax.ShapeDtypeStruct((n, d), f32),
        input_output_aliases={3: 0},
        compiler_params=pltpu.CompilerParams(dimension_semantics=("arbitrary",), vmem_limit_bytes=VMEM_LIMIT_BYTES),
        name="peer_experts_dense",
    )(staged, h, g, xflat, mod)


def _expert_table(u, v):
    half = lambda a: lax.bitcast_convert_type(a.astype(bf16), jnp.uint16).astype(jnp.uint32)
    return (half(u) << 16) | half(v)


def _final_norm_kernel(x_ref, w_ref, o_ref):
    x = x_ref[0]
    o_ref[0] = x * lax.rsqrt(jnp.mean(x * x, axis=-1, keepdims=True) + EPS) * w_ref[...]


def _final_norm(xall, w, n_ctx):
    nb, ta, d = xall.shape
    off = n_ctx // TM
    return pl.pallas_call(
        _final_norm_kernel,
        grid=(nb, (ta - n_ctx) // TM),
        in_specs=[pl.BlockSpec((1, TM, d), lambda b, i: (b, i + off, 0)), pl.BlockSpec((1, d), lambda b, i: (0, 0))],
        out_specs=pl.BlockSpec((1, TM, d), lambda b, i: (b, i, 0)),
        out_shape=jax.ShapeDtypeStruct((nb, ta - n_ctx, d), f32),
        compiler_params=pltpu.CompilerParams(dimension_semantics=("arbitrary", "arbitrary")),
        name="final_norm",
    )(xall, w.reshape(1, d))


def _rope_tables(n_ctx, seq):
    nf = HEAD_DIM // 4
    pos = np.arange(seq)
    inv = ROPE_BASE ** (-np.arange(nf, dtype=np.float32) / nf)
    ar = (pos // GRID_W).astype(np.float32)[:, None] * inv
    ac = (pos % GRID_W).astype(np.float32)[:, None] * inv
    cos = np.concatenate([np.cos(ar), np.cos(ar), np.cos(ac), np.cos(ac)], axis=-1)
    sin = np.concatenate([-np.sin(ar), np.sin(ar), -np.sin(ac), np.sin(ac)], axis=-1)
    cos = np.concatenate([np.ones((n_ctx, HEAD_DIM), np.float32), cos.astype(np.float32)], axis=0)
    sin = np.concatenate([np.zeros((n_ctx, HEAD_DIM), np.float32), sin.astype(np.float32)], axis=0)
    return jnp.asarray(np.stack([np.tile(cos, (1, 2)), np.tile(sin, (1, 2))]))


def _tri_tables():
    r = np.arange(TM)[:, None]
    c = np.arange(TM)[None, :]
    same = (r // HG_SUB) == (c // HG_SUB)
    return jnp.asarray(np.stack([same & (c <= r), same & (c >= r)]).astype(np.float32), dtype=bf16)


def _block_diag_ones():
    r = np.arange(HG_WIDTH)
    return jnp.asarray((r[:, None] // HEAD_DIM == r[None, :] // HEAD_DIM).astype(np.float32), dtype=bf16)


def _swa_perm():
    return np.concatenate([np.arange(h * HEAD_DIM, (h + 1) * HEAD_DIM) for h in SWA_HEAD_ORDER])


def kernel(x, c, ctx, c_ctx, w_mod, b_mod, norm_mix_w, norm_ffn_w, w_in, hgrn_lb_logits, hgrn_norm_w, swa_sink, nat_rpb,
           w_out, peer_wq, peer_subkeys, peer_u, peer_v, final_norm_w):
    nb, seq, d = x.shape
    n_ctx = ctx.shape[1]
    depth = w_in.shape[0]
    ta = n_ctx + seq
    assert n_ctx == TM and seq % TM == 0 and seq % (GRID_W * NAT_ROWS) == 0 and d % LANE == 0
    assert w_in.shape[-1] == PROJ_WIDTH and peer_wq.shape[-1] == PEER_HEADS * PEER_DK

    mods = _mods(c, c_ctx, w_mod, b_mod)
    cs = _rope_tables(n_ctx, seq)
    tri = _tri_tables()
    bd = _block_diag_ones()
    perm = _swa_perm()
    q0 = 5 * HG_WIDTH
    col_perm = np.concatenate([np.arange(q0), q0 + perm, np.arange(q0 + SWA_Q, PROJ_WIDTH)])
    w_in_b = w_in[:, :, col_perm].astype(bf16)
    w_out_b = w_out.astype(bf16)
    wa = w_out_b[:, :HG_WIDTH]
    wb = w_out_b[:, HG_WIDTH:HG_WIDTH + SWA_Q][:, perm]
    wn = w_out_b[:, HG_WIDTH + SWA_Q:]
    wq_b = peer_wq.astype(bf16)
    sub_b = peer_subkeys.astype(bf16)
    lb = jnp.cumsum(jax.nn.softmax(hgrn_lb_logits.astype(f32), axis=1), axis=1)
    lb = lb - lb[:, :1]
    gp = jnp.stack([jnp.log(lb[0]), jnp.log1p(-lb[0]), 1.0 - lb[0], jnp.log(lb[1]), jnp.log1p(-lb[1]), 1.0 - lb[1],
                    jnp.zeros_like(lb[0]), jnp.zeros_like(lb[0])], axis=1)
    sink_rows = jnp.zeros((depth, 8, LANE), f32).at[:, :SWA_HEADS].set(
        jnp.broadcast_to(swa_sink.astype(f32)[:, list(SWA_HEAD_ORDER), None], (depth, SWA_HEADS, LANE)))

    words = _expert_table(peer_u, peer_v)
    n_exp = words.shape[1]
    tab = words.reshape(depth, n_exp, d // LANE, 1, LANE)
    tab_sc = words.reshape(depth, n_exp, d // LANE, LANE)
    n_tc = nb * ta // PEER_TB - PEER_SC_BLOCKS

    xall = jnp.concatenate([ctx, x], axis=1)
    for l in range(depth):
        mod = mods[l]
        hq, cum, kk, hv, sg, sq, sk, sv, nq, nk, nv = _proj(xall, norm_mix_w[l].reshape(1, d), mod, w_in_b[l], cs, gp[l], tri)
        o = _hgrn_scan(hq, cum, kk, hv, bd)
        so = _swa(sq, sk, sv, sink_rows[l], n_ctx)
        no = _nat(nq, nk, nv, _nat_bias_tables(nat_rpb[l]), n_ctx)
        xall = _outproj(o, sg, hgrn_norm_w[l].reshape(1, HG_WIDTH), bd, so, no, wa[l], wb[l], wn[l], mod, xall)
        h2, idx, g = _route(xall, norm_ffn_w[l].reshape(1, d), mod, wq_b[l], sub_b[l])
        idx, h2, g = idx.reshape(nb * ta, PEER_SLOTS), h2.reshape(nb * ta, d), g.reshape(nb * ta, PEER_SLOTS)
        staged = _sc_gather(tab_sc[l], idx[n_tc * PEER_TB:].reshape(-1))
        xf = _experts(idx, h2, g, xall.reshape(nb * ta, d), mod, tab[l], ta, n_ctx, n_tc)
        xf = _experts_dense(staged.reshape(-1, LANE), h2, g, xf, mod, ta, n_ctx, n_tc)
        xall = xf.reshape(nb, ta, d)
    return _final_norm(xall, final_norm_w, n_ctx)
```

```python
import functools

import numpy as np
import jax
import jax.numpy as jnp
from jax import lax
from jax.experimental import pallas as pl
from jax.experimental.pallas import tpu as pltpu
from jax.experimental.pallas import tpu_sc as plsc

f32 = jnp.float32
bf16 = jnp.bfloat16
i32 = jnp.int32

LANE = 128
VMEM_LIMIT_BYTES = 56 * 1024 * 1024

GRID_W = 64
HEAD_DIM = 64
EPS = 1e-6
HG_HEADS = 4
HG_WIDTH = 256
HG_SUB = 16
SWA_HEADS = 6
SWA_KV_HEADS = 2
SWA_WINDOW = 128
SWA_BLOCK = 128
ROPE_BASE = 10000.0
NAT_HEADS = 6
NAT_ROWS = 8
NAT_COLS = 16
SWA_Q = SWA_HEADS * HEAD_DIM
SWA_KV = SWA_KV_HEADS * HEAD_DIM
NAT_W = NAT_HEADS * HEAD_DIM
PROJ_WIDTH = 5 * HG_WIDTH + SWA_Q + 2 * SWA_KV + 3 * NAT_W
PEER_HEADS = 8
PEER_NKEYS = 128
PEER_DK = 256
PEER_TOPK = 16
PEER_SLOTS = PEER_HEADS * PEER_TOPK

TM = 256
PEER_TB = 128
PEER_NBUF = 16
PEER_AHEAD = PEER_NBUF - 1
PEER_SC_BLOCKS = 132
PEER_DENSE_TB = 16
SC_CHUNK = 32
NEG = -1e30

SWA_HEAD_ORDER = (0, 3, 1, 4, 2, 5)

_NT = (((1,), (1,)), ((), ()))
_TN = (((0,), (0,)), ((), ()))


def _silu(x):
    return x * (1.0 / (1.0 + jnp.exp(-x)))


def _split2(x):
    hi = x.astype(bf16)
    lo = (x - hi.astype(f32)).astype(bf16)
    return hi, lo


def _mods_kernel(sc_ref, w_ref, b_ref, o_ref):
    s = _silu(sc_ref[...])
    o_ref[0] = jnp.dot(s.astype(bf16), w_ref[0].astype(bf16), preferred_element_type=f32) + b_ref[0]


def _mods(c, c_ctx, w_mod, b_mod):
    depth, d, d6 = w_mod.shape
    nb = c.shape[0]
    rows = jnp.zeros((8, d), f32).at[:nb].set(c).at[nb].set(c_ctx)
    tn = d6 // 4
    out = pl.pallas_call(
        _mods_kernel,
        grid=(depth, 4),
        in_specs=[pl.BlockSpec((8, d), lambda l, j: (0, 0)),
                  pl.BlockSpec((1, d, tn), lambda l, j: (l, 0, j)),
                  pl.BlockSpec((1, 1, tn), lambda l, j: (l, 0, j))],
        out_specs=pl.BlockSpec((1, 8, tn), lambda l, j: (l, 0, j)),
        out_shape=jax.ShapeDtypeStruct((depth, 8, d6), f32),
        compiler_params=pltpu.CompilerParams(dimension_semantics=("arbitrary", "arbitrary"),
                                             vmem_limit_bytes=VMEM_LIMIT_BYTES),
        name="adaln_mods",
    )(rows, w_mod, b_mod.reshape(depth, 1, d6))
    lat = out[:, :nb].reshape(depth, nb, 6, d)
    ctx = jnp.broadcast_to(out[:, nb].reshape(depth, 1, 6, d), (depth, nb, 6, d))
    return jnp.stack([ctx, lat], axis=2)


def _norm_mod(x, nw, shift, scale):
    y = x * lax.rsqrt(jnp.mean(x * x, axis=-1, keepdims=True) + EPS) * nw
    return y * (1.0 + scale) + shift


def _rope(t, cos, sins):
    lane = lax.broadcasted_iota(i32, t.shape, 1)
    sw = jnp.where((lane % 32) < 16, pltpu.roll(t, LANE - 16, 1), pltpu.roll(t, 16, 1))
    return t * cos + sw * sins


def _hgrn_gates(z, log_lb, log1m_lb, one_m_lb):
    log_sig = jnp.minimum(z, 0.0) - jnp.log1p(jnp.exp(-jnp.abs(z)))
    c = log1m_lb + log_sig
    m = jnp.maximum(log_lb, c)
    log_f = m + jnp.log1p(jnp.exp(-jnp.abs(log_lb - c)))
    k = one_m_lb * (1.0 / (1.0 + jnp.exp(z)))
    return log_f, k


def _cumsum16(tri, log_f):
    h1 = log_f.astype(bf16)
    r1 = log_f - h1.astype(f32)
    h2 = r1.astype(bf16)
    h3 = (r1 - h2.astype(f32)).astype(bf16)
    return (jnp.dot(tri, h1, preferred_element_type=f32) + jnp.dot(tri, h2, preferred_element_type=f32)
            + jnp.dot(tri, h3, preferred_element_type=f32))


def _proj_kernel(x_ref, nw_ref, mod_ref, w_ref, cs_ref, gp_ref, tri_ref,
                 hq_ref, cum_ref, kk_ref, hv_ref, sg_ref, sq_ref, sk_ref, sv_ref, nq_ref, nk_ref, nv_ref):
    h = _norm_mod(x_ref[0], nw_ref[...], mod_ref[0, 0, 0:1, :], mod_ref[0, 0, 1:2, :])
    p = jnp.dot(h.astype(bf16), w_ref[...], preferred_element_type=f32)
    w = HG_WIDTH
    hq_ref[0] = _silu(p[:, 0:w])
    for d in range(2):
        log_f, k = _hgrn_gates(p[:, (1 + d) * w:(2 + d) * w], gp_ref[3 * d:3 * d + 1, :],
                               gp_ref[3 * d + 1:3 * d + 2, :], gp_ref[3 * d + 2:3 * d + 3, :])
        cum_ref[d, 0] = _cumsum16(tri_ref[d], log_f)
        kk_ref[d, 0] = k
    hv_ref[0] = p[:, 3 * w:4 * w]
    sg_ref[0] = _silu(p[:, 4 * w:5 * w])
    cos, sins = cs_ref[0], cs_ref[1]
    o = 5 * w
    scale = HEAD_DIM ** -0.5
    for j in range(SWA_Q // LANE):
        sq_ref[0, :, j * LANE:(j + 1) * LANE] = (_rope(p[:, o + j * LANE:o + (j + 1) * LANE], cos, sins) * scale).astype(bf16)
    o += SWA_Q
    sk_ref[0] = _rope(p[:, o:o + SWA_KV], cos, sins).astype(bf16)
    o += SWA_KV
    sv_ref[0] = p[:, o:o + SWA_KV].astype(bf16)
    o += SWA_KV
    nq_ref[0] = (p[:, o:o + NAT_W] * scale).astype(bf16)
    o += NAT_W
    nk_ref[0] = p[:, o:o + NAT_W].astype(bf16)
    o += NAT_W
    nv_ref[0] = p[:, o:o + NAT_W].astype(bf16)


def _proj(xall, nw, mod, w_in, cs, gp, tri):
    nb, ta, d = xall.shape
    nt = ta // TM
    tok = lambda width, dt: jax.ShapeDtypeStruct((nb, ta, width), dt)
    tok2 = jax.ShapeDtypeStruct((2, nb, ta, HG_WIDTH), f32)
    bs = lambda width: pl.BlockSpec((1, TM, width), lambda b, i: (b, i, 0))
    bs2 = pl.BlockSpec((2, 1, TM, HG_WIDTH), lambda b, i: (0, b, i, 0))
    return pl.pallas_call(
        _proj_kernel,
        grid=(nb, nt),
        in_specs=[pl.BlockSpec((1, TM, d), lambda b, i: (b, i, 0)),
                  pl.BlockSpec((1, d), lambda b, i: (0, 0)),
                  pl.BlockSpec((1, 1, 6, d), lambda b, i: (b, jnp.minimum(i, 1), 0, 0)),
                  pl.BlockSpec((d, PROJ_WIDTH), lambda b, i: (0, 0)),
                  pl.BlockSpec((2, TM, LANE), lambda b, i: (0, i, 0)),
                  pl.BlockSpec((8, HG_WIDTH), lambda b, i: (0, 0)),
                  pl.BlockSpec((2, TM, TM), lambda b, i: (0, 0, 0))],
        out_specs=[bs(HG_WIDTH), bs2, bs2, bs(HG_WIDTH), bs(HG_WIDTH), bs(SWA_Q), bs(SWA_KV), bs(SWA_KV),
                   bs(NAT_W), bs(NAT_W), bs(NAT_W)],
        out_shape=[tok(HG_WIDTH, f32), tok2, tok2, tok(HG_WIDTH, f32), tok(HG_WIDTH, f32), tok(SWA_Q, bf16),
                   tok(SWA_KV, bf16), tok(SWA_KV, bf16), tok(NAT_W, bf16), tok(NAT_W, bf16), tok(NAT_W, bf16)],
        compiler_params=pltpu.CompilerParams(dimension_semantics=("arbitrary", "arbitrary"),
                                             vmem_limit_bytes=VMEM_LIMIT_BYTES),
        name="norm_in_proj",
    )(xall, nw, mod, w_in, cs, gp, tri)


def _hgrn_kernel(q_ref, cum_ref, k_ref, v_ref, bd_ref, o_ref, st_ref):
    d = pl.program_id(1)
    fwd = d == 0

    @pl.when(pl.program_id(2) == 0)
    def _():
        st_ref[...] = jnp.zeros_like(st_ref)

    bd = bd_ref[...]
    bd_mask = bd > 0
    sgn = jnp.where(fwd, 1, -1)
    t_sgn = lax.broadcasted_iota(i32, (HG_SUB, 1), 0) * sgn
    nsteps = TM // HG_SUB

    def step(i, carry):
        sc = jnp.where(fwd, i, nsteps - 1 - i)
        r0 = pl.multiple_of(sc * HG_SUB, HG_SUB)
        q = q_ref[0, pl.ds(r0, HG_SUB), :]
        b = cum_ref[0, 0, pl.ds(r0, HG_SUB), :]
        k = k_ref[0, 0, pl.ds(r0, HG_SUB), :]
        v = v_ref[0, pl.ds(r0, HG_SUB), :]
        b_end = jnp.where(fwd, b[HG_SUB - 1:HG_SUB, :], b[0:1, :])
        st = st_ref[...]
        o = lax.dot_general((q * jnp.exp(b)).astype(bf16), st.astype(bf16), _NT, preferred_element_type=f32)
        rows = []
        for s in range(HG_SUB):
            valid = t_sgn >= s * sgn
            decay = jnp.exp(jnp.minimum(b - b[s:s + 1, :], 0.0))
            rows.append(jnp.where(valid, decay * q * k[s:s + 1, :], 0.0))
        hi, lo = _split2(jnp.concatenate(rows, axis=0))
        pe = jnp.dot(hi, bd, preferred_element_type=f32) + jnp.dot(lo, bd, preferred_element_type=f32)
        for s in range(HG_SUB):
            o = o + pe[s * HG_SUB:(s + 1) * HG_SUB, :] * v[s:s + 1, :]
        o_ref[0, 0, pl.ds(r0, HG_SUB), :] = o
        kd = (k * jnp.exp(b_end - b)).astype(bf16)
        kv_t = lax.dot_general(v.astype(bf16), kd, _TN, preferred_element_type=f32)
        st_ref[...] = st * jnp.exp(b_end) + jnp.where(bd_mask, kv_t, 0.0)
        return carry

    lax.fori_loop(0, nsteps, step, 0, unroll=2)


def _hgrn_scan(hq, cum, kk, hv, bd):
    nb, ta, w = hq.shape
    nt = ta // TM

    def blk(d, j):
        return jnp.where(d == 0, j, jnp.where(j == 0, 0, nt - j))

    return pl.pallas_call(
        _hgrn_kernel,
        grid=(nb, 2, nt),
        in_specs=[pl.BlockSpec((1, TM, w), lambda b, d, j: (b, blk(d, j), 0)),
                  pl.BlockSpec((1, 1, TM, w), lambda b, d, j: (d, b, blk(d, j), 0)),
                  pl.BlockSpec((1, 1, TM, w), lambda b, d, j: (d, b, blk(d, j), 0)),
                  pl.BlockSpec((1, TM, w), lambda b, d, j: (b, blk(d, j), 0)),
                  pl.BlockSpec((w, w), lambda b, d, j: (0, 0))],
        out_specs=pl.BlockSpec((1, 1, TM, w), lambda b, d, j: (d, b, blk(d, j), 0)),
        out_shape=jax.ShapeDtypeStruct((2, nb, ta, w), f32),
        scratch_shapes=[pltpu.VMEM((w, w), f32)],
        compiler_params=pltpu.CompilerParams(dimension_semantics=("arbitrary", "arbitrary", "arbitrary")),
        name="hgrn_scan",
    )(hq, cum, kk, hv, bd)


def _pair_rows(q2):
    lane = lax.broadcasted_iota(i32, q2.shape, 1)
    zero = jnp.zeros_like(q2)
    return jnp.concatenate([jnp.where(lane < HEAD_DIM, q2, zero), jnp.where(lane >= HEAD_DIM, q2, zero)], axis=0)


def _pair_merge(o):
    m = o.shape[0] // 2
    lane = lax.broadcasted_iota(i32, (m, LANE), 1)
    return jnp.where(lane < HEAD_DIM, o[:m], o[m:])


def _joint_attention(s_loc, s_ctx, v_loc, v_ctx, sink=None):
    m = jnp.maximum(jnp.max(s_loc, axis=-1, keepdims=True), jnp.max(s_ctx, axis=-1, keepdims=True))
    if sink is not None:
        m = jnp.maximum(m, sink)
    p_loc = jnp.exp(s_loc - m)
    p_ctx = jnp.exp(s_ctx - m)
    den = jnp.sum(p_loc, axis=-1, keepdims=True) + jnp.sum(p_ctx, axis=-1, keepdims=True)
    if sink is not None:
        den = den + jnp.exp(sink - m)
    o = (jnp.dot(p_loc.astype(bf16), v_loc, preferred_element_type=f32)
         + jnp.dot(p_ctx.astype(bf16), v_ctx, preferred_element_type=f32))
    return o / den


def _swa_kernel(q_ref, kp_ref, kc_ref, kn_ref, vp_ref, vc_ref, vn_ref, kx_ref, vx_ref, sink_ref, o_ref, *, n_ctx_blocks, seq):
    g = pl.program_id(1)
    i = g - n_ctx_blocks
    k_loc = jnp.concatenate([kp_ref[0], kc_ref[0], kn_ref[0]], axis=0)
    v_loc = jnp.concatenate([vp_ref[0], vc_ref[0], vn_ref[0]], axis=0)
    row = lax.broadcasted_iota(i32, (SWA_BLOCK, 3 * SWA_BLOCK), 0)
    col = lax.broadcasted_iota(i32, (SWA_BLOCK, 3 * SWA_BLOCK), 1)
    rel = col - SWA_BLOCK - row
    kpos = (i - 1) * SWA_BLOCK + col
    ok = (jnp.abs(rel) <= SWA_WINDOW) & (kpos >= 0) & (kpos < seq) & (i >= 0)
    pen = jnp.where(ok, 0.0, NEG)
    pen = jnp.concatenate([pen, pen], axis=0)
    for j in range(SWA_Q // LANE):
        q = _pair_rows(q_ref[0, :, j * LANE:(j + 1) * LANE])
        s_loc = lax.dot_general(q, k_loc, _NT, preferred_element_type=f32) + pen
        s_ctx = lax.dot_general(q, kx_ref[0], _NT, preferred_element_type=f32)
        sink = jnp.concatenate([jnp.broadcast_to(sink_ref[2 * j:2 * j + 1, 0:1], (SWA_BLOCK, 1)),
                                jnp.broadcast_to(sink_ref[2 * j + 1:2 * j + 2, 0:1], (SWA_BLOCK, 1))], axis=0)
        o = _joint_attention(s_loc, s_ctx, v_loc, vx_ref[0], sink)
        o_ref[0, :, j * LANE:(j + 1) * LANE] = _pair_merge(o).astype(bf16)


def _swa(sq, sk, sv, sink_rows, n_ctx):
    nb, ta, _ = sq.shape
    ncb = n_ctx // SWA_BLOCK
    nblk = ta // SWA_BLOCK
    nlat = nblk - ncb
    prev = lambda b, g: (b, jnp.clip(g - ncb - 1, 0, nlat - 1) + ncb, 0)
    cur = lambda b, g: (b, g, 0)
    nxt = lambda b, g: (b, jnp.clip(g - ncb + 1, 0, nlat - 1) + ncb, 0)
    kvs = lambda f: pl.BlockSpec((1, SWA_BLOCK, SWA_KV), f)
    ctx = pl.BlockSpec((1, n_ctx, SWA_KV), lambda b, g: (b, 0, 0))
    return pl.pallas_call(
        functools.partial(_swa_kernel, n_ctx_blocks=ncb, seq=ta - n_ctx),
        grid=(nb, nblk),
        in_specs=[pl.BlockSpec((1, SWA_BLOCK, SWA_Q), cur), kvs(prev), kvs(cur), kvs(nxt), kvs(prev), kvs(cur), kvs(nxt),
                  ctx, ctx, pl.BlockSpec((8, LANE), lambda b, g: (0, 0))],
        out_specs=pl.BlockSpec((1, SWA_BLOCK, SWA_Q), cur),
        out_shape=jax.ShapeDtypeStruct((nb, ta, SWA_Q), bf16),
        compiler_params=pltpu.CompilerParams(dimension_semantics=("arbitrary", "arbitrary")),
        name="swa_attention",
    )(sq, sk, sk, sk, sv, sv, sv, sk, sv, sink_rows)


def _nat_kernel(q_ref, k_ref, v_ref, bias_ref, o_ref, *, n_ctx, n_rows):
    g = pl.program_id(1)
    is_ctx = g == 0
    ctx_pen = jnp.where(is_ctx, NEG, 0.0)
    rows_per_tile = TM // GRID_W
    nwin = NAT_ROWS * GRID_W
    for rr in range(rows_per_tile):
        r = jnp.maximum((g - 1) * rows_per_tile + rr, 0)
        rs = jnp.clip(r - NAT_ROWS // 2, 0, n_rows - NAT_ROWS)
        dr0 = rs - r + NAT_ROWS - 1
        start = pl.multiple_of(n_ctx + rs * GRID_W, GRID_W)
        for p in range(NAT_W // LANE):
            ls = slice(p * LANE, (p + 1) * LANE)
            q = _pair_rows(q_ref[0, rr * GRID_W:(rr + 1) * GRID_W, ls])
            s_loc = (lax.dot_general(q, k_ref[0, pl.ds(start, nwin), ls], _NT, preferred_element_type=f32)
                     + bias_ref[p, dr0] + ctx_pen)
            s_ctx = lax.dot_general(q, k_ref[0, 0:n_ctx, ls], _NT, preferred_element_type=f32)
            o = _joint_attention(s_loc, s_ctx, v_ref[0, pl.ds(start, nwin), ls], v_ref[0, 0:n_ctx, ls])
            o_ref[0, rr * GRID_W:(rr + 1) * GRID_W, ls] = _pair_merge(o).astype(bf16)


def _nat(nq, nk, nv, bias, n_ctx):
    nb, ta, w = nq.shape
    nt = ta // TM
    whole = pl.BlockSpec((1, ta, w), lambda b, g: (b, 0, 0), pipeline_mode=pl.Buffered(1))
    return pl.pallas_call(
        functools.partial(_nat_kernel, n_ctx=n_ctx, n_rows=(ta - n_ctx) // GRID_W),
        grid=(nb, nt),
        in_specs=[pl.BlockSpec((1, TM, w), lambda b, g: (b, g, 0)), whole, whole,
                  pl.BlockSpec(bias.shape, lambda b, g: (0, 0, 0, 0), pipeline_mode=pl.Buffered(1))],
        out_specs=pl.BlockSpec((1, TM, w), lambda b, g: (b, g, 0)),
        out_shape=jax.ShapeDtypeStruct((nb, ta, w), bf16),
        compiler_params=pltpu.CompilerParams(dimension_semantics=("arbitrary", "arbitrary"),
                                             vmem_limit_bytes=VMEM_LIMIT_BYTES),
        name="nat_attention",
    )(nq, nk, nv, bias)


def _nat_bias_tables(rpb):
    c = np.arange(GRID_W)[:, None]
    kc = np.arange(GRID_W)[None, :]
    ws = np.clip(c - NAT_COLS // 2, 0, GRID_W - NAT_COLS)
    valid = (kc >= ws) & (kc < ws + NAT_COLS)
    col_idx = np.clip(kc - c + NAT_COLS - 1, 0, 2 * NAT_COLS - 2)
    dr = np.arange(NAT_ROWS)[:, None] + np.arange(NAT_ROWS)[None, :]
    t = rpb.astype(f32)[:, dr][:, :, :, col_idx]
    t = jnp.where(jnp.asarray(valid)[None, None, None], t, NEG)
    t = t.transpose(0, 1, 3, 2, 4).reshape(NAT_HEADS // 2, 2, NAT_ROWS, GRID_W, NAT_ROWS * GRID_W)
    return t.transpose(0, 2, 1, 3, 4).reshape(NAT_HEADS // 2, NAT_ROWS, 2 * GRID_W, NAT_ROWS * GRID_W)


def _outproj_kernel(o_ref, sg_ref, hnw_ref, bd_ref, so_ref, no_ref, wa_ref, wb_ref, wn_ref, mod_ref, x_ref, y_ref):
    oo = o_ref[0, 0] + o_ref[1, 0]
    hi, lo = _split2(oo * oo)
    bd = bd_ref[...]
    msq = (jnp.dot(hi, bd, preferred_element_type=f32) + jnp.dot(lo, bd, preferred_element_type=f32)) * (1.0 / HEAD_DIM)
    a = oo * lax.rsqrt(msq + EPS) * hnw_ref[...] * sg_ref[0]
    y = (jnp.dot(a.astype(bf16), wa_ref[...], preferred_element_type=f32)
         + jnp.dot(so_ref[0], wb_ref[...], preferred_element_type=f32)
         + jnp.dot(no_ref[0], wn_ref[...], preferred_element_type=f32))
    y_ref[0] = x_ref[0] + mod_ref[0, 0, 2:3, :] * y


def _outproj(o, sg, hnw, bd, so, no, wa, wb, wn, mod, xall):
    nb, ta, d = xall.shape
    nt = ta // TM
    bs = lambda width: pl.BlockSpec((1, TM, width), lambda b, i: (b, i, 0))
    full = lambda a: pl.BlockSpec(a.shape, lambda b, i: (0,) * a.ndim)
    return pl.pallas_call(
        _outproj_kernel,
        grid=(nb, nt),
        in_specs=[pl.BlockSpec((2, 1, TM, HG_WIDTH), lambda b, i: (0, b, i, 0)), bs(HG_WIDTH), full(hnw), full(bd),
                  bs(SWA_Q), bs(NAT_W), full(wa), full(wb), full(wn),
                  pl.BlockSpec((1, 1, 6, d), lambda b, i: (b, jnp.minimum(i, 1), 0, 0)), bs(d)],
        out_specs=bs(d),
        out_shape=jax.ShapeDtypeStruct(xall.shape, f32),
        compiler_params=pltpu.CompilerParams(dimension_semantics=("arbitrary", "arbitrary"),
                                             vmem_limit_bytes=VMEM_LIMIT_BYTES),
        name="out_proj",
    )(o, sg, hnw, bd, so, no, wa, wb, wn, mod, xall)


def _top16_rows(vals, payload=None):
    n = vals.shape[0]
    iota = lax.broadcasted_iota(i32, vals.shape, 0)
    best, picked = [], []
    for _ in range(PEER_TOPK):
        m = jnp.max(vals, axis=0, keepdims=True)
        idx = jnp.min(jnp.where(vals == m, iota, n), axis=0, keepdims=True)
        hit = iota == idx
        best.append(m)
        picked.append(idx if payload is None else jnp.max(jnp.where(hit, payload, -1), axis=0, keepdims=True))
        vals = jnp.where(hit, -jnp.inf, vals)
    return jnp.concatenate(best, axis=0), jnp.concatenate(picked, axis=0)


def _product_candidates(first, second, combine):
    half = PEER_TOPK // 2
    rows = [combine(first[0:1, :], second)]
    rows += [combine(first[a:a + 1, :], second[0:half, :]) for a in range(1, half)]
    rows.append(combine(first[half:, :], second[0:1, :]))
    return jnp.concatenate(rows, axis=0)


def _route_kernel(x_ref, nw_ref, mod_ref, wq_ref, sub_ref, h_ref, idx_ref, g_ref):
    h = _norm_mod(x_ref[0], nw_ref[...], mod_ref[0, 0, 3:4, :], mod_ref[0, 0, 4:5, :])
    h_ref[0] = h
    q = jnp.dot(h.astype(bf16), wq_ref[...], preferred_element_type=f32).astype(bf16)
    half = PEER_DK // 2
    ids, gates = [], []
    for hh in range(PEER_HEADS):
        top_s, top_i = [], []
        for p in range(2):
            c0 = (hh * 2 + p) * half
            s = lax.dot_general(sub_ref[hh, p], q[:, c0:c0 + half], _NT, preferred_element_type=f32)
            ts, ti = _top16_rows(s)
            top_s.append(ts)
            top_i.append(ti)
        cand_s = _product_candidates(top_s[0], top_s[1], lambda a, b: a + b)
        cand_i = _product_candidates(top_i[0], top_i[1], lambda a, b: a * PEER_NKEYS + b)
        best, eid = _top16_rows(cand_s, cand_i)
        ids.append(eid)
        e = jnp.exp(best - best[0:1, :])
        gates.append(e / jnp.sum(e, axis=0, keepdims=True))
    idx_ref[0] = jnp.concatenate(ids, axis=0).T
    g_ref[0] = jnp.concatenate(gates, axis=0).T


def _route(xall, nw, mod, wq, sub):
    nb, ta, d = xall.shape
    nt = ta // TM
    full = lambda a: pl.BlockSpec(a.shape, lambda b, i: (0,) * a.ndim)
    return pl.pallas_call(
        _route_kernel,
        grid=(nb, nt),
        in_specs=[pl.BlockSpec((1, TM, d), lambda b, i: (b, i, 0)), full(nw),
                  pl.BlockSpec((1, 1, 6, d), lambda b, i: (b, jnp.minimum(i, 1), 0, 0)), full(wq), full(sub)],
        out_specs=[pl.BlockSpec((1, TM, d), lambda b, i: (b, i, 0)),
                   pl.BlockSpec((1, TM, PEER_SLOTS), lambda b, i: (b, i, 0)),
                   pl.BlockSpec((1, TM, PEER_SLOTS), lambda b, i: (b, i, 0))],
        out_shape=[jax.ShapeDtypeStruct(xall.shape, f32), jax.ShapeDtypeStruct((nb, ta, PEER_SLOTS), i32),
                   jax.ShapeDtypeStruct((nb, ta, PEER_SLOTS), f32)],
        compiler_params=pltpu.CompilerParams(dimension_semantics=("arbitrary", "arbitrary"),
                                             vmem_limit_bytes=VMEM_LIMIT_BYTES),
        name="peer_route",
    )(xall, nw, mod, wq, sub)


def _expert_kernel(idx_ref, idx_next_ref, h_ref, g_ref, x_ref, mod_ref, tab_hbm, o_ref, buf, sem):
    d = h_ref.shape[-1]
    nct = d // LANE
    groups = PEER_SLOTS // 8

    per_piece = PEER_SLOTS // nct

    def issue(ids_ref, t, slot, piece):
        for k in range(piece * per_piece, (piece + 1) * per_piece):
            pltpu.make_async_copy(tab_hbm.at[ids_ref[t, k]], buf.at[slot, k // 8, :, pl.ds(k % 8, 1), :],
                                  sem.at[slot]).start(priority=k % 2)

    def wait(slot):
        pltpu.make_async_copy(buf.at[slot], buf.at[slot], sem.at[slot]).wait()

    def cols(slot, c, high):
        words = jnp.concatenate([jnp.concatenate([buf[slot, kg, c], buf[slot, kg, c + 1]], axis=1)
                                 for kg in range(groups)], axis=0)
        bits = (words & jnp.uint32(0xFFFF0000)) if high else (words << 16)
        return lax.bitcast_convert_type(bits, f32).astype(bf16)

    not_last = pl.program_id(0) < pl.num_programs(0) - 1

    def step(t, w8, do_issue, do_u, do_v):
        nslot = lax.rem(t + 1, PEER_NBUF)
        slot = lax.rem(t, PEER_NBUF)
        islot = lax.rem(t + PEER_AHEAD, PEER_NBUF)
        if do_issue == "next":
            @pl.when(not_last)
            def _():
                for j in range(nct):
                    issue(idx_next_ref, t + PEER_AHEAD - PEER_TB, islot, j)
        if do_u:
            wait(nslot)
            h8 = jnp.broadcast_to(h_ref[pl.ds(t + 1, 1), :], (8, d)).astype(bf16)
            gate = g_ref[pl.ds(t + 1, 1), :]
            a = jnp.zeros((8, PEER_SLOTS), f32)
        ys = []
        for j in range(nct):
            if 2 * j < nct:
                c = 2 * j
                if do_u:
                    a = a + lax.dot_general(h8[:, c * LANE:(c + 2) * LANE], cols(nslot, c, True), _NT, preferred_element_type=f32)
                if do_v:
                    ys.append(jnp.dot(w8, cols(slot, c, False), preferred_element_type=f32)[0:1, :])
            if do_issue is True:
                issue(idx_ref, t + PEER_AHEAD, islot, j)
        if do_v:
            o_ref[pl.ds(t, 1), :] = jnp.concatenate(ys, axis=1)
        if do_u:
            a1 = a[0:1, :]
            w = gate * (0.5 * a1 * (1.0 + lax.erf(a1 * (2.0 ** -0.5))))
            w8 = jnp.broadcast_to(w, (8, PEER_SLOTS)).astype(bf16)
        return w8

    @pl.when(pl.program_id(0) == 0)
    def _():
        def prologue(t, carry):
            for j in range(nct):
                issue(idx_ref, t, t, j)
            return carry

        lax.fori_loop(0, PEER_AHEAD, prologue, 0)

    w8 = step(-1, jnp.zeros((8, PEER_SLOTS), bf16), False, True, False)
    w8 = lax.fori_loop(0, PEER_TB - PEER_AHEAD, lambda t, w: step(t, w, True, True, True), w8)
    w8 = lax.fori_loop(PEER_TB - PEER_AHEAD, PEER_TB - 1, lambda t, w: step(t, w, "next", True, True), w8)
    step(PEER_TB - 1, w8, "next", False, True)
    o_ref[...] = x_ref[...] + mod_ref[0, 0, 5:6, :] * o_ref[...]


def _mod_index(tokens_per_step, ta, n_ctx, first_step=0):
    spb = ta // tokens_per_step
    cs = n_ctx // tokens_per_step
    return lambda i: ((i + first_step) // spb, jnp.where((i + first_step) % spb < cs, 0, 1), 0, 0)


def _experts(idx, h, g, xflat, mod, tab, ta, n_ctx, nblocks):
    n, d = xflat.shape
    tokb = lambda width: pl.BlockSpec((PEER_TB, width), lambda i: (i, 0))
    return pl.pallas_call(
        _expert_kernel,
        grid=(nblocks,),
        in_specs=[pl.BlockSpec((PEER_TB, PEER_SLOTS), lambda i: (i, 0), memory_space=pltpu.SMEM),
                  pl.BlockSpec((PEER_TB, PEER_SLOTS), lambda i: (jnp.minimum(i + 1, nblocks - 1), 0), memory_space=pltpu.SMEM),
                  tokb(d), tokb(PEER_SLOTS), tokb(d),
                  pl.BlockSpec((1, 1, 6, d), _mod_index(PEER_TB, ta, n_ctx)),
                  pl.BlockSpec(memory_space=pl.ANY)],
        out_specs=tokb(d),
        out_shape=jax.ShapeDtypeStruct((n, d), f32),
        input_output_aliases={4: 0},
        scratch_shapes=[pltpu.VMEM((PEER_NBUF, PEER_SLOTS // 8, d // LANE, 8, LANE), jnp.uint32),
                        pltpu.SemaphoreType.DMA((PEER_NBUF,))],
        compiler_params=pltpu.CompilerParams(dimension_semantics=("arbitrary",), vmem_limit_bytes=VMEM_LIMIT_BYTES),
        name="peer_experts",
    )(idx, idx, h, g, xflat, mod, tab)


def _sc_gather(table, idx):
    info = plsc.get_sparse_core_info()
    nc = info.num_cores
    nw = nc * info.num_subcores
    r = idx.shape[0]
    per_w = r // nw
    nch = per_w // SC_CHUNK
    assert per_w * nw == r and nch * SC_CHUNK == per_w and nch % 2 == 0
    rows_shape = (SC_CHUNK,) + table.shape[1:]
    mesh = plsc.VectorSubcoreMesh(core_axis_name="c", subcore_axis_name="s")

    @functools.partial(
        pl.kernel, mesh=mesh, out_type=jax.ShapeDtypeStruct((r,) + table.shape[1:], table.dtype),
        scratch_types=[pltpu.VMEM((2, SC_CHUNK), i32), pltpu.VMEM((2,) + rows_shape, table.dtype),
                       pltpu.SemaphoreType.DMA((2,)), pltpu.SemaphoreType.DMA((2,)), pltpu.SemaphoreType.DMA((2,))])
    def gather_kernel(table_hbm, idx_hbm, out_hbm, idx_v, rows_v, isem, gsem, wsem):
        base = (lax.axis_index("s") * nc + lax.axis_index("c")) * per_w

        def rows_of(j):
            return pl.ds(pl.multiple_of(base + j * SC_CHUNK, 8), SC_CHUNK)

        def fetch_idx(j, b):
            return pltpu.make_async_copy(idx_hbm.at[rows_of(j)], idx_v.at[b], isem.at[b])

        def gather(b):
            return pltpu.make_async_copy(table_hbm.at[idx_v.at[b]], rows_v.at[b], gsem.at[b])

        def write(j, b):
            return pltpu.make_async_copy(rows_v.at[b], out_hbm.at[rows_of(j)], wsem.at[b])

        fetch_idx(0, 0).start()
        fetch_idx(1, 1).start()
        fetch_idx(0, 0).wait()
        gather(0).start()

        @pl.loop(0, nch, step=2)
        def _(j0):
            for b in (0, 1):
                j = j0 + b

                @pl.when(j >= 1)
                def _():
                    write(j - 1, 1 - b).wait()

                @pl.when(j + 1 < nch)
                def _():
                    fetch_idx(j + 1, 1 - b).wait()
                    gather(1 - b).start()

                gather(b).wait()

                @pl.when(j + 2 < nch)
                def _():
                    fetch_idx(j + 2, b).start()

                write(j, b).start()

        write(nch - 1, 1).wait()

    return gather_kernel(table, idx)


def _expert_dense_kernel(st_ref, h_ref, g_ref, x_ref, mod_ref, o_ref):
    d = h_ref.shape[-1]
    nct = d // LANE
    groups = PEER_SLOTS // 8
    rows_per_token = PEER_SLOTS * nct

    def cols(t, c, high):
        base = t * rows_per_token
        tile = lambda kg, cc: st_ref[pl.ds(base + kg * 8 * nct + cc, 8, stride=nct), :]
        words = jnp.concatenate([jnp.concatenate([tile(kg, c), tile(kg, c + 1)], axis=1) for kg in range(groups)], axis=0)
        bits = (words & jnp.uint32(0xFFFF0000)) if high else (words << 16)
        return lax.bitcast_convert_type(bits, f32).astype(bf16)

    def u_stage(t):
        h8 = jnp.broadcast_to(h_ref[pl.ds(t, 1), :], (8, d)).astype(bf16)
        a = jnp.zeros((8, PEER_SLOTS), f32)
        for c in range(0, nct, 2):
            a = a + lax.dot_general(h8[:, c * LANE:(c + 2) * LANE], cols(t, c, True), _NT, preferred_element_type=f32)
        a1 = a[0:1, :]
        return g_ref[pl.ds(t, 1), :] * (0.5 * a1 * (1.0 + lax.erf(a1 * (2.0 ** -0.5))))

    lane = lax.broadcasted_iota(i32, (8, LANE), 1)
    sub = lax.broadcasted_iota(i32, (8, LANE), 0)

    def v_stage(t, w_row):
        base = t * rows_per_token
        wb = jnp.broadcast_to(w_row, (8, LANE))
        acc = [None] * nct
        for kg in range(groups):
            wcol = jnp.sum(jnp.where(lane == kg * 8 + sub, wb, 0.0), axis=-1, keepdims=True)
            for c in range(nct):
                words = st_ref[pl.ds(base + kg * 8 * nct + c, 8, stride=nct), :]
                term = lax.bitcast_convert_type(words << 16, f32) * wcol
                acc[c] = term if acc[c] is None else acc[c] + term
        o_ref[pl.ds(t, 1), :] = jnp.concatenate([jnp.sum(a, axis=0, keepdims=True) for a in acc], axis=1)

    def body(t, w_row):
        w_next = u_stage(t + 1)
        v_stage(t, w_row)
        return w_next

    w_row = lax.fori_loop(0, PEER_DENSE_TB - 1, body, u_stage(0), unroll=5)
    v_stage(PEER_DENSE_TB - 1, w_row)
    o_ref[...] = x_ref[...] + mod_ref[0, 0, 5:6, :] * o_ref[...]


def _experts_dense(staged, h, g, xflat, mod, ta, n_ctx, first_block):
    n, d = xflat.shape
    rows_per_token = PEER_SLOTS * (d // LANE)
    steps = staged.shape[0] // (PEER_DENSE_TB * rows_per_token)
    off = first_block * PEER_TB // PEER_DENSE_TB
    tokb = lambda width: pl.BlockSpec((PEER_DENSE_TB, width), lambda i: (i + off, 0))
    return pl.pallas_call(
        _expert_dense_kernel,
        grid=(steps,),
        in_specs=[pl.BlockSpec((PEER_DENSE_TB * rows_per_token, LANE), lambda i: (i, 0)),
                  tokb(d), tokb(PEER_SLOTS), tokb(d),
                  pl.BlockSpec((1, 1, 6, d), _mod_index(PEER_DENSE_TB, ta, n_ctx, off))],
        out_specs=tokb(d),
        out_shape=jax.ShapeDtypeStruct((n, d), f32),
        input_output_aliases={3: 0},
        compiler_params=pltpu.CompilerParams(dimension_semantics=("arbitrary",), vmem_limit_bytes=VMEM_LIMIT_BYTES),
        name="peer_experts_dense",
    )(staged, h, g, xflat, mod)


def _expert_table(u, v):
    half = lambda a: lax.bitcast_convert_type(a.astype(bf16), jnp.uint16).astype(jnp.uint32)
    return (half(u) << 16) | half(v)


def _final_norm_kernel(x_ref, w_ref, o_ref):
    x = x_ref[0]
    o_ref[0] = x * lax.rsqrt(jnp.mean(x * x, axis=-1, keepdims=True) + EPS) * w_ref[...]


def _final_norm(xall, w, n_ctx):
    nb, ta, d = xall.shape
    off = n_ctx // TM
    return pl.pallas_call(
        _final_norm_kernel,
        grid=(nb, (ta - n_ctx) // TM),
        in_specs=[pl.BlockSpec((1, TM, d), lambda b, i: (b, i + off, 0)), pl.BlockSpec((1, d), lambda b, i: (0, 0))],
        out_specs=pl.BlockSpec((1, TM, d), lambda b, i: (b, i, 0)),
        out_shape=jax.ShapeDtypeStruct((nb, ta - n_ctx, d), f32),
        compiler_params=pltpu.CompilerParams(dimension_semantics=("arbitrary", "arbitrary")),
        name="final_norm",
    )(xall, w.reshape(1, d))


def _rope_tables(n_ctx, seq):
    nf = HEAD_DIM // 4
    pos = np.arange(seq)
    inv = ROPE_BASE ** (-np.arange(nf, dtype=np.float32) / nf)
    ar = (pos // GRID_W).astype(np.float32)[:, None] * inv
    ac = (pos % GRID_W).astype(np.float32)[:, None] * inv
    cos = np.concatenate([np.cos(ar), np.cos(ar), np.cos(ac), np.cos(ac)], axis=-1)
    sin = np.concatenate([-np.sin(ar), np.sin(ar), -np.sin(ac), np.sin(ac)], axis=-1)
    cos = np.concatenate([np.ones((n_ctx, HEAD_DIM), np.float32), cos.astype(np.float32)], axis=0)
    sin = np.concatenate([np.zeros((n_ctx, HEAD_DIM), np.float32), sin.astype(np.float32)], axis=0)
    return jnp.asarray(np.stack([np.tile(cos, (1, 2)), np.tile(sin, (1, 2))]))


def _tri_tables():
    r = np.arange(TM)[:, None]
    c = np.arange(TM)[None, :]
    same = (r // HG_SUB) == (c // HG_SUB)
    return jnp.asarray(np.stack([same & (c <= r), same & (c >= r)]).astype(np.float32), dtype=bf16)


def _block_diag_ones():
    r = np.arange(HG_WIDTH)
    return jnp.asarray((r[:, None] // HEAD_DIM == r[None, :] // HEAD_DIM).astype(np.float32), dtype=bf16)


def _swa_perm():
    return np.concatenate([np.arange(h * HEAD_DIM, (h + 1) * HEAD_DIM) for h in SWA_HEAD_ORDER])


def kernel(x, c, ctx, c_ctx, w_mod, b_mod, norm_mix_w, norm_ffn_w, w_in, hgrn_lb_logits, hgrn_norm_w, swa_sink, nat_rpb,
           w_out, peer_wq, peer_subkeys, peer_u, peer_v, final_norm_w):
    nb, seq, d = x.shape
    n_ctx = ctx.shape[1]
    depth = w_in.shape[0]
    ta = n_ctx + seq
    assert n_ctx == TM and seq % TM == 0 and seq % (GRID_W * NAT_ROWS) == 0 and d % LANE == 0
    assert w_in.shape[-1] == PROJ_WIDTH and peer_wq.shape[-1] == PEER_HEADS * PEER_DK

    mods = _mods(c, c_ctx, w_mod, b_mod)
    cs = _rope_tables(n_ctx, seq)
    tri = _tri_tables()
    bd = _block_diag_ones()
    perm = _swa_perm()
    q0 = 5 * HG_WIDTH
    col_perm = np.concatenate([np.arange(q0), q0 + perm, np.arange(q0 + SWA_Q, PROJ_WIDTH)])
    w_in_b = w_in[:, :, col_perm].astype(bf16)
    w_out_b = w_out.astype(bf16)
    wa = w_out_b[:, :HG_WIDTH]
    wb = w_out_b[:, HG_WIDTH:HG_WIDTH + SWA_Q][:, perm]
    wn = w_out_b[:, HG_WIDTH + SWA_Q:]
    wq_b = peer_wq.astype(bf16)
    sub_b = peer_subkeys.astype(bf16)
    lb = jnp.cumsum(jax.nn.softmax(hgrn_lb_logits.astype(f32), axis=1), axis=1)
    lb = lb - lb[:, :1]
    gp = jnp.stack([jnp.log(lb[0]), jnp.log1p(-lb[0]), 1.0 - lb[0], jnp.log(lb[1]), jnp.log1p(-lb[1]), 1.0 - lb[1],
                    jnp.zeros_like(lb[0]), jnp.zeros_like(lb[0])], axis=1)
    sink_rows = jnp.zeros((depth, 8, LANE), f32).at[:, :SWA_HEADS].set(
        jnp.broadcast_to(swa_sink.astype(f32)[:, list(SWA_HEAD_ORDER), None], (depth, SWA_HEADS, LANE)))

    words = _expert_table(peer_u, peer_v)
    n_exp = words.shape[1]
    tab = words.reshape(depth, n_exp, d // LANE, 1, LANE)
    tab_sc = words.reshape(depth, n_exp, d // LANE, LANE)
    n_tc = nb * ta // PEER_TB - PEER_SC_BLOCKS

    xall = jnp.concatenate([ctx, x], axis=1)
    for l in range(depth):
        mod = mods[l]
        hq, cum, kk, hv, sg, sq, sk, sv, nq, nk, nv = _proj(xall, norm_mix_w[l].reshape(1, d), mod, w_in_b[l], cs, gp[l], tri)
        o = _hgrn_scan(hq, cum, kk, hv, bd)
        so = _swa(sq, sk, sv, sink_rows[l], n_ctx)
        no = _nat(nq, nk, nv, _nat_bias_tables(nat_rpb[l]), n_ctx)
        xall = _outproj(o, sg, hgrn_norm_w[l].reshape(1, HG_WIDTH), bd, so, no, wa[l], wb[l], wn[l], mod, xall)
        h2, idx, g = _route(xall, norm_ffn_w[l].reshape(1, d), mod, wq_b[l], sub_b[l])
        idx, h2, g = idx.reshape(nb * ta, PEER_SLOTS), h2.reshape(nb * ta, d), g.reshape(nb * ta, PEER_SLOTS)
        staged = _sc_gather(tab_sc[l], idx[n_tc * PEER_TB:].reshape(-1))
        xf = _experts(idx, h2, g, xall.reshape(nb * ta, d), mod, tab[l], ta, n_ctx, n_tc)
        xf = _experts_dense(staged.reshape(-1, LANE), h2, g, xf, mod, ta, n_ctx, n_tc)
        xall = xf.reshape(nb, ta, d)
    return _final_norm(xall, final_norm_w, n_ctx)
```

```python
import functools

import numpy as np
import jax
import jax.numpy as jnp
from jax import lax
from jax.experimental import pallas as pl
from jax.experimental.pallas import tpu as pltpu
from jax.experimental.pallas import tpu_sc as plsc

f32 = jnp.float32
bf16 = jnp.bfloat16
i32 = jnp.int32

LANE = 128
VMEM_LIMIT_BYTES = 56 * 1024 * 1024

GRID_W = 64
HEAD_DIM = 64
EPS = 1e-6
HG_HEADS = 4
HG_WIDTH = 256
HG_SUB = 16
SWA_HEADS = 6
SWA_KV_HEADS = 2
SWA_WINDOW = 128
SWA_BLOCK = 128
ROPE_BASE = 10000.0
NAT_HEADS = 6
NAT_ROWS = 8
NAT_COLS = 16
SWA_Q = SWA_HEADS * HEAD_DIM
SWA_KV = SWA_KV_HEADS * HEAD_DIM
NAT_W = NAT_HEADS * HEAD_DIM
PROJ_WIDTH = 5 * HG_WIDTH + SWA_Q + 2 * SWA_KV + 3 * NAT_W
PEER_HEADS = 8
PEER_NKEYS = 128
PEER_DK = 256
PEER_TOPK = 16
PEER_SLOTS = PEER_HEADS * PEER_TOPK

TM = 256
PEER_TB = 128
PEER_NBUF = 16
PEER_AHEAD = PEER_NBUF - 1
PEER_SC_BLOCKS = 132
PEER_DENSE_TB = 16
SC_CHUNK = 32
NEG = -1e30

SWA_HEAD_ORDER = (0, 3, 1, 4, 2, 5)

_NT = (((1,), (1,)), ((), ()))
_TN = (((0,), (0,)), ((), ()))


def _silu(x):
    return x * (1.0 / (1.0 + jnp.exp(-x)))


def _split2(x):
    hi = x.astype(bf16)
    lo = (x - hi.astype(f32)).astype(bf16)
    return hi, lo


def _mods_kernel(sc_ref, w_ref, b_ref, o_ref):
    s = _silu(sc_ref[...])
    o_ref[0] = jnp.dot(s.astype(bf16), w_ref[0].astype(bf16), preferred_element_type=f32) + b_ref[0]


def _mods(c, c_ctx, w_mod, b_mod):
    depth, d, d6 = w_mod.shape
    nb = c.shape[0]
    rows = jnp.zeros((8, d), f32).at[:nb].set(c).at[nb].set(c_ctx)
    tn = d6 // 4
    out = pl.pallas_call(
        _mods_kernel,
        grid=(depth, 4),
        in_specs=[pl.BlockSpec((8, d), lambda l, j: (0, 0)),
                  pl.BlockSpec((1, d, tn), lambda l, j: (l, 0, j)),
                  pl.BlockSpec((1, 1, tn), lambda l, j: (l, 0, j))],
        out_specs=pl.BlockSpec((1, 8, tn), lambda l, j: (l, 0, j)),
        out_shape=jax.ShapeDtypeStruct((depth, 8, d6), f32),
        compiler_params=pltpu.CompilerParams(dimension_semantics=("arbitrary", "arbitrary"),
                                             vmem_limit_bytes=VMEM_LIMIT_BYTES),
        name="adaln_mods",
    )(rows, w_mod, b_mod.reshape(depth, 1, d6))
    lat = out[:, :nb].reshape(depth, nb, 6, d)
    ctx = jnp.broadcast_to(out[:, nb].reshape(depth, 1, 6, d), (depth, nb, 6, d))
    return jnp.stack([ctx, lat], axis=2)


def _norm_mod(x, nw, shift, scale):
    y = x * lax.rsqrt(jnp.mean(x * x, axis=-1, keepdims=True) + EPS) * nw
    return y * (1.0 + scale) + shift


def _rope(t, cos, sins):
    lane = lax.broadcasted_iota(i32, t.shape, 1)
    sw = jnp.where((lane % 32) < 16, pltpu.roll(t, LANE - 16, 1), pltpu.roll(t, 16, 1))
    return t * cos + sw * sins


def _hgrn_gates(z, log_lb, log1m_lb, one_m_lb):
    log_sig = jnp.minimum(z, 0.0) - jnp.log1p(jnp.exp(-jnp.abs(z)))
    c = log1m_lb + log_sig
    m = jnp.maximum(log_lb, c)
    log_f = m + jnp.log1p(jnp.exp(-jnp.abs(log_lb - c)))
    k = one_m_lb * (1.0 / (1.0 + jnp.exp(z)))
    return log_f, k


def _cumsum16(tri, log_f):
    h1 = log_f.astype(bf16)
    r1 = log_f - h1.astype(f32)
    h2 = r1.astype(bf16)
    h3 = (r1 - h2.astype(f32)).astype(bf16)
    return (jnp.dot(tri, h1, preferred_element_type=f32) + jnp.dot(tri, h2, preferred_element_type=f32)
            + jnp.dot(tri, h3, preferred_element_type=f32))


def _proj_kernel(x_ref, nw_ref, mod_ref, w_ref, cs_ref, gp_ref, tri_ref,
                 hq_ref, cum_ref, kk_ref, hv_ref, sg_ref, sq_ref, sk_ref, sv_ref, nq_ref, nk_ref, nv_ref):
    h = _norm_mod(x_ref[0], nw_ref[...], mod_ref[0, 0, 0:1, :], mod_ref[0, 0, 1:2, :])
    p = jnp.dot(h.astype(bf16), w_ref[...], preferred_element_type=f32)
    w = HG_WIDTH
    hq_ref[0] = _silu(p[:, 0:w])
    for d in range(2):
        log_f, k = _hgrn_gates(p[:, (1 + d) * w:(2 + d) * w], gp_ref[3 * d:3 * d + 1, :],
                               gp_ref[3 * d + 1:3 * d + 2, :], gp_ref[3 * d + 2:3 * d + 3, :])
        cum_ref[d, 0] = _cumsum16(tri_ref[d], log_f)
        kk_ref[d, 0] = k
    hv_ref[0] = p[:, 3 * w:4 * w]
    sg_ref[0] = _silu(p[:, 4 * w:5 * w])
    cos, sins = cs_ref[0], cs_ref[1]
    o = 5 * w
    scale = HEAD_DIM ** -0.5
    for j in range(SWA_Q // LANE):
        sq_ref[0, :, j * LANE:(j + 1) * LANE] = (_rope(p[:, o + j * LANE:o + (j + 1) * LANE], cos, sins) * scale).astype(bf16)
    o += SWA_Q
    sk_ref[0] = _rope(p[:, o:o + SWA_KV], cos, sins).astype(bf16)
    o += SWA_KV
    sv_ref[0] = p[:, o:o + SWA_KV].astype(bf16)
    o += SWA_KV
    nq_ref[0] = (p[:, o:o + NAT_W] * scale).astype(bf16)
    o += NAT_W
    nk_ref[0] = p[:, o:o + NAT_W].astype(bf16)
    o += NAT_W
    nv_ref[0] = p[:, o:o + NAT_W].astype(bf16)


def _proj(xall, nw, mod, w_in, cs, gp, tri):
    nb, ta, d = xall.shape
    nt = ta // TM
    tok = lambda width, dt: jax.ShapeDtypeStruct((nb, ta, width), dt)
    tok2 = jax.ShapeDtypeStruct((2, nb, ta, HG_WIDTH), f32)
    bs = lambda width: pl.BlockSpec((1, TM, width), lambda b, i: (b, i, 0))
    bs2 = pl.BlockSpec((2, 1, TM, HG_WIDTH), lambda b, i: (0, b, i, 0))
    return pl.pallas_call(
        _proj_kernel,
        grid=(nb, nt),
        in_specs=[pl.BlockSpec((1, TM, d), lambda b, i: (b, i, 0)),
                  pl.BlockSpec((1, d), lambda b, i: (0, 0)),
                  pl.BlockSpec((1, 1, 6, d), lambda b, i: (b, jnp.minimum(i, 1), 0, 0)),
                  pl.BlockSpec((d, PROJ_WIDTH), lambda b, i: (0, 0)),
                  pl.BlockSpec((2, TM, LANE), lambda b, i: (0, i, 0)),
                  pl.BlockSpec((8, HG_WIDTH), lambda b, i: (0, 0)),
                  pl.BlockSpec((2, TM, TM), lambda b, i: (0, 0, 0))],
        out_specs=[bs(HG_WIDTH), bs2, bs2, bs(HG_WIDTH), bs(HG_WIDTH), bs(SWA_Q), bs(SWA_KV), bs(SWA_KV),
                   bs(NAT_W), bs(NAT_W), bs(NAT_W)],
        out_shape=[tok(HG_WIDTH, f32), tok2, tok2, tok(HG_WIDTH, f32), tok(HG_WIDTH, f32), tok(SWA_Q, bf16),
                   tok(SWA_KV, bf16), tok(SWA_KV, bf16), tok(NAT_W, bf16), tok(NAT_W, bf16), tok(NAT_W, bf16)],
        compiler_params=pltpu.CompilerParams(dimension_semantics=("arbitrary", "arbitrary"),
                                             vmem_limit_bytes=VMEM_LIMIT_BYTES),
        name="norm_in_proj",
    )(xall, nw, mod, w_in, cs, gp, tri)


def _hgrn_kernel(q_ref, cum_ref, k_ref, v_ref, bd_ref, o_ref, st_ref):
    d = pl.program_id(1)
    fwd = d == 0

    @pl.when(pl.program_id(2) == 0)
    def _():
        st_ref[...] = jnp.zeros_like(st_ref)

    bd = bd_ref[...]
    bd_mask = bd > 0
    sgn = jnp.where(fwd, 1, -1)
    t_sgn = lax.broadcasted_iota(i32, (HG_SUB, 1), 0) * sgn
    nsteps = TM // HG_SUB

    def step(i, carry):
        sc = jnp.where(fwd, i, nsteps - 1 - i)
        r0 = pl.multiple_of(sc * HG_SUB, HG_SUB)
        q = q_ref[0, pl.ds(r0, HG_SUB), :]
        b = cum_ref[0, 0, pl.ds(r0, HG_SUB), :]
        k = k_ref[0, 0, pl.ds(r0, HG_SUB), :]
        v = v_ref[0, pl.ds(r0, HG_SUB), :]
        b_end = jnp.where(fwd, b[HG_SUB - 1:HG_SUB, :], b[0:1, :])
        st = st_ref[...]
        o = lax.dot_general((q * jnp.exp(b)).astype(bf16), st.astype(bf16), _NT, preferred_element_type=f32)
        rows = []
        for s in range(HG_SUB):
            valid = t_sgn >= s * sgn
            decay = jnp.exp(jnp.minimum(b - b[s:s + 1, :], 0.0))
            rows.append(jnp.where(valid, decay * q * k[s:s + 1, :], 0.0))
        hi, lo = _split2(jnp.concatenate(rows, axis=0))
        pe = jnp.dot(hi, bd, preferred_element_type=f32) + jnp.dot(lo, bd, preferred_element_type=f32)
        for s in range(HG_SUB):
            o = o + pe[s * HG_SUB:(s + 1) * HG_SUB, :] * v[s:s + 1, :]
        o_ref[0, 0, pl.ds(r0, HG_SUB), :] = o
        kd = (k * jnp.exp(b_end - b)).astype(bf16)
        kv_t = lax.dot_general(v.astype(bf16), kd, _TN, preferred_element_type=f32)
        st_ref[...] = st * jnp.exp(b_end) + jnp.where(bd_mask, kv_t, 0.0)
        return carry

    lax.fori_loop(0, nsteps, step, 0, unroll=4)


def _hgrn_scan(hq, cum, kk, hv, bd):
    nb, ta, w = hq.shape
    nt = ta // TM

    def blk(d, j):
        return jnp.where(d == 0, j, jnp.where(j == 0, 0, nt - j))

    return pl.pallas_call(
        _hgrn_kernel,
        grid=(nb, 2, nt),
        in_specs=[pl.BlockSpec((1, TM, w), lambda b, d, j: (b, blk(d, j), 0)),
                  pl.BlockSpec((1, 1, TM, w), lambda b, d, j: (d, b, blk(d, j), 0)),
                  pl.BlockSpec((1, 1, TM, w), lambda b, d, j: (d, b, blk(d, j), 0)),
                  pl.BlockSpec((1, TM, w), lambda b, d, j: (b, blk(d, j), 0)),
                  pl.BlockSpec((w, w), lambda b, d, j: (0, 0))],
        out_specs=pl.BlockSpec((1, 1, TM, w), lambda b, d, j: (d, b, blk(d, j), 0)),
        out_shape=jax.ShapeDtypeStruct((2, nb, ta, w), f32),
        scratch_shapes=[pltpu.VMEM((w, w), f32)],
        compiler_params=pltpu.CompilerParams(dimension_semantics=("arbitrary", "arbitrary", "arbitrary")),
        name="hgrn_scan",
    )(hq, cum, kk, hv, bd)


def _pair_rows(q2):
    lane = lax.broadcasted_iota(i32, q2.shape, 1)
    zero = jnp.zeros_like(q2)
    return jnp.concatenate([jnp.where(lane < HEAD_DIM, q2, zero), jnp.where(lane >= HEAD_DIM, q2, zero)], axis=0)


def _pair_merge(o):
    m = o.shape[0] // 2
    lane = lax.broadcasted_iota(i32, (m, LANE), 1)
    return jnp.where(lane < HEAD_DIM, o[:m], o[m:])


def _joint_attention(s_loc, s_ctx, v_loc, v_ctx, sink=None):
    m = jnp.maximum(jnp.max(s_loc, axis=-1, keepdims=True), jnp.max(s_ctx, axis=-1, keepdims=True))
    if sink is not None:
        m = jnp.maximum(m, sink)
    p_loc = jnp.exp(s_loc - m)
    p_ctx = jnp.exp(s_ctx - m)
    den = jnp.sum(p_loc, axis=-1, keepdims=True) + jnp.sum(p_ctx, axis=-1, keepdims=True)
    if sink is not None:
        den = den + jnp.exp(sink - m)
    o = (jnp.dot(p_loc.astype(bf16), v_loc, preferred_element_type=f32)
         + jnp.dot(p_ctx.astype(bf16), v_ctx, preferred_element_type=f32))
    return o / den


def _swa_kernel(q_ref, kp_ref, kc_ref, kn_ref, vp_ref, vc_ref, vn_ref, kx_ref, vx_ref, sink_ref, o_ref, *, n_ctx_blocks, seq):
    g = pl.program_id(1)
    i = g - n_ctx_blocks
    k_loc = jnp.concatenate([kp_ref[0], kc_ref[0], kn_ref[0]], axis=0)
    v_loc = jnp.concatenate([vp_ref[0], vc_ref[0], vn_ref[0]], axis=0)
    row = lax.broadcasted_iota(i32, (SWA_BLOCK, 3 * SWA_BLOCK), 0)
    col = lax.broadcasted_iota(i32, (SWA_BLOCK, 3 * SWA_BLOCK), 1)
    rel = col - SWA_BLOCK - row
    kpos = (i - 1) * SWA_BLOCK + col
    ok = (jnp.abs(rel) <= SWA_WINDOW) & (kpos >= 0) & (kpos < seq) & (i >= 0)
    pen = jnp.where(ok, 0.0, NEG)
    pen = jnp.concatenate([pen, pen], axis=0)
    for j in range(SWA_Q // LANE):
        q = _pair_rows(q_ref[0, :, j * LANE:(j + 1) * LANE])
        s_loc = lax.dot_general(q, k_loc, _NT, preferred_element_type=f32) + pen
        s_ctx = lax.dot_general(q, kx_ref[0], _NT, preferred_element_type=f32)
        sink = jnp.concatenate([jnp.broadcast_to(sink_ref[2 * j:2 * j + 1, 0:1], (SWA_BLOCK, 1)),
                                jnp.broadcast_to(sink_ref[2 * j + 1:2 * j + 2, 0:1], (SWA_BLOCK, 1))], axis=0)
        o = _joint_attention(s_loc, s_ctx, v_loc, vx_ref[0], sink)
        o_ref[0, :, j * LANE:(j + 1) * LANE] = _pair_merge(o).astype(bf16)


def _swa(sq, sk, sv, sink_rows, n_ctx):
    nb, ta, _ = sq.shape
    ncb = n_ctx // SWA_BLOCK
    nblk = ta // SWA_BLOCK
    nlat = nblk - ncb
    prev = lambda b, g: (b, jnp.clip(g - ncb - 1, 0, nlat - 1) + ncb, 0)
    cur = lambda b, g: (b, g, 0)
    nxt = lambda b, g: (b, jnp.clip(g - ncb + 1, 0, nlat - 1) + ncb, 0)
    kvs = lambda f: pl.BlockSpec((1, SWA_BLOCK, SWA_KV), f)
    ctx = pl.BlockSpec((1, n_ctx, SWA_KV), lambda b, g: (b, 0, 0))
    return pl.pallas_call(
        functools.partial(_swa_kernel, n_ctx_blocks=ncb, seq=ta - n_ctx),
        grid=(nb, nblk),
        in_specs=[pl.BlockSpec((1, SWA_BLOCK, SWA_Q), cur), kvs(prev), kvs(cur), kvs(nxt), kvs(prev), kvs(cur), kvs(nxt),
                  ctx, ctx, pl.BlockSpec((8, LANE), lambda b, g: (0, 0))],
        out_specs=pl.BlockSpec((1, SWA_BLOCK, SWA_Q), cur),
        out_shape=jax.ShapeDtypeStruct((nb, ta, SWA_Q), bf16),
        compiler_params=pltpu.CompilerParams(dimension_semantics=("arbitrary", "arbitrary")),
        name="swa_attention",
    )(sq, sk, sk, sk, sv, sv, sv, sk, sv, sink_rows)


def _nat_kernel(q_ref, k_ref, v_ref, bias_ref, o_ref, *, n_ctx, n_rows):
    g = pl.program_id(1)
    is_ctx = g == 0
    ctx_pen = jnp.where(is_ctx, NEG, 0.0)
    rows_per_tile = TM // GRID_W
    nwin = NAT_ROWS * GRID_W
    for rr in range(rows_per_tile):
        r = jnp.maximum((g - 1) * rows_per_tile + rr, 0)
        rs = jnp.clip(r - NAT_ROWS // 2, 0, n_rows - NAT_ROWS)
        dr0 = rs - r + NAT_ROWS - 1
        start = pl.multiple_of(n_ctx + rs * GRID_W, GRID_W)
        for p in range(NAT_W // LANE):
            ls = slice(p * LANE, (p + 1) * LANE)
            q = _pair_rows(q_ref[0, rr * GRID_W:(rr + 1) * GRID_W, ls])
            s_loc = (lax.dot_general(q, k_ref[0, pl.ds(start, nwin), ls], _NT, preferred_element_type=f32)
                     + bias_ref[p, dr0] + ctx_pen)
            s_ctx = lax.dot_general(q, k_ref[0, 0:n_ctx, ls], _NT, preferred_element_type=f32)
            o = _joint_attention(s_loc, s_ctx, v_ref[0, pl.ds(start, nwin), ls], v_ref[0, 0:n_ctx, ls])
            o_ref[0, rr * GRID_W:(rr + 1) * GRID_W, ls] = _pair_merge(o).astype(bf16)


def _nat(nq, nk, nv, bias, n_ctx):
    nb, ta, w = nq.shape
    nt = ta // TM
    whole = pl.BlockSpec((1, ta, w), lambda b, g: (b, 0, 0), pipeline_mode=pl.Buffered(1))
    return pl.pallas_call(
        functools.partial(_nat_kernel, n_ctx=n_ctx, n_rows=(ta - n_ctx) // GRID_W),
        grid=(nb, nt),
        in_specs=[pl.BlockSpec((1, TM, w), lambda b, g: (b, g, 0)), whole, whole,
                  pl.BlockSpec(bias.shape, lambda b, g: (0, 0, 0, 0), pipeline_mode=pl.Buffered(1))],
        out_specs=pl.BlockSpec((1, TM, w), lambda b, g: (b, g, 0)),
        out_shape=jax.ShapeDtypeStruct((nb, ta, w), bf16),
        compiler_params=pltpu.CompilerParams(dimension_semantics=("arbitrary", "arbitrary"),
                                             vmem_limit_bytes=VMEM_LIMIT_BYTES),
        name="nat_attention",
    )(nq, nk, nv, bias)


def _nat_bias_tables(rpb):
    c = np.arange(GRID_W)[:, None]
    kc = np.arange(GRID_W)[None, :]
    ws = np.clip(c - NAT_COLS // 2, 0, GRID_W - NAT_COLS)
    valid = (kc >= ws) & (kc < ws + NAT_COLS)
    col_idx = np.clip(kc - c + NAT_COLS - 1, 0, 2 * NAT_COLS - 2)
    dr = np.arange(NAT_ROWS)[:, None] + np.arange(NAT_ROWS)[None, :]
    t = rpb.astype(f32)[:, dr][:, :, :, col_idx]
    t = jnp.where(jnp.asarray(valid)[None, None, None], t, NEG)
    t = t.transpose(0, 1, 3, 2, 4).reshape(NAT_HEADS // 2, 2, NAT_ROWS, GRID_W, NAT_ROWS * GRID_W)
    return t.transpose(0, 2, 1, 3, 4).reshape(NAT_HEADS // 2, NAT_ROWS, 2 * GRID_W, NAT_ROWS * GRID_W)


def _outproj_kernel(o_ref, sg_ref, hnw_ref, bd_ref, so_ref, no_ref, wa_ref, wb_ref, wn_ref, mod_ref, x_ref, y_ref):
    oo = o_ref[0, 0] + o_ref[1, 0]
    hi, lo = _split2(oo * oo)
    bd = bd_ref[...]
    msq = (jnp.dot(hi, bd, preferred_element_type=f32) + jnp.dot(lo, bd, preferred_element_type=f32)) * (1.0 / HEAD_DIM)
    a = oo * lax.rsqrt(msq + EPS) * hnw_ref[...] * sg_ref[0]
    y = (jnp.dot(a.astype(bf16), wa_ref[...], preferred_element_type=f32)
         + jnp.dot(so_ref[0], wb_ref[...], preferred_element_type=f32)
         + jnp.dot(no_ref[0], wn_ref[...], preferred_element_type=f32))
    y_ref[0] = x_ref[0] + mod_ref[0, 0, 2:3, :] * y


def _outproj(o, sg, hnw, bd, so, no, wa, wb, wn, mod, xall):
    nb, ta, d = xall.shape
    nt = ta // TM
    bs = lambda width: pl.BlockSpec((1, TM, width), lambda b, i: (b, i, 0))
    full = lambda a: pl.BlockSpec(a.shape, lambda b, i: (0,) * a.ndim)
    return pl.pallas_call(
        _outproj_kernel,
        grid=(nb, nt),
        in_specs=[pl.BlockSpec((2, 1, TM, HG_WIDTH), lambda b, i: (0, b, i, 0)), bs(HG_WIDTH), full(hnw), full(bd),
                  bs(SWA_Q), bs(NAT_W), full(wa), full(wb), full(wn),
                  pl.BlockSpec((1, 1, 6, d), lambda b, i: (b, jnp.minimum(i, 1), 0, 0)), bs(d)],
        out_specs=bs(d),
        out_shape=jax.ShapeDtypeStruct(xall.shape, f32),
        compiler_params=pltpu.CompilerParams(dimension_semantics=("arbitrary", "arbitrary"),
                                             vmem_limit_bytes=VMEM_LIMIT_BYTES),
        name="out_proj",
    )(o, sg, hnw, bd, so, no, wa, wb, wn, mod, xall)


def _top16_rows(vals, payload=None):
    n = vals.shape[0]
    iota = lax.broadcasted_iota(i32, vals.shape, 0)
    best, picked = [], []
    for _ in range(PEER_TOPK):
        m = jnp.max(vals, axis=0, keepdims=True)
        idx = jnp.min(jnp.where(vals == m, iota, n), axis=0, keepdims=True)
        hit = iota == idx
        best.append(m)
        picked.append(idx if payload is None else jnp.max(jnp.where(hit, payload, -1), axis=0, keepdims=True))
        vals = jnp.where(hit, -jnp.inf, vals)
    return jnp.concatenate(best, axis=0), jnp.concatenate(picked, axis=0)


def _product_candidates(first, second, combine):
    half = PEER_TOPK // 2
    rows = [combine(first[0:1, :], second)]
    rows += [combine(first[a:a + 1, :], second[0:half, :]) for a in range(1, half)]
    rows.append(combine(first[half:, :], second[0:1, :]))
    return jnp.concatenate(rows, axis=0)


def _route_kernel(x_ref, nw_ref, mod_ref, wq_ref, sub_ref, h_ref, idx_ref, g_ref):
    h = _norm_mod(x_ref[0], nw_ref[...], mod_ref[0, 0, 3:4, :], mod_ref[0, 0, 4:5, :])
    h_ref[0] = h
    q = jnp.dot(h.astype(bf16), wq_ref[...], preferred_element_type=f32).astype(bf16)
    half = PEER_DK // 2
    ids, gates = [], []
    for hh in range(PEER_HEADS):
        top_s, top_i = [], []
        for p in range(2):
            c0 = (hh * 2 + p) * half
            s = lax.dot_general(sub_ref[hh, p], q[:, c0:c0 + half], _NT, preferred_element_type=f32)
            ts, ti = _top16_rows(s)
            top_s.append(ts)
            top_i.append(ti)
        cand_s = _product_candidates(top_s[0], top_s[1], lambda a, b: a + b)
        cand_i = _product_candidates(top_i[0], top_i[1], lambda a, b: a * PEER_NKEYS + b)
        best, eid = _top16_rows(cand_s, cand_i)
        ids.append(eid)
        e = jnp.exp(best - best[0:1, :])
        gates.append(e / jnp.sum(e, axis=0, keepdims=True))
    idx_ref[0] = jnp.concatenate(ids, axis=0).T
    g_ref[0] = jnp.concatenate(gates, axis=0).T


def _route(xall, nw, mod, wq, sub):
    nb, ta, d = xall.shape
    nt = ta // TM
    full = lambda a: pl.BlockSpec(a.shape, lambda b, i: (0,) * a.ndim)
    return pl.pallas_call(
        _route_kernel,
        grid=(nb, nt),
        in_specs=[pl.BlockSpec((1, TM, d), lambda b, i: (b, i, 0)), full(nw),
                  pl.BlockSpec((1, 1, 6, d), lambda b, i: (b, jnp.minimum(i, 1), 0, 0)), full(wq), full(sub)],
        out_specs=[pl.BlockSpec((1, TM, d), lambda b, i: (b, i, 0)),
                   pl.BlockSpec((1, TM, PEER_SLOTS), lambda b, i: (b, i, 0)),
                   pl.BlockSpec((1, TM, PEER_SLOTS), lambda b, i: (b, i, 0))],
        out_shape=[jax.ShapeDtypeStruct(xall.shape, f32), jax.ShapeDtypeStruct((nb, ta, PEER_SLOTS), i32),
                   jax.ShapeDtypeStruct((nb, ta, PEER_SLOTS), f32)],
        compiler_params=pltpu.CompilerParams(dimension_semantics=("arbitrary", "arbitrary"),
                                             vmem_limit_bytes=VMEM_LIMIT_BYTES),
        name="peer_route",
    )(xall, nw, mod, wq, sub)


def _expert_kernel(idx_ref, idx_next_ref, h_ref, g_ref, x_ref, mod_ref, tab_hbm, o_ref, buf, sem):
    d = h_ref.shape[-1]
    nct = d // LANE
    groups = PEER_SLOTS // 8

    per_piece = PEER_SLOTS // nct

    def issue(ids_ref, t, slot, piece):
        for k in range(piece * per_piece, (piece + 1) * per_piece):
            pltpu.make_async_copy(tab_hbm.at[ids_ref[t, k]], buf.at[slot, k // 8, :, pl.ds(k % 8, 1), :],
                                  sem.at[slot]).start(priority=k % 2)

    def wait(slot):
        pltpu.make_async_copy(buf.at[slot], buf.at[slot], sem.at[slot]).wait()

    def cols(slot, c, high):
        words = jnp.concatenate([jnp.concatenate([buf[slot, kg, c], buf[slot, kg, c + 1]], axis=1)
                                 for kg in range(groups)], axis=0)
        bits = (words & jnp.uint32(0xFFFF0000)) if high else (words << 16)
        return lax.bitcast_convert_type(bits, f32).astype(bf16)

    not_last = pl.program_id(0) < pl.num_programs(0) - 1

    def step(t, w8, do_issue, do_u, do_v):
        nslot = lax.rem(t + 1, PEER_NBUF)
        slot = lax.rem(t, PEER_NBUF)
        islot = lax.rem(t + PEER_AHEAD, PEER_NBUF)
        if do_issue == "next":
            @pl.when(not_last)
            def _():
                for j in range(nct):
                    issue(idx_next_ref, t + PEER_AHEAD - PEER_TB, islot, j)
        if do_u:
            wait(nslot)
            h8 = jnp.broadcast_to(h_ref[pl.ds(t + 1, 1), :], (8, d)).astype(bf16)
            gate = g_ref[pl.ds(t + 1, 1), :]
            a = jnp.zeros((8, PEER_SLOTS), f32)
        ys = []
        for j in range(nct):
            if 2 * j < nct:
                c = 2 * j
                if do_u:
                    a = a + lax.dot_general(h8[:, c * LANE:(c + 2) * LANE], cols(nslot, c, True), _NT, preferred_element_type=f32)
                if do_v:
                    ys.append(jnp.dot(w8, cols(slot, c, False), preferred_element_type=f32)[0:1, :])
            if do_issue is True:
                issue(idx_ref, t + PEER_AHEAD, islot, j)
        if do_v:
            o_ref[pl.ds(t, 1), :] = jnp.concatenate(ys, axis=1)
        if do_u:
            a1 = a[0:1, :]
            w = gate * (0.5 * a1 * (1.0 + lax.erf(a1 * (2.0 ** -0.5))))
            w8 = jnp.broadcast_to(w, (8, PEER_SLOTS)).astype(bf16)
        return w8

    @pl.when(pl.program_id(0) == 0)
    def _():
        def prologue(t, carry):
            for j in range(nct):
                issue(idx_ref, t, t, j)
            return carry

        lax.fori_loop(0, PEER_AHEAD, prologue, 0)

    w8 = step(-1, jnp.zeros((8, PEER_SLOTS), bf16), False, True, False)
    w8 = lax.fori_loop(0, PEER_TB - PEER_AHEAD, lambda t, w: step(t, w, True, True, True), w8)
    w8 = lax.fori_loop(PEER_TB - PEER_AHEAD, PEER_TB - 1, lambda t, w: step(t, w, "next", True, True), w8)
    step(PEER_TB - 1, w8, "next", False, True)
    o_ref[...] = x_ref[...] + mod_ref[0, 0, 5:6, :] * o_ref[...]


def _mod_index(tokens_per_step, ta, n_ctx, first_step=0):
    spb = ta // tokens_per_step
    cs = n_ctx // tokens_per_step
    return lambda i: ((i + first_step) // spb, jnp.where((i + first_step) % spb < cs, 0, 1), 0, 0)


def _experts(idx, h, g, xflat, mod, tab, ta, n_ctx, nblocks):
    n, d = xflat.shape
    tokb = lambda width: pl.BlockSpec((PEER_TB, width), lambda i: (i, 0))
    return pl.pallas_call(
        _expert_kernel,
        grid=(nblocks,),
        in_specs=[pl.BlockSpec((PEER_TB, PEER_SLOTS), lambda i: (i, 0), memory_space=pltpu.SMEM),
                  pl.BlockSpec((PEER_TB, PEER_SLOTS), lambda i: (jnp.minimum(i + 1, nblocks - 1), 0), memory_space=pltpu.SMEM),
                  tokb(d), tokb(PEER_SLOTS), tokb(d),
                  pl.BlockSpec((1, 1, 6, d), _mod_index(PEER_TB, ta, n_ctx)),
                  pl.BlockSpec(memory_space=pl.ANY)],
        out_specs=tokb(d),
        out_shape=jax.ShapeDtypeStruct((n, d), f32),
        input_output_aliases={4: 0},
        scratch_shapes=[pltpu.VMEM((PEER_NBUF, PEER_SLOTS // 8, d // LANE, 8, LANE), jnp.uint32),
                        pltpu.SemaphoreType.DMA((PEER_NBUF,))],
        compiler_params=pltpu.CompilerParams(dimension_semantics=("arbitrary",), vmem_limit_bytes=VMEM_LIMIT_BYTES),
        name="peer_experts",
    )(idx, idx, h, g, xflat, mod, tab)


def _sc_gather(table, idx):
    info = plsc.get_sparse_core_info()
    nc = info.num_cores
    nw = nc * info.num_subcores
    r = idx.shape[0]
    per_w = r // nw
    nch = per_w // SC_CHUNK
    assert per_w * nw == r and nch * SC_CHUNK == per_w and nch % 2 == 0
    rows_shape = (SC_CHUNK,) + table.shape[1:]
    mesh = plsc.VectorSubcoreMesh(core_axis_name="c", subcore_axis_name="s")

    @functools.partial(
        pl.kernel, mesh=mesh, out_type=jax.ShapeDtypeStruct((r,) + table.shape[1:], table.dtype),
        scratch_types=[pltpu.VMEM((2, SC_CHUNK), i32), pltpu.VMEM((2,) + rows_shape, table.dtype),
                       pltpu.SemaphoreType.DMA((2,)), pltpu.SemaphoreType.DMA((2,)), pltpu.SemaphoreType.DMA((2,))])
    def gather_kernel(table_hbm, idx_hbm, out_hbm, idx_v, rows_v, isem, gsem, wsem):
        base = (lax.axis_index("s") * nc + lax.axis_index("c")) * per_w

        def rows_of(j):
            return pl.ds(pl.multiple_of(base + j * SC_CHUNK, 8), SC_CHUNK)

        def fetch_idx(j, b):
            return pltpu.make_async_copy(idx_hbm.at[rows_of(j)], idx_v.at[b], isem.at[b])

        def gather(b):
            return pltpu.make_async_copy(table_hbm.at[idx_v.at[b]], rows_v.at[b], gsem.at[b])

        def write(j, b):
            return pltpu.make_async_copy(rows_v.at[b], out_hbm.at[rows_of(j)], wsem.at[b])

        fetch_idx(0, 0).start()
        fetch_idx(1, 1).start()
        fetch_idx(0, 0).wait()
        gather(0).start()

        @pl.loop(0, nch, step=2)
        def _(j0):
            for b in (0, 1):
                j = j0 + b

                @pl.when(j >= 1)
                def _():
                    write(j - 1, 1 - b).wait()

                @pl.when(j + 1 < nch)
                def _():
                    fetch_idx(j + 1, 1 - b).wait()
                    gather(1 - b).start()

                gather(b).wait()

                @pl.when(j + 2 < nch)
                def _():
                    fetch_idx(j + 2, b).start()

                write(j, b).start()

        write(nch - 1, 1).wait()

    return gather_kernel(table, idx)


def _expert_dense_kernel(st_ref, h_ref, g_ref, x_ref, mod_ref, o_ref):
    d = h_ref.shape[-1]
    nct = d // LANE
    groups = PEER_SLOTS // 8
    rows_per_token = PEER_SLOTS * nct

    def cols(t, c, high):
        base = t * rows_per_token
        tile = lambda kg, cc: st_ref[pl.ds(base + kg * 8 * nct + cc, 8, stride=nct), :]
        words = jnp.concatenate([jnp.concatenate([tile(kg, c), tile(kg, c + 1)], axis=1) for kg in range(groups)], axis=0)
        bits = (words & jnp.uint32(0xFFFF0000)) if high else (words << 16)
        return lax.bitcast_convert_type(bits, f32).astype(bf16)

    def u_stage(t):
        h8 = jnp.broadcast_to(h_ref[pl.ds(t, 1), :], (8, d)).astype(bf16)
        a = jnp.zeros((8, PEER_SLOTS), f32)
        for c in range(0, nct, 2):
            a = a + lax.dot_general(h8[:, c * LANE:(c + 2) * LANE], cols(t, c, True), _NT, preferred_element_type=f32)
        a1 = a[0:1, :]
        return g_ref[pl.ds(t, 1), :] * (0.5 * a1 * (1.0 + lax.erf(a1 * (2.0 ** -0.5))))

    lane = lax.broadcasted_iota(i32, (8, LANE), 1)
    sub = lax.broadcasted_iota(i32, (8, LANE), 0)

    def v_stage(t, w_row):
        base = t * rows_per_token
        wb = jnp.broadcast_to(w_row, (8, LANE))
        acc = [None] * nct
        for kg in range(groups):
            wcol = jnp.sum(jnp.where(lane == kg * 8 + sub, wb, 0.0), axis=-1, keepdims=True)
            for c in range(nct):
                words = st_ref[pl.ds(base + kg * 8 * nct + c, 8, stride=nct), :]
                term = lax.bitcast_convert_type(words << 16, f32) * wcol
                acc[c] = term if acc[c] is None else acc[c] + term
        o_ref[pl.ds(t, 1), :] = jnp.concatenate([jnp.sum(a, axis=0, keepdims=True) for a in acc], axis=1)

    def body(t, w_row):
        w_next = u_stage(t + 1)
        v_stage(t, w_row)
        return w_next

    w_row = lax.fori_loop(0, PEER_DENSE_TB - 1, body, u_stage(0), unroll=5)
    v_stage(PEER_DENSE_TB - 1, w_row)
    o_ref[...] = x_ref[...] + mod_ref[0, 0, 5:6, :] * o_ref[...]


def _experts_dense(staged, h, g, xflat, mod, ta, n_ctx, first_block):
    n, d = xflat.shape
    rows_per_token = PEER_SLOTS * (d // LANE)
    steps = staged.shape[0] // (PEER_DENSE_TB * rows_per_token)
    off = first_block * PEER_TB // PEER_DENSE_TB
    tokb = lambda width: pl.BlockSpec((PEER_DENSE_TB, width), lambda i: (i + off, 0))
    return pl.pallas_call(
        _expert_dense_kernel,
        grid=(steps,),
        in_specs=[pl.BlockSpec((PEER_DENSE_TB * rows_per_token, LANE), lambda i: (i, 0)),
                  tokb(d), tokb(PEER_SLOTS), tokb(d),
                  pl.BlockSpec((1, 1, 6, d), _mod_index(PEER_DENSE_TB, ta, n_ctx, off))],
        out_specs=tokb(d),
        out_shape=jax.ShapeDtypeStruct((n, d), f32),
        input_output_aliases={3: 0},
        compiler_params=pltpu.CompilerParams(dimension_semantics=("arbitrary",), vmem_limit_bytes=VMEM_LIMIT_BYTES),
        name="peer_experts_dense",
    )(staged, h, g, xflat, mod)


def _expert_table(u, v):
    half = lambda a: lax.bitcast_convert_type(a.astype(bf16), jnp.uint16).astype(jnp.uint32)
    return (half(u) << 16) | half(v)


def _final_norm_kernel(x_ref, w_ref, o_ref):
    x = x_ref[0]
    o_ref[0] = x * lax.rsqrt(jnp.mean(x * x, axis=-1, keepdims=True) + EPS) * w_ref[...]


def _final_norm(xall, w, n_ctx):
    nb, ta, d = xall.shape
    off = n_ctx // TM
    return pl.pallas_call(
        _final_norm_kernel,
        grid=(nb, (ta - n_ctx) // TM),
        in_specs=[pl.BlockSpec((1, TM, d), lambda b, i: (b, i + off, 0)), pl.BlockSpec((1, d), lambda b, i: (0, 0))],
        out_specs=pl.BlockSpec((1, TM, d), lambda b, i: (b, i, 0)),
        out_shape=jax.ShapeDtypeStruct((nb, ta - n_ctx, d), f32),
        compiler_params=pltpu.CompilerParams(dimension_semantics=("arbitrary", "arbitrary")),
        name="final_norm",
    )(xall, w.reshape(1, d))


def _rope_tables(n_ctx, seq):
    nf = HEAD_DIM // 4
    pos = np.arange(seq)
    inv = ROPE_BASE ** (-np.arange(nf, dtype=np.float32) / nf)
    ar = (pos // GRID_W).astype(np.float32)[:, None] * inv
    ac = (pos % GRID_W).astype(np.float32)[:, None] * inv
    cos = np.concatenate([np.cos(ar), np.cos(ar), np.cos(ac), np.cos(ac)], axis=-1)
    sin = np.concatenate([-np.sin(ar), np.sin(ar), -np.sin(ac), np.sin(ac)], axis=-1)
    cos = np.concatenate([np.ones((n_ctx, HEAD_DIM), np.float32), cos.astype(np.float32)], axis=0)
    sin = np.concatenate([np.zeros((n_ctx, HEAD_DIM), np.float32), sin.astype(np.float32)], axis=0)
    return jnp.asarray(np.stack([np.tile(cos, (1, 2)), np.tile(sin, (1, 2))]))


def _tri_tables():
    r = np.arange(TM)[:, None]
    c = np.arange(TM)[None, :]
    same = (r // HG_SUB) == (c // HG_SUB)
    return jnp.asarray(np.stack([same & (c <= r), same & (c >= r)]).astype(np.float32), dtype=bf16)


def _block_diag_ones():
    r = np.arange(HG_WIDTH)
    return jnp.asarray((r[:, None] // HEAD_DIM == r[None, :] // HEAD_DIM).astype(np.float32), dtype=bf16)


def _swa_perm():
    return np.concatenate([np.arange(h * HEAD_DIM, (h + 1) * HEAD_DIM) for h in SWA_HEAD_ORDER])


def kernel(x, c, ctx, c_ctx, w_mod, b_mod, norm_mix_w, norm_ffn_w, w_in, hgrn_lb_logits, hgrn_norm_w, swa_sink, nat_rpb,
           w_out, peer_wq, peer_subkeys, peer_u, peer_v, final_norm_w):
    nb, seq, d = x.shape
    n_ctx = ctx.shape[1]
    depth = w_in.shape[0]
    ta = n_ctx + seq
    assert n_ctx == TM and seq % TM == 0 and seq % (GRID_W * NAT_ROWS) == 0 and d % LANE == 0
    assert w_in.shape[-1] == PROJ_WIDTH and peer_wq.shape[-1] == PEER_HEADS * PEER_DK

    mods = _mods(c, c_ctx, w_mod, b_mod)
    cs = _rope_tables(n_ctx, seq)
    tri = _tri_tables()
    bd = _block_diag_ones()
    perm = _swa_perm()
    q0 = 5 * HG_WIDTH
    col_perm = np.concatenate([np.arange(q0), q0 + perm, np.arange(q0 + SWA_Q, PROJ_WIDTH)])
    w_in_b = w_in[:, :, col_perm].astype(bf16)
    w_out_b = w_out.astype(bf16)
    wa = w_out_b[:, :HG_WIDTH]
    wb = w_out_b[:, HG_WIDTH:HG_WIDTH + SWA_Q][:, perm]
    wn = w_out_b[:, HG_WIDTH + SWA_Q:]
    wq_b = peer_wq.astype(bf16)
    sub_b = peer_subkeys.astype(bf16)
    lb = jnp.cumsum(jax.nn.softmax(hgrn_lb_logits.astype(f32), axis=1), axis=1)
    lb = lb - lb[:, :1]
    gp = jnp.stack([jnp.log(lb[0]), jnp.log1p(-lb[0]), 1.0 - lb[0], jnp.log(lb[1]), jnp.log1p(-lb[1]), 1.0 - lb[1],
                    jnp.zeros_like(lb[0]), jnp.zeros_like(lb[0])], axis=1)
    sink_rows = jnp.zeros((depth, 8, LANE), f32).at[:, :SWA_HEADS].set(
        jnp.broadcast_to(swa_sink.astype(f32)[:, list(SWA_HEAD_ORDER), None], (depth, SWA_HEADS, LANE)))

    words = _expert_table(peer_u, peer_v)
    n_exp = words.shape[1]
    tab = words.reshape(depth, n_exp, d // LANE, 1, LANE)
    tab_sc = words.reshape(depth, n_exp, d // LANE, LANE)
    n_tc = nb * ta // PEER_TB - PEER_SC_BLOCKS

    xall = jnp.concatenate([ctx, x], axis=1)
    for l in range(depth):
        mod = mods[l]
        hq, cum, kk, hv, sg, sq, sk, sv, nq, nk, nv = _proj(xall, norm_mix_w[l].reshape(1, d), mod, w_in_b[l], cs, gp[l], tri)
        o = _hgrn_scan(hq, cum, kk, hv, bd)
        so = _swa(sq, sk, sv, sink_rows[l], n_ctx)
        no = _nat(nq, nk, nv, _nat_bias_tables(nat_rpb[l]), n_ctx)
        xall = _outproj(o, sg, hgrn_norm_w[l].reshape(1, HG_WIDTH), bd, so, no, wa[l], wb[l], wn[l], mod, xall)
        h2, idx, g = _route(xall, norm_ffn_w[l].reshape(1, d), mod, wq_b[l], sub_b[l])
        idx, h2, g = idx.reshape(nb * ta, PEER_SLOTS), h2.reshape(nb * ta, d), g.reshape(nb * ta, PEER_SLOTS)
        staged = _sc_gather(tab_sc[l], idx[n_tc * PEER_TB:].reshape(-1))
        xf = _experts(idx, h2, g, xall.reshape(nb * ta, d), mod, tab[l], ta, n_ctx, n_tc)
        xf = _experts_dense(staged.reshape(-1, LANE), h2, g, xf, mod, ta, n_ctx, n_tc)
        xall = xf.reshape(nb, ta, d)
    return _final_norm(xall, final_norm_w, n_ctx)
```

```python
import functools

import numpy as np
import jax
import jax.numpy as jnp
from jax import lax
from jax.experimental import pallas as pl
from jax.experimental.pallas import tpu as pltpu
from jax.experimental.pallas import tpu_sc as plsc

f32 = jnp.float32
bf16 = jnp.bfloat16
i32 = jnp.int32

LANE = 128
VMEM_LIMIT_BYTES = 56 * 1024 * 1024

GRID_W = 64
HEAD_DIM = 64
EPS = 1e-6
HG_HEADS = 4
HG_WIDTH = 256
HG_SUB = 16
SWA_HEADS = 6
SWA_KV_HEADS = 2
SWA_WINDOW = 128
SWA_BLOCK = 128
ROPE_BASE = 10000.0
NAT_HEADS = 6
NAT_ROWS = 8
NAT_COLS = 16
SWA_Q = SWA_HEADS * HEAD_DIM
SWA_KV = SWA_KV_HEADS * HEAD_DIM
NAT_W = NAT_HEADS * HEAD_DIM
PROJ_WIDTH = 5 * HG_WIDTH + SWA_Q + 2 * SWA_KV + 3 * NAT_W
PEER_HEADS = 8
PEER_NKEYS = 128
PEER_DK = 256
PEER_TOPK = 16
PEER_SLOTS = PEER_HEADS * PEER_TOPK

TM = 256
PEER_TB = 128
PEER_NBUF = 16
PEER_AHEAD = PEER_NBUF - 1
PEER_SC_SHARE = 0.508
PEER_DENSE_TB = 32
SC_CHUNK = 32
NEG = -1e30

SWA_HEAD_ORDER = (0, 3, 1, 4, 2, 5)

_NT = (((1,), (1,)), ((), ()))
_TN = (((0,), (0,)), ((), ()))


def _silu(x):
    return x * (1.0 / (1.0 + jnp.exp(-x)))


def _split2(x):
    hi = x.astype(bf16)
    lo = (x - hi.astype(f32)).astype(bf16)
    return hi, lo


def _mods_kernel(sc_ref, w_ref, b_ref, o_ref):
    s = _silu(sc_ref[...])
    o_ref[0] = jnp.dot(s.astype(bf16), w_ref[0].astype(bf16), preferred_element_type=f32) + b_ref[0]


def _mods(c, c_ctx, w_mod, b_mod):
    depth, d, d6 = w_mod.shape
    nb = c.shape[0]
    rows = jnp.zeros((8, d), f32).at[:nb].set(c).at[nb].set(c_ctx)
    tn = d6 // 4
    out = pl.pallas_call(
        _mods_kernel,
        grid=(depth, 4),
        in_specs=[pl.BlockSpec((8, d), lambda l, j: (0, 0)),
                  pl.BlockSpec((1, d, tn), lambda l, j: (l, 0, j)),
                  pl.BlockSpec((1, 1, tn), lambda l, j: (l, 0, j))],
        out_specs=pl.BlockSpec((1, 8, tn), lambda l, j: (l, 0, j)),
        out_shape=jax.ShapeDtypeStruct((depth, 8, d6), f32),
        compiler_params=pltpu.CompilerParams(dimension_semantics=("arbitrary", "arbitrary"),
                                             vmem_limit_bytes=VMEM_LIMIT_BYTES),
        name="adaln_mods",
    )(rows, w_mod, b_mod.reshape(depth, 1, d6))
    lat = out[:, :nb].reshape(depth, nb, 6, d)
    ctx = jnp.broadcast_to(out[:, nb].reshape(depth, 1, 6, d), (depth, nb, 6, d))
    return jnp.stack([ctx, lat], axis=2)


def _norm_mod(x, nw, shift, scale):
    y = x * lax.rsqrt(jnp.mean(x * x, axis=-1, keepdims=True) + EPS) * nw
    return y * (1.0 + scale) + shift


def _rope(t, cos, sins):
    lane = lax.broadcasted_iota(i32, t.shape, 1)
    sw = jnp.where((lane % 32) < 16, pltpu.roll(t, LANE - 16, 1), pltpu.roll(t, 16, 1))
    return t * cos + sw * sins


def _hgrn_gates(z, log_lb, log1m_lb, one_m_lb):
    log_sig = jnp.minimum(z, 0.0) - jnp.log1p(jnp.exp(-jnp.abs(z)))
    c = log1m_lb + log_sig
    m = jnp.maximum(log_lb, c)
    log_f = m + jnp.log1p(jnp.exp(-jnp.abs(log_lb - c)))
    k = one_m_lb * (1.0 / (1.0 + jnp.exp(z)))
    return log_f, k


def _cumsum16(tri, log_f):
    h1 = log_f.astype(bf16)
    r1 = log_f - h1.astype(f32)
    h2 = r1.astype(bf16)
    h3 = (r1 - h2.astype(f32)).astype(bf16)
    return (jnp.dot(tri, h1, preferred_element_type=f32) + jnp.dot(tri, h2, preferred_element_type=f32)
            + jnp.dot(tri, h3, preferred_element_type=f32))


def _proj_kernel(x_ref, nw_ref, mod_ref, w_ref, cs_ref, gp_ref, tri_ref,
                 hq_ref, cum_ref, kk_ref, hv_ref, sg_ref, sq_ref, sk_ref, sv_ref, nq_ref, nk_ref, nv_ref):
    h = _norm_mod(x_ref[0], nw_ref[...], mod_ref[0, 0, 0:1, :], mod_ref[0, 0, 1:2, :])
    p = jnp.dot(h.astype(bf16), w_ref[...], preferred_element_type=f32)
    w = HG_WIDTH
    hq_ref[0] = _silu(p[:, 0:w])
    for d in range(2):
        log_f, k = _hgrn_gates(p[:, (1 + d) * w:(2 + d) * w], gp_ref[3 * d:3 * d + 1, :],
                               gp_ref[3 * d + 1:3 * d + 2, :], gp_ref[3 * d + 2:3 * d + 3, :])
        cum_ref[d, 0] = _cumsum16(tri_ref[d], log_f)
        kk_ref[d, 0] = k
    hv_ref[0] = p[:, 3 * w:4 * w]
    sg_ref[0] = _silu(p[:, 4 * w:5 * w])
    cos, sins = cs_ref[0], cs_ref[1]
    o = 5 * w
    scale = HEAD_DIM ** -0.5
    for j in range(SWA_Q // LANE):
        sq_ref[0, :, j * LANE:(j + 1) * LANE] = (_rope(p[:, o + j * LANE:o + (j + 1) * LANE], cos, sins) * scale).astype(bf16)
    o += SWA_Q
    sk_ref[0] = _rope(p[:, o:o + SWA_KV], cos, sins).astype(bf16)
    o += SWA_KV
    sv_ref[0] = p[:, o:o + SWA_KV].astype(bf16)
    o += SWA_KV
    nq_ref[0] = (p[:, o:o + NAT_W] * scale).astype(bf16)
    o += NAT_W
    nk_ref[0] = p[:, o:o + NAT_W].astype(bf16)
    o += NAT_W
    nv_ref[0] = p[:, o:o + NAT_W].astype(bf16)


def _proj(xall, nw, mod, w_in, cs, gp, tri):
    nb, ta, d = xall.shape
    nt = ta // TM
    tok = lambda width, dt: jax.ShapeDtypeStruct((nb, ta, width), dt)
    tok2 = jax.ShapeDtypeStruct((2, nb, ta, HG_WIDTH), f32)
    bs = lambda width: pl.BlockSpec((1, TM, width), lambda b, i: (b, i, 0))
    bs2 = pl.BlockSpec((2, 1, TM, HG_WIDTH), lambda b, i: (0, b, i, 0))
    return pl.pallas_call(
        _proj_kernel,
        grid=(nb, nt),
        in_specs=[pl.BlockSpec((1, TM, d), lambda b, i: (b, i, 0)),
                  pl.BlockSpec((1, d), lambda b, i: (0, 0)),
                  pl.BlockSpec((1, 1, 6, d), lambda b, i: (b, jnp.minimum(i, 1), 0, 0)),
                  pl.BlockSpec((d, PROJ_WIDTH), lambda b, i: (0, 0)),
                  pl.BlockSpec((2, TM, LANE), lambda b, i: (0, i, 0)),
                  pl.BlockSpec((8, HG_WIDTH), lambda b, i: (0, 0)),
                  pl.BlockSpec((2, TM, TM), lambda b, i: (0, 0, 0))],
        out_specs=[bs(HG_WIDTH), bs2, bs2, bs(HG_WIDTH), bs(HG_WIDTH), bs(SWA_Q), bs(SWA_KV), bs(SWA_KV),
                   bs(NAT_W), bs(NAT_W), bs(NAT_W)],
        out_shape=[tok(HG_WIDTH, f32), tok2, tok2, tok(HG_WIDTH, f32), tok(HG_WIDTH, f32), tok(SWA_Q, bf16),
                   tok(SWA_KV, bf16), tok(SWA_KV, bf16), tok(NAT_W, bf16), tok(NAT_W, bf16), tok(NAT_W, bf16)],
        compiler_params=pltpu.CompilerParams(dimension_semantics=("arbitrary", "arbitrary"),
                                             vmem_limit_bytes=VMEM_LIMIT_BYTES),
        name="norm_in_proj",
    )(xall, nw, mod, w_in, cs, gp, tri)


def _hgrn_kernel(q_ref, cum_ref, k_ref, v_ref, bd_ref, o_ref, st_ref):
    d = pl.program_id(1)
    fwd = d == 0

    @pl.when(pl.program_id(2) == 0)
    def _():
        st_ref[...] = jnp.zeros_like(st_ref)

    bd = bd_ref[...]
    bd_mask = bd > 0
    sgn = jnp.where(fwd, 1, -1)
    t_sgn = lax.broadcasted_iota(i32, (HG_SUB, 1), 0) * sgn
    nsteps = TM // HG_SUB

    def step(i, carry):
        sc = jnp.where(fwd, i, nsteps - 1 - i)
        r0 = pl.multiple_of(sc * HG_SUB, HG_SUB)
        q = q_ref[0, pl.ds(r0, HG_SUB), :]
        b = cum_ref[0, 0, pl.ds(r0, HG_SUB), :]
        k = k_ref[0, 0, pl.ds(r0, HG_SUB), :]
        v = v_ref[0, pl.ds(r0, HG_SUB), :]
        b_end = jnp.where(fwd, b[HG_SUB - 1:HG_SUB, :], b[0:1, :])
        st = st_ref[...]
        o = lax.dot_general((q * jnp.exp(b)).astype(bf16), st.astype(bf16), _NT, preferred_element_type=f32)
        rows = []
        for s in range(HG_SUB):
            valid = t_sgn >= s * sgn
            decay = jnp.exp(jnp.minimum(b - b[s:s + 1, :], 0.0))
            rows.append(jnp.where(valid, decay * q * k[s:s + 1, :], 0.0))
        hi, lo = _split2(jnp.concatenate(rows, axis=0))
        pe = jnp.dot(hi, bd, preferred_element_type=f32) + jnp.dot(lo, bd, preferred_element_type=f32)
        for s in range(HG_SUB):
            o = o + pe[s * HG_SUB:(s + 1) * HG_SUB, :] * v[s:s + 1, :]
        o_ref[0, 0, pl.ds(r0, HG_SUB), :] = o
        kd = (k * jnp.exp(b_end - b)).astype(bf16)
        kv_t = lax.dot_general(v.astype(bf16), kd, _TN, preferred_element_type=f32)
        st_ref[...] = st * jnp.exp(b_end) + jnp.where(bd_mask, kv_t, 0.0)
        return carry

    lax.fori_loop(0, nsteps, step, 0, unroll=8)


def _hgrn_scan(hq, cum, kk, hv, bd):
    nb, ta, w = hq.shape
    nt = ta // TM

    def blk(d, j):
        return jnp.where(d == 0, j, jnp.where(j == 0, 0, nt - j))

    return pl.pallas_call(
        _hgrn_kernel,
        grid=(nb, 2, nt),
        in_specs=[pl.BlockSpec((1, TM, w), lambda b, d, j: (b, blk(d, j), 0)),
                  pl.BlockSpec((1, 1, TM, w), lambda b, d, j: (d, b, blk(d, j), 0)),
                  pl.BlockSpec((1, 1, TM, w), lambda b, d, j: (d, b, blk(d, j), 0)),
                  pl.BlockSpec((1, TM, w), lambda b, d, j: (b, blk(d, j), 0)),
                  pl.BlockSpec((w, w), lambda b, d, j: (0, 0))],
        out_specs=pl.BlockSpec((1, 1, TM, w), lambda b, d, j: (d, b, blk(d, j), 0)),
        out_shape=jax.ShapeDtypeStruct((2, nb, ta, w), f32),
        scratch_shapes=[pltpu.VMEM((w, w), f32)],
        compiler_params=pltpu.CompilerParams(dimension_semantics=("arbitrary", "arbitrary", "arbitrary")),
        name="hgrn_scan",
    )(hq, cum, kk, hv, bd)


def _pair_rows(q2):
    lane = lax.broadcasted_iota(i32, q2.shape, 1)
    zero = jnp.zeros_like(q2)
    return jnp.concatenate([jnp.where(lane < HEAD_DIM, q2, zero), jnp.where(lane >= HEAD_DIM, q2, zero)], axis=0)


def _pair_merge(o):
    m = o.shape[0] // 2
    lane = lax.broadcasted_iota(i32, (m, LANE), 1)
    return jnp.where(lane < HEAD_DIM, o[:m], o[m:])


def _joint_attention(s_loc, s_ctx, v_loc, v_ctx, sink=None):
    m = jnp.maximum(jnp.max(s_loc, axis=-1, keepdims=True), jnp.max(s_ctx, axis=-1, keepdims=True))
    if sink is not None:
        m = jnp.maximum(m, sink)
    p_loc = jnp.exp(s_loc - m)
    p_ctx = jnp.exp(s_ctx - m)
    den = jnp.sum(p_loc, axis=-1, keepdims=True) + jnp.sum(p_ctx, axis=-1, keepdims=True)
    if sink is not None:
        den = den + jnp.exp(sink - m)
    o = (jnp.dot(p_loc.astype(bf16), v_loc, preferred_element_type=f32)
         + jnp.dot(p_ctx.astype(bf16), v_ctx, preferred_element_type=f32))
    return o / den


def _swa_kernel(q_ref, kp_ref, kc_ref, kn_ref, vp_ref, vc_ref, vn_ref, kx_ref, vx_ref, sink_ref, o_ref, *, n_ctx_blocks, seq):
    g = pl.program_id(1)
    i = g - n_ctx_blocks
    k_loc = jnp.concatenate([kp_ref[0], kc_ref[0], kn_ref[0]], axis=0)
    v_loc = jnp.concatenate([vp_ref[0], vc_ref[0], vn_ref[0]], axis=0)
    row = lax.broadcasted_iota(i32, (SWA_BLOCK, 3 * SWA_BLOCK), 0)
    col = lax.broadcasted_iota(i32, (SWA_BLOCK, 3 * SWA_BLOCK), 1)
    rel = col - SWA_BLOCK - row
    kpos = (i - 1) * SWA_BLOCK + col
    ok = (jnp.abs(rel) <= SWA_WINDOW) & (kpos >= 0) & (kpos < seq) & (i >= 0)
    pen = jnp.where(ok, 0.0, NEG)
    pen = jnp.concatenate([pen, pen], axis=0)
    for j in range(SWA_Q // LANE):
        q = _pair_rows(q_ref[0, :, j * LANE:(j + 1) * LANE])
        s_loc = lax.dot_general(q, k_loc, _NT, preferred_element_type=f32) + pen
        s_ctx = lax.dot_general(q, kx_ref[0], _NT, preferred_element_type=f32)
        sink = jnp.concatenate([jnp.broadcast_to(sink_ref[2 * j:2 * j + 1, 0:1], (SWA_BLOCK, 1)),
                                jnp.broadcast_to(sink_ref[2 * j + 1:2 * j + 2, 0:1], (SWA_BLOCK, 1))], axis=0)
        o = _joint_attention(s_loc, s_ctx, v_loc, vx_ref[0], sink)
        o_ref[0, :, j * LANE:(j + 1) * LANE] = _pair_merge(o).astype(bf16)


def _swa(sq, sk, sv, sink_rows, n_ctx):
    nb, ta, _ = sq.shape
    ncb = n_ctx // SWA_BLOCK
    nblk = ta // SWA_BLOCK
    nlat = nblk - ncb
    prev = lambda b, g: (b, jnp.clip(g - ncb - 1, 0, nlat - 1) + ncb, 0)
    cur = lambda b, g: (b, g, 0)
    nxt = lambda b, g: (b, jnp.clip(g - ncb + 1, 0, nlat - 1) + ncb, 0)
    kvs = lambda f: pl.BlockSpec((1, SWA_BLOCK, SWA_KV), f)
    ctx = pl.BlockSpec((1, n_ctx, SWA_KV), lambda b, g: (b, 0, 0))
    return pl.pallas_call(
        functools.partial(_swa_kernel, n_ctx_blocks=ncb, seq=ta - n_ctx),
        grid=(nb, nblk),
        in_specs=[pl.BlockSpec((1, SWA_BLOCK, SWA_Q), cur), kvs(prev), kvs(cur), kvs(nxt), kvs(prev), kvs(cur), kvs(nxt),
                  ctx, ctx, pl.BlockSpec((8, LANE), lambda b, g: (0, 0))],
        out_specs=pl.BlockSpec((1, SWA_BLOCK, SWA_Q), cur),
        out_shape=jax.ShapeDtypeStruct((nb, ta, SWA_Q), bf16),
        compiler_params=pltpu.CompilerParams(dimension_semantics=("arbitrary", "arbitrary")),
        name="swa_attention",
    )(sq, sk, sk, sk, sv, sv, sv, sk, sv, sink_rows)


def _nat_kernel(q_ref, k_ref, v_ref, bias_ref, o_ref, *, n_ctx, n_rows):
    g = pl.program_id(1)
    is_ctx = g == 0
    ctx_pen = jnp.where(is_ctx, NEG, 0.0)
    rows_per_tile = TM // GRID_W
    nwin = NAT_ROWS * GRID_W
    for rr in range(rows_per_tile):
        r = jnp.maximum((g - 1) * rows_per_tile + rr, 0)
        rs = jnp.clip(r - NAT_ROWS // 2, 0, n_rows - NAT_ROWS)
        dr0 = rs - r + NAT_ROWS - 1
        start = pl.multiple_of(n_ctx + rs * GRID_W, GRID_W)
        for p in range(NAT_W // LANE):
            ls = slice(p * LANE, (p + 1) * LANE)
            q = _pair_rows(q_ref[0, rr * GRID_W:(rr + 1) * GRID_W, ls])
            s_loc = (lax.dot_general(q, k_ref[0, pl.ds(start, nwin), ls], _NT, preferred_element_type=f32)
                     + bias_ref[p, dr0] + ctx_pen)
            s_ctx = lax.dot_general(q, k_ref[0, 0:n_ctx, ls], _NT, preferred_element_type=f32)
            o = _joint_attention(s_loc, s_ctx, v_ref[0, pl.ds(start, nwin), ls], v_ref[0, 0:n_ctx, ls])
            o_ref[0, rr * GRID_W:(rr + 1) * GRID_W, ls] = _pair_merge(o).astype(bf16)


def _nat(nq, nk, nv, bias, n_ctx):
    nb, ta, w = nq.shape
    nt = ta // TM
    whole = pl.BlockSpec((1, ta, w), lambda b, g: (b, 0, 0), pipeline_mode=pl.Buffered(1))
    return pl.pallas_call(
        functools.partial(_nat_kernel, n_ctx=n_ctx, n_rows=(ta - n_ctx) // GRID_W),
        grid=(nb, nt),
        in_specs=[pl.BlockSpec((1, TM, w), lambda b, g: (b, g, 0)), whole, whole,
                  pl.BlockSpec(bias.shape, lambda b, g: (0, 0, 0, 0), pipeline_mode=pl.Buffered(1))],
        out_specs=pl.BlockSpec((1, TM, w), lambda b, g: (b, g, 0)),
        out_shape=jax.ShapeDtypeStruct((nb, ta, w), bf16),
        compiler_params=pltpu.CompilerParams(dimension_semantics=("arbitrary", "arbitrary"),
                                             vmem_limit_bytes=VMEM_LIMIT_BYTES),
        name="nat_attention",
    )(nq, nk, nv, bias)


def _nat_bias_tables(rpb):
    c = np.arange(GRID_W)[:, None]
    kc = np.arange(GRID_W)[None, :]
    ws = np.clip(c - NAT_COLS // 2, 0, GRID_W - NAT_COLS)
    valid = (kc >= ws) & (kc < ws + NAT_COLS)
    col_idx = np.clip(kc - c + NAT_COLS - 1, 0, 2 * NAT_COLS - 2)
    dr = np.arange(NAT_ROWS)[:, None] + np.arange(NAT_ROWS)[None, :]
    t = rpb.astype(f32)[:, dr][:, :, :, col_idx]
    t = jnp.where(jnp.asarray(valid)[None, None, None], t, NEG)
    t = t.transpose(0, 1, 3, 2, 4).reshape(NAT_HEADS // 2, 2, NAT_ROWS, GRID_W, NAT_ROWS * GRID_W)
    return t.transpose(0, 2, 1, 3, 4).reshape(NAT_HEADS // 2, NAT_ROWS, 2 * GRID_W, NAT_ROWS * GRID_W)


def _outproj_kernel(o_ref, sg_ref, hnw_ref, bd_ref, so_ref, no_ref, wa_ref, wb_ref, wn_ref, mod_ref, x_ref, y_ref):
    oo = o_ref[0, 0] + o_ref[1, 0]
    hi, lo = _split2(oo * oo)
    bd = bd_ref[...]
    msq = (jnp.dot(hi, bd, preferred_element_type=f32) + jnp.dot(lo, bd, preferred_element_type=f32)) * (1.0 / HEAD_DIM)
    a = oo * lax.rsqrt(msq + EPS) * hnw_ref[...] * sg_ref[0]
    y = (jnp.dot(a.astype(bf16), wa_ref[...], preferred_element_type=f32)
         + jnp.dot(so_ref[0], wb_ref[...], preferred_element_type=f32)
         + jnp.dot(no_ref[0], wn_ref[...], preferred_element_type=f32))
    y_ref[0] = x_ref[0] + mod_ref[0, 0, 2:3, :] * y


def _outproj(o, sg, hnw, bd, so, no, wa, wb, wn, mod, xall):
    nb, ta, d = xall.shape
    nt = ta // TM
    bs = lambda width: pl.BlockSpec((1, TM, width), lambda b, i: (b, i, 0))
    full = lambda a: pl.BlockSpec(a.shape, lambda b, i: (0,) * a.ndim)
    return pl.pallas_call(
        _outproj_kernel,
        grid=(nb, nt),
        in_specs=[pl.BlockSpec((2, 1, TM, HG_WIDTH), lambda b, i: (0, b, i, 0)), bs(HG_WIDTH), full(hnw), full(bd),
                  bs(SWA_Q), bs(NAT_W), full(wa), full(wb), full(wn),
                  pl.BlockSpec((1, 1, 6, d), lambda b, i: (b, jnp.minimum(i, 1), 0, 0)), bs(d)],
        out_specs=bs(d),
        out_shape=jax.ShapeDtypeStruct(xall.shape, f32),
        compiler_params=pltpu.CompilerParams(dimension_semantics=("arbitrary", "arbitrary"),
                                             vmem_limit_bytes=VMEM_LIMIT_BYTES),
        name="out_proj",
    )(o, sg, hnw, bd, so, no, wa, wb, wn, mod, xall)


def _top16_rows(vals, payload=None):
    n = vals.shape[0]
    iota = lax.broadcasted_iota(i32, vals.shape, 0)
    best, picked = [], []
    for _ in range(PEER_TOPK):
        m = jnp.max(vals, axis=0, keepdims=True)
        idx = jnp.min(jnp.where(vals == m, iota, n), axis=0, keepdims=True)
        hit = iota == idx
        best.append(m)
        picked.append(idx if payload is None else jnp.max(jnp.where(hit, payload, -1), axis=0, keepdims=True))
        vals = jnp.where(hit, -jnp.inf, vals)
    return jnp.concatenate(best, axis=0), jnp.concatenate(picked, axis=0)


def _product_candidates(first, second, combine):
    half = PEER_TOPK // 2
    rows = [combine(first[0:1, :], second)]
    rows += [combine(first[a:a + 1, :], second[0:half, :]) for a in range(1, half)]
    rows.append(combine(first[half:, :], second[0:1, :]))
    return jnp.concatenate(rows, axis=0)


def _route_kernel(x_ref, nw_ref, mod_ref, wq_ref, sub_ref, h_ref, idx_ref, g_ref):
    h = _norm_mod(x_ref[0], nw_ref[...], mod_ref[0, 0, 3:4, :], mod_ref[0, 0, 4:5, :])
    h_ref[0] = h
    q = jnp.dot(h.astype(bf16), wq_ref[...], preferred_element_type=f32).astype(bf16)
    half = PEER_DK // 2
    ids, gates = [], []
    for hh in range(PEER_HEADS):
        top_s, top_i = [], []
        for p in range(2):
            c0 = (hh * 2 + p) * half
            s = lax.dot_general(sub_ref[hh, p], q[:, c0:c0 + half], _NT, preferred_element_type=f32)
            ts, ti = _top16_rows(s)
            top_s.append(ts)
            top_i.append(ti)
        cand_s = _product_candidates(top_s[0], top_s[1], lambda a, b: a + b)
        cand_i = _product_candidates(top_i[0], top_i[1], lambda a, b: a * PEER_NKEYS + b)
        best, eid = _top16_rows(cand_s, cand_i)
        ids.append(eid)
        e = jnp.exp(best - best[0:1, :])
        gates.append(e / jnp.sum(e, axis=0, keepdims=True))
    idx_ref[0] = jnp.concatenate(ids, axis=0).T
    g_ref[0] = jnp.concatenate(gates, axis=0).T


def _route(xall, nw, mod, wq, sub):
    nb, ta, d = xall.shape
    nt = ta // TM
    full = lambda a: pl.BlockSpec(a.shape, lambda b, i: (0,) * a.ndim)
    return pl.pallas_call(
        _route_kernel,
        grid=(nb, nt),
        in_specs=[pl.BlockSpec((1, TM, d), lambda b, i: (b, i, 0)), full(nw),
                  pl.BlockSpec((1, 1, 6, d), lambda b, i: (b, jnp.minimum(i, 1), 0, 0)), full(wq), full(sub)],
        out_specs=[pl.BlockSpec((1, TM, d), lambda b, i: (b, i, 0)),
                   pl.BlockSpec((1, TM, PEER_SLOTS), lambda b, i: (b, i, 0)),
                   pl.BlockSpec((1, TM, PEER_SLOTS), lambda b, i: (b, i, 0))],
        out_shape=[jax.ShapeDtypeStruct(xall.shape, f32), jax.ShapeDtypeStruct((nb, ta, PEER_SLOTS), i32),
                   jax.ShapeDtypeStruct((nb, ta, PEER_SLOTS), f32)],
        compiler_params=pltpu.CompilerParams(dimension_semantics=("arbitrary", "arbitrary"),
                                             vmem_limit_bytes=VMEM_LIMIT_BYTES),
        name="peer_route",
    )(xall, nw, mod, wq, sub)


def _expert_kernel(idx_ref, idx_next_ref, h_ref, g_ref, x_ref, mod_ref, tab_hbm, o_ref, buf, sem):
    d = h_ref.shape[-1]
    nct = d // LANE
    groups = PEER_SLOTS // 8

    per_piece = PEER_SLOTS // nct

    def issue(ids_ref, t, slot, piece):
        for k in range(piece * per_piece, (piece + 1) * per_piece):
            pltpu.make_async_copy(tab_hbm.at[ids_ref[t, k]], buf.at[slot, k // 8, :, pl.ds(k % 8, 1), :],
                                  sem.at[slot]).start(priority=k % 2)

    def wait(slot):
        pltpu.make_async_copy(buf.at[slot], buf.at[slot], sem.at[slot]).wait()

    def cols(slot, c, high):
        words = jnp.concatenate([jnp.concatenate([buf[slot, kg, c], buf[slot, kg, c + 1]], axis=1)
                                 for kg in range(groups)], axis=0)
        bits = (words & jnp.uint32(0xFFFF0000)) if high else (words << 16)
        return lax.bitcast_convert_type(bits, f32).astype(bf16)

    not_last = pl.program_id(0) < pl.num_programs(0) - 1

    def step(t, w8, do_issue, do_u, do_v):
        nslot = lax.rem(t + 1, PEER_NBUF)
        slot = lax.rem(t, PEER_NBUF)
        islot = lax.rem(t + PEER_AHEAD, PEER_NBUF)
        if do_issue == "next":
            @pl.when(not_last)
            def _():
                for j in range(nct):
                    issue(idx_next_ref, t + PEER_AHEAD - PEER_TB, islot, j)
        if do_u:
            wait(nslot)
            h8 = jnp.broadcast_to(h_ref[pl.ds(t + 1, 1), :], (8, d)).astype(bf16)
            gate = g_ref[pl.ds(t + 1, 1), :]
            a = jnp.zeros((8, PEER_SLOTS), f32)
        ys = []
        for j in range(nct):
            if 2 * j < nct:
                c = 2 * j
                if do_u:
                    a = a + lax.dot_general(h8[:, c * LANE:(c + 2) * LANE], cols(nslot, c, True), _NT, preferred_element_type=f32)
                if do_v:
                    ys.append(jnp.dot(w8, cols(slot, c, False), preferred_element_type=f32)[0:1, :])
            if do_issue is True:
                issue(idx_ref, t + PEER_AHEAD, islot, j)
        if do_v:
            o_ref[pl.ds(t, 1), :] = jnp.concatenate(ys, axis=1)
        if do_u:
            a1 = a[0:1, :]
            w = gate * (0.5 * a1 * (1.0 + lax.erf(a1 * (2.0 ** -0.5))))
            w8 = jnp.broadcast_to(w, (8, PEER_SLOTS)).astype(bf16)
        return w8

    @pl.when(pl.program_id(0) == 0)
    def _():
        def prologue(t, carry):
            for j in range(nct):
                issue(idx_ref, t, t, j)
            return carry

        lax.fori_loop(0, PEER_AHEAD, prologue, 0)

    w8 = step(-1, jnp.zeros((8, PEER_SLOTS), bf16), False, True, False)
    w8 = lax.fori_loop(0, PEER_TB - PEER_AHEAD, lambda t, w: step(t, w, True, True, True), w8)
    w8 = lax.fori_loop(PEER_TB - PEER_AHEAD, PEER_TB - 1, lambda t, w: step(t, w, "next", True, True), w8)
    step(PEER_TB - 1, w8, "next", False, True)
    o_ref[...] = x_ref[...] + mod_ref[0, 0, 5:6, :] * o_ref[...]


def _mod_index(tokens_per_step, ta, n_ctx, first_step=0):
    spb = ta // tokens_per_step
    cs = n_ctx // tokens_per_step
    return lambda i: ((i + first_step) // spb, jnp.where((i + first_step) % spb < cs, 0, 1), 0, 0)


def _experts(idx, h, g, xflat, mod, tab, ta, n_ctx, nblocks):
    n, d = xflat.shape
    tokb = lambda width: pl.BlockSpec((PEER_TB, width), lambda i: (i, 0))
    return pl.pallas_call(
        _expert_kernel,
        grid=(nblocks,),
        in_specs=[pl.BlockSpec((PEER_TB, PEER_SLOTS), lambda i: (i, 0), memory_space=pltpu.SMEM),
                  pl.BlockSpec((PEER_TB, PEER_SLOTS), lambda i: (jnp.minimum(i + 1, nblocks - 1), 0), memory_space=pltpu.SMEM),
                  tokb(d), tokb(PEER_SLOTS), tokb(d),
                  pl.BlockSpec((1, 1, 6, d), _mod_index(PEER_TB, ta, n_ctx)),
                  pl.BlockSpec(memory_space=pl.ANY)],
        out_specs=tokb(d),
        out_shape=jax.ShapeDtypeStruct((n, d), f32),
        input_output_aliases={4: 0},
        scratch_shapes=[pltpu.VMEM((PEER_NBUF, PEER_SLOTS // 8, d // LANE, 8, LANE), jnp.uint32),
                        pltpu.SemaphoreType.DMA((PEER_NBUF,))],
        compiler_params=pltpu.CompilerParams(dimension_semantics=("arbitrary",), vmem_limit_bytes=VMEM_LIMIT_BYTES),
        name="peer_experts",
    )(idx, idx, h, g, xflat, mod, tab)


def _sc_gather(table, idx):
    info = plsc.get_sparse_core_info()
    nc = info.num_cores
    nw = nc * info.num_subcores
    r = idx.shape[0]
    per_w = r // nw
    nch = per_w // SC_CHUNK
    assert per_w * nw == r and nch * SC_CHUNK == per_w and nch % 2 == 0
    rows_shape = (SC_CHUNK,) + table.shape[1:]
    mesh = plsc.VectorSubcoreMesh(core_axis_name="c", subcore_axis_name="s")

    @functools.partial(
        pl.kernel, mesh=mesh, out_type=jax.ShapeDtypeStruct((r,) + table.shape[1:], table.dtype),
        scratch_types=[pltpu.VMEM((2, SC_CHUNK), i32), pltpu.VMEM((2,) + rows_shape, table.dtype),
                       pltpu.SemaphoreType.DMA((2,)), pltpu.SemaphoreType.DMA((2,)), pltpu.SemaphoreType.DMA((2,))])
    def gather_kernel(table_hbm, idx_hbm, out_hbm, idx_v, rows_v, isem, gsem, wsem):
        base = (lax.axis_index("s") * nc + lax.axis_index("c")) * per_w

        def rows_of(j):
            return pl.ds(pl.multiple_of(base + j * SC_CHUNK, 8), SC_CHUNK)

        def fetch_idx(j, b):
            return pltpu.make_async_copy(idx_hbm.at[rows_of(j)], idx_v.at[b], isem.at[b])

        def gather(b):
            return pltpu.make_async_copy(table_hbm.at[idx_v.at[b]], rows_v.at[b], gsem.at[b])

        def write(j, b):
            return pltpu.make_async_copy(rows_v.at[b], out_hbm.at[rows_of(j)], wsem.at[b])

        fetch_idx(0, 0).start()
        fetch_idx(1, 1).start()
        fetch_idx(0, 0).wait()
        gather(0).start()

        @pl.loop(0, nch, step=2)
        def _(j0):
            for b in (0, 1):
                j = j0 + b

                @pl.when(j >= 1)
                def _():
                    write(j - 1, 1 - b).wait()

                @pl.when(j + 1 < nch)
                def _():
                    fetch_idx(j + 1, 1 - b).wait()
                    gather(1 - b).start()

                gather(b).wait()

                @pl.when(j + 2 < nch)
                def _():
                    fetch_idx(j + 2, b).start()

                write(j, b).start()

        write(nch - 1, 1).wait()

    return gather_kernel(table, idx)


def _expert_dense_kernel(st_ref, h_ref, g_ref, x_ref, mod_ref, o_ref):
    d = h_ref.shape[-1]
    nct = d // LANE
    groups = PEER_SLOTS // 8
    rows_per_token = PEER_SLOTS * nct

    def cols(t, c, high):
        base = t * rows_per_token
        tile = lambda kg, cc: st_ref[pl.ds(base + kg * 8 * nct + cc, 8, stride=nct), :]
        words = jnp.concatenate([jnp.concatenate([tile(kg, c), tile(kg, c + 1)], axis=1) for kg in range(groups)], axis=0)
        bits = (words & jnp.uint32(0xFFFF0000)) if high else (words << 16)
        return lax.bitcast_convert_type(bits, f32).astype(bf16)

    def u_stage(t):
        h8 = jnp.broadcast_to(h_ref[pl.ds(t, 1), :], (8, d)).astype(bf16)
        a = jnp.zeros((8, PEER_SLOTS), f32)
        for c in range(0, nct, 2):
            a = a + lax.dot_general(h8[:, c * LANE:(c + 2) * LANE], cols(t, c, True), _NT, preferred_element_type=f32)
        a1 = a[0:1, :]
        return g_ref[pl.ds(t, 1), :] * (0.5 * a1 * (1.0 + lax.erf(a1 * (2.0 ** -0.5))))

    lane = lax.broadcasted_iota(i32, (8, LANE), 1)
    sub = lax.broadcasted_iota(i32, (8, LANE), 0)

    def v_stage(t, w_row):
        base = t * rows_per_token
        wb = jnp.broadcast_to(w_row, (8, LANE))
        acc = [None] * nct
        for kg in range(groups):
            wcol = jnp.sum(jnp.where(lane == kg * 8 + sub, wb, 0.0), axis=-1, keepdims=True)
            for c in range(nct):
                words = st_ref[pl.ds(base + kg * 8 * nct + c, 8, stride=nct), :]
                term = lax.bitcast_convert_type(words << 16, f32) * wcol
                acc[c] = term if acc[c] is None else acc[c] + term
        o_ref[pl.ds(t, 1), :] = jnp.concatenate([jnp.sum(a, axis=0, keepdims=True) for a in acc], axis=1)

    def body(t, w_row):
        w_next = u_stage(t + 1)
        v_stage(t, w_row)
        return w_next

    w_row = lax.fori_loop(0, PEER_DENSE_TB - 1, body, u_stage(0), unroll=8)
    v_stage(PEER_DENSE_TB - 1, w_row)
    o_ref[...] = x_ref[...] + mod_ref[0, 0, 5:6, :] * o_ref[...]


def _experts_dense(staged, h, g, xflat, mod, ta, n_ctx, first_block):
    n, d = xflat.shape
    rows_per_token = PEER_SLOTS * (d // LANE)
    steps = staged.shape[0] // (PEER_DENSE_TB * rows_per_token)
    off = first_block * PEER_TB // PEER_DENSE_TB
    tokb = lambda width: pl.BlockSpec((PEER_DENSE_TB, width), lambda i: (i + off, 0))
    return pl.pallas_call(
        _expert_dense_kernel,
        grid=(steps,),
        in_specs=[pl.BlockSpec((PEER_DENSE_TB * rows_per_token, LANE), lambda i: (i, 0)),
                  tokb(d), tokb(PEER_SLOTS), tokb(d),
                  pl.BlockSpec((1, 1, 6, d), _mod_index(PEER_DENSE_TB, ta, n_ctx, off))],
        out_specs=tokb(d),
        out_shape=jax.ShapeDtypeStruct((n, d), f32),
        input_output_aliases={3: 0},
        compiler_params=pltpu.CompilerParams(dimension_semantics=("arbitrary",), vmem_limit_bytes=VMEM_LIMIT_BYTES),
        name="peer_experts_dense",
    )(staged, h, g, xflat, mod)


def _expert_table(u, v):
    half = lambda a: lax.bitcast_convert_type(a.astype(bf16), jnp.uint16).astype(jnp.uint32)
    return (half(u) << 16) | half(v)


def _final_norm_kernel(x_ref, w_ref, o_ref):
    x = x_ref[0]
    o_ref[0] = x * lax.rsqrt(jnp.mean(x * x, axis=-1, keepdims=True) + EPS) * w_ref[...]


def _final_norm(xall, w, n_ctx):
    nb, ta, d = xall.shape
    off = n_ctx // TM
    return pl.pallas_call(
        _final_norm_kernel,
        grid=(nb, (ta - n_ctx) // TM),
        in_specs=[pl.BlockSpec((1, TM, d), lambda b, i: (b, i + off, 0)), pl.BlockSpec((1, d), lambda b, i: (0, 0))],
        out_specs=pl.BlockSpec((1, TM, d), lambda b, i: (b, i, 0)),
        out_shape=jax.ShapeDtypeStruct((nb, ta - n_ctx, d), f32),
        compiler_params=pltpu.CompilerParams(dimension_semantics=("arbitrary", "arbitrary")),
        name="final_norm",
    )(xall, w.reshape(1, d))


def _rope_tables(n_ctx, seq):
    nf = HEAD_DIM // 4
    pos = np.arange(seq)
    inv = ROPE_BASE ** (-np.arange(nf, dtype=np.float32) / nf)
    ar = (pos // GRID_W).astype(np.float32)[:, None] * inv
    ac = (pos % GRID_W).astype(np.float32)[:, None] * inv
    cos = np.concatenate([np.cos(ar), np.cos(ar), np.cos(ac), np.cos(ac)], axis=-1)
    sin = np.concatenate([-np.sin(ar), np.sin(ar), -np.sin(ac), np.sin(ac)], axis=-1)
    cos = np.concatenate([np.ones((n_ctx, HEAD_DIM), np.float32), cos.astype(np.float32)], axis=0)
    sin = np.concatenate([np.zeros((n_ctx, HEAD_DIM), np.float32), sin.astype(np.float32)], axis=0)
    return jnp.asarray(np.stack([np.tile(cos, (1, 2)), np.tile(sin, (1, 2))]))


def _tri_tables():
    r = np.arange(TM)[:, None]
    c = np.arange(TM)[None, :]
    same = (r // HG_SUB) == (c // HG_SUB)
    return jnp.asarray(np.stack([same & (c <= r), same & (c >= r)]).astype(np.float32), dtype=bf16)


def _block_diag_ones():
    r = np.arange(HG_WIDTH)
    return jnp.asarray((r[:, None] // HEAD_DIM == r[None, :] // HEAD_DIM).astype(np.float32), dtype=bf16)


def _swa_perm():
    return np.concatenate([np.arange(h * HEAD_DIM, (h + 1) * HEAD_DIM) for h in SWA_HEAD_ORDER])


def kernel(x, c, ctx, c_ctx, w_mod, b_mod, norm_mix_w, norm_ffn_w, w_in, hgrn_lb_logits, hgrn_norm_w, swa_sink, nat_rpb,
           w_out, peer_wq, peer_subkeys, peer_u, peer_v, final_norm_w):
    nb, seq, d = x.shape
    n_ctx = ctx.shape[1]
    depth = w_in.shape[0]
    ta = n_ctx + seq
    assert n_ctx == TM and seq % TM == 0 and seq % (GRID_W * NAT_ROWS) == 0 and d % LANE == 0
    assert w_in.shape[-1] == PROJ_WIDTH and peer_wq.shape[-1] == PEER_HEADS * PEER_DK

    mods = _mods(c, c_ctx, w_mod, b_mod)
    cs = _rope_tables(n_ctx, seq)
    tri = _tri_tables()
    bd = _block_diag_ones()
    perm = _swa_perm()
    q0 = 5 * HG_WIDTH
    col_perm = np.concatenate([np.arange(q0), q0 + perm, np.arange(q0 + SWA_Q, PROJ_WIDTH)])
    w_in_b = w_in[:, :, col_perm].astype(bf16)
    w_out_b = w_out.astype(bf16)
    wa = w_out_b[:, :HG_WIDTH]
    wb = w_out_b[:, HG_WIDTH:HG_WIDTH + SWA_Q][:, perm]
    wn = w_out_b[:, HG_WIDTH + SWA_Q:]
    wq_b = peer_wq.astype(bf16)
    sub_b = peer_subkeys.astype(bf16)
    lb = jnp.cumsum(jax.nn.softmax(hgrn_lb_logits.astype(f32), axis=1), axis=1)
    lb = lb - lb[:, :1]
    gp = jnp.stack([jnp.log(lb[0]), jnp.log1p(-lb[0]), 1.0 - lb[0], jnp.log(lb[1]), jnp.log1p(-lb[1]), 1.0 - lb[1],
                    jnp.zeros_like(lb[0]), jnp.zeros_like(lb[0])], axis=1)
    sink_rows = jnp.zeros((depth, 8, LANE), f32).at[:, :SWA_HEADS].set(
        jnp.broadcast_to(swa_sink.astype(f32)[:, list(SWA_HEAD_ORDER), None], (depth, SWA_HEADS, LANE)))

    words = _expert_table(peer_u, peer_v)
    n_exp = words.shape[1]
    tab = words.reshape(depth, n_exp, d // LANE, 1, LANE)
    tab_sc = words.reshape(depth, n_exp, d // LANE, LANE)
    n_blocks = nb * ta // PEER_TB
    n_tc = n_blocks - int(n_blocks * PEER_SC_SHARE + 0.5)
    assert ta % PEER_TB == 0 and PEER_TB % PEER_NBUF == 0 and PEER_TB % PEER_DENSE_TB == 0 and 0 < n_tc < n_blocks

    xall = jnp.concatenate([ctx, x], axis=1)
    for l in range(depth):
        mod = mods[l]
        hq, cum, kk, hv, sg, sq, sk, sv, nq, nk, nv = _proj(xall, norm_mix_w[l].reshape(1, d), mod, w_in_b[l], cs, gp[l], tri)
        o = _hgrn_scan(hq, cum, kk, hv, bd)
        so = _swa(sq, sk, sv, sink_rows[l], n_ctx)
        no = _nat(nq, nk, nv, _nat_bias_tables(nat_rpb[l]), n_ctx)
        xall = _outproj(o, sg, hgrn_norm_w[l].reshape(1, HG_WIDTH), bd, so, no, wa[l], wb[l], wn[l], mod, xall)
        h2, idx, g = _route(xall, norm_ffn_w[l].reshape(1, d), mod, wq_b[l], sub_b[l])
        idx, h2, g = idx.reshape(nb * ta, PEER_SLOTS), h2.reshape(nb * ta, d), g.reshape(nb * ta, PEER_SLOTS)
        staged = _sc_gather(tab_sc[l], idx[n_tc * PEER_TB:].reshape(-1))
        xf = _experts(idx, h2, g, xall.reshape(nb * ta, d), mod, tab[l], ta, n_ctx, n_tc)
        xf = _experts_dense(staged.reshape(-1, LANE), h2, g, xf, mod, ta, n_ctx, n_tc)
        xall = xf.reshape(nb, ta, d)
    return _final_norm(xall, final_norm_w, n_ctx)
```

```python
import functools

import numpy as np
import jax
import jax.numpy as jnp
from jax import lax
from jax.experimental import pallas as pl
from jax.experimental.pallas import tpu as pltpu
from jax.experimental.pallas import tpu_sc as plsc

f32 = jnp.float32
bf16 = jnp.bfloat16
i32 = jnp.int32

LANE = 128
VMEM_LIMIT_BYTES = 56 * 1024 * 1024

GRID_W = 64
HEAD_DIM = 64
EPS = 1e-6
HG_HEADS = 4
HG_WIDTH = 256
HG_SUB = 16
SWA_HEADS = 6
SWA_KV_HEADS = 2
SWA_WINDOW = 128
SWA_BLOCK = 128
ROPE_BASE = 10000.0
NAT_HEADS = 6
NAT_ROWS = 8
NAT_COLS = 16
SWA_Q = SWA_HEADS * HEAD_DIM
SWA_KV = SWA_KV_HEADS * HEAD_DIM
NAT_W = NAT_HEADS * HEAD_DIM
PROJ_WIDTH = 5 * HG_WIDTH + SWA_Q + 2 * SWA_KV + 3 * NAT_W
PEER_HEADS = 8
PEER_NKEYS = 128
PEER_DK = 256
PEER_TOPK = 16
PEER_SLOTS = PEER_HEADS * PEER_TOPK

TM = 256
PEER_TB = 128
PEER_NBUF = 16
PEER_AHEAD = PEER_NBUF - 1
PEER_SC_SHARE = 0.508
PEER_DENSE_TB = 32
SC_CHUNK = 32
NEG = -1e30

SWA_HEAD_ORDER = (0, 3, 1, 4, 2, 5)

_NT = (((1,), (1,)), ((), ()))
_TN = (((0,), (0,)), ((), ()))


def _silu(x):
    return x * (1.0 / (1.0 + jnp.exp(-x)))


def _split2(x):
    hi = x.astype(bf16)
    lo = (x - hi.astype(f32)).astype(bf16)
    return hi, lo


def _mods_kernel(sc_ref, w_ref, b_ref, o_ref):
    s = _silu(sc_ref[...])
    o_ref[0] = jnp.dot(s.astype(bf16), w_ref[0].astype(bf16), preferred_element_type=f32) + b_ref[0]


def _mods(c, c_ctx, w_mod, b_mod):
    depth, d, d6 = w_mod.shape
    nb = c.shape[0]
    rows = jnp.zeros((8, d), f32).at[:nb].set(c).at[nb].set(c_ctx)
    tn = d6 // 4
    out = pl.pallas_call(
        _mods_kernel,
        grid=(depth, 4),
        in_specs=[pl.BlockSpec((8, d), lambda l, j: (0, 0)),
                  pl.BlockSpec((1, d, tn), lambda l, j: (l, 0, j)),
                  pl.BlockSpec((1, 1, tn), lambda l, j: (l, 0, j))],
        out_specs=pl.BlockSpec((1, 8, tn), lambda l, j: (l, 0, j)),
        out_shape=jax.ShapeDtypeStruct((depth, 8, d6), f32),
        compiler_params=pltpu.CompilerParams(dimension_semantics=("arbitrary", "arbitrary"),
                                             vmem_limit_bytes=VMEM_LIMIT_BYTES),
        name="adaln_mods",
    )(rows, w_mod, b_mod.reshape(depth, 1, d6))
    lat = out[:, :nb].reshape(depth, nb, 6, d)
    ctx = jnp.broadcast_to(out[:, nb].reshape(depth, 1, 6, d), (depth, nb, 6, d))
    return jnp.stack([ctx, lat], axis=2)


def _norm_mod(x, nw, shift, scale):
    y = x * lax.rsqrt(jnp.mean(x * x, axis=-1, keepdims=True) + EPS) * nw
    return y * (1.0 + scale) + shift


def _rope(t, cos, sins):
    lane = lax.broadcasted_iota(i32, t.shape, 1)
    sw = jnp.where((lane % 32) < 16, pltpu.roll(t, LANE - 16, 1), pltpu.roll(t, 16, 1))
    return t * cos + sw * sins


def _hgrn_gates(z, log_lb, log1m_lb, one_m_lb):
    log_sig = jnp.minimum(z, 0.0) - jnp.log1p(jnp.exp(-jnp.abs(z)))
    c = log1m_lb + log_sig
    m = jnp.maximum(log_lb, c)
    log_f = m + jnp.log1p(jnp.exp(-jnp.abs(log_lb - c)))
    k = one_m_lb * (1.0 / (1.0 + jnp.exp(z)))
    return log_f, k


def _cumsum16(tri, log_f):
    h1 = log_f.astype(bf16)
    r1 = log_f - h1.astype(f32)
    h2 = r1.astype(bf16)
    h3 = (r1 - h2.astype(f32)).astype(bf16)
    return (jnp.dot(tri, h1, preferred_element_type=f32) + jnp.dot(tri, h2, preferred_element_type=f32)
            + jnp.dot(tri, h3, preferred_element_type=f32))


def _proj_kernel(x_ref, nw_ref, mod_ref, w_ref, cs_ref, gp_ref, tri_ref,
                 hq_ref, cum_ref, kk_ref, hv_ref, sg_ref, sq_ref, sk_ref, sv_ref, nq_ref, nk_ref, nv_ref):
    h = _norm_mod(x_ref[0], nw_ref[...], mod_ref[0, 0, 0:1, :], mod_ref[0, 0, 1:2, :])
    p = jnp.dot(h.astype(bf16), w_ref[...], preferred_element_type=f32)
    w = HG_WIDTH
    hq_ref[0] = _silu(p[:, 0:w])
    for d in range(2):
        log_f, k = _hgrn_gates(p[:, (1 + d) * w:(2 + d) * w], gp_ref[3 * d:3 * d + 1, :],
                               gp_ref[3 * d + 1:3 * d + 2, :], gp_ref[3 * d + 2:3 * d + 3, :])
        cum_ref[d, 0] = _cumsum16(tri_ref[d], log_f)
        kk_ref[d, 0] = k
    hv_ref[0] = p[:, 3 * w:4 * w]
    sg_ref[0] = _silu(p[:, 4 * w:5 * w])
    cos, sins = cs_ref[0], cs_ref[1]
    o = 5 * w
    scale = HEAD_DIM ** -0.5
    for j in range(SWA_Q // LANE):
        sq_ref[0, :, j * LANE:(j + 1) * LANE] = (_rope(p[:, o + j * LANE:o + (j + 1) * LANE], cos, sins) * scale).astype(bf16)
    o += SWA_Q
    sk_ref[0] = _rope(p[:, o:o + SWA_KV], cos, sins).astype(bf16)
    o += SWA_KV
    sv_ref[0] = p[:, o:o + SWA_KV].astype(bf16)
    o += SWA_KV
    nq_ref[0] = (p[:, o:o + NAT_W] * scale).astype(bf16)
    o += NAT_W
    nk_ref[0] = p[:, o:o + NAT_W].astype(bf16)
    o += NAT_W
    nv_ref[0] = p[:, o:o + NAT_W].astype(bf16)


def _proj(xall, nw, mod, w_in, cs, gp, tri):
    nb, ta, d = xall.shape
    nt = ta // TM
    tok = lambda width, dt: jax.ShapeDtypeStruct((nb, ta, width), dt)
    tok2 = jax.ShapeDtypeStruct((2, nb, ta, HG_WIDTH), f32)
    bs = lambda width: pl.BlockSpec((1, TM, width), lambda b, i: (b, i, 0))
    bs2 = pl.BlockSpec((2, 1, TM, HG_WIDTH), lambda b, i: (0, b, i, 0))
    return pl.pallas_call(
        _proj_kernel,
        grid=(nb, nt),
        in_specs=[pl.BlockSpec((1, TM, d), lambda b, i: (b, i, 0)),
                  pl.BlockSpec((1, d), lambda b, i: (0, 0)),
                  pl.BlockSpec((1, 1, 6, d), lambda b, i: (b, jnp.minimum(i, 1), 0, 0)),
                  pl.BlockSpec((d, PROJ_WIDTH), lambda b, i: (0, 0)),
                  pl.BlockSpec((2, TM, LANE), lambda b, i: (0, i, 0)),
                  pl.BlockSpec((8, HG_WIDTH), lambda b, i: (0, 0)),
                  pl.BlockSpec((2, TM, TM), lambda b, i: (0, 0, 0))],
        out_specs=[bs(HG_WIDTH), bs2, bs2, bs(HG_WIDTH), bs(HG_WIDTH), bs(SWA_Q), bs(SWA_KV), bs(SWA_KV),
                   bs(NAT_W), bs(NAT_W), bs(NAT_W)],
        out_shape=[tok(HG_WIDTH, f32), tok2, tok2, tok(HG_WIDTH, f32), tok(HG_WIDTH, f32), tok(SWA_Q, bf16),
                   tok(SWA_KV, bf16), tok(SWA_KV, bf16), tok(NAT_W, bf16), tok(NAT_W, bf16), tok(NAT_W, bf16)],
        compiler_params=pltpu.CompilerParams(dimension_semantics=("arbitrary", "arbitrary"),
                                             vmem_limit_bytes=VMEM_LIMIT_BYTES),
        name="norm_in_proj",
    )(xall, nw, mod, w_in, cs, gp, tri)


def _hgrn_kernel(q_ref, cum_ref, k_ref, v_ref, bd_ref, o_ref, st_ref):
    d = pl.program_id(1)
    fwd = d == 0

    @pl.when(pl.program_id(2) == 0)
    def _():
        st_ref[...] = jnp.zeros_like(st_ref)

    bd = bd_ref[...]
    bd_mask = bd > 0
    sgn = jnp.where(fwd, 1, -1)
    t_sgn = lax.broadcasted_iota(i32, (HG_SUB, 1), 0) * sgn
    nsteps = TM // HG_SUB

    def step(i, carry):
        sc = jnp.where(fwd, i, nsteps - 1 - i)
        r0 = pl.multiple_of(sc * HG_SUB, HG_SUB)
        q = q_ref[0, pl.ds(r0, HG_SUB), :]
        b = cum_ref[0, 0, pl.ds(r0, HG_SUB), :]
        k = k_ref[0, 0, pl.ds(r0, HG_SUB), :]
        v = v_ref[0, pl.ds(r0, HG_SUB), :]
        b_end = jnp.where(fwd, b[HG_SUB - 1:HG_SUB, :], b[0:1, :])
        st = st_ref[...]
        o = lax.dot_general((q * jnp.exp(b)).astype(bf16), st.astype(bf16), _NT, preferred_element_type=f32)
        rows = []
        for s in range(HG_SUB):
            valid = t_sgn >= s * sgn
            decay = jnp.exp(jnp.minimum(b - b[s:s + 1, :], 0.0))
            rows.append(jnp.where(valid, decay * q * k[s:s + 1, :], 0.0))
        pe = jnp.dot(jnp.concatenate(rows, axis=0).astype(bf16), bd, preferred_element_type=f32)
        for s in range(HG_SUB):
            o = o + pe[s * HG_SUB:(s + 1) * HG_SUB, :] * v[s:s + 1, :]
        o_ref[0, 0, pl.ds(r0, HG_SUB), :] = o
        kd = (k * jnp.exp(b_end - b)).astype(bf16)
        kv_t = lax.dot_general(v.astype(bf16), kd, _TN, preferred_element_type=f32)
        st_ref[...] = st * jnp.exp(b_end) + jnp.where(bd_mask, kv_t, 0.0)
        return carry

    lax.fori_loop(0, nsteps, step, 0, unroll=8)


def _hgrn_scan(hq, cum, kk, hv, bd):
    nb, ta, w = hq.shape
    nt = ta // TM

    def blk(d, j):
        return jnp.where(d == 0, j, jnp.where(j == 0, 0, nt - j))

    return pl.pallas_call(
        _hgrn_kernel,
        grid=(nb, 2, nt),
        in_specs=[pl.BlockSpec((1, TM, w), lambda b, d, j: (b, blk(d, j), 0)),
                  pl.BlockSpec((1, 1, TM, w), lambda b, d, j: (d, b, blk(d, j), 0)),
                  pl.BlockSpec((1, 1, TM, w), lambda b, d, j: (d, b, blk(d, j), 0)),
                  pl.BlockSpec((1, TM, w), lambda b, d, j: (b, blk(d, j), 0)),
                  pl.BlockSpec((w, w), lambda b, d, j: (0, 0))],
        out_specs=pl.BlockSpec((1, 1, TM, w), lambda b, d, j: (d, b, blk(d, j), 0)),
        out_shape=jax.ShapeDtypeStruct((2, nb, ta, w), f32),
        scratch_shapes=[pltpu.VMEM((w, w), f32)],
        compiler_params=pltpu.CompilerParams(dimension_semantics=("arbitrary", "arbitrary", "arbitrary")),
        name="hgrn_scan",
    )(hq, cum, kk, hv, bd)


def _pair_rows(q2):
    lane = lax.broadcasted_iota(i32, q2.shape, 1)
    zero = jnp.zeros_like(q2)
    return jnp.concatenate([jnp.where(lane < HEAD_DIM, q2, zero), jnp.where(lane >= HEAD_DIM, q2, zero)], axis=0)


def _pair_merge(o):
    m = o.shape[0] // 2
    lane = lax.broadcasted_iota(i32, (m, LANE), 1)
    return jnp.where(lane < HEAD_DIM, o[:m], o[m:])


def _joint_attention(s_loc, s_ctx, v_loc, v_ctx, sink=None):
    m = jnp.maximum(jnp.max(s_loc, axis=-1, keepdims=True), jnp.max(s_ctx, axis=-1, keepdims=True))
    if sink is not None:
        m = jnp.maximum(m, sink)
    p_loc = jnp.exp(s_loc - m)
    p_ctx = jnp.exp(s_ctx - m)
    den = jnp.sum(p_loc, axis=-1, keepdims=True) + jnp.sum(p_ctx, axis=-1, keepdims=True)
    if sink is not None:
        den = den + jnp.exp(sink - m)
    o = (jnp.dot(p_loc.astype(bf16), v_loc, preferred_element_type=f32)
         + jnp.dot(p_ctx.astype(bf16), v_ctx, preferred_element_type=f32))
    return o / den


def _swa_kernel(q_ref, kp_ref, kc_ref, kn_ref, vp_ref, vc_ref, vn_ref, kx_ref, vx_ref, sink_ref, o_ref, *, n_ctx_blocks, seq):
    g = pl.program_id(1)
    i = g - n_ctx_blocks
    k_loc = jnp.concatenate([kp_ref[0], kc_ref[0], kn_ref[0]], axis=0)
    v_loc = jnp.concatenate([vp_ref[0], vc_ref[0], vn_ref[0]], axis=0)
    row = lax.broadcasted_iota(i32, (SWA_BLOCK, 3 * SWA_BLOCK), 0)
    col = lax.broadcasted_iota(i32, (SWA_BLOCK, 3 * SWA_BLOCK), 1)
    rel = col - SWA_BLOCK - row
    kpos = (i - 1) * SWA_BLOCK + col
    ok = (jnp.abs(rel) <= SWA_WINDOW) & (kpos >= 0) & (kpos < seq) & (i >= 0)
    pen = jnp.where(ok, 0.0, NEG)
    pen = jnp.concatenate([pen, pen], axis=0)
    for j in range(SWA_Q // LANE):
        q = _pair_rows(q_ref[0, :, j * LANE:(j + 1) * LANE])
        s_loc = lax.dot_general(q, k_loc, _NT, preferred_element_type=f32) + pen
        s_ctx = lax.dot_general(q, kx_ref[0], _NT, preferred_element_type=f32)
        sink = jnp.concatenate([jnp.broadcast_to(sink_ref[2 * j:2 * j + 1, 0:1], (SWA_BLOCK, 1)),
                                jnp.broadcast_to(sink_ref[2 * j + 1:2 * j + 2, 0:1], (SWA_BLOCK, 1))], axis=0)
        o = _joint_attention(s_loc, s_ctx, v_loc, vx_ref[0], sink)
        o_ref[0, :, j * LANE:(j + 1) * LANE] = _pair_merge(o).astype(bf16)


def _swa(sq, sk, sv, sink_rows, n_ctx):
    nb, ta, _ = sq.shape
    ncb = n_ctx // SWA_BLOCK
    nblk = ta // SWA_BLOCK
    nlat = nblk - ncb
    prev = lambda b, g: (b, jnp.clip(g - ncb - 1, 0, nlat - 1) + ncb, 0)
    cur = lambda b, g: (b, g, 0)
    nxt = lambda b, g: (b, jnp.clip(g - ncb + 1, 0, nlat - 1) + ncb, 0)
    kvs = lambda f: pl.BlockSpec((1, SWA_BLOCK, SWA_KV), f)
    ctx = pl.BlockSpec((1, n_ctx, SWA_KV), lambda b, g: (b, 0, 0))
    return pl.pallas_call(
        functools.partial(_swa_kernel, n_ctx_blocks=ncb, seq=ta - n_ctx),
        grid=(nb, nblk),
        in_specs=[pl.BlockSpec((1, SWA_BLOCK, SWA_Q), cur), kvs(prev), kvs(cur), kvs(nxt), kvs(prev), kvs(cur), kvs(nxt),
                  ctx, ctx, pl.BlockSpec((8, LANE), lambda b, g: (0, 0))],
        out_specs=pl.BlockSpec((1, SWA_BLOCK, SWA_Q), cur),
        out_shape=jax.ShapeDtypeStruct((nb, ta, SWA_Q), bf16),
        compiler_params=pltpu.CompilerParams(dimension_semantics=("arbitrary", "arbitrary")),
        name="swa_attention",
    )(sq, sk, sk, sk, sv, sv, sv, sk, sv, sink_rows)


def _nat_kernel(q_ref, k_ref, v_ref, bias_ref, o_ref, *, n_ctx, n_rows):
    g = pl.program_id(1)
    is_ctx = g == 0
    ctx_pen = jnp.where(is_ctx, NEG, 0.0)
    rows_per_tile = TM // GRID_W
    nwin = NAT_ROWS * GRID_W
    for rr in range(rows_per_tile):
        r = jnp.maximum((g - 1) * rows_per_tile + rr, 0)
        rs = jnp.clip(r - NAT_ROWS // 2, 0, n_rows - NAT_ROWS)
        dr0 = rs - r + NAT_ROWS - 1
        start = pl.multiple_of(n_ctx + rs * GRID_W, GRID_W)
        for p in range(NAT_W // LANE):
            ls = slice(p * LANE, (p + 1) * LANE)
            q = _pair_rows(q_ref[0, rr * GRID_W:(rr + 1) * GRID_W, ls])
            s_loc = (lax.dot_general(q, k_ref[0, pl.ds(start, nwin), ls], _NT, preferred_element_type=f32)
                     + bias_ref[p, dr0] + ctx_pen)
            s_ctx = lax.dot_general(q, k_ref[0, 0:n_ctx, ls], _NT, preferred_element_type=f32)
            o = _joint_attention(s_loc, s_ctx, v_ref[0, pl.ds(start, nwin), ls], v_ref[0, 0:n_ctx, ls])
            o_ref[0, rr * GRID_W:(rr + 1) * GRID_W, ls] = _pair_merge(o).astype(bf16)


def _nat(nq, nk, nv, bias, n_ctx):
    nb, ta, w = nq.shape
    nt = ta // TM
    whole = pl.BlockSpec((1, ta, w), lambda b, g: (b, 0, 0), pipeline_mode=pl.Buffered(1))
    return pl.pallas_call(
        functools.partial(_nat_kernel, n_ctx=n_ctx, n_rows=(ta - n_ctx) // GRID_W),
        grid=(nb, nt),
        in_specs=[pl.BlockSpec((1, TM, w), lambda b, g: (b, g, 0)), whole, whole,
                  pl.BlockSpec(bias.shape, lambda b, g: (0, 0, 0, 0), pipeline_mode=pl.Buffered(1))],
        out_specs=pl.BlockSpec((1, TM, w), lambda b, g: (b, g, 0)),
        out_shape=jax.ShapeDtypeStruct((nb, ta, w), bf16),
        compiler_params=pltpu.CompilerParams(dimension_semantics=("arbitrary", "arbitrary"),
                                             vmem_limit_bytes=VMEM_LIMIT_BYTES),
        name="nat_attention",
    )(nq, nk, nv, bias)


def _nat_bias_tables(rpb):
    c = np.arange(GRID_W)[:, None]
    kc = np.arange(GRID_W)[None, :]
    ws = np.clip(c - NAT_COLS // 2, 0, GRID_W - NAT_COLS)
    valid = (kc >= ws) & (kc < ws + NAT_COLS)
    col_idx = np.clip(kc - c + NAT_COLS - 1, 0, 2 * NAT_COLS - 2)
    dr = np.arange(NAT_ROWS)[:, None] + np.arange(NAT_ROWS)[None, :]
    t = rpb.astype(f32)[:, dr][:, :, :, col_idx]
    t = jnp.where(jnp.asarray(valid)[None, None, None], t, NEG)
    t = t.transpose(0, 1, 3, 2, 4).reshape(NAT_HEADS // 2, 2, NAT_ROWS, GRID_W, NAT_ROWS * GRID_W)
    return t.transpose(0, 2, 1, 3, 4).reshape(NAT_HEADS // 2, NAT_ROWS, 2 * GRID_W, NAT_ROWS * GRID_W)


def _outproj_kernel(o_ref, sg_ref, hnw_ref, bd_ref, so_ref, no_ref, wa_ref, wb_ref, wn_ref, mod_ref, x_ref, y_ref):
    oo = o_ref[0, 0] + o_ref[1, 0]
    hi, lo = _split2(oo * oo)
    bd = bd_ref[...]
    msq = (jnp.dot(hi, bd, preferred_element_type=f32) + jnp.dot(lo, bd, preferred_element_type=f32)) * (1.0 / HEAD_DIM)
    a = oo * lax.rsqrt(msq + EPS) * hnw_ref[...] * sg_ref[0]
    y = (jnp.dot(a.astype(bf16), wa_ref[...], preferred_element_type=f32)
         + jnp.dot(so_ref[0], wb_ref[...], preferred_element_type=f32)
         + jnp.dot(no_ref[0], wn_ref[...], preferred_element_type=f32))
    y_ref[0] = x_ref[0] + mod_ref[0, 0, 2:3, :] * y


def _outproj(o, sg, hnw, bd, so, no, wa, wb, wn, mod, xall):
    nb, ta, d = xall.shape
    nt = ta // TM
    bs = lambda width: pl.BlockSpec((1, TM, width), lambda b, i: (b, i, 0))
    full = lambda a: pl.BlockSpec(a.shape, lambda b, i: (0,) * a.ndim)
    return pl.pallas_call(
        _outproj_kernel,
        grid=(nb, nt),
        in_specs=[pl.BlockSpec((2, 1, TM, HG_WIDTH), lambda b, i: (0, b, i, 0)), bs(HG_WIDTH), full(hnw), full(bd),
                  bs(SWA_Q), bs(NAT_W), full(wa), full(wb), full(wn),
                  pl.BlockSpec((1, 1, 6, d), lambda b, i: (b, jnp.minimum(i, 1), 0, 0)), bs(d)],
        out_specs=bs(d),
        out_shape=jax.ShapeDtypeStruct(xall.shape, f32),
        compiler_params=pltpu.CompilerParams(dimension_semantics=("arbitrary", "arbitrary"),
                                             vmem_limit_bytes=VMEM_LIMIT_BYTES),
        name="out_proj",
    )(o, sg, hnw, bd, so, no, wa, wb, wn, mod, xall)


def _top16_rows(vals, payload=None):
    n = vals.shape[0]
    iota = lax.broadcasted_iota(i32, vals.shape, 0)
    best, picked = [], []
    for _ in range(PEER_TOPK):
        m = jnp.max(vals, axis=0, keepdims=True)
        idx = jnp.min(jnp.where(vals == m, iota, n), axis=0, keepdims=True)
        hit = iota == idx
        best.append(m)
        picked.append(idx if payload is None else jnp.max(jnp.where(hit, payload, -1), axis=0, keepdims=True))
        vals = jnp.where(hit, -jnp.inf, vals)
    return jnp.concatenate(best, axis=0), jnp.concatenate(picked, axis=0)


def _product_candidates(first, second, combine):
    half = PEER_TOPK // 2
    rows = [combine(first[0:1, :], second)]
    rows += [combine(first[a:a + 1, :], second[0:half, :]) for a in range(1, half)]
    rows.append(combine(first[half:, :], second[0:1, :]))
    return jnp.concatenate(rows, axis=0)


def _route_kernel(x_ref, nw_ref, mod_ref, wq_ref, sub_ref, h_ref, idx_ref, g_ref):
    h = _norm_mod(x_ref[0], nw_ref[...], mod_ref[0, 0, 3:4, :], mod_ref[0, 0, 4:5, :])
    h_ref[0] = h
    q = jnp.dot(h.astype(bf16), wq_ref[...], preferred_element_type=f32).astype(bf16)
    half = PEER_DK // 2
    ids, gates = [], []
    for hh in range(PEER_HEADS):
        top_s, top_i = [], []
        for p in range(2):
            c0 = (hh * 2 + p) * half
            s = lax.dot_general(sub_ref[hh, p], q[:, c0:c0 + half], _NT, preferred_element_type=f32)
            ts, ti = _top16_rows(s)
            top_s.append(ts)
            top_i.append(ti)
        cand_s = _product_candidates(top_s[0], top_s[1], lambda a, b: a + b)
        cand_i = _product_candidates(top_i[0], top_i[1], lambda a, b: a * PEER_NKEYS + b)
        best, eid = _top16_rows(cand_s, cand_i)
        ids.append(eid)
        e = jnp.exp(best - best[0:1, :])
        gates.append(e / jnp.sum(e, axis=0, keepdims=True))
    idx_ref[0] = jnp.concatenate(ids, axis=0).T
    g_ref[0] = jnp.concatenate(gates, axis=0).T


def _route(xall, nw, mod, wq, sub):
    nb, ta, d = xall.shape
    nt = ta // TM
    full = lambda a: pl.BlockSpec(a.shape, lambda b, i: (0,) * a.ndim)
    return pl.pallas_call(
        _route_kernel,
        grid=(nb, nt),
        in_specs=[pl.BlockSpec((1, TM, d), lambda b, i: (b, i, 0)), full(nw),
                  pl.BlockSpec((1, 1, 6, d), lambda b, i: (b, jnp.minimum(i, 1), 0, 0)), full(wq), full(sub)],
        out_specs=[pl.BlockSpec((1, TM, d), lambda b, i: (b, i, 0)),
                   pl.BlockSpec((1, TM, PEER_SLOTS), lambda b, i: (b, i, 0)),
                   pl.BlockSpec((1, TM, PEER_SLOTS), lambda b, i: (b, i, 0))],
        out_shape=[jax.ShapeDtypeStruct(xall.shape, f32), jax.ShapeDtypeStruct((nb, ta, PEER_SLOTS), i32),
                   jax.ShapeDtypeStruct((nb, ta, PEER_SLOTS), f32)],
        compiler_params=pltpu.CompilerParams(dimension_semantics=("arbitrary", "arbitrary"),
                                             vmem_limit_bytes=VMEM_LIMIT_BYTES),
        name="peer_route",
    )(xall, nw, mod, wq, sub)


def _expert_kernel(idx_ref, idx_next_ref, h_ref, g_ref, x_ref, mod_ref, tab_hbm, o_ref, buf, sem):
    d = h_ref.shape[-1]
    nct = d // LANE
    groups = PEER_SLOTS // 8

    per_piece = PEER_SLOTS // nct

    def issue(ids_ref, t, slot, piece):
        for k in range(piece * per_piece, (piece + 1) * per_piece):
            pltpu.make_async_copy(tab_hbm.at[ids_ref[t, k]], buf.at[slot, k // 8, :, pl.ds(k % 8, 1), :],
                                  sem.at[slot]).start(priority=k % 2)

    def wait(slot):
        pltpu.make_async_copy(buf.at[slot], buf.at[slot], sem.at[slot]).wait()

    def cols(slot, c, high):
        words = jnp.concatenate([jnp.concatenate([buf[slot, kg, c], buf[slot, kg, c + 1]], axis=1)
                                 for kg in range(groups)], axis=0)
        bits = (words & jnp.uint32(0xFFFF0000)) if high else (words << 16)
        return lax.bitcast_convert_type(bits, f32).astype(bf16)

    not_last = pl.program_id(0) < pl.num_programs(0) - 1

    def step(t, w8, do_issue, do_u, do_v):
        nslot = lax.rem(t + 1, PEER_NBUF)
        slot = lax.rem(t, PEER_NBUF)
        islot = lax.rem(t + PEER_AHEAD, PEER_NBUF)
        if do_issue == "next":
            @pl.when(not_last)
            def _():
                for j in range(nct):
                    issue(idx_next_ref, t + PEER_AHEAD - PEER_TB, islot, j)
        if do_u:
            wait(nslot)
            h8 = jnp.broadcast_to(h_ref[pl.ds(t + 1, 1), :], (8, d)).astype(bf16)
            gate = g_ref[pl.ds(t + 1, 1), :]
            a = jnp.zeros((8, PEER_SLOTS), f32)
        ys = []
        for j in range(nct):
            if 2 * j < nct:
                c = 2 * j
                if do_u:
                    a = a + lax.dot_general(h8[:, c * LANE:(c + 2) * LANE], cols(nslot, c, True), _NT, preferred_element_type=f32)
                if do_v:
                    ys.append(jnp.dot(w8, cols(slot, c, False), preferred_element_type=f32)[0:1, :])
            if do_issue is True:
                issue(idx_ref, t + PEER_AHEAD, islot, j)
        if do_v:
            o_ref[pl.ds(t, 1), :] = jnp.concatenate(ys, axis=1)
        if do_u:
            a1 = a[0:1, :]
            w = gate * (0.5 * a1 * (1.0 + lax.erf(a1 * (2.0 ** -0.5))))
            w8 = jnp.broadcast_to(w, (8, PEER_SLOTS)).astype(bf16)
        return w8

    @pl.when(pl.program_id(0) == 0)
    def _():
        def prologue(t, carry):
            for j in range(nct):
                issue(idx_ref, t, t, j)
            return carry

        lax.fori_loop(0, PEER_AHEAD, prologue, 0)

    w8 = step(-1, jnp.zeros((8, PEER_SLOTS), bf16), False, True, False)
    w8 = lax.fori_loop(0, PEER_TB - PEER_AHEAD, lambda t, w: step(t, w, True, True, True), w8)
    w8 = lax.fori_loop(PEER_TB - PEER_AHEAD, PEER_TB - 1, lambda t, w: step(t, w, "next", True, True), w8)
    step(PEER_TB - 1, w8, "next", False, True)
    o_ref[...] = x_ref[...] + mod_ref[0, 0, 5:6, :] * o_ref[...]


def _mod_index(tokens_per_step, ta, n_ctx, first_step=0):
    spb = ta // tokens_per_step
    cs = n_ctx // tokens_per_step
    return lambda i: ((i + first_step) // spb, jnp.where((i + first_step) % spb < cs, 0, 1), 0, 0)


def _experts(idx, h, g, xflat, mod, tab, ta, n_ctx, nblocks):
    n, d = xflat.shape
    tokb = lambda width: pl.BlockSpec((PEER_TB, width), lambda i: (i, 0))
    return pl.pallas_call(
        _expert_kernel,
        grid=(nblocks,),
        in_specs=[pl.BlockSpec((PEER_TB, PEER_SLOTS), lambda i: (i, 0), memory_space=pltpu.SMEM),
                  pl.BlockSpec((PEER_TB, PEER_SLOTS), lambda i: (jnp.minimum(i + 1, nblocks - 1), 0), memory_space=pltpu.SMEM),
                  tokb(d), tokb(PEER_SLOTS), tokb(d),
                  pl.BlockSpec((1, 1, 6, d), _mod_index(PEER_TB, ta, n_ctx)),
                  pl.BlockSpec(memory_space=pl.ANY)],
        out_specs=tokb(d),
        out_shape=jax.ShapeDtypeStruct((n, d), f32),
        input_output_aliases={4: 0},
        scratch_shapes=[pltpu.VMEM((PEER_NBUF, PEER_SLOTS // 8, d // LANE, 8, LANE), jnp.uint32),
                        pltpu.SemaphoreType.DMA((PEER_NBUF,))],
        compiler_params=pltpu.CompilerParams(dimension_semantics=("arbitrary",), vmem_limit_bytes=VMEM_LIMIT_BYTES),
        name="peer_experts",
    )(idx, idx, h, g, xflat, mod, tab)


def _sc_gather(table, idx):
    info = plsc.get_sparse_core_info()
    nc = info.num_cores
    nw = nc * info.num_subcores
    r = idx.shape[0]
    per_w = r // nw
    nch = per_w // SC_CHUNK
    assert per_w * nw == r and nch * SC_CHUNK == per_w and nch % 2 == 0
    rows_shape = (SC_CHUNK,) + table.shape[1:]
    mesh = plsc.VectorSubcoreMesh(core_axis_name="c", subcore_axis_name="s")

    @functools.partial(
        pl.kernel, mesh=mesh, out_type=jax.ShapeDtypeStruct((r,) + table.shape[1:], table.dtype),
        scratch_types=[pltpu.VMEM((2, SC_CHUNK), i32), pltpu.VMEM((2,) + rows_shape, table.dtype),
                       pltpu.SemaphoreType.DMA((2,)), pltpu.SemaphoreType.DMA((2,)), pltpu.SemaphoreType.DMA((2,))])
    def gather_kernel(table_hbm, idx_hbm, out_hbm, idx_v, rows_v, isem, gsem, wsem):
        base = (lax.axis_index("s") * nc + lax.axis_index("c")) * per_w

        def rows_of(j):
            return pl.ds(pl.multiple_of(base + j * SC_CHUNK, 8), SC_CHUNK)

        def fetch_idx(j, b):
            return pltpu.make_async_copy(idx_hbm.at[rows_of(j)], idx_v.at[b], isem.at[b])

        def gather(b):
            return pltpu.make_async_copy(table_hbm.at[idx_v.at[b]], rows_v.at[b], gsem.at[b])

        def write(j, b):
            return pltpu.make_async_copy(rows_v.at[b], out_hbm.at[rows_of(j)], wsem.at[b])

        fetch_idx(0, 0).start()
        fetch_idx(1, 1).start()
        fetch_idx(0, 0).wait()
        gather(0).start()

        @pl.loop(0, nch, step=2)
        def _(j0):
            for b in (0, 1):
                j = j0 + b

                @pl.when(j >= 1)
                def _():
                    write(j - 1, 1 - b).wait()

                @pl.when(j + 1 < nch)
                def _():
                    fetch_idx(j + 1, 1 - b).wait()
                    gather(1 - b).start()

                gather(b).wait()

                @pl.when(j + 2 < nch)
                def _():
                    fetch_idx(j + 2, b).start()

                write(j, b).start()

        write(nch - 1, 1).wait()

    return gather_kernel(table, idx)


def _expert_dense_kernel(st_ref, h_ref, g_ref, x_ref, mod_ref, o_ref):
    d = h_ref.shape[-1]
    nct = d // LANE
    groups = PEER_SLOTS // 8
    rows_per_token = PEER_SLOTS * nct

    def cols(t, c, high):
        base = t * rows_per_token
        tile = lambda kg, cc: st_ref[pl.ds(base + kg * 8 * nct + cc, 8, stride=nct), :]
        words = jnp.concatenate([jnp.concatenate([tile(kg, c), tile(kg, c + 1)], axis=1) for kg in range(groups)], axis=0)
        bits = (words & jnp.uint32(0xFFFF0000)) if high else (words << 16)
        return lax.bitcast_convert_type(bits, f32).astype(bf16)

    def u_stage(t):
        h8 = jnp.broadcast_to(h_ref[pl.ds(t, 1), :], (8, d)).astype(bf16)
        a = jnp.zeros((8, PEER_SLOTS), f32)
        for c in range(0, nct, 2):
            a = a + lax.dot_general(h8[:, c * LANE:(c + 2) * LANE], cols(t, c, True), _NT, preferred_element_type=f32)
        a1 = a[0:1, :]
        return g_ref[pl.ds(t, 1), :] * (0.5 * a1 * (1.0 + lax.erf(a1 * (2.0 ** -0.5))))

    lane = lax.broadcasted_iota(i32, (8, LANE), 1)
    sub = lax.broadcasted_iota(i32, (8, LANE), 0)

    def v_stage(t, w_row):
        base = t * rows_per_token
        wb = jnp.broadcast_to(w_row, (8, LANE))
        acc = [None] * nct
        for kg in range(groups):
            wcol = jnp.sum(jnp.where(lane == kg * 8 + sub, wb, 0.0), axis=-1, keepdims=True)
            for c in range(nct):
                words = st_ref[pl.ds(base + kg * 8 * nct + c, 8, stride=nct), :]
                term = lax.bitcast_convert_type(words << 16, f32) * wcol
                acc[c] = term if acc[c] is None else acc[c] + term
        o_ref[pl.ds(t, 1), :] = jnp.concatenate([jnp.sum(a, axis=0, keepdims=True) for a in acc], axis=1)

    def body(t, w_row):
        w_next = u_stage(t + 1)
        v_stage(t, w_row)
        return w_next

    w_row = lax.fori_loop(0, PEER_DENSE_TB - 1, body, u_stage(0), unroll=8)
    v_stage(PEER_DENSE_TB - 1, w_row)
    o_ref[...] = x_ref[...] + mod_ref[0, 0, 5:6, :] * o_ref[...]


def _experts_dense(staged, h, g, xflat, mod, ta, n_ctx, first_block):
    n, d = xflat.shape
    rows_per_token = PEER_SLOTS * (d // LANE)
    steps = staged.shape[0] // (PEER_DENSE_TB * rows_per_token)
    off = first_block * PEER_TB // PEER_DENSE_TB
    tokb = lambda width: pl.BlockSpec((PEER_DENSE_TB, width), lambda i: (i + off, 0))
    return pl.pallas_call(
        _expert_dense_kernel,
        grid=(steps,),
        in_specs=[pl.BlockSpec((PEER_DENSE_TB * rows_per_token, LANE), lambda i: (i, 0)),
                  tokb(d), tokb(PEER_SLOTS), tokb(d),
                  pl.BlockSpec((1, 1, 6, d), _mod_index(PEER_DENSE_TB, ta, n_ctx, off))],
        out_specs=tokb(d),
        out_shape=jax.ShapeDtypeStruct((n, d), f32),
        input_output_aliases={3: 0},
        compiler_params=pltpu.CompilerParams(dimension_semantics=("arbitrary",), vmem_limit_bytes=VMEM_LIMIT_BYTES),
        name="peer_experts_dense",
    )(staged, h, g, xflat, mod)


def _expert_table(u, v):
    half = lambda a: lax.bitcast_convert_type(a.astype(bf16), jnp.uint16).astype(jnp.uint32)
    return (half(u) << 16) | half(v)


def _final_norm_kernel(x_ref, w_ref, o_ref):
    x = x_ref[0]
    o_ref[0] = x * lax.rsqrt(jnp.mean(x * x, axis=-1, keepdims=True) + EPS) * w_ref[...]


def _final_norm(xall, w, n_ctx):
    nb, ta, d = xall.shape
    off = n_ctx // TM
    return pl.pallas_call(
        _final_norm_kernel,
        grid=(nb, (ta - n_ctx) // TM),
        in_specs=[pl.BlockSpec((1, TM, d), lambda b, i: (b, i + off, 0)), pl.BlockSpec((1, d), lambda b, i: (0, 0))],
        out_specs=pl.BlockSpec((1, TM, d), lambda b, i: (b, i, 0)),
        out_shape=jax.ShapeDtypeStruct((nb, ta - n_ctx, d), f32),
        compiler_params=pltpu.CompilerParams(dimension_semantics=("arbitrary", "arbitrary")),
        name="final_norm",
    )(xall, w.reshape(1, d))


def _rope_tables(n_ctx, seq):
    nf = HEAD_DIM // 4
    pos = np.arange(seq)
    inv = ROPE_BASE ** (-np.arange(nf, dtype=np.float32) / nf)
    ar = (pos // GRID_W).astype(np.float32)[:, None] * inv
    ac = (pos % GRID_W).astype(np.float32)[:, None] * inv
    cos = np.concatenate([np.cos(ar), np.cos(ar), np.cos(ac), np.cos(ac)], axis=-1)
    sin = np.concatenate([-np.sin(ar), np.sin(ar), -np.sin(ac), np.sin(ac)], axis=-1)
    cos = np.concatenate([np.ones((n_ctx, HEAD_DIM), np.float32), cos.astype(np.float32)], axis=0)
    sin = np.concatenate([np.zeros((n_ctx, HEAD_DIM), np.float32), sin.astype(np.float32)], axis=0)
    return jnp.asarray(np.stack([np.tile(cos, (1, 2)), np.tile(sin, (1, 2))]))


def _tri_tables():
    r = np.arange(TM)[:, None]
    c = np.arange(TM)[None, :]
    same = (r // HG_SUB) == (c // HG_SUB)
    return jnp.asarray(np.stack([same & (c <= r), same & (c >= r)]).astype(np.float32), dtype=bf16)


def _block_diag_ones():
    r = np.arange(HG_WIDTH)
    return jnp.asarray((r[:, None] // HEAD_DIM == r[None, :] // HEAD_DIM).astype(np.float32), dtype=bf16)


def _swa_perm():
    return np.concatenate([np.arange(h * HEAD_DIM, (h + 1) * HEAD_DIM) for h in SWA_HEAD_ORDER])


def kernel(x, c, ctx, c_ctx, w_mod, b_mod, norm_mix_w, norm_ffn_w, w_in, hgrn_lb_logits, hgrn_norm_w, swa_sink, nat_rpb,
           w_out, peer_wq, peer_subkeys, peer_u, peer_v, final_norm_w):
    nb, seq, d = x.shape
    n_ctx = ctx.shape[1]
    depth = w_in.shape[0]
    ta = n_ctx + seq
    assert n_ctx == TM and seq % TM == 0 and seq % (GRID_W * NAT_ROWS) == 0 and d % LANE == 0
    assert w_in.shape[-1] == PROJ_WIDTH and peer_wq.shape[-1] == PEER_HEADS * PEER_DK

    mods = _mods(c, c_ctx, w_mod, b_mod)
    cs = _rope_tables(n_ctx, seq)
    tri = _tri_tables()
    bd = _block_diag_ones()
    perm = _swa_perm()
    q0 = 5 * HG_WIDTH
    col_perm = np.concatenate([np.arange(q0), q0 + perm, np.arange(q0 + SWA_Q, PROJ_WIDTH)])
    w_in_b = w_in[:, :, col_perm].astype(bf16)
    w_out_b = w_out.astype(bf16)
    wa = w_out_b[:, :HG_WIDTH]
    wb = w_out_b[:, HG_WIDTH:HG_WIDTH + SWA_Q][:, perm]
    wn = w_out_b[:, HG_WIDTH + SWA_Q:]
    wq_b = peer_wq.astype(bf16)
    sub_b = peer_subkeys.astype(bf16)
    lb = jnp.cumsum(jax.nn.softmax(hgrn_lb_logits.astype(f32), axis=1), axis=1)
    lb = lb - lb[:, :1]
    gp = jnp.stack([jnp.log(lb[0]), jnp.log1p(-lb[0]), 1.0 - lb[0], jnp.log(lb[1]), jnp.log1p(-lb[1]), 1.0 - lb[1],
                    jnp.zeros_like(lb[0]), jnp.zeros_like(lb[0])], axis=1)
    sink_rows = jnp.zeros((depth, 8, LANE), f32).at[:, :SWA_HEADS].set(
        jnp.broadcast_to(swa_sink.astype(f32)[:, list(SWA_HEAD_ORDER), None], (depth, SWA_HEADS, LANE)))

    words = _expert_table(peer_u, peer_v)
    n_exp = words.shape[1]
    tab = words.reshape(depth, n_exp, d // LANE, 1, LANE)
    tab_sc = words.reshape(depth, n_exp, d // LANE, LANE)
    n_blocks = nb * ta // PEER_TB
    n_tc = n_blocks - int(n_blocks * PEER_SC_SHARE + 0.5)
    assert ta % PEER_TB == 0 and PEER_TB % PEER_NBUF == 0 and PEER_TB % PEER_DENSE_TB == 0 and 0 < n_tc < n_blocks

    xall = jnp.concatenate([ctx, x], axis=1)
    for l in range(depth):
        mod = mods[l]
        hq, cum, kk, hv, sg, sq, sk, sv, nq, nk, nv = _proj(xall, norm_mix_w[l].reshape(1, d), mod, w_in_b[l], cs, gp[l], tri)
        o = _hgrn_scan(hq, cum, kk, hv, bd)
        so = _swa(sq, sk, sv, sink_rows[l], n_ctx)
        no = _nat(nq, nk, nv, _nat_bias_tables(nat_rpb[l]), n_ctx)
        xall = _outproj(o, sg, hgrn_norm_w[l].reshape(1, HG_WIDTH), bd, so, no, wa[l], wb[l], wn[l], mod, xall)
        h2, idx, g = _route(xall, norm_ffn_w[l].reshape(1, d), mod, wq_b[l], sub_b[l])
        idx, h2, g = idx.reshape(nb * ta, PEER_SLOTS), h2.reshape(nb * ta, d), g.reshape(nb * ta, PEER_SLOTS)
        staged = _sc_gather(tab_sc[l], idx[n_tc * PEER_TB:].reshape(-1))
        xf = _experts(idx, h2, g, xall.reshape(nb * ta, d), mod, tab[l], ta, n_ctx, n_tc)
        xf = _experts_dense(staged.reshape(-1, LANE), h2, g, xf, mod, ta, n_ctx, n_tc)
        xall = xf.reshape(nb, ta, d)
    return _final_norm(xall, final_norm_w, n_ctx)
```

```python
import functools

import numpy as np
import jax
import jax.numpy as jnp
from jax import lax
from jax.experimental import pallas as pl
from jax.experimental.pallas import tpu as pltpu
from jax.experimental.pallas import tpu_sc as plsc

f32 = jnp.float32
bf16 = jnp.bfloat16
i32 = jnp.int32

LANE = 128
VMEM_LIMIT_BYTES = 56 * 1024 * 1024

GRID_W = 64
HEAD_DIM = 64
EPS = 1e-6
HG_HEADS = 4
HG_WIDTH = 256
HG_SUB = 16
SWA_HEADS = 6
SWA_KV_HEADS = 2
SWA_WINDOW = 128
SWA_BLOCK = 128
ROPE_BASE = 10000.0
NAT_HEADS = 6
NAT_ROWS = 8
NAT_COLS = 16
SWA_Q = SWA_HEADS * HEAD_DIM
SWA_KV = SWA_KV_HEADS * HEAD_DIM
NAT_W = NAT_HEADS * HEAD_DIM
PROJ_WIDTH = 5 * HG_WIDTH + SWA_Q + 2 * SWA_KV + 3 * NAT_W
PEER_HEADS = 8
PEER_NKEYS = 128
PEER_DK = 256
PEER_TOPK = 16
PEER_SLOTS = PEER_HEADS * PEER_TOPK

TM = 256
PEER_TB = 128
PEER_NBUF = 16
PEER_AHEAD = PEER_NBUF - 1
PEER_SC_SHARE = 0.508
PEER_DENSE_TB = 32
SC_CHUNK = 32
NEG = -1e30

SWA_HEAD_ORDER = (0, 3, 1, 4, 2, 5)

_NT = (((1,), (1,)), ((), ()))
_TN = (((0,), (0,)), ((), ()))


def _silu(x):
    return x * (1.0 / (1.0 + jnp.exp(-x)))


def _split2(x):
    hi = x.astype(bf16)
    lo = (x - hi.astype(f32)).astype(bf16)
    return hi, lo


def _mods_kernel(sc_ref, w_ref, b_ref, o_ref):
    s = _silu(sc_ref[...])
    o_ref[0] = jnp.dot(s.astype(bf16), w_ref[0].astype(bf16), preferred_element_type=f32) + b_ref[0]


def _mods(c, c_ctx, w_mod, b_mod):
    depth, d, d6 = w_mod.shape
    nb = c.shape[0]
    rows = jnp.zeros((8, d), f32).at[:nb].set(c).at[nb].set(c_ctx)
    tn = d6 // 4
    out = pl.pallas_call(
        _mods_kernel,
        grid=(depth, 4),
        in_specs=[pl.BlockSpec((8, d), lambda l, j: (0, 0)),
                  pl.BlockSpec((1, d, tn), lambda l, j: (l, 0, j)),
                  pl.BlockSpec((1, 1, tn), lambda l, j: (l, 0, j))],
        out_specs=pl.BlockSpec((1, 8, tn), lambda l, j: (l, 0, j)),
        out_shape=jax.ShapeDtypeStruct((depth, 8, d6), f32),
        compiler_params=pltpu.CompilerParams(dimension_semantics=("arbitrary", "arbitrary"),
                                             vmem_limit_bytes=VMEM_LIMIT_BYTES),
        name="adaln_mods",
    )(rows, w_mod, b_mod.reshape(depth, 1, d6))
    lat = out[:, :nb].reshape(depth, nb, 6, d)
    ctx = jnp.broadcast_to(out[:, nb].reshape(depth, 1, 6, d), (depth, nb, 6, d))
    return jnp.stack([ctx, lat], axis=2)


def _norm_mod(x, nw, shift, scale):
    y = x * lax.rsqrt(jnp.mean(x * x, axis=-1, keepdims=True) + EPS) * nw
    return y * (1.0 + scale) + shift


def _rope(t, cos, sins):
    lane = lax.broadcasted_iota(i32, t.shape, 1)
    sw = jnp.where((lane % 32) < 16, pltpu.roll(t, LANE - 16, 1), pltpu.roll(t, 16, 1))
    return t * cos + sw * sins


def _hgrn_gates(z, log_lb, log1m_lb, one_m_lb):
    log_sig = jnp.minimum(z, 0.0) - jnp.log1p(jnp.exp(-jnp.abs(z)))
    c = log1m_lb + log_sig
    m = jnp.maximum(log_lb, c)
    log_f = m + jnp.log1p(jnp.exp(-jnp.abs(log_lb - c)))
    k = one_m_lb * (1.0 / (1.0 + jnp.exp(z)))
    return log_f, k


def _cumsum16(tri, log_f):
    h1 = log_f.astype(bf16)
    r1 = log_f - h1.astype(f32)
    h2 = r1.astype(bf16)
    h3 = (r1 - h2.astype(f32)).astype(bf16)
    return (jnp.dot(tri, h1, preferred_element_type=f32) + jnp.dot(tri, h2, preferred_element_type=f32)
            + jnp.dot(tri, h3, preferred_element_type=f32))


def _proj_kernel(x_ref, nw_ref, mod_ref, w_ref, cs_ref, gp_ref, tri_ref,
                 hq_ref, cum_ref, kk_ref, hv_ref, sg_ref, sq_ref, sk_ref, sv_ref, nq_ref, nk_ref, nv_ref):
    h = _norm_mod(x_ref[0], nw_ref[...], mod_ref[0, 0, 0:1, :], mod_ref[0, 0, 1:2, :])
    p = jnp.dot(h.astype(bf16), w_ref[...], preferred_element_type=f32)
    w = HG_WIDTH
    hq_ref[0] = _silu(p[:, 0:w])
    for d in range(2):
        log_f, k = _hgrn_gates(p[:, (1 + d) * w:(2 + d) * w], gp_ref[3 * d:3 * d + 1, :],
                               gp_ref[3 * d + 1:3 * d + 2, :], gp_ref[3 * d + 2:3 * d + 3, :])
        cum_ref[d, 0] = _cumsum16(tri_ref[d], log_f)
        kk_ref[d, 0] = k
    hv_ref[0] = p[:, 3 * w:4 * w]
    sg_ref[0] = _silu(p[:, 4 * w:5 * w])
    cos, sins = cs_ref[0], cs_ref[1]
    o = 5 * w
    scale = HEAD_DIM ** -0.5
    for j in range(SWA_Q // LANE):
        sq_ref[0, :, j * LANE:(j + 1) * LANE] = (_rope(p[:, o + j * LANE:o + (j + 1) * LANE], cos, sins) * scale).astype(bf16)
    o += SWA_Q
    sk_ref[0] = _rope(p[:, o:o + SWA_KV], cos, sins).astype(bf16)
    o += SWA_KV
    sv_ref[0] = p[:, o:o + SWA_KV].astype(bf16)
    o += SWA_KV
    nq_ref[0] = (p[:, o:o + NAT_W] * scale).astype(bf16)
    o += NAT_W
    nk_ref[0] = p[:, o:o + NAT_W].astype(bf16)
    o += NAT_W
    nv_ref[0] = p[:, o:o + NAT_W].astype(bf16)


def _proj(xall, nw, mod, w_in, cs, gp, tri):
    nb, ta, d = xall.shape
    nt = ta // TM
    tok = lambda width, dt: jax.ShapeDtypeStruct((nb, ta, width), dt)
    tok2 = jax.ShapeDtypeStruct((2, nb, ta, HG_WIDTH), f32)
    bs = lambda width: pl.BlockSpec((1, TM, width), lambda b, i: (b, i, 0))
    bs2 = pl.BlockSpec((2, 1, TM, HG_WIDTH), lambda b, i: (0, b, i, 0))
    return pl.pallas_call(
        _proj_kernel,
        grid=(nb, nt),
        in_specs=[pl.BlockSpec((1, TM, d), lambda b, i: (b, i, 0)),
                  pl.BlockSpec((1, d), lambda b, i: (0, 0)),
                  pl.BlockSpec((1, 1, 6, d), lambda b, i: (b, jnp.minimum(i, 1), 0, 0)),
                  pl.BlockSpec((d, PROJ_WIDTH), lambda b, i: (0, 0)),
                  pl.BlockSpec((2, TM, LANE), lambda b, i: (0, i, 0)),
                  pl.BlockSpec((8, HG_WIDTH), lambda b, i: (0, 0)),
                  pl.BlockSpec((2, TM, TM), lambda b, i: (0, 0, 0))],
        out_specs=[bs(HG_WIDTH), bs2, bs2, bs(HG_WIDTH), bs(HG_WIDTH), bs(SWA_Q), bs(SWA_KV), bs(SWA_KV),
                   bs(NAT_W), bs(NAT_W), bs(NAT_W)],
        out_shape=[tok(HG_WIDTH, f32), tok2, tok2, tok(HG_WIDTH, f32), tok(HG_WIDTH, f32), tok(SWA_Q, bf16),
                   tok(SWA_KV, bf16), tok(SWA_KV, bf16), tok(NAT_W, bf16), tok(NAT_W, bf16), tok(NAT_W, bf16)],
        compiler_params=pltpu.CompilerParams(dimension_semantics=("arbitrary", "arbitrary"),
                                             vmem_limit_bytes=VMEM_LIMIT_BYTES),
        name="norm_in_proj",
    )(xall, nw, mod, w_in, cs, gp, tri)


def _hgrn_kernel(q_ref, cum_ref, k_ref, v_ref, bd_ref, o_ref, st_ref):
    d = pl.program_id(1)
    fwd = d == 0

    @pl.when(pl.program_id(2) == 0)
    def _():
        st_ref[...] = jnp.zeros_like(st_ref)

    bd = bd_ref[...]
    bd_mask = bd > 0
    sgn = jnp.where(fwd, 1, -1)
    t_sgn = lax.broadcasted_iota(i32, (HG_SUB, 1), 0) * sgn
    nsteps = TM // HG_SUB

    def step(i, carry):
        sc = jnp.where(fwd, i, nsteps - 1 - i)
        r0 = pl.multiple_of(sc * HG_SUB, HG_SUB)
        q = q_ref[0, pl.ds(r0, HG_SUB), :]
        b = cum_ref[0, 0, pl.ds(r0, HG_SUB), :]
        k = k_ref[0, 0, pl.ds(r0, HG_SUB), :]
        v = v_ref[0, pl.ds(r0, HG_SUB), :]
        b_end = jnp.where(fwd, b[HG_SUB - 1:HG_SUB, :], b[0:1, :])
        st = st_ref[...]
        o = lax.dot_general((q * jnp.exp(b)).astype(bf16), st.astype(bf16), _NT, preferred_element_type=f32)
        rows = []
        for s in range(HG_SUB):
            valid = t_sgn >= s * sgn
            decay = jnp.exp(jnp.minimum(b - b[s:s + 1, :], 0.0))
            rows.append(jnp.where(valid, decay * q * k[s:s + 1, :], 0.0))
        hi, lo = _split2(jnp.concatenate(rows, axis=0))
        pe = jnp.dot(hi, bd, preferred_element_type=f32) + jnp.dot(lo, bd, preferred_element_type=f32)
        for s in range(HG_SUB):
            o = o + pe[s * HG_SUB:(s + 1) * HG_SUB, :] * v[s:s + 1, :]
        o_ref[0, 0, pl.ds(r0, HG_SUB), :] = o
        kd = (k * jnp.exp(b_end - b)).astype(bf16)
        kv_t = lax.dot_general(v.astype(bf16), kd, _TN, preferred_element_type=f32)
        st_ref[...] = st * jnp.exp(b_end) + jnp.where(bd_mask, kv_t, 0.0)
        return carry

    lax.fori_loop(0, nsteps, step, 0, unroll=8)


def _hgrn_scan(hq, cum, kk, hv, bd):
    nb, ta, w = hq.shape
    nt = ta // TM

    def blk(d, j):
        return jnp.where(d == 0, j, jnp.where(j == 0, 0, nt - j))

    return pl.pallas_call(
        _hgrn_kernel,
        grid=(nb, 2, nt),
        in_specs=[pl.BlockSpec((1, TM, w), lambda b, d, j: (b, blk(d, j), 0)),
                  pl.BlockSpec((1, 1, TM, w), lambda b, d, j: (d, b, blk(d, j), 0)),
                  pl.BlockSpec((1, 1, TM, w), lambda b, d, j: (d, b, blk(d, j), 0)),
                  pl.BlockSpec((1, TM, w), lambda b, d, j: (b, blk(d, j), 0)),
                  pl.BlockSpec((w, w), lambda b, d, j: (0, 0))],
        out_specs=pl.BlockSpec((1, 1, TM, w), lambda b, d, j: (d, b, blk(d, j), 0)),
        out_shape=jax.ShapeDtypeStruct((2, nb, ta, w), f32),
        scratch_shapes=[pltpu.VMEM((w, w), f32)],
        compiler_params=pltpu.CompilerParams(dimension_semantics=("arbitrary", "arbitrary", "arbitrary")),
        name="hgrn_scan",
    )(hq, cum, kk, hv, bd)


def _pair_rows(q2):
    lane = lax.broadcasted_iota(i32, q2.shape, 1)
    zero = jnp.zeros_like(q2)
    return jnp.concatenate([jnp.where(lane < HEAD_DIM, q2, zero), jnp.where(lane >= HEAD_DIM, q2, zero)], axis=0)


def _pair_merge(o):
    m = o.shape[0] // 2
    lane = lax.broadcasted_iota(i32, (m, LANE), 1)
    return jnp.where(lane < HEAD_DIM, o[:m], o[m:])


def _joint_attention(s_loc, s_ctx, v_loc, v_ctx, sink=None):
    m = jnp.maximum(jnp.max(s_loc, axis=-1, keepdims=True), jnp.max(s_ctx, axis=-1, keepdims=True))
    if sink is not None:
        m = jnp.maximum(m, sink)
    p_loc = jnp.exp(s_loc - m)
    p_ctx = jnp.exp(s_ctx - m)
    den = jnp.sum(p_loc, axis=-1, keepdims=True) + jnp.sum(p_ctx, axis=-1, keepdims=True)
    if sink is not None:
        den = den + jnp.exp(sink - m)
    o = (jnp.dot(p_loc.astype(bf16), v_loc, preferred_element_type=f32)
         + jnp.dot(p_ctx.astype(bf16), v_ctx, preferred_element_type=f32))
    return o / den


def _swa_kernel(q_ref, kp_ref, kc_ref, kn_ref, vp_ref, vc_ref, vn_ref, kx_ref, vx_ref, sink_ref, o_ref, *, n_ctx_blocks, seq):
    g = pl.program_id(1)
    i = g - n_ctx_blocks
    k_loc = jnp.concatenate([kp_ref[0], kc_ref[0], kn_ref[0]], axis=0)
    v_loc = jnp.concatenate([vp_ref[0], vc_ref[0], vn_ref[0]], axis=0)
    row = lax.broadcasted_iota(i32, (SWA_BLOCK, 3 * SWA_BLOCK), 0)
    col = lax.broadcasted_iota(i32, (SWA_BLOCK, 3 * SWA_BLOCK), 1)
    rel = col - SWA_BLOCK - row
    kpos = (i - 1) * SWA_BLOCK + col
    ok = (jnp.abs(rel) <= SWA_WINDOW) & (kpos >= 0) & (kpos < seq) & (i >= 0)
    pen = jnp.where(ok, 0.0, NEG)
    pen = jnp.concatenate([pen, pen], axis=0)
    for j in range(SWA_Q // LANE):
        q = _pair_rows(q_ref[0, :, j * LANE:(j + 1) * LANE])
        s_loc = lax.dot_general(q, k_loc, _NT, preferred_element_type=f32) + pen
        s_ctx = lax.dot_general(q, kx_ref[0], _NT, preferred_element_type=f32)
        sink = jnp.concatenate([jnp.broadcast_to(sink_ref[2 * j:2 * j + 1, 0:1], (SWA_BLOCK, 1)),
                                jnp.broadcast_to(sink_ref[2 * j + 1:2 * j + 2, 0:1], (SWA_BLOCK, 1))], axis=0)
        o = _joint_attention(s_loc, s_ctx, v_loc, vx_ref[0], sink)
        o_ref[0, :, j * LANE:(j + 1) * LANE] = _pair_merge(o).astype(bf16)


def _swa(sq, sk, sv, sink_rows, n_ctx):
    nb, ta, _ = sq.shape
    ncb = n_ctx // SWA_BLOCK
    nblk = ta // SWA_BLOCK
    nlat = nblk - ncb
    prev = lambda b, g: (b, jnp.clip(g - ncb - 1, 0, nlat - 1) + ncb, 0)
    cur = lambda b, g: (b, g, 0)
    nxt = lambda b, g: (b, jnp.clip(g - ncb + 1, 0, nlat - 1) + ncb, 0)
    kvs = lambda f: pl.BlockSpec((1, SWA_BLOCK, SWA_KV), f)
    ctx = pl.BlockSpec((1, n_ctx, SWA_KV), lambda b, g: (b, 0, 0))
    return pl.pallas_call(
        functools.partial(_swa_kernel, n_ctx_blocks=ncb, seq=ta - n_ctx),
        grid=(nb, nblk),
        in_specs=[pl.BlockSpec((1, SWA_BLOCK, SWA_Q), cur), kvs(prev), kvs(cur), kvs(nxt), kvs(prev), kvs(cur), kvs(nxt),
                  ctx, ctx, pl.BlockSpec((8, LANE), lambda b, g: (0, 0))],
        out_specs=pl.BlockSpec((1, SWA_BLOCK, SWA_Q), cur),
        out_shape=jax.ShapeDtypeStruct((nb, ta, SWA_Q), bf16),
        compiler_params=pltpu.CompilerParams(dimension_semantics=("arbitrary", "arbitrary")),
        name="swa_attention",
    )(sq, sk, sk, sk, sv, sv, sv, sk, sv, sink_rows)


def _nat_kernel(q_ref, k_ref, v_ref, bias_ref, o_ref, *, n_ctx, n_rows):
    g = pl.program_id(1)
    is_ctx = g == 0
    ctx_pen = jnp.where(is_ctx, NEG, 0.0)
    rows_per_tile = TM // GRID_W
    nwin = NAT_ROWS * GRID_W
    for rr in range(rows_per_tile):
        r = jnp.maximum((g - 1) * rows_per_tile + rr, 0)
        rs = jnp.clip(r - NAT_ROWS // 2, 0, n_rows - NAT_ROWS)
        dr0 = rs - r + NAT_ROWS - 1
        start = pl.multiple_of(n_ctx + rs * GRID_W, GRID_W)
        for p in range(NAT_W // LANE):
            ls = slice(p * LANE, (p + 1) * LANE)
            q = _pair_rows(q_ref[0, rr * GRID_W:(rr + 1) * GRID_W, ls])
            s_loc = (lax.dot_general(q, k_ref[0, pl.ds(start, nwin), ls], _NT, preferred_element_type=f32)
                     + bias_ref[p, dr0] + ctx_pen)
            s_ctx = lax.dot_general(q, k_ref[0, 0:n_ctx, ls], _NT, preferred_element_type=f32)
            o = _joint_attention(s_loc, s_ctx, v_ref[0, pl.ds(start, nwin), ls], v_ref[0, 0:n_ctx, ls])
            o_ref[0, rr * GRID_W:(rr + 1) * GRID_W, ls] = _pair_merge(o).astype(bf16)


def _nat(nq, nk, nv, bias, n_ctx):
    nb, ta, w = nq.shape
    nt = ta // TM
    whole = pl.BlockSpec((1, ta, w), lambda b, g: (b, 0, 0), pipeline_mode=pl.Buffered(1))
    return pl.pallas_call(
        functools.partial(_nat_kernel, n_ctx=n_ctx, n_rows=(ta - n_ctx) // GRID_W),
        grid=(nb, nt),
        in_specs=[pl.BlockSpec((1, TM, w), lambda b, g: (b, g, 0)), whole, whole,
                  pl.BlockSpec(bias.shape, lambda b, g: (0, 0, 0, 0), pipeline_mode=pl.Buffered(1))],
        out_specs=pl.BlockSpec((1, TM, w), lambda b, g: (b, g, 0)),
        out_shape=jax.ShapeDtypeStruct((nb, ta, w), bf16),
        compiler_params=pltpu.CompilerParams(dimension_semantics=("arbitrary", "arbitrary"),
                                             vmem_limit_bytes=VMEM_LIMIT_BYTES),
        name="nat_attention",
    )(nq, nk, nv, bias)


def _nat_bias_tables(rpb):
    c = np.arange(GRID_W)[:, None]
    kc = np.arange(GRID_W)[None, :]
    ws = np.clip(c - NAT_COLS // 2, 0, GRID_W - NAT_COLS)
    valid = (kc >= ws) & (kc < ws + NAT_COLS)
    col_idx = np.clip(kc - c + NAT_COLS - 1, 0, 2 * NAT_COLS - 2)
    dr = np.arange(NAT_ROWS)[:, None] + np.arange(NAT_ROWS)[None, :]
    t = rpb.astype(f32)[:, dr][:, :, :, col_idx]
    t = jnp.where(jnp.asarray(valid)[None, None, None], t, NEG)
    t = t.transpose(0, 1, 3, 2, 4).reshape(NAT_HEADS // 2, 2, NAT_ROWS, GRID_W, NAT_ROWS * GRID_W)
    return t.transpose(0, 2, 1, 3, 4).reshape(NAT_HEADS // 2, NAT_ROWS, 2 * GRID_W, NAT_ROWS * GRID_W)


def _outproj_kernel(o_ref, sg_ref, hnw_ref, bd_ref, so_ref, no_ref, wa_ref, wb_ref, wn_ref, mod_ref, x_ref, y_ref):
    oo = o_ref[0, 0] + o_ref[1, 0]
    hi, lo = _split2(oo * oo)
    bd = bd_ref[...]
    msq = (jnp.dot(hi, bd, preferred_element_type=f32) + jnp.dot(lo, bd, preferred_element_type=f32)) * (1.0 / HEAD_DIM)
    a = oo * lax.rsqrt(msq + EPS) * hnw_ref[...] * sg_ref[0]
    y = (jnp.dot(a.astype(bf16), wa_ref[...], preferred_element_type=f32)
         + jnp.dot(so_ref[0], wb_ref[...], preferred_element_type=f32)
         + jnp.dot(no_ref[0], wn_ref[...], preferred_element_type=f32))
    y_ref[0] = x_ref[0] + mod_ref[0, 0, 2:3, :] * y


def _outproj(o, sg, hnw, bd, so, no, wa, wb, wn, mod, xall):
    nb, ta, d = xall.shape
    nt = ta // TM
    bs = lambda width: pl.BlockSpec((1, TM, width), lambda b, i: (b, i, 0))
    full = lambda a: pl.BlockSpec(a.shape, lambda b, i: (0,) * a.ndim)
    return pl.pallas_call(
        _outproj_kernel,
        grid=(nb, nt),
        in_specs=[pl.BlockSpec((2, 1, TM, HG_WIDTH), lambda b, i: (0, b, i, 0)), bs(HG_WIDTH), full(hnw), full(bd),
                  bs(SWA_Q), bs(NAT_W), full(wa), full(wb), full(wn),
                  pl.BlockSpec((1, 1, 6, d), lambda b, i: (b, jnp.minimum(i, 1), 0, 0)), bs(d)],
        out_specs=bs(d),
        out_shape=jax.ShapeDtypeStruct(xall.shape, f32),
        compiler_params=pltpu.CompilerParams(dimension_semantics=("arbitrary", "arbitrary"),
                                             vmem_limit_bytes=VMEM_LIMIT_BYTES),
        name="out_proj",
    )(o, sg, hnw, bd, so, no, wa, wb, wn, mod, xall)


def _top16_rows(vals, payload=None):
    n = vals.shape[0]
    iota = lax.broadcasted_iota(i32, vals.shape, 0)
    best, picked = [], []
    for _ in range(PEER_TOPK):
        m = jnp.max(vals, axis=0, keepdims=True)
        idx = jnp.min(jnp.where(vals == m, iota, n), axis=0, keepdims=True)
        hit = iota == idx
        best.append(m)
        picked.append(idx if payload is None else jnp.max(jnp.where(hit, payload, -1), axis=0, keepdims=True))
        vals = jnp.where(hit, -jnp.inf, vals)
    return jnp.concatenate(best, axis=0), jnp.concatenate(picked, axis=0)


def _product_candidates(first, second, combine):
    half = PEER_TOPK // 2
    rows = [combine(first[0:1, :], second)]
    rows += [combine(first[a:a + 1, :], second[0:half, :]) for a in range(1, half)]
    rows.append(combine(first[half:, :], second[0:1, :]))
    return jnp.concatenate(rows, axis=0)


def _route_kernel(x_ref, nw_ref, mod_ref, wq_ref, sub_ref, h_ref, idx_ref, g_ref):
    h = _norm_mod(x_ref[0], nw_ref[...], mod_ref[0, 0, 3:4, :], mod_ref[0, 0, 4:5, :])
    h_ref[0] = h
    q = jnp.dot(h.astype(bf16), wq_ref[...], preferred_element_type=f32).astype(bf16)
    half = PEER_DK // 2
    ids, gates = [], []
    for hh in range(PEER_HEADS):
        top_s, top_i = [], []
        for p in range(2):
            c0 = (hh * 2 + p) * half
            s = lax.dot_general(sub_ref[hh, p], q[:, c0:c0 + half], _NT, preferred_element_type=f32)
            ts, ti = _top16_rows(s)
            top_s.append(ts)
            top_i.append(ti)
        cand_s = _product_candidates(top_s[0], top_s[1], lambda a, b: a + b)
        cand_i = _product_candidates(top_i[0], top_i[1], lambda a, b: a * PEER_NKEYS + b)
        best, eid = _top16_rows(cand_s, cand_i)
        ids.append(eid)
        e = jnp.exp(best - best[0:1, :])
        gates.append(e / jnp.sum(e, axis=0, keepdims=True))
    idx_ref[0] = jnp.concatenate(ids, axis=0).T
    g_ref[0] = jnp.concatenate(gates, axis=0).T


def _route(xall, nw, mod, wq, sub):
    nb, ta, d = xall.shape
    nt = ta // TM
    full = lambda a: pl.BlockSpec(a.shape, lambda b, i: (0,) * a.ndim)
    return pl.pallas_call(
        _route_kernel,
        grid=(nb, nt),
        in_specs=[pl.BlockSpec((1, TM, d), lambda b, i: (b, i, 0)), full(nw),
                  pl.BlockSpec((1, 1, 6, d), lambda b, i: (b, jnp.minimum(i, 1), 0, 0)), full(wq), full(sub)],
        out_specs=[pl.BlockSpec((1, TM, d), lambda b, i: (b, i, 0)),
                   pl.BlockSpec((1, TM, PEER_SLOTS), lambda b, i: (b, i, 0)),
                   pl.BlockSpec((1, TM, PEER_SLOTS), lambda b, i: (b, i, 0))],
        out_shape=[jax.ShapeDtypeStruct(xall.shape, f32), jax.ShapeDtypeStruct((nb, ta, PEER_SLOTS), i32),
                   jax.ShapeDtypeStruct((nb, ta, PEER_SLOTS), f32)],
        compiler_params=pltpu.CompilerParams(dimension_semantics=("arbitrary", "arbitrary"),
                                             vmem_limit_bytes=VMEM_LIMIT_BYTES),
        name="peer_route",
    )(xall, nw, mod, wq, sub)


def _expert_kernel(idx_ref, idx_next_ref, h_ref, g_ref, x_ref, mod_ref, tab_hbm, o_ref, buf, sem):
    d = h_ref.shape[-1]
    nct = d // LANE
    groups = PEER_SLOTS // 8

    per_piece = PEER_SLOTS // nct

    def issue(ids_ref, t, slot, piece):
        for k in range(piece * per_piece, (piece + 1) * per_piece):
            pltpu.make_async_copy(tab_hbm.at[ids_ref[t, k]], buf.at[slot, k // 8, :, pl.ds(k % 8, 1), :],
                                  sem.at[slot]).start(priority=k % 2)

    def wait(slot):
        pltpu.make_async_copy(buf.at[slot], buf.at[slot], sem.at[slot]).wait()

    def cols(slot, c, high):
        words = jnp.concatenate([jnp.concatenate([buf[slot, kg, c], buf[slot, kg, c + 1]], axis=1)
                                 for kg in range(groups)], axis=0)
        bits = (words & jnp.uint32(0xFFFF0000)) if high else (words << 16)
        return lax.bitcast_convert_type(bits, f32).astype(bf16)

    not_last = pl.program_id(0) < pl.num_programs(0) - 1

    def step(t, w8, do_issue, do_u, do_v):
        nslot = lax.rem(t + 1, PEER_NBUF)
        slot = lax.rem(t, PEER_NBUF)
        islot = lax.rem(t + PEER_AHEAD, PEER_NBUF)
        if do_issue == "next":
            @pl.when(not_last)
            def _():
                for j in range(nct):
                    issue(idx_next_ref, t + PEER_AHEAD - PEER_TB, islot, j)
        if do_u:
            wait(nslot)
            h8 = jnp.broadcast_to(h_ref[pl.ds(t + 1, 1), :], (8, d)).astype(bf16)
            gate = g_ref[pl.ds(t + 1, 1), :]
            a = jnp.zeros((8, PEER_SLOTS), f32)
        ys = []
        for j in range(nct):
            if 2 * j < nct:
                c = 2 * j
                if do_u:
                    a = a + lax.dot_general(h8[:, c * LANE:(c + 2) * LANE], cols(nslot, c, True), _NT, preferred_element_type=f32)
                if do_v:
                    ys.append(jnp.dot(w8, cols(slot, c, False), preferred_element_type=f32)[0:1, :])
            if do_issue is True:
                issue(idx_ref, t + PEER_AHEAD, islot, j)
        if do_v:
            o_ref[pl.ds(t, 1), :] = jnp.concatenate(ys, axis=1)
        if do_u:
            a1 = a[0:1, :]
            w = gate * (0.5 * a1 * (1.0 + lax.erf(a1 * (2.0 ** -0.5))))
            w8 = jnp.broadcast_to(w, (8, PEER_SLOTS)).astype(bf16)
        return w8

    @pl.when(pl.program_id(0) == 0)
    def _():
        def prologue(t, carry):
            for j in range(nct):
                issue(idx_ref, t, t, j)
            return carry

        lax.fori_loop(0, PEER_AHEAD, prologue, 0)

    w8 = step(-1, jnp.zeros((8, PEER_SLOTS), bf16), False, True, False)
    w8 = lax.fori_loop(0, PEER_TB - PEER_AHEAD, lambda t, w: step(t, w, True, True, True), w8)
    w8 = lax.fori_loop(PEER_TB - PEER_AHEAD, PEER_TB - 1, lambda t, w: step(t, w, "next", True, True), w8)
    step(PEER_TB - 1, w8, "next", False, True)
    o_ref[...] = x_ref[...] + mod_ref[0, 0, 5:6, :] * o_ref[...]


def _mod_index(tokens_per_step, ta, n_ctx, first_step=0):
    spb = ta // tokens_per_step
    cs = n_ctx // tokens_per_step
    return lambda i: ((i + first_step) // spb, jnp.where((i + first_step) % spb < cs, 0, 1), 0, 0)


def _experts(idx, h, g, xflat, mod, tab, ta, n_ctx, nblocks):
    n, d = xflat.shape
    tokb = lambda width: pl.BlockSpec((PEER_TB, width), lambda i: (i, 0))
    return pl.pallas_call(
        _expert_kernel,
        grid=(nblocks,),
        in_specs=[pl.BlockSpec((PEER_TB, PEER_SLOTS), lambda i: (i, 0), memory_space=pltpu.SMEM),
                  pl.BlockSpec((PEER_TB, PEER_SLOTS), lambda i: (jnp.minimum(i + 1, nblocks - 1), 0), memory_space=pltpu.SMEM),
                  tokb(d), tokb(PEER_SLOTS), tokb(d),
                  pl.BlockSpec((1, 1, 6, d), _mod_index(PEER_TB, ta, n_ctx)),
                  pl.BlockSpec(memory_space=pl.ANY)],
        out_specs=tokb(d),
        out_shape=jax.ShapeDtypeStruct((n, d), f32),
        input_output_aliases={4: 0},
        scratch_shapes=[pltpu.VMEM((PEER_NBUF, PEER_SLOTS // 8, d // LANE, 8, LANE), jnp.uint32),
                        pltpu.SemaphoreType.DMA((PEER_NBUF,))],
        compiler_params=pltpu.CompilerParams(dimension_semantics=("arbitrary",), vmem_limit_bytes=VMEM_LIMIT_BYTES),
        name="peer_experts",
    )(idx, idx, h, g, xflat, mod, tab)


def _sc_gather(table, idx):
    info = plsc.get_sparse_core_info()
    nc = info.num_cores
    nw = nc * info.num_subcores
    r = idx.shape[0]
    per_w = r // nw
    nch = per_w // SC_CHUNK
    assert per_w * nw == r and nch * SC_CHUNK == per_w and nch % 2 == 0
    rows_shape = (SC_CHUNK,) + table.shape[1:]
    mesh = plsc.VectorSubcoreMesh(core_axis_name="c", subcore_axis_name="s")

    @functools.partial(
        pl.kernel, mesh=mesh, out_type=jax.ShapeDtypeStruct((r,) + table.shape[1:], table.dtype),
        scratch_types=[pltpu.VMEM((2, SC_CHUNK), i32), pltpu.VMEM((2,) + rows_shape, table.dtype),
                       pltpu.SemaphoreType.DMA((2,)), pltpu.SemaphoreType.DMA((2,)), pltpu.SemaphoreType.DMA((2,))])
    def gather_kernel(table_hbm, idx_hbm, out_hbm, idx_v, rows_v, isem, gsem, wsem):
        base = (lax.axis_index("s") * nc + lax.axis_index("c")) * per_w

        def rows_of(j):
            return pl.ds(pl.multiple_of(base + j * SC_CHUNK, 8), SC_CHUNK)

        def fetch_idx(j, b):
            return pltpu.make_async_copy(idx_hbm.at[rows_of(j)], idx_v.at[b], isem.at[b])

        def gather(b):
            return pltpu.make_async_copy(table_hbm.at[idx_v.at[b]], rows_v.at[b], gsem.at[b])

        def write(j, b):
            return pltpu.make_async_copy(rows_v.at[b], out_hbm.at[rows_of(j)], wsem.at[b])

        fetch_idx(0, 0).start()
        fetch_idx(1, 1).start()
        fetch_idx(0, 0).wait()
        gather(0).start()

        @pl.loop(0, nch, step=2)
        def _(j0):
            for b in (0, 1):
                j = j0 + b

                @pl.when(j >= 1)
                def _():
                    write(j - 1, 1 - b).wait()

                @pl.when(j + 1 < nch)
                def _():
                    fetch_idx(j + 1, 1 - b).wait()
                    gather(1 - b).start()

                gather(b).wait()

                @pl.when(j + 2 < nch)
                def _():
                    fetch_idx(j + 2, b).start()

                write(j, b).start()

        write(nch - 1, 1).wait()

    return gather_kernel(table, idx)


def _expert_dense_kernel(st_ref, h_ref, g_ref, x_ref, mod_ref, o_ref):
    d = h_ref.shape[-1]
    nct = d // LANE
    groups = PEER_SLOTS // 8
    rows_per_token = PEER_SLOTS * nct

    per_row = LANE // nct
    lane = lax.broadcasted_iota(i32, (nct, LANE), 1)
    sub = lax.broadcasted_iota(i32, (nct, LANE), 0)
    lane8 = lax.broadcasted_iota(i32, (8, LANE), 1)
    sub8 = lax.broadcasted_iota(i32, (8, LANE), 0)

    def u_stage(t):
        base = pl.multiple_of(t * rows_per_token, rows_per_token)
        hrow = h_ref[pl.ds(t, 1), :]
        hch = jnp.concatenate([hrow[:, c * LANE:(c + 1) * LANE] for c in range(nct)], axis=0).astype(bf16)
        words = st_ref[pl.ds(base, rows_per_token), :]
        u = lax.bitcast_convert_type(words & jnp.uint32(0xFFFF0000), f32).astype(bf16)
        q = lax.dot_general(hch, u, _NT, preferred_element_type=f32)
        x = jnp.concatenate([jnp.sum(jnp.where(lane % nct == sub, q[:, r * LANE:(r + 1) * LANE], 0.0), axis=0, keepdims=True)
                             for r in range(nct)], axis=0)
        shift = 1
        while shift < nct:
            x = x + pltpu.roll(x, LANE - shift, 1)
            shift *= 2
        return g_ref[t] * (0.5 * x * (1.0 + lax.erf(x * (2.0 ** -0.5))))

    def v_stage(t, w):
        base = t * rows_per_token
        acc = [None] * nct
        for kg in range(groups):
            k0 = kg * 8
            wb = jnp.broadcast_to(w[k0 // per_row:k0 // per_row + 1, :], (8, LANE))
            wcol = jnp.sum(jnp.where(lane8 == (k0 % per_row + sub8) * nct, wb, 0.0), axis=-1, keepdims=True)
            for c in range(nct):
                words = st_ref[pl.ds(base + kg * 8 * nct + c, 8, stride=nct), :]
                term = lax.bitcast_convert_type(words << 16, f32) * wcol
                acc[c] = term if acc[c] is None else acc[c] + term
        o_ref[pl.ds(t, 1), :] = jnp.concatenate([jnp.sum(a, axis=0, keepdims=True) for a in acc], axis=1)

    def body(t, w):
        w_next = u_stage(t + 1)
        v_stage(t, w)
        return w_next

    w = lax.fori_loop(0, PEER_DENSE_TB - 1, body, u_stage(0), unroll=8)
    v_stage(PEER_DENSE_TB - 1, w)
    o_ref[...] = x_ref[...] + mod_ref[0, 0, 5:6, :] * o_ref[...]


def _experts_dense(staged, h, g, xflat, mod, ta, n_ctx, first_block):
    n, d = xflat.shape
    nct = d // LANE
    rows_per_token = PEER_SLOTS * nct
    g_staged = g[first_block * PEER_TB:]
    g_spread = jnp.zeros((g_staged.shape[0], nct, LANE // nct, nct), f32).at[..., 0].set(
        g_staged.reshape(-1, nct, LANE // nct)).reshape(-1, nct, LANE)
    steps = staged.shape[0] // (PEER_DENSE_TB * rows_per_token)
    off = first_block * PEER_TB // PEER_DENSE_TB
    tokb = lambda width: pl.BlockSpec((PEER_DENSE_TB, width), lambda i: (i + off, 0))
    return pl.pallas_call(
        _expert_dense_kernel,
        grid=(steps,),
        in_specs=[pl.BlockSpec((PEER_DENSE_TB * rows_per_token, LANE), lambda i: (i, 0)),
                  tokb(d), pl.BlockSpec((PEER_DENSE_TB, nct, LANE), lambda i: (i, 0, 0)), tokb(d),
                  pl.BlockSpec((1, 1, 6, d), _mod_index(PEER_DENSE_TB, ta, n_ctx, off))],
        out_specs=tokb(d),
        out_shape=jax.ShapeDtypeStruct((n, d), f32),
        input_output_aliases={3: 0},
        compiler_params=pltpu.CompilerParams(dimension_semantics=("arbitrary",), vmem_limit_bytes=VMEM_LIMIT_BYTES),
        name="peer_experts_dense",
    )(staged, h, g_spread, xflat, mod)


def _expert_table(u, v):
    half = lambda a: lax.bitcast_convert_type(a.astype(bf16), jnp.uint16).astype(jnp.uint32)
    return (half(u) << 16) | half(v)


def _final_norm_kernel(x_ref, w_ref, o_ref):
    x = x_ref[0]
    o_ref[0] = x * lax.rsqrt(jnp.mean(x * x, axis=-1, keepdims=True) + EPS) * w_ref[...]


def _final_norm(xall, w, n_ctx):
    nb, ta, d = xall.shape
    off = n_ctx // TM
    return pl.pallas_call(
        _final_norm_kernel,
        grid=(nb, (ta - n_ctx) // TM),
        in_specs=[pl.BlockSpec((1, TM, d), lambda b, i: (b, i + off, 0)), pl.BlockSpec((1, d), lambda b, i: (0, 0))],
        out_specs=pl.BlockSpec((1, TM, d), lambda b, i: (b, i, 0)),
        out_shape=jax.ShapeDtypeStruct((nb, ta - n_ctx, d), f32),
        compiler_params=pltpu.CompilerParams(dimension_semantics=("arbitrary", "arbitrary")),
        name="final_norm",
    )(xall, w.reshape(1, d))


def _rope_tables(n_ctx, seq):
    nf = HEAD_DIM // 4
    pos = np.arange(seq)
    inv = ROPE_BASE ** (-np.arange(nf, dtype=np.float32) / nf)
    ar = (pos // GRID_W).astype(np.float32)[:, None] * inv
    ac = (pos % GRID_W).astype(np.float32)[:, None] * inv
    cos = np.concatenate([np.cos(ar), np.cos(ar), np.cos(ac), np.cos(ac)], axis=-1)
    sin = np.concatenate([-np.sin(ar), np.sin(ar), -np.sin(ac), np.sin(ac)], axis=-1)
    cos = np.concatenate([np.ones((n_ctx, HEAD_DIM), np.float32), cos.astype(np.float32)], axis=0)
    sin = np.concatenate([np.zeros((n_ctx, HEAD_DIM), np.float32), sin.astype(np.float32)], axis=0)
    return jnp.asarray(np.stack([np.tile(cos, (1, 2)), np.tile(sin, (1, 2))]))


def _tri_tables():
    r = np.arange(TM)[:, None]
    c = np.arange(TM)[None, :]
    same = (r // HG_SUB) == (c // HG_SUB)
    return jnp.asarray(np.stack([same & (c <= r), same & (c >= r)]).astype(np.float32), dtype=bf16)


def _block_diag_ones():
    r = np.arange(HG_WIDTH)
    return jnp.asarray((r[:, None] // HEAD_DIM == r[None, :] // HEAD_DIM).astype(np.float32), dtype=bf16)


def _swa_perm():
    return np.concatenate([np.arange(h * HEAD_DIM, (h + 1) * HEAD_DIM) for h in SWA_HEAD_ORDER])


def kernel(x, c, ctx, c_ctx, w_mod, b_mod, norm_mix_w, norm_ffn_w, w_in, hgrn_lb_logits, hgrn_norm_w, swa_sink, nat_rpb,
           w_out, peer_wq, peer_subkeys, peer_u, peer_v, final_norm_w):
    nb, seq, d = x.shape
    n_ctx = ctx.shape[1]
    depth = w_in.shape[0]
    ta = n_ctx + seq
    assert n_ctx == TM and seq % TM == 0 and seq % (GRID_W * NAT_ROWS) == 0 and d % LANE == 0
    assert w_in.shape[-1] == PROJ_WIDTH and peer_wq.shape[-1] == PEER_HEADS * PEER_DK

    mods = _mods(c, c_ctx, w_mod, b_mod)
    cs = _rope_tables(n_ctx, seq)
    tri = _tri_tables()
    bd = _block_diag_ones()
    perm = _swa_perm()
    q0 = 5 * HG_WIDTH
    col_perm = np.concatenate([np.arange(q0), q0 + perm, np.arange(q0 + SWA_Q, PROJ_WIDTH)])
    w_in_b = w_in[:, :, col_perm].astype(bf16)
    w_out_b = w_out.astype(bf16)
    wa = w_out_b[:, :HG_WIDTH]
    wb = w_out_b[:, HG_WIDTH:HG_WIDTH + SWA_Q][:, perm]
    wn = w_out_b[:, HG_WIDTH + SWA_Q:]
    wq_b = peer_wq.astype(bf16)
    sub_b = peer_subkeys.astype(bf16)
    lb = jnp.cumsum(jax.nn.softmax(hgrn_lb_logits.astype(f32), axis=1), axis=1)
    lb = lb - lb[:, :1]
    gp = jnp.stack([jnp.log(lb[0]), jnp.log1p(-lb[0]), 1.0 - lb[0], jnp.log(lb[1]), jnp.log1p(-lb[1]), 1.0 - lb[1],
                    jnp.zeros_like(lb[0]), jnp.zeros_like(lb[0])], axis=1)
    sink_rows = jnp.zeros((depth, 8, LANE), f32).at[:, :SWA_HEADS].set(
        jnp.broadcast_to(swa_sink.astype(f32)[:, list(SWA_HEAD_ORDER), None], (depth, SWA_HEADS, LANE)))

    words = _expert_table(peer_u, peer_v)
    n_exp = words.shape[1]
    tab = words.reshape(depth, n_exp, d // LANE, 1, LANE)
    tab_sc = words.reshape(depth, n_exp, d // LANE, LANE)
    n_blocks = nb * ta // PEER_TB
    n_tc = n_blocks - int(n_blocks * PEER_SC_SHARE + 0.5)
    assert ta % PEER_TB == 0 and PEER_TB % PEER_NBUF == 0 and PEER_TB % PEER_DENSE_TB == 0 and 0 < n_tc < n_blocks

    xall = jnp.concatenate([ctx, x], axis=1)
    for l in range(depth):
        mod = mods[l]
        hq, cum, kk, hv, sg, sq, sk, sv, nq, nk, nv = _proj(xall, norm_mix_w[l].reshape(1, d), mod, w_in_b[l], cs, gp[l], tri)
        o = _hgrn_scan(hq, cum, kk, hv, bd)
        so = _swa(sq, sk, sv, sink_rows[l], n_ctx)
        no = _nat(nq, nk, nv, _nat_bias_tables(nat_rpb[l]), n_ctx)
        xall = _outproj(o, sg, hgrn_norm_w[l].reshape(1, HG_WIDTH), bd, so, no, wa[l], wb[l], wn[l], mod, xall)
        h2, idx, g = _route(xall, norm_ffn_w[l].reshape(1, d), mod, wq_b[l], sub_b[l])
        idx, h2, g = idx.reshape(nb * ta, PEER_SLOTS), h2.reshape(nb * ta, d), g.reshape(nb * ta, PEER_SLOTS)
        staged = _sc_gather(tab_sc[l], idx[n_tc * PEER_TB:].reshape(-1))
        xf = _experts(idx, h2, g, xall.reshape(nb * ta, d), mod, tab[l], ta, n_ctx, n_tc)
        xf = _experts_dense(staged.reshape(-1, LANE), h2, g, xf, mod, ta, n_ctx, n_tc)
        xall = xf.reshape(nb, ta, d)
    return _final_norm(xall, final_norm_w, n_ctx)
```

```python
import functools

import numpy as np
import jax
import jax.numpy as jnp
from jax import lax
from jax.experimental import pallas as pl
from jax.experimental.pallas import tpu as pltpu
from jax.experimental.pallas import tpu_sc as plsc

f32 = jnp.float32
bf16 = jnp.bfloat16
i32 = jnp.int32

LANE = 128
VMEM_LIMIT_BYTES = 56 * 1024 * 1024

GRID_W = 64
HEAD_DIM = 64
EPS = 1e-6
HG_HEADS = 4
HG_WIDTH = 256
HG_SUB = 16
SWA_HEADS = 6
SWA_KV_HEADS = 2
SWA_WINDOW = 128
SWA_BLOCK = 128
ROPE_BASE = 10000.0
NAT_HEADS = 6
NAT_ROWS = 8
NAT_COLS = 16
SWA_Q = SWA_HEADS * HEAD_DIM
SWA_KV = SWA_KV_HEADS * HEAD_DIM
NAT_W = NAT_HEADS * HEAD_DIM
PROJ_WIDTH = 5 * HG_WIDTH + SWA_Q + 2 * SWA_KV + 3 * NAT_W
PEER_HEADS = 8
PEER_NKEYS = 128
PEER_DK = 256
PEER_TOPK = 16
PEER_SLOTS = PEER_HEADS * PEER_TOPK

TM = 256
PEER_TB = 128
PEER_NBUF = 16
PEER_AHEAD = PEER_NBUF - 1
PEER_SC_SHARE = 0.508
PEER_DENSE_TB = 32
SC_CHUNK = 32
NEG = -1e30

SWA_HEAD_ORDER = (0, 3, 1, 4, 2, 5)

_NT = (((1,), (1,)), ((), ()))
_TN = (((0,), (0,)), ((), ()))


def _silu(x):
    return x * (1.0 / (1.0 + jnp.exp(-x)))


def _split2(x):
    hi = x.astype(bf16)
    lo = (x - hi.astype(f32)).astype(bf16)
    return hi, lo


def _mods_kernel(sc_ref, w_ref, b_ref, o_ref):
    s = _silu(sc_ref[...])
    o_ref[0] = jnp.dot(s.astype(bf16), w_ref[0].astype(bf16), preferred_element_type=f32) + b_ref[0]


def _mods(c, c_ctx, w_mod, b_mod):
    depth, d, d6 = w_mod.shape
    nb = c.shape[0]
    rows = jnp.zeros((8, d), f32).at[:nb].set(c).at[nb].set(c_ctx)
    tn = d6 // 4
    out = pl.pallas_call(
        _mods_kernel,
        grid=(depth, 4),
        in_specs=[pl.BlockSpec((8, d), lambda l, j: (0, 0)),
                  pl.BlockSpec((1, d, tn), lambda l, j: (l, 0, j)),
                  pl.BlockSpec((1, 1, tn), lambda l, j: (l, 0, j))],
        out_specs=pl.BlockSpec((1, 8, tn), lambda l, j: (l, 0, j)),
        out_shape=jax.ShapeDtypeStruct((depth, 8, d6), f32),
        compiler_params=pltpu.CompilerParams(dimension_semantics=("arbitrary", "arbitrary"),
                                             vmem_limit_bytes=VMEM_LIMIT_BYTES),
        name="adaln_mods",
    )(rows, w_mod, b_mod.reshape(depth, 1, d6))
    lat = out[:, :nb].reshape(depth, nb, 6, d)
    ctx = jnp.broadcast_to(out[:, nb].reshape(depth, 1, 6, d), (depth, nb, 6, d))
    return jnp.stack([ctx, lat], axis=2)


def _norm_mod(x, nw, shift, scale):
    y = x * lax.rsqrt(jnp.mean(x * x, axis=-1, keepdims=True) + EPS) * nw
    return y * (1.0 + scale) + shift


def _rope(t, cos, sins):
    lane = lax.broadcasted_iota(i32, t.shape, 1)
    sw = jnp.where((lane % 32) < 16, pltpu.roll(t, LANE - 16, 1), pltpu.roll(t, 16, 1))
    return t * cos + sw * sins


def _hgrn_gates(z, log_lb, log1m_lb, one_m_lb):
    log_sig = jnp.minimum(z, 0.0) - jnp.log1p(jnp.exp(-jnp.abs(z)))
    c = log1m_lb + log_sig
    m = jnp.maximum(log_lb, c)
    log_f = m + jnp.log1p(jnp.exp(-jnp.abs(log_lb - c)))
    k = one_m_lb * (1.0 / (1.0 + jnp.exp(z)))
    return log_f, k


def _cumsum16(tri, log_f):
    h1 = log_f.astype(bf16)
    r1 = log_f - h1.astype(f32)
    h2 = r1.astype(bf16)
    h3 = (r1 - h2.astype(f32)).astype(bf16)
    return (jnp.dot(tri, h1, preferred_element_type=f32) + jnp.dot(tri, h2, preferred_element_type=f32)
            + jnp.dot(tri, h3, preferred_element_type=f32))


def _proj_kernel(x_ref, nw_ref, mod_ref, w_ref, cs_ref, gp_ref, tri_ref,
                 hq_ref, cum_ref, kk_ref, hv_ref, sg_ref, sq_ref, sk_ref, sv_ref, nq_ref, nk_ref, nv_ref):
    h = _norm_mod(x_ref[0], nw_ref[...], mod_ref[0, 0, 0:1, :], mod_ref[0, 0, 1:2, :])
    p = jnp.dot(h.astype(bf16), w_ref[...], preferred_element_type=f32)
    w = HG_WIDTH
    hq_ref[0] = _silu(p[:, 0:w])
    for d in range(2):
        log_f, k = _hgrn_gates(p[:, (1 + d) * w:(2 + d) * w], gp_ref[3 * d:3 * d + 1, :],
                               gp_ref[3 * d + 1:3 * d + 2, :], gp_ref[3 * d + 2:3 * d + 3, :])
        cum_ref[d, 0] = _cumsum16(tri_ref[d], log_f)
        kk_ref[d, 0] = k
    hv_ref[0] = p[:, 3 * w:4 * w]
    sg_ref[0] = _silu(p[:, 4 * w:5 * w])
    cos, sins = cs_ref[0], cs_ref[1]
    o = 5 * w
    scale = HEAD_DIM ** -0.5
    for j in range(SWA_Q // LANE):
        sq_ref[0, :, j * LANE:(j + 1) * LANE] = (_rope(p[:, o + j * LANE:o + (j + 1) * LANE], cos, sins) * scale).astype(bf16)
    o += SWA_Q
    sk_ref[0] = _rope(p[:, o:o + SWA_KV], cos, sins).astype(bf16)
    o += SWA_KV
    sv_ref[0] = p[:, o:o + SWA_KV].astype(bf16)
    o += SWA_KV
    nq_ref[0] = (p[:, o:o + NAT_W] * scale).astype(bf16)
    o += NAT_W
    nk_ref[0] = p[:, o:o + NAT_W].astype(bf16)
    o += NAT_W
    nv_ref[0] = p[:, o:o + NAT_W].astype(bf16)


def _proj(xall, nw, mod, w_in, cs, gp, tri):
    nb, ta, d = xall.shape
    nt = ta // TM
    tok = lambda width, dt: jax.ShapeDtypeStruct((nb, ta, width), dt)
    tok2 = jax.ShapeDtypeStruct((2, nb, ta, HG_WIDTH), f32)
    bs = lambda width: pl.BlockSpec((1, TM, width), lambda b, i: (b, i, 0))
    bs2 = pl.BlockSpec((2, 1, TM, HG_WIDTH), lambda b, i: (0, b, i, 0))
    return pl.pallas_call(
        _proj_kernel,
        grid=(nb, nt),
        in_specs=[pl.BlockSpec((1, TM, d), lambda b, i: (b, i, 0)),
                  pl.BlockSpec((1, d), lambda b, i: (0, 0)),
                  pl.BlockSpec((1, 1, 6, d), lambda b, i: (b, jnp.minimum(i, 1), 0, 0)),
                  pl.BlockSpec((d, PROJ_WIDTH), lambda b, i: (0, 0)),
                  pl.BlockSpec((2, TM, LANE), lambda b, i: (0, i, 0)),
                  pl.BlockSpec((8, HG_WIDTH), lambda b, i: (0, 0)),
                  pl.BlockSpec((2, TM, TM), lambda b, i: (0, 0, 0))],
        out_specs=[bs(HG_WIDTH), bs2, bs2, bs(HG_WIDTH), bs(HG_WIDTH), bs(SWA_Q), bs(SWA_KV), bs(SWA_KV),
                   bs(NAT_W), bs(NAT_W), bs(NAT_W)],
        out_shape=[tok(HG_WIDTH, f32), tok2, tok2, tok(HG_WIDTH, f32), tok(HG_WIDTH, f32), tok(SWA_Q, bf16),
                   tok(SWA_KV, bf16), tok(SWA_KV, bf16), tok(NAT_W, bf16), tok(NAT_W, bf16), tok(NAT_W, bf16)],
        compiler_params=pltpu.CompilerParams(dimension_semantics=("arbitrary", "arbitrary"),
                                             vmem_limit_bytes=VMEM_LIMIT_BYTES),
        name="norm_in_proj",
    )(xall, nw, mod, w_in, cs, gp, tri)


def _hgrn_kernel(q_ref, cum_ref, k_ref, v_ref, bd_ref, o_ref, st_ref):
    d = pl.program_id(1)
    fwd = d == 0

    @pl.when(pl.program_id(2) == 0)
    def _():
        st_ref[...] = jnp.zeros_like(st_ref)

    bd = bd_ref[...]
    bd_mask = bd > 0
    sgn = jnp.where(fwd, 1, -1)
    t_sgn = lax.broadcasted_iota(i32, (HG_SUB, 1), 0) * sgn
    nsteps = TM // HG_SUB

    def step(i, carry):
        sc = jnp.where(fwd, i, nsteps - 1 - i)
        r0 = pl.multiple_of(sc * HG_SUB, HG_SUB)
        q = q_ref[0, pl.ds(r0, HG_SUB), :]
        b = cum_ref[0, 0, pl.ds(r0, HG_SUB), :]
        k = k_ref[0, 0, pl.ds(r0, HG_SUB), :]
        v = v_ref[0, pl.ds(r0, HG_SUB), :]
        b_end = jnp.where(fwd, b[HG_SUB - 1:HG_SUB, :], b[0:1, :])
        st = st_ref[...]
        o = lax.dot_general((q * jnp.exp(b)).astype(bf16), st.astype(bf16), _NT, preferred_element_type=f32)
        rows = []
        for s in range(HG_SUB):
            valid = t_sgn >= s * sgn
            decay = jnp.exp(jnp.minimum(b - b[s:s + 1, :], 0.0))
            rows.append(jnp.where(valid, decay * q * k[s:s + 1, :], 0.0))
        hi, lo = _split2(jnp.concatenate(rows, axis=0))
        pe = jnp.dot(hi, bd, preferred_element_type=f32) + jnp.dot(lo, bd, preferred_element_type=f32)
        for s in range(HG_SUB):
            o = o + pe[s * HG_SUB:(s + 1) * HG_SUB, :] * v[s:s + 1, :]
        o_ref[0, 0, pl.ds(r0, HG_SUB), :] = o
        kd = (k * jnp.exp(b_end - b)).astype(bf16)
        kv_t = lax.dot_general(v.astype(bf16), kd, _TN, preferred_element_type=f32)
        st_ref[...] = st * jnp.exp(b_end) + jnp.where(bd_mask, kv_t, 0.0)
        return carry

    lax.fori_loop(0, nsteps, step, 0, unroll=8)


def _hgrn_scan(hq, cum, kk, hv, bd):
    nb, ta, w = hq.shape
    nt = ta // TM

    def blk(d, j):
        return jnp.where(d == 0, j, jnp.where(j == 0, 0, nt - j))

    return pl.pallas_call(
        _hgrn_kernel,
        grid=(nb, 2, nt),
        in_specs=[pl.BlockSpec((1, TM, w), lambda b, d, j: (b, blk(d, j), 0)),
                  pl.BlockSpec((1, 1, TM, w), lambda b, d, j: (d, b, blk(d, j), 0)),
                  pl.BlockSpec((1, 1, TM, w), lambda b, d, j: (d, b, blk(d, j), 0)),
                  pl.BlockSpec((1, TM, w), lambda b, d, j: (b, blk(d, j), 0)),
                  pl.BlockSpec((w, w), lambda b, d, j: (0, 0))],
        out_specs=pl.BlockSpec((1, 1, TM, w), lambda b, d, j: (d, b, blk(d, j), 0)),
        out_shape=jax.ShapeDtypeStruct((2, nb, ta, w), f32),
        scratch_shapes=[pltpu.VMEM((w, w), f32)],
        compiler_params=pltpu.CompilerParams(dimension_semantics=("arbitrary", "arbitrary", "arbitrary")),
        name="hgrn_scan",
    )(hq, cum, kk, hv, bd)


def _pair_rows(q2):
    lane = lax.broadcasted_iota(i32, q2.shape, 1)
    zero = jnp.zeros_like(q2)
    return jnp.concatenate([jnp.where(lane < HEAD_DIM, q2, zero), jnp.where(lane >= HEAD_DIM, q2, zero)], axis=0)


def _pair_merge(o):
    m = o.shape[0] // 2
    lane = lax.broadcasted_iota(i32, (m, LANE), 1)
    return jnp.where(lane < HEAD_DIM, o[:m], o[m:])


def _joint_attention(s_loc, s_ctx, v_loc, v_ctx, sink=None):
    m = jnp.maximum(jnp.max(s_loc, axis=-1, keepdims=True), jnp.max(s_ctx, axis=-1, keepdims=True))
    if sink is not None:
        m = jnp.maximum(m, sink)
    p_loc = jnp.exp(s_loc - m)
    p_ctx = jnp.exp(s_ctx - m)
    den = jnp.sum(p_loc, axis=-1, keepdims=True) + jnp.sum(p_ctx, axis=-1, keepdims=True)
    if sink is not None:
        den = den + jnp.exp(sink - m)
    o = (jnp.dot(p_loc.astype(bf16), v_loc, preferred_element_type=f32)
         + jnp.dot(p_ctx.astype(bf16), v_ctx, preferred_element_type=f32))
    return o / den


def _swa_kernel(q_ref, kp_ref, kc_ref, kn_ref, vp_ref, vc_ref, vn_ref, kx_ref, vx_ref, sink_ref, o_ref, *, n_ctx_blocks, seq):
    g = pl.program_id(1)
    i = g - n_ctx_blocks
    k_loc = jnp.concatenate([kp_ref[0], kc_ref[0], kn_ref[0]], axis=0)
    v_loc = jnp.concatenate([vp_ref[0], vc_ref[0], vn_ref[0]], axis=0)
    row = lax.broadcasted_iota(i32, (SWA_BLOCK, 3 * SWA_BLOCK), 0)
    col = lax.broadcasted_iota(i32, (SWA_BLOCK, 3 * SWA_BLOCK), 1)
    rel = col - SWA_BLOCK - row
    kpos = (i - 1) * SWA_BLOCK + col
    ok = (jnp.abs(rel) <= SWA_WINDOW) & (kpos >= 0) & (kpos < seq) & (i >= 0)
    pen = jnp.where(ok, 0.0, NEG)
    pen = jnp.concatenate([pen, pen], axis=0)
    for j in range(SWA_Q // LANE):
        q = _pair_rows(q_ref[0, :, j * LANE:(j + 1) * LANE])
        s_loc = lax.dot_general(q, k_loc, _NT, preferred_element_type=f32) + pen
        s_ctx = lax.dot_general(q, kx_ref[0], _NT, preferred_element_type=f32)
        sink = jnp.concatenate([jnp.broadcast_to(sink_ref[2 * j:2 * j + 1, 0:1], (SWA_BLOCK, 1)),
                                jnp.broadcast_to(sink_ref[2 * j + 1:2 * j + 2, 0:1], (SWA_BLOCK, 1))], axis=0)
        o = _joint_attention(s_loc, s_ctx, v_loc, vx_ref[0], sink)
        o_ref[0, :, j * LANE:(j + 1) * LANE] = _pair_merge(o).astype(bf16)


def _swa(sq, sk, sv, sink_rows, n_ctx):
    nb, ta, _ = sq.shape
    ncb = n_ctx // SWA_BLOCK
    nblk = ta // SWA_BLOCK
    nlat = nblk - ncb
    prev = lambda b, g: (b, jnp.clip(g - ncb - 1, 0, nlat - 1) + ncb, 0)
    cur = lambda b, g: (b, g, 0)
    nxt = lambda b, g: (b, jnp.clip(g - ncb + 1, 0, nlat - 1) + ncb, 0)
    kvs = lambda f: pl.BlockSpec((1, SWA_BLOCK, SWA_KV), f)
    ctx = pl.BlockSpec((1, n_ctx, SWA_KV), lambda b, g: (b, 0, 0))
    return pl.pallas_call(
        functools.partial(_swa_kernel, n_ctx_blocks=ncb, seq=ta - n_ctx),
        grid=(nb, nblk),
        in_specs=[pl.BlockSpec((1, SWA_BLOCK, SWA_Q), cur), kvs(prev), kvs(cur), kvs(nxt), kvs(prev), kvs(cur), kvs(nxt),
                  ctx, ctx, pl.BlockSpec((8, LANE), lambda b, g: (0, 0))],
        out_specs=pl.BlockSpec((1, SWA_BLOCK, SWA_Q), cur),
        out_shape=jax.ShapeDtypeStruct((nb, ta, SWA_Q), bf16),
        compiler_params=pltpu.CompilerParams(dimension_semantics=("arbitrary", "arbitrary")),
        name="swa_attention",
    )(sq, sk, sk, sk, sv, sv, sv, sk, sv, sink_rows)


def _nat_kernel(q_ref, k_ref, v_ref, bias_ref, o_ref, *, n_ctx, n_rows):
    g = pl.program_id(1)
    is_ctx = g == 0
    ctx_pen = jnp.where(is_ctx, NEG, 0.0)
    rows_per_tile = TM // GRID_W
    nwin = NAT_ROWS * GRID_W
    for rr in range(rows_per_tile):
        r = jnp.maximum((g - 1) * rows_per_tile + rr, 0)
        rs = jnp.clip(r - NAT_ROWS // 2, 0, n_rows - NAT_ROWS)
        dr0 = rs - r + NAT_ROWS - 1
        start = pl.multiple_of(n_ctx + rs * GRID_W, GRID_W)
        for p in range(NAT_W // LANE):
            ls = slice(p * LANE, (p + 1) * LANE)
            q = _pair_rows(q_ref[0, rr * GRID_W:(rr + 1) * GRID_W, ls])
            s_loc = (lax.dot_general(q, k_ref[0, pl.ds(start, nwin), ls], _NT, preferred_element_type=f32)
                     + bias_ref[p, dr0] + ctx_pen)
            s_ctx = lax.dot_general(q, k_ref[0, 0:n_ctx, ls], _NT, preferred_element_type=f32)
            o = _joint_attention(s_loc, s_ctx, v_ref[0, pl.ds(start, nwin), ls], v_ref[0, 0:n_ctx, ls])
            o_ref[0, rr * GRID_W:(rr + 1) * GRID_W, ls] = _pair_merge(o).astype(bf16)


def _nat(nq, nk, nv, bias, n_ctx):
    nb, ta, w = nq.shape
    nt = ta // TM
    whole = pl.BlockSpec((1, ta, w), lambda b, g: (b, 0, 0), pipeline_mode=pl.Buffered(1))
    return pl.pallas_call(
        functools.partial(_nat_kernel, n_ctx=n_ctx, n_rows=(ta - n_ctx) // GRID_W),
        grid=(nb, nt),
        in_specs=[pl.BlockSpec((1, TM, w), lambda b, g: (b, g, 0)), whole, whole,
                  pl.BlockSpec(bias.shape, lambda b, g: (0, 0, 0, 0), pipeline_mode=pl.Buffered(1))],
        out_specs=pl.BlockSpec((1, TM, w), lambda b, g: (b, g, 0)),
        out_shape=jax.ShapeDtypeStruct((nb, ta, w), bf16),
        compiler_params=pltpu.CompilerParams(dimension_semantics=("arbitrary", "arbitrary"),
                                             vmem_limit_bytes=VMEM_LIMIT_BYTES),
        name="nat_attention",
    )(nq, nk, nv, bias)


def _nat_bias_tables(rpb):
    c = np.arange(GRID_W)[:, None]
    kc = np.arange(GRID_W)[None, :]
    ws = np.clip(c - NAT_COLS // 2, 0, GRID_W - NAT_COLS)
    valid = (kc >= ws) & (kc < ws + NAT_COLS)
    col_idx = np.clip(kc - c + NAT_COLS - 1, 0, 2 * NAT_COLS - 2)
    dr = np.arange(NAT_ROWS)[:, None] + np.arange(NAT_ROWS)[None, :]
    t = rpb.astype(f32)[:, dr][:, :, :, col_idx]
    t = jnp.where(jnp.asarray(valid)[None, None, None], t, NEG)
    t = t.transpose(0, 1, 3, 2, 4).reshape(NAT_HEADS // 2, 2, NAT_ROWS, GRID_W, NAT_ROWS * GRID_W)
    return t.transpose(0, 2, 1, 3, 4).reshape(NAT_HEADS // 2, NAT_ROWS, 2 * GRID_W, NAT_ROWS * GRID_W)


def _outproj_kernel(o_ref, sg_ref, hnw_ref, bd_ref, so_ref, no_ref, wa_ref, wb_ref, wn_ref, mod_ref, x_ref, y_ref):
    oo = o_ref[0, 0] + o_ref[1, 0]
    hi, lo = _split2(oo * oo)
    bd = bd_ref[...]
    msq = (jnp.dot(hi, bd, preferred_element_type=f32) + jnp.dot(lo, bd, preferred_element_type=f32)) * (1.0 / HEAD_DIM)
    a = oo * lax.rsqrt(msq + EPS) * hnw_ref[...] * sg_ref[0]
    y = (jnp.dot(a.astype(bf16), wa_ref[...], preferred_element_type=f32)
         + jnp.dot(so_ref[0], wb_ref[...], preferred_element_type=f32)
         + jnp.dot(no_ref[0], wn_ref[...], preferred_element_type=f32))
    y_ref[0] = x_ref[0] + mod_ref[0, 0, 2:3, :] * y


def _outproj(o, sg, hnw, bd, so, no, wa, wb, wn, mod, xall):
    nb, ta, d = xall.shape
    nt = ta // TM
    bs = lambda width: pl.BlockSpec((1, TM, width), lambda b, i: (b, i, 0))
    full = lambda a: pl.BlockSpec(a.shape, lambda b, i: (0,) * a.ndim)
    return pl.pallas_call(
        _outproj_kernel,
        grid=(nb, nt),
        in_specs=[pl.BlockSpec((2, 1, TM, HG_WIDTH), lambda b, i: (0, b, i, 0)), bs(HG_WIDTH), full(hnw), full(bd),
                  bs(SWA_Q), bs(NAT_W), full(wa), full(wb), full(wn),
                  pl.BlockSpec((1, 1, 6, d), lambda b, i: (b, jnp.minimum(i, 1), 0, 0)), bs(d)],
        out_specs=bs(d),
        out_shape=jax.ShapeDtypeStruct(xall.shape, f32),
        compiler_params=pltpu.CompilerParams(dimension_semantics=("arbitrary", "arbitrary"),
                                             vmem_limit_bytes=VMEM_LIMIT_BYTES),
        name="out_proj",
    )(o, sg, hnw, bd, so, no, wa, wb, wn, mod, xall)


def _top16_rows(vals, payload=None):
    n = vals.shape[0]
    iota = lax.broadcasted_iota(i32, vals.shape, 0)
    best, picked = [], []
    for _ in range(PEER_TOPK):
        m = jnp.max(vals, axis=0, keepdims=True)
        idx = jnp.min(jnp.where(vals == m, iota, n), axis=0, keepdims=True)
        hit = iota == idx
        best.append(m)
        picked.append(idx if payload is None else jnp.max(jnp.where(hit, payload, -1), axis=0, keepdims=True))
        vals = jnp.where(hit, -jnp.inf, vals)
    return jnp.concatenate(best, axis=0), jnp.concatenate(picked, axis=0)


def _product_candidates(first, second, combine):
    half = PEER_TOPK // 2
    rows = [combine(first[0:1, :], second)]
    rows += [combine(first[a:a + 1, :], second[0:half, :]) for a in range(1, half)]
    rows.append(combine(first[half:, :], second[0:1, :]))
    return jnp.concatenate(rows, axis=0)


def _route_kernel(x_ref, nw_ref, mod_ref, wq_ref, sub_ref, h_ref, idx_ref, g_ref):
    h = _norm_mod(x_ref[0], nw_ref[...], mod_ref[0, 0, 3:4, :], mod_ref[0, 0, 4:5, :])
    h_ref[0] = h
    q = jnp.dot(h.astype(bf16), wq_ref[...], preferred_element_type=f32).astype(bf16)
    half = PEER_DK // 2
    ids, gates = [], []
    for hh in range(PEER_HEADS):
        top_s, top_i = [], []
        for p in range(2):
            c0 = (hh * 2 + p) * half
            s = lax.dot_general(sub_ref[hh, p], q[:, c0:c0 + half], _NT, preferred_element_type=f32)
            ts, ti = _top16_rows(s)
            top_s.append(ts)
            top_i.append(ti)
        cand_s = _product_candidates(top_s[0], top_s[1], lambda a, b: a + b)
        cand_i = _product_candidates(top_i[0], top_i[1], lambda a, b: a * PEER_NKEYS + b)
        best, eid = _top16_rows(cand_s, cand_i)
        ids.append(eid)
        e = jnp.exp(best - best[0:1, :])
        gates.append(e / jnp.sum(e, axis=0, keepdims=True))
    idx_ref[0] = jnp.concatenate(ids, axis=0).T
    g_ref[0] = jnp.concatenate(gates, axis=0).T


def _route(xall, nw, mod, wq, sub):
    nb, ta, d = xall.shape
    nt = ta // TM
    full = lambda a: pl.BlockSpec(a.shape, lambda b, i: (0,) * a.ndim)
    return pl.pallas_call(
        _route_kernel,
        grid=(nb, nt),
        in_specs=[pl.BlockSpec((1, TM, d), lambda b, i: (b, i, 0)), full(nw),
                  pl.BlockSpec((1, 1, 6, d), lambda b, i: (b, jnp.minimum(i, 1), 0, 0)), full(wq), full(sub)],
        out_specs=[pl.BlockSpec((1, TM, d), lambda b, i: (b, i, 0)),
                   pl.BlockSpec((1, TM, PEER_SLOTS), lambda b, i: (b, i, 0)),
                   pl.BlockSpec((1, TM, PEER_SLOTS), lambda b, i: (b, i, 0))],
        out_shape=[jax.ShapeDtypeStruct(xall.shape, f32), jax.ShapeDtypeStruct((nb, ta, PEER_SLOTS), i32),
                   jax.ShapeDtypeStruct((nb, ta, PEER_SLOTS), f32)],
        compiler_params=pltpu.CompilerParams(dimension_semantics=("arbitrary", "arbitrary"),
                                             vmem_limit_bytes=VMEM_LIMIT_BYTES),
        name="peer_route",
    )(xall, nw, mod, wq, sub)


def _expert_kernel(idx_ref, idx_next_ref, h_ref, g_ref, x_ref, mod_ref, tab_hbm, o_ref, buf, sem):
    d = h_ref.shape[-1]
    nct = d // LANE
    groups = PEER_SLOTS // 8

    per_piece = PEER_SLOTS // nct

    def issue(ids_ref, t, slot, piece):
        for k in range(piece * per_piece, (piece + 1) * per_piece):
            pltpu.make_async_copy(tab_hbm.at[ids_ref[t, k]], buf.at[slot, k // 8, :, pl.ds(k % 8, 1), :],
                                  sem.at[slot]).start(priority=k % 2)

    def wait(slot):
        pltpu.make_async_copy(buf.at[slot], buf.at[slot], sem.at[slot]).wait()

    def cols(slot, c, high):
        words = jnp.concatenate([jnp.concatenate([buf[slot, kg, c], buf[slot, kg, c + 1]], axis=1)
                                 for kg in range(groups)], axis=0)
        bits = (words & jnp.uint32(0xFFFF0000)) if high else (words << 16)
        return lax.bitcast_convert_type(bits, f32).astype(bf16)

    not_last = pl.program_id(0) < pl.num_programs(0) - 1

    def step(t, w8, do_issue, do_u, do_v):
        nslot = lax.rem(t + 1, PEER_NBUF)
        slot = lax.rem(t, PEER_NBUF)
        islot = lax.rem(t + PEER_AHEAD, PEER_NBUF)
        if do_issue == "next":
            @pl.when(not_last)
            def _():
                for j in range(nct):
                    issue(idx_next_ref, t + PEER_AHEAD - PEER_TB, islot, j)
        if do_u:
            wait(nslot)
            h8 = jnp.broadcast_to(h_ref[pl.ds(t + 1, 1), :], (8, d)).astype(bf16)
            gate = g_ref[pl.ds(t + 1, 1), :]
            a = jnp.zeros((8, PEER_SLOTS), f32)
        ys = []
        for j in range(nct):
            if 2 * j < nct:
                c = 2 * j
                if do_u:
                    a = a + lax.dot_general(h8[:, c * LANE:(c + 2) * LANE], cols(nslot, c, True), _NT, preferred_element_type=f32)
                if do_v:
                    ys.append(jnp.dot(w8, cols(slot, c, False), preferred_element_type=f32)[0:1, :])
            if do_issue is True:
                issue(idx_ref, t + PEER_AHEAD, islot, j)
        if do_v:
            o_ref[pl.ds(t, 1), :] = jnp.concatenate(ys, axis=1)
        if do_u:
            a1 = a[0:1, :]
            w = gate * (0.5 * a1 * (1.0 + lax.erf(a1 * (2.0 ** -0.5))))
            w8 = jnp.broadcast_to(w, (8, PEER_SLOTS)).astype(bf16)
        return w8

    @pl.when(pl.program_id(0) == 0)
    def _():
        def prologue(t, carry):
            for j in range(nct):
                issue(idx_ref, t, t, j)
            return carry

        lax.fori_loop(0, PEER_AHEAD, prologue, 0)

    w8 = step(-1, jnp.zeros((8, PEER_SLOTS), bf16), False, True, False)
    w8 = lax.fori_loop(0, PEER_TB - PEER_AHEAD, lambda t, w: step(t, w, True, True, True), w8)
    w8 = lax.fori_loop(PEER_TB - PEER_AHEAD, PEER_TB - 1, lambda t, w: step(t, w, "next", True, True), w8)
    step(PEER_TB - 1, w8, "next", False, True)
    o_ref[...] = x_ref[...] + mod_ref[0, 0, 5:6, :] * o_ref[...]


def _mod_index(tokens_per_step, ta, n_ctx, first_step=0):
    spb = ta // tokens_per_step
    cs = n_ctx // tokens_per_step
    return lambda i: ((i + first_step) // spb, jnp.where((i + first_step) % spb < cs, 0, 1), 0, 0)


def _experts(idx, h, g, xflat, mod, tab, ta, n_ctx, nblocks):
    n, d = xflat.shape
    tokb = lambda width: pl.BlockSpec((PEER_TB, width), lambda i: (i, 0))
    return pl.pallas_call(
        _expert_kernel,
        grid=(nblocks,),
        in_specs=[pl.BlockSpec((PEER_TB, PEER_SLOTS), lambda i: (i, 0), memory_space=pltpu.SMEM),
                  pl.BlockSpec((PEER_TB, PEER_SLOTS), lambda i: (jnp.minimum(i + 1, nblocks - 1), 0), memory_space=pltpu.SMEM),
                  tokb(d), tokb(PEER_SLOTS), tokb(d),
                  pl.BlockSpec((1, 1, 6, d), _mod_index(PEER_TB, ta, n_ctx)),
                  pl.BlockSpec(memory_space=pl.ANY)],
        out_specs=tokb(d),
        out_shape=jax.ShapeDtypeStruct((n, d), f32),
        input_output_aliases={4: 0},
        scratch_shapes=[pltpu.VMEM((PEER_NBUF, PEER_SLOTS // 8, d // LANE, 8, LANE), jnp.uint32),
                        pltpu.SemaphoreType.DMA((PEER_NBUF,))],
        compiler_params=pltpu.CompilerParams(dimension_semantics=("arbitrary",), vmem_limit_bytes=VMEM_LIMIT_BYTES),
        name="peer_experts",
    )(idx, idx, h, g, xflat, mod, tab)


def _sc_gather(table, idx):
    info = plsc.get_sparse_core_info()
    nc = info.num_cores
    nw = nc * info.num_subcores
    r = idx.shape[0]
    per_w = r // nw
    nch = per_w // SC_CHUNK
    assert per_w * nw == r and nch * SC_CHUNK == per_w and nch % 2 == 0
    rows_shape = (SC_CHUNK,) + table.shape[1:]
    mesh = plsc.VectorSubcoreMesh(core_axis_name="c", subcore_axis_name="s")

    @functools.partial(
        pl.kernel, mesh=mesh, out_type=jax.ShapeDtypeStruct((r,) + table.shape[1:], table.dtype),
        scratch_types=[pltpu.VMEM((2, SC_CHUNK), i32), pltpu.VMEM((2,) + rows_shape, table.dtype),
                       pltpu.SemaphoreType.DMA((2,)), pltpu.SemaphoreType.DMA((2,)), pltpu.SemaphoreType.DMA((2,))])
    def gather_kernel(table_hbm, idx_hbm, out_hbm, idx_v, rows_v, isem, gsem, wsem):
        base = (lax.axis_index("s") * nc + lax.axis_index("c")) * per_w

        def rows_of(j):
            return pl.ds(pl.multiple_of(base + j * SC_CHUNK, 8), SC_CHUNK)

        def fetch_idx(j, b):
            return pltpu.make_async_copy(idx_hbm.at[rows_of(j)], idx_v.at[b], isem.at[b])

        def gather(b):
            return pltpu.make_async_copy(table_hbm.at[idx_v.at[b]], rows_v.at[b], gsem.at[b])

        def write(j, b):
            return pltpu.make_async_copy(rows_v.at[b], out_hbm.at[rows_of(j)], wsem.at[b])

        fetch_idx(0, 0).start()
        fetch_idx(1, 1).start()
        fetch_idx(0, 0).wait()
        gather(0).start()

        @pl.loop(0, nch, step=2)
        def _(j0):
            for b in (0, 1):
                j = j0 + b

                @pl.when(j >= 1)
                def _():
                    write(j - 1, 1 - b).wait()

                @pl.when(j + 1 < nch)
                def _():
                    fetch_idx(j + 1, 1 - b).wait()
                    gather(1 - b).start()

                gather(b).wait()

                @pl.when(j + 2 < nch)
                def _():
                    fetch_idx(j + 2, b).start()

                write(j, b).start()

        write(nch - 1, 1).wait()

    return gather_kernel(table, idx)


def _expert_dense_kernel(st_ref, h_ref, g_ref, x_ref, mod_ref, o_ref):
    d = h_ref.shape[-1]
    nct = d // LANE
    groups = PEER_SLOTS // 8
    rows_per_token = PEER_SLOTS * nct

    per_row = LANE // nct
    lane = lax.broadcasted_iota(i32, (nct, LANE), 1)
    sub = lax.broadcasted_iota(i32, (nct, LANE), 0)
    lane8 = lax.broadcasted_iota(i32, (8, LANE), 1)
    sub8 = lax.broadcasted_iota(i32, (8, LANE), 0)

    def u_stage(t):
        base = pl.multiple_of(t * rows_per_token, rows_per_token)
        hrow = h_ref[pl.ds(t, 1), :]
        hch = jnp.concatenate([hrow[:, c * LANE:(c + 1) * LANE] for c in range(nct)], axis=0).astype(bf16)
        words = st_ref[pl.ds(base, rows_per_token), :]
        u = lax.bitcast_convert_type(words & jnp.uint32(0xFFFF0000), f32).astype(bf16)
        q = lax.dot_general(hch, u, _NT, preferred_element_type=f32)
        x = jnp.concatenate([jnp.sum(jnp.where(lane % nct == sub, q[:, r * LANE:(r + 1) * LANE], 0.0), axis=0, keepdims=True)
                             for r in range(nct)], axis=0)
        shift = 1
        while shift < nct:
            x = x + pltpu.roll(x, LANE - shift, 1)
            shift *= 2
        return 0.5 * x * (1.0 + lax.erf(x * (2.0 ** -0.5)))

    def v_stage(t, act):
        base = t * rows_per_token
        gb = jnp.broadcast_to(g_ref[pl.ds(t, 1), :], (8, LANE))
        acc = [None] * nct
        for kg in range(groups):
            k0 = kg * 8
            ab = jnp.broadcast_to(act[k0 // per_row:k0 // per_row + 1, :], (8, LANE))
            wcol = (jnp.sum(jnp.where(lane8 == (k0 % per_row + sub8) * nct, ab, 0.0), axis=-1, keepdims=True)
                    * jnp.sum(jnp.where(lane8 == k0 + sub8, gb, 0.0), axis=-1, keepdims=True))
            for c in range(nct):
                words = st_ref[pl.ds(base + kg * 8 * nct + c, 8, stride=nct), :]
                term = lax.bitcast_convert_type(words << 16, f32) * wcol
                acc[c] = term if acc[c] is None else acc[c] + term
        o_ref[pl.ds(t, 1), :] = jnp.concatenate([jnp.sum(a, axis=0, keepdims=True) for a in acc], axis=1)

    def body(t, w_row):
        w_next = u_stage(t + 1)
        v_stage(t, w_row)
        return w_next

    w_row = lax.fori_loop(0, PEER_DENSE_TB - 1, body, u_stage(0), unroll=8)
    v_stage(PEER_DENSE_TB - 1, w_row)
    o_ref[...] = x_ref[...] + mod_ref[0, 0, 5:6, :] * o_ref[...]


def _experts_dense(staged, h, g, xflat, mod, ta, n_ctx, first_block):
    n, d = xflat.shape
    rows_per_token = PEER_SLOTS * (d // LANE)
    steps = staged.shape[0] // (PEER_DENSE_TB * rows_per_token)
    off = first_block * PEER_TB // PEER_DENSE_TB
    tokb = lambda width: pl.BlockSpec((PEER_DENSE_TB, width), lambda i: (i + off, 0))
    return pl.pallas_call(
        _expert_dense_kernel,
        grid=(steps,),
        in_specs=[pl.BlockSpec((PEER_DENSE_TB * rows_per_token, LANE), lambda i: (i, 0)),
                  tokb(d), tokb(PEER_SLOTS), tokb(d),
                  pl.BlockSpec((1, 1, 6, d), _mod_index(PEER_DENSE_TB, ta, n_ctx, off))],
        out_specs=tokb(d),
        out_shape=jax.ShapeDtypeStruct((n, d), f32),
        input_output_aliases={3: 0},
        compiler_params=pltpu.CompilerParams(dimension_semantics=("arbitrary",), vmem_limit_bytes=VMEM_LIMIT_BYTES),
        name="peer_experts_dense",
    )(staged, h, g, xflat, mod)


def _expert_table(u, v):
    half = lambda a: lax.bitcast_convert_type(a.astype(bf16), jnp.uint16).astype(jnp.uint32)
    return (half(u) << 16) | half(v)


def _final_norm_kernel(x_ref, w_ref, o_ref):
    x = x_ref[0]
    o_ref[0] = x * lax.rsqrt(jnp.mean(x * x, axis=-1, keepdims=True) + EPS) * w_ref[...]


def _final_norm(xall, w, n_ctx):
    nb, ta, d = xall.shape
    off = n_ctx // TM
    return pl.pallas_call(
        _final_norm_kernel,
        grid=(nb, (ta - n_ctx) // TM),
        in_specs=[pl.BlockSpec((1, TM, d), lambda b, i: (b, i + off, 0)), pl.BlockSpec((1, d), lambda b, i: (0, 0))],
        out_specs=pl.BlockSpec((1, TM, d), lambda b, i: (b, i, 0)),
        out_shape=jax.ShapeDtypeStruct((nb, ta - n_ctx, d), f32),
        compiler_params=pltpu.CompilerParams(dimension_semantics=("arbitrary", "arbitrary")),
        name="final_norm",
    )(xall, w.reshape(1, d))


def _rope_tables(n_ctx, seq):
    nf = HEAD_DIM // 4
    pos = np.arange(seq)
    inv = ROPE_BASE ** (-np.arange(nf, dtype=np.float32) / nf)
    ar = (pos // GRID_W).astype(np.float32)[:, None] * inv
    ac = (pos % GRID_W).astype(np.float32)[:, None] * inv
    cos = np.concatenate([np.cos(ar), np.cos(ar), np.cos(ac), np.cos(ac)], axis=-1)
    sin = np.concatenate([-np.sin(ar), np.sin(ar), -np.sin(ac), np.sin(ac)], axis=-1)
    cos = np.concatenate([np.ones((n_ctx, HEAD_DIM), np.float32), cos.astype(np.float32)], axis=0)
    sin = np.concatenate([np.zeros((n_ctx, HEAD_DIM), np.float32), sin.astype(np.float32)], axis=0)
    return jnp.asarray(np.stack([np.tile(cos, (1, 2)), np.tile(sin, (1, 2))]))


def _tri_tables():
    r = np.arange(TM)[:, None]
    c = np.arange(TM)[None, :]
    same = (r // HG_SUB) == (c // HG_SUB)
    return jnp.asarray(np.stack([same & (c <= r), same & (c >= r)]).astype(np.float32), dtype=bf16)


def _block_diag_ones():
    r = np.arange(HG_WIDTH)
    return jnp.asarray((r[:, None] // HEAD_DIM == r[None, :] // HEAD_DIM).astype(np.float32), dtype=bf16)


def _swa_perm():
    return np.concatenate([np.arange(h * HEAD_DIM, (h + 1) * HEAD_DIM) for h in SWA_HEAD_ORDER])


def kernel(x, c, ctx, c_ctx, w_mod, b_mod, norm_mix_w, norm_ffn_w, w_in, hgrn_lb_logits, hgrn_norm_w, swa_sink, nat_rpb,
           w_out, peer_wq, peer_subkeys, peer_u, peer_v, final_norm_w):
    nb, seq, d = x.shape
    n_ctx = ctx.shape[1]
    depth = w_in.shape[0]
    ta = n_ctx + seq
    assert n_ctx == TM and seq % TM == 0 and seq % (GRID_W * NAT_ROWS) == 0 and d % LANE == 0
    assert w_in.shape[-1] == PROJ_WIDTH and peer_wq.shape[-1] == PEER_HEADS * PEER_DK

    mods = _mods(c, c_ctx, w_mod, b_mod)
    cs = _rope_tables(n_ctx, seq)
    tri = _tri_tables()
    bd = _block_diag_ones()
    perm = _swa_perm()
    q0 = 5 * HG_WIDTH
    col_perm = np.concatenate([np.arange(q0), q0 + perm, np.arange(q0 + SWA_Q, PROJ_WIDTH)])
    w_in_b = w_in[:, :, col_perm].astype(bf16)
    w_out_b = w_out.astype(bf16)
    wa = w_out_b[:, :HG_WIDTH]
    wb = w_out_b[:, HG_WIDTH:HG_WIDTH + SWA_Q][:, perm]
    wn = w_out_b[:, HG_WIDTH + SWA_Q:]
    wq_b = peer_wq.astype(bf16)
    sub_b = peer_subkeys.astype(bf16)
    lb = jnp.cumsum(jax.nn.softmax(hgrn_lb_logits.astype(f32), axis=1), axis=1)
    lb = lb - lb[:, :1]
    gp = jnp.stack([jnp.log(lb[0]), jnp.log1p(-lb[0]), 1.0 - lb[0], jnp.log(lb[1]), jnp.log1p(-lb[1]), 1.0 - lb[1],
                    jnp.zeros_like(lb[0]), jnp.zeros_like(lb[0])], axis=1)
    sink_rows = jnp.zeros((depth, 8, LANE), f32).at[:, :SWA_HEADS].set(
        jnp.broadcast_to(swa_sink.astype(f32)[:, list(SWA_HEAD_ORDER), None], (depth, SWA_HEADS, LANE)))

    words = _expert_table(peer_u, peer_v)
    n_exp = words.shape[1]
    tab = words.reshape(depth, n_exp, d // LANE, 1, LANE)
    tab_sc = words.reshape(depth, n_exp, d // LANE, LANE)
    n_blocks = nb * ta // PEER_TB
    n_tc = n_blocks - int(n_blocks * PEER_SC_SHARE + 0.5)
    assert ta % PEER_TB == 0 and PEER_TB % PEER_NBUF == 0 and PEER_TB % PEER_DENSE_TB == 0 and 0 < n_tc < n_blocks

    xall = jnp.concatenate([ctx, x], axis=1)
    for l in range(depth):
        mod = mods[l]
        hq, cum, kk, hv, sg, sq, sk, sv, nq, nk, nv = _proj(xall, norm_mix_w[l].reshape(1, d), mod, w_in_b[l], cs, gp[l], tri)
        o = _hgrn_scan(hq, cum, kk, hv, bd)
        so = _swa(sq, sk, sv, sink_rows[l], n_ctx)
        no = _nat(nq, nk, nv, _nat_bias_tables(nat_rpb[l]), n_ctx)
        xall = _outproj(o, sg, hgrn_norm_w[l].reshape(1, HG_WIDTH), bd, so, no, wa[l], wb[l], wn[l], mod, xall)
        h2, idx, g = _route(xall, norm_ffn_w[l].reshape(1, d), mod, wq_b[l], sub_b[l])
        idx, h2, g = idx.reshape(nb * ta, PEER_SLOTS), h2.reshape(nb * ta, d), g.reshape(nb * ta, PEER_SLOTS)
        staged = _sc_gather(tab_sc[l], idx[n_tc * PEER_TB:].reshape(-1))
        xf = _experts(idx, h2, g, xall.reshape(nb * ta, d), mod, tab[l], ta, n_ctx, n_tc)
        xf = _experts_dense(staged.reshape(-1, LANE), h2, g, xf, mod, ta, n_ctx, n_tc)
        xall = xf.reshape(nb, ta, d)
    return _final_norm(xall, final_norm_w, n_ctx)
```

```python
import functools

import numpy as np
import jax
import jax.numpy as jnp
from jax import lax
from jax.experimental import pallas as pl
from jax.experimental.pallas import tpu as pltpu
from jax.experimental.pallas import tpu_sc as plsc

f32 = jnp.float32
bf16 = jnp.bfloat16
i32 = jnp.int32

LANE = 128
VMEM_LIMIT_BYTES = 56 * 1024 * 1024

GRID_W = 64
HEAD_DIM = 64
EPS = 1e-6
HG_HEADS = 4
HG_WIDTH = 256
HG_SUB = 16
SWA_HEADS = 6
SWA_KV_HEADS = 2
SWA_WINDOW = 128
SWA_BLOCK = 128
ROPE_BASE = 10000.0
NAT_HEADS = 6
NAT_ROWS = 8
NAT_COLS = 16
SWA_Q = SWA_HEADS * HEAD_DIM
SWA_KV = SWA_KV_HEADS * HEAD_DIM
NAT_W = NAT_HEADS * HEAD_DIM
PROJ_WIDTH = 5 * HG_WIDTH + SWA_Q + 2 * SWA_KV + 3 * NAT_W
PEER_HEADS = 8
PEER_NKEYS = 128
PEER_DK = 256
PEER_TOPK = 16
PEER_SLOTS = PEER_HEADS * PEER_TOPK

TM = 256
PEER_TB = 128
PEER_NBUF = 16
PEER_AHEAD = PEER_NBUF - 1
PEER_SC_SHARE = 0.508
PEER_DENSE_TB = 32
SC_CHUNK = 32
NEG = -1e30

SWA_HEAD_ORDER = (0, 3, 1, 4, 2, 5)

_NT = (((1,), (1,)), ((), ()))
_TN = (((0,), (0,)), ((), ()))


def _silu(x):
    return x * (1.0 / (1.0 + jnp.exp(-x)))


def _split2(x):
    hi = x.astype(bf16)
    lo = (x - hi.astype(f32)).astype(bf16)
    return hi, lo


def _mods_kernel(sc_ref, w_ref, b_ref, o_ref):
    s = _silu(sc_ref[...])
    o_ref[0] = jnp.dot(s.astype(bf16), w_ref[0].astype(bf16), preferred_element_type=f32) + b_ref[0]


def _mods(c, c_ctx, w_mod, b_mod):
    depth, d, d6 = w_mod.shape
    nb = c.shape[0]
    rows = jnp.zeros((8, d), f32).at[:nb].set(c).at[nb].set(c_ctx)
    tn = d6 // 4
    out = pl.pallas_call(
        _mods_kernel,
        grid=(depth, 4),
        in_specs=[pl.BlockSpec((8, d), lambda l, j: (0, 0)),
                  pl.BlockSpec((1, d, tn), lambda l, j: (l, 0, j)),
                  pl.BlockSpec((1, 1, tn), lambda l, j: (l, 0, j))],
        out_specs=pl.BlockSpec((1, 8, tn), lambda l, j: (l, 0, j)),
        out_shape=jax.ShapeDtypeStruct((depth, 8, d6), f32),
        compiler_params=pltpu.CompilerParams(dimension_semantics=("arbitrary", "arbitrary"),
                                             vmem_limit_bytes=VMEM_LIMIT_BYTES),
        name="adaln_mods",
    )(rows, w_mod, b_mod.reshape(depth, 1, d6))
    lat = out[:, :nb].reshape(depth, nb, 6, d)
    ctx = jnp.broadcast_to(out[:, nb].reshape(depth, 1, 6, d), (depth, nb, 6, d))
    return jnp.stack([ctx, lat], axis=2)


def _norm_mod(x, nw, shift, scale):
    y = x * lax.rsqrt(jnp.mean(x * x, axis=-1, keepdims=True) + EPS) * nw
    return y * (1.0 + scale) + shift


def _rope(t, cos, sins):
    lane = lax.broadcasted_iota(i32, t.shape, 1)
    sw = jnp.where((lane % 32) < 16, pltpu.roll(t, LANE - 16, 1), pltpu.roll(t, 16, 1))
    return t * cos + sw * sins


def _hgrn_gates(z, log_lb, log1m_lb, one_m_lb):
    log_sig = jnp.minimum(z, 0.0) - jnp.log1p(jnp.exp(-jnp.abs(z)))
    c = log1m_lb + log_sig
    m = jnp.maximum(log_lb, c)
    log_f = m + jnp.log1p(jnp.exp(-jnp.abs(log_lb - c)))
    k = one_m_lb * (1.0 / (1.0 + jnp.exp(z)))
    return log_f, k


def _cumsum16(tri, log_f):
    h1 = log_f.astype(bf16)
    r1 = log_f - h1.astype(f32)
    h2 = r1.astype(bf16)
    h3 = (r1 - h2.astype(f32)).astype(bf16)
    return (jnp.dot(tri, h1, preferred_element_type=f32) + jnp.dot(tri, h2, preferred_element_type=f32)
            + jnp.dot(tri, h3, preferred_element_type=f32))


def _proj_kernel(x_ref, nw_ref, mod_ref, w_ref, cs_ref, gp_ref, tri_ref,
                 hq_ref, cum_ref, kk_ref, hv_ref, sg_ref, sq_ref, sk_ref, sv_ref, nq_ref, nk_ref, nv_ref):
    h = _norm_mod(x_ref[0], nw_ref[...], mod_ref[0, 0, 0:1, :], mod_ref[0, 0, 1:2, :])
    p = jnp.dot(h.astype(bf16), w_ref[...], preferred_element_type=f32)
    w = HG_WIDTH
    hq_ref[0] = _silu(p[:, 0:w])
    for d in range(2):
        log_f, k = _hgrn_gates(p[:, (1 + d) * w:(2 + d) * w], gp_ref[3 * d:3 * d + 1, :],
                               gp_ref[3 * d + 1:3 * d + 2, :], gp_ref[3 * d + 2:3 * d + 3, :])
        cum_ref[d, 0] = _cumsum16(tri_ref[d], log_f)
        kk_ref[d, 0] = k
    hv_ref[0] = p[:, 3 * w:4 * w]
    sg_ref[0] = _silu(p[:, 4 * w:5 * w])
    cos, sins = cs_ref[0], cs_ref[1]
    o = 5 * w
    scale = HEAD_DIM ** -0.5
    for j in range(SWA_Q // LANE):
        sq_ref[0, :, j * LANE:(j + 1) * LANE] = (_rope(p[:, o + j * LANE:o + (j + 1) * LANE], cos, sins) * scale).astype(bf16)
    o += SWA_Q
    sk_ref[0] = _rope(p[:, o:o + SWA_KV], cos, sins).astype(bf16)
    o += SWA_KV
    sv_ref[0] = p[:, o:o + SWA_KV].astype(bf16)
    o += SWA_KV
    nq_ref[0] = (p[:, o:o + NAT_W] * scale).astype(bf16)
    o += NAT_W
    nk_ref[0] = p[:, o:o + NAT_W].astype(bf16)
    o += NAT_W
    nv_ref[0] = p[:, o:o + NAT_W].astype(bf16)


def _proj(xall, nw, mod, w_in, cs, gp, tri):
    nb, ta, d = xall.shape
    nt = ta // TM
    tok = lambda width, dt: jax.ShapeDtypeStruct((nb, ta, width), dt)
    tok2 = jax.ShapeDtypeStruct((2, nb, ta, HG_WIDTH), f32)
    bs = lambda width: pl.BlockSpec((1, TM, width), lambda b, i: (b, i, 0))
    bs2 = pl.BlockSpec((2, 1, TM, HG_WIDTH), lambda b, i: (0, b, i, 0))
    return pl.pallas_call(
        _proj_kernel,
        grid=(nb, nt),
        in_specs=[pl.BlockSpec((1, TM, d), lambda b, i: (b, i, 0)),
                  pl.BlockSpec((1, d), lambda b, i: (0, 0)),
                  pl.BlockSpec((1, 1, 6, d), lambda b, i: (b, jnp.minimum(i, 1), 0, 0)),
                  pl.BlockSpec((d, PROJ_WIDTH), lambda b, i: (0, 0)),
                  pl.BlockSpec((2, TM, LANE), lambda b, i: (0, i, 0)),
                  pl.BlockSpec((8, HG_WIDTH), lambda b, i: (0, 0)),
                  pl.BlockSpec((2, TM, TM), lambda b, i: (0, 0, 0))],
        out_specs=[bs(HG_WIDTH), bs2, bs2, bs(HG_WIDTH), bs(HG_WIDTH), bs(SWA_Q), bs(SWA_KV), bs(SWA_KV),
                   bs(NAT_W), bs(NAT_W), bs(NAT_W)],
        out_shape=[tok(HG_WIDTH, f32), tok2, tok2, tok(HG_WIDTH, f32), tok(HG_WIDTH, f32), tok(SWA_Q, bf16),
                   tok(SWA_KV, bf16), tok(SWA_KV, bf16), tok(NAT_W, bf16), tok(NAT_W, bf16), tok(NAT_W, bf16)],
        compiler_params=pltpu.CompilerParams(dimension_semantics=("arbitrary", "arbitrary"),
                                             vmem_limit_bytes=VMEM_LIMIT_BYTES),
        name="norm_in_proj",
    )(xall, nw, mod, w_in, cs, gp, tri)


def _hgrn_kernel(q_ref, cum_ref, k_ref, v_ref, bd_ref, o_ref, st_ref):
    d = pl.program_id(1)
    fwd = d == 0

    @pl.when(pl.program_id(2) == 0)
    def _():
        st_ref[...] = jnp.zeros_like(st_ref)

    bd = bd_ref[...]
    half = HG_WIDTH // 2
    quad_mask = bd[0:half, 0:half] > 0
    sgn = jnp.where(fwd, 1, -1)
    t_sgn = lax.broadcasted_iota(i32, (HG_SUB, 1), 0) * sgn
    nsteps = TM // HG_SUB

    def step(i, carry):
        sc = jnp.where(fwd, i, nsteps - 1 - i)
        r0 = pl.multiple_of(sc * HG_SUB, HG_SUB)
        q = q_ref[0, pl.ds(r0, HG_SUB), :]
        b = cum_ref[0, 0, pl.ds(r0, HG_SUB), :]
        k = k_ref[0, 0, pl.ds(r0, HG_SUB), :]
        v = v_ref[0, pl.ds(r0, HG_SUB), :]
        b_end = jnp.where(fwd, b[HG_SUB - 1:HG_SUB, :], b[0:1, :])
        qd = (q * jnp.exp(b)).astype(bf16)
        st = [st_ref[p] for p in range(2)]
        o = jnp.concatenate([lax.dot_general(qd[:, p * half:(p + 1) * half], st[p].astype(bf16), _NT, preferred_element_type=f32)
                             for p in range(2)], axis=1)
        rows = []
        for s in range(HG_SUB):
            valid = t_sgn >= s * sgn
            decay = jnp.exp(jnp.minimum(b - b[s:s + 1, :], 0.0))
            rows.append(jnp.where(valid, decay * q * k[s:s + 1, :], 0.0))
        hi, lo = _split2(jnp.concatenate(rows, axis=0))
        pe = jnp.dot(hi, bd, preferred_element_type=f32) + jnp.dot(lo, bd, preferred_element_type=f32)
        for s in range(HG_SUB):
            o = o + pe[s * HG_SUB:(s + 1) * HG_SUB, :] * v[s:s + 1, :]
        o_ref[0, 0, pl.ds(r0, HG_SUB), :] = o
        kd = (k * jnp.exp(b_end - b)).astype(bf16)
        vb = v.astype(bf16)
        for p in range(2):
            sl = slice(p * half, (p + 1) * half)
            kv_t = lax.dot_general(vb[:, sl], kd[:, sl], _TN, preferred_element_type=f32)
            st_ref[p] = st[p] * jnp.exp(b_end[:, sl]) + jnp.where(quad_mask, kv_t, 0.0)
        return carry

    lax.fori_loop(0, nsteps, step, 0, unroll=8)


def _hgrn_scan(hq, cum, kk, hv, bd):
    nb, ta, w = hq.shape
    nt = ta // TM

    def blk(d, j):
        return jnp.where(d == 0, j, jnp.where(j == 0, 0, nt - j))

    return pl.pallas_call(
        _hgrn_kernel,
        grid=(nb, 2, nt),
        in_specs=[pl.BlockSpec((1, TM, w), lambda b, d, j: (b, blk(d, j), 0)),
                  pl.BlockSpec((1, 1, TM, w), lambda b, d, j: (d, b, blk(d, j), 0)),
                  pl.BlockSpec((1, 1, TM, w), lambda b, d, j: (d, b, blk(d, j), 0)),
                  pl.BlockSpec((1, TM, w), lambda b, d, j: (b, blk(d, j), 0)),
                  pl.BlockSpec((w, w), lambda b, d, j: (0, 0))],
        out_specs=pl.BlockSpec((1, 1, TM, w), lambda b, d, j: (d, b, blk(d, j), 0)),
        out_shape=jax.ShapeDtypeStruct((2, nb, ta, w), f32),
        scratch_shapes=[pltpu.VMEM((2, w // 2, w // 2), f32)],
        compiler_params=pltpu.CompilerParams(dimension_semantics=("arbitrary", "arbitrary", "arbitrary")),
        name="hgrn_scan",
    )(hq, cum, kk, hv, bd)


def _pair_rows(q2):
    lane = lax.broadcasted_iota(i32, q2.shape, 1)
    zero = jnp.zeros_like(q2)
    return jnp.concatenate([jnp.where(lane < HEAD_DIM, q2, zero), jnp.where(lane >= HEAD_DIM, q2, zero)], axis=0)


def _pair_merge(o):
    m = o.shape[0] // 2
    lane = lax.broadcasted_iota(i32, (m, LANE), 1)
    return jnp.where(lane < HEAD_DIM, o[:m], o[m:])


def _joint_attention(s_loc, s_ctx, v_loc, v_ctx, sink=None):
    m = jnp.maximum(jnp.max(s_loc, axis=-1, keepdims=True), jnp.max(s_ctx, axis=-1, keepdims=True))
    if sink is not None:
        m = jnp.maximum(m, sink)
    p_loc = jnp.exp(s_loc - m)
    p_ctx = jnp.exp(s_ctx - m)
    den = jnp.sum(p_loc, axis=-1, keepdims=True) + jnp.sum(p_ctx, axis=-1, keepdims=True)
    if sink is not None:
        den = den + jnp.exp(sink - m)
    o = (jnp.dot(p_loc.astype(bf16), v_loc, preferred_element_type=f32)
         + jnp.dot(p_ctx.astype(bf16), v_ctx, preferred_element_type=f32))
    return o / den


def _swa_kernel(q_ref, kp_ref, kc_ref, kn_ref, vp_ref, vc_ref, vn_ref, kx_ref, vx_ref, sink_ref, o_ref, *, n_ctx_blocks, seq):
    g = pl.program_id(1)
    i = g - n_ctx_blocks
    k_loc = jnp.concatenate([kp_ref[0], kc_ref[0], kn_ref[0]], axis=0)
    v_loc = jnp.concatenate([vp_ref[0], vc_ref[0], vn_ref[0]], axis=0)
    row = lax.broadcasted_iota(i32, (SWA_BLOCK, 3 * SWA_BLOCK), 0)
    col = lax.broadcasted_iota(i32, (SWA_BLOCK, 3 * SWA_BLOCK), 1)
    rel = col - SWA_BLOCK - row
    kpos = (i - 1) * SWA_BLOCK + col
    ok = (jnp.abs(rel) <= SWA_WINDOW) & (kpos >= 0) & (kpos < seq) & (i >= 0)
    pen = jnp.where(ok, 0.0, NEG)
    pen = jnp.concatenate([pen, pen], axis=0)
    for j in range(SWA_Q // LANE):
        q = _pair_rows(q_ref[0, :, j * LANE:(j + 1) * LANE])
        s_loc = lax.dot_general(q, k_loc, _NT, preferred_element_type=f32) + pen
        s_ctx = lax.dot_general(q, kx_ref[0], _NT, preferred_element_type=f32)
        sink = jnp.concatenate([jnp.broadcast_to(sink_ref[2 * j:2 * j + 1, 0:1], (SWA_BLOCK, 1)),
                                jnp.broadcast_to(sink_ref[2 * j + 1:2 * j + 2, 0:1], (SWA_BLOCK, 1))], axis=0)
        o = _joint_attention(s_loc, s_ctx, v_loc, vx_ref[0], sink)
        o_ref[0, :, j * LANE:(j + 1) * LANE] = _pair_merge(o).astype(bf16)


def _swa(sq, sk, sv, sink_rows, n_ctx):
    nb, ta, _ = sq.shape
    ncb = n_ctx // SWA_BLOCK
    nblk = ta // SWA_BLOCK
    nlat = nblk - ncb
    prev = lambda b, g: (b, jnp.clip(g - ncb - 1, 0, nlat - 1) + ncb, 0)
    cur = lambda b, g: (b, g, 0)
    nxt = lambda b, g: (b, jnp.clip(g - ncb + 1, 0, nlat - 1) + ncb, 0)
    kvs = lambda f: pl.BlockSpec((1, SWA_BLOCK, SWA_KV), f)
    ctx = pl.BlockSpec((1, n_ctx, SWA_KV), lambda b, g: (b, 0, 0))
    return pl.pallas_call(
        functools.partial(_swa_kernel, n_ctx_blocks=ncb, seq=ta - n_ctx),
        grid=(nb, nblk),
        in_specs=[pl.BlockSpec((1, SWA_BLOCK, SWA_Q), cur), kvs(prev), kvs(cur), kvs(nxt), kvs(prev), kvs(cur), kvs(nxt),
                  ctx, ctx, pl.BlockSpec((8, LANE), lambda b, g: (0, 0))],
        out_specs=pl.BlockSpec((1, SWA_BLOCK, SWA_Q), cur),
        out_shape=jax.ShapeDtypeStruct((nb, ta, SWA_Q), bf16),
        compiler_params=pltpu.CompilerParams(dimension_semantics=("arbitrary", "arbitrary")),
        name="swa_attention",
    )(sq, sk, sk, sk, sv, sv, sv, sk, sv, sink_rows)


def _nat_kernel(q_ref, k_ref, v_ref, bias_ref, o_ref, *, n_ctx, n_rows):
    g = pl.program_id(1)
    is_ctx = g == 0
    ctx_pen = jnp.where(is_ctx, NEG, 0.0)
    rows_per_tile = TM // GRID_W
    nwin = NAT_ROWS * GRID_W
    for rr in range(rows_per_tile):
        r = jnp.maximum((g - 1) * rows_per_tile + rr, 0)
        rs = jnp.clip(r - NAT_ROWS // 2, 0, n_rows - NAT_ROWS)
        dr0 = rs - r + NAT_ROWS - 1
        start = pl.multiple_of(n_ctx + rs * GRID_W, GRID_W)
        for p in range(NAT_W // LANE):
            ls = slice(p * LANE, (p + 1) * LANE)
            q = _pair_rows(q_ref[0, rr * GRID_W:(rr + 1) * GRID_W, ls])
            s_loc = (lax.dot_general(q, k_ref[0, pl.ds(start, nwin), ls], _NT, preferred_element_type=f32)
                     + bias_ref[p, dr0] + ctx_pen)
            s_ctx = lax.dot_general(q, k_ref[0, 0:n_ctx, ls], _NT, preferred_element_type=f32)
            o = _joint_attention(s_loc, s_ctx, v_ref[0, pl.ds(start, nwin), ls], v_ref[0, 0:n_ctx, ls])
            o_ref[0, rr * GRID_W:(rr + 1) * GRID_W, ls] = _pair_merge(o).astype(bf16)


def _nat(nq, nk, nv, bias, n_ctx):
    nb, ta, w = nq.shape
    nt = ta // TM
    whole = pl.BlockSpec((1, ta, w), lambda b, g: (b, 0, 0), pipeline_mode=pl.Buffered(1))
    return pl.pallas_call(
        functools.partial(_nat_kernel, n_ctx=n_ctx, n_rows=(ta - n_ctx) // GRID_W),
        grid=(nb, nt),
        in_specs=[pl.BlockSpec((1, TM, w), lambda b, g: (b, g, 0)), whole, whole,
                  pl.BlockSpec(bias.shape, lambda b, g: (0, 0, 0, 0), pipeline_mode=pl.Buffered(1))],
        out_specs=pl.BlockSpec((1, TM, w), lambda b, g: (b, g, 0)),
        out_shape=jax.ShapeDtypeStruct((nb, ta, w), bf16),
        compiler_params=pltpu.CompilerParams(dimension_semantics=("arbitrary", "arbitrary"),
                                             vmem_limit_bytes=VMEM_LIMIT_BYTES),
        name="nat_attention",
    )(nq, nk, nv, bias)


def _nat_bias_tables(rpb):
    c = np.arange(GRID_W)[:, None]
    kc = np.arange(GRID_W)[None, :]
    ws = np.clip(c - NAT_COLS // 2, 0, GRID_W - NAT_COLS)
    valid = (kc >= ws) & (kc < ws + NAT_COLS)
    col_idx = np.clip(kc - c + NAT_COLS - 1, 0, 2 * NAT_COLS - 2)
    dr = np.arange(NAT_ROWS)[:, None] + np.arange(NAT_ROWS)[None, :]
    t = rpb.astype(f32)[:, dr][:, :, :, col_idx]
    t = jnp.where(jnp.asarray(valid)[None, None, None], t, NEG)
    t = t.transpose(0, 1, 3, 2, 4).reshape(NAT_HEADS // 2, 2, NAT_ROWS, GRID_W, NAT_ROWS * GRID_W)
    return t.transpose(0, 2, 1, 3, 4).reshape(NAT_HEADS // 2, NAT_ROWS, 2 * GRID_W, NAT_ROWS * GRID_W)


def _outproj_kernel(o_ref, sg_ref, hnw_ref, bd_ref, so_ref, no_ref, wa_ref, wb_ref, wn_ref, mod_ref, x_ref, y_ref):
    oo = o_ref[0, 0] + o_ref[1, 0]
    hi, lo = _split2(oo * oo)
    bd = bd_ref[...]
    msq = (jnp.dot(hi, bd, preferred_element_type=f32) + jnp.dot(lo, bd, preferred_element_type=f32)) * (1.0 / HEAD_DIM)
    a = oo * lax.rsqrt(msq + EPS) * hnw_ref[...] * sg_ref[0]
    y = (jnp.dot(a.astype(bf16), wa_ref[...], preferred_element_type=f32)
         + jnp.dot(so_ref[0], wb_ref[...], preferred_element_type=f32)
         + jnp.dot(no_ref[0], wn_ref[...], preferred_element_type=f32))
    y_ref[0] = x_ref[0] + mod_ref[0, 0, 2:3, :] * y


def _outproj(o, sg, hnw, bd, so, no, wa, wb, wn, mod, xall):
    nb, ta, d = xall.shape
    nt = ta // TM
    bs = lambda width: pl.BlockSpec((1, TM, width), lambda b, i: (b, i, 0))
    full = lambda a: pl.BlockSpec(a.shape, lambda b, i: (0,) * a.ndim)
    return pl.pallas_call(
        _outproj_kernel,
        grid=(nb, nt),
        in_specs=[pl.BlockSpec((2, 1, TM, HG_WIDTH), lambda b, i: (0, b, i, 0)), bs(HG_WIDTH), full(hnw), full(bd),
                  bs(SWA_Q), bs(NAT_W), full(wa), full(wb), full(wn),
                  pl.BlockSpec((1, 1, 6, d), lambda b, i: (b, jnp.minimum(i, 1), 0, 0)), bs(d)],
        out_specs=bs(d),
        out_shape=jax.ShapeDtypeStruct(xall.shape, f32),
        compiler_params=pltpu.CompilerParams(dimension_semantics=("arbitrary", "arbitrary"),
                                             vmem_limit_bytes=VMEM_LIMIT_BYTES),
        name="out_proj",
    )(o, sg, hnw, bd, so, no, wa, wb, wn, mod, xall)


def _top16_rows(vals, payload=None):
    n = vals.shape[0]
    iota = lax.broadcasted_iota(i32, vals.shape, 0)
    best, picked = [], []
    for _ in range(PEER_TOPK):
        m = jnp.max(vals, axis=0, keepdims=True)
        idx = jnp.min(jnp.where(vals == m, iota, n), axis=0, keepdims=True)
        hit = iota == idx
        best.append(m)
        picked.append(idx if payload is None else jnp.max(jnp.where(hit, payload, -1), axis=0, keepdims=True))
        vals = jnp.where(hit, -jnp.inf, vals)
    return jnp.concatenate(best, axis=0), jnp.concatenate(picked, axis=0)


def _product_candidates(first, second, combine):
    half = PEER_TOPK // 2
    rows = [combine(first[0:1, :], second)]
    rows += [combine(first[a:a + 1, :], second[0:half, :]) for a in range(1, half)]
    rows.append(combine(first[half:, :], second[0:1, :]))
    return jnp.concatenate(rows, axis=0)


def _route_kernel(x_ref, nw_ref, mod_ref, wq_ref, sub_ref, h_ref, idx_ref, g_ref):
    h = _norm_mod(x_ref[0], nw_ref[...], mod_ref[0, 0, 3:4, :], mod_ref[0, 0, 4:5, :])
    h_ref[0] = h
    q = jnp.dot(h.astype(bf16), wq_ref[...], preferred_element_type=f32).astype(bf16)
    half = PEER_DK // 2
    ids, gates = [], []
    for hh in range(PEER_HEADS):
        top_s, top_i = [], []
        for p in range(2):
            c0 = (hh * 2 + p) * half
            s = lax.dot_general(sub_ref[hh, p], q[:, c0:c0 + half], _NT, preferred_element_type=f32)
            ts, ti = _top16_rows(s)
            top_s.append(ts)
            top_i.append(ti)
        cand_s = _product_candidates(top_s[0], top_s[1], lambda a, b: a + b)
        cand_i = _product_candidates(top_i[0], top_i[1], lambda a, b: a * PEER_NKEYS + b)
        best, eid = _top16_rows(cand_s, cand_i)
        ids.append(eid)
        e = jnp.exp(best - best[0:1, :])
        gates.append(e / jnp.sum(e, axis=0, keepdims=True))
    idx_ref[0] = jnp.concatenate(ids, axis=0).T
    g_ref[0] = jnp.concatenate(gates, axis=0).T


def _route(xall, nw, mod, wq, sub):
    nb, ta, d = xall.shape
    nt = ta // TM
    full = lambda a: pl.BlockSpec(a.shape, lambda b, i: (0,) * a.ndim)
    return pl.pallas_call(
        _route_kernel,
        grid=(nb, nt),
        in_specs=[pl.BlockSpec((1, TM, d), lambda b, i: (b, i, 0)), full(nw),
                  pl.BlockSpec((1, 1, 6, d), lambda b, i: (b, jnp.minimum(i, 1), 0, 0)), full(wq), full(sub)],
        out_specs=[pl.BlockSpec((1, TM, d), lambda b, i: (b, i, 0)),
                   pl.BlockSpec((1, TM, PEER_SLOTS), lambda b, i: (b, i, 0)),
                   pl.BlockSpec((1, TM, PEER_SLOTS), lambda b, i: (b, i, 0))],
        out_shape=[jax.ShapeDtypeStruct(xall.shape, f32), jax.ShapeDtypeStruct((nb, ta, PEER_SLOTS), i32),
                   jax.ShapeDtypeStruct((nb, ta, PEER_SLOTS), f32)],
        compiler_params=pltpu.CompilerParams(dimension_semantics=("arbitrary", "arbitrary"),
                                             vmem_limit_bytes=VMEM_LIMIT_BYTES),
        name="peer_route",
    )(xall, nw, mod, wq, sub)


def _expert_kernel(idx_ref, idx_next_ref, h_ref, g_ref, x_ref, mod_ref, tab_hbm, o_ref, buf, sem):
    d = h_ref.shape[-1]
    nct = d // LANE
    groups = PEER_SLOTS // 8

    per_piece = PEER_SLOTS // nct

    def issue(ids_ref, t, slot, piece):
        for k in range(piece * per_piece, (piece + 1) * per_piece):
            pltpu.make_async_copy(tab_hbm.at[ids_ref[t, k]], buf.at[slot, k // 8, :, pl.ds(k % 8, 1), :],
                                  sem.at[slot]).start(priority=k % 2)

    def wait(slot):
        pltpu.make_async_copy(buf.at[slot], buf.at[slot], sem.at[slot]).wait()

    def cols(slot, c, high):
        words = jnp.concatenate([jnp.concatenate([buf[slot, kg, c], buf[slot, kg, c + 1]], axis=1)
                                 for kg in range(groups)], axis=0)
        bits = (words & jnp.uint32(0xFFFF0000)) if high else (words << 16)
        return lax.bitcast_convert_type(bits, f32).astype(bf16)

    not_last = pl.program_id(0) < pl.num_programs(0) - 1

    def step(t, w8, do_issue, do_u, do_v):
        nslot = lax.rem(t + 1, PEER_NBUF)
        slot = lax.rem(t, PEER_NBUF)
        islot = lax.rem(t + PEER_AHEAD, PEER_NBUF)
        if do_issue == "next":
            @pl.when(not_last)
            def _():
                for j in range(nct):
                    issue(idx_next_ref, t + PEER_AHEAD - PEER_TB, islot, j)
        if do_u:
            wait(nslot)
            h8 = jnp.broadcast_to(h_ref[pl.ds(t + 1, 1), :], (8, d)).astype(bf16)
            gate = g_ref[pl.ds(t + 1, 1), :]
            a = jnp.zeros((8, PEER_SLOTS), f32)
        ys = []
        for j in range(nct):
            if 2 * j < nct:
                c = 2 * j
                if do_u:
                    a = a + lax.dot_general(h8[:, c * LANE:(c + 2) * LANE], cols(nslot, c, True), _NT, preferred_element_type=f32)
                if do_v:
                    ys.append(jnp.dot(w8, cols(slot, c, False), preferred_element_type=f32)[0:1, :])
            if do_issue is True:
                issue(idx_ref, t + PEER_AHEAD, islot, j)
        if do_v:
            o_ref[pl.ds(t, 1), :] = jnp.concatenate(ys, axis=1)
        if do_u:
            a1 = a[0:1, :]
            w = gate * (0.5 * a1 * (1.0 + lax.erf(a1 * (2.0 ** -0.5))))
            w8 = jnp.broadcast_to(w, (8, PEER_SLOTS)).astype(bf16)
        return w8

    @pl.when(pl.program_id(0) == 0)
    def _():
        def prologue(t, carry):
            for j in range(nct):
                issue(idx_ref, t, t, j)
            return carry

        lax.fori_loop(0, PEER_AHEAD, prologue, 0)

    w8 = step(-1, jnp.zeros((8, PEER_SLOTS), bf16), False, True, False)
    w8 = lax.fori_loop(0, PEER_TB - PEER_AHEAD, lambda t, w: step(t, w, True, True, True), w8)
    w8 = lax.fori_loop(PEER_TB - PEER_AHEAD, PEER_TB - 1, lambda t, w: step(t, w, "next", True, True), w8)
    step(PEER_TB - 1, w8, "next", False, True)
    o_ref[...] = x_ref[...] + mod_ref[0, 0, 5:6, :] * o_ref[...]


def _mod_index(tokens_per_step, ta, n_ctx, first_step=0):
    spb = ta // tokens_per_step
    cs = n_ctx // tokens_per_step
    return lambda i: ((i + first_step) // spb, jnp.where((i + first_step) % spb < cs, 0, 1), 0, 0)


def _experts(idx, h, g, xflat, mod, tab, ta, n_ctx, nblocks):
    n, d = xflat.shape
    tokb = lambda width: pl.BlockSpec((PEER_TB, width), lambda i: (i, 0))
    return pl.pallas_call(
        _expert_kernel,
        grid=(nblocks,),
        in_specs=[pl.BlockSpec((PEER_TB, PEER_SLOTS), lambda i: (i, 0), memory_space=pltpu.SMEM),
                  pl.BlockSpec((PEER_TB, PEER_SLOTS), lambda i: (jnp.minimum(i + 1, nblocks - 1), 0), memory_space=pltpu.SMEM),
                  tokb(d), tokb(PEER_SLOTS), tokb(d),
                  pl.BlockSpec((1, 1, 6, d), _mod_index(PEER_TB, ta, n_ctx)),
                  pl.BlockSpec(memory_space=pl.ANY)],
        out_specs=tokb(d),
        out_shape=jax.ShapeDtypeStruct((n, d), f32),
        input_output_aliases={4: 0},
        scratch_shapes=[pltpu.VMEM((PEER_NBUF, PEER_SLOTS // 8, d // LANE, 8, LANE), jnp.uint32),
                        pltpu.SemaphoreType.DMA((PEER_NBUF,))],
        compiler_params=pltpu.CompilerParams(dimension_semantics=("arbitrary",), vmem_limit_bytes=VMEM_LIMIT_BYTES),
        name="peer_experts",
    )(idx, idx, h, g, xflat, mod, tab)


def _sc_gather(table, idx):
    info = plsc.get_sparse_core_info()
    nc = info.num_cores
    nw = nc * info.num_subcores
    r = idx.shape[0]
    per_w = r // nw
    nch = per_w // SC_CHUNK
    assert per_w * nw == r and nch * SC_CHUNK == per_w and nch % 2 == 0
    rows_shape = (SC_CHUNK,) + table.shape[1:]
    mesh = plsc.VectorSubcoreMesh(core_axis_name="c", subcore_axis_name="s")

    @functools.partial(
        pl.kernel, mesh=mesh, out_type=jax.ShapeDtypeStruct((r,) + table.shape[1:], table.dtype),
        scratch_types=[pltpu.VMEM((2, SC_CHUNK), i32), pltpu.VMEM((2,) + rows_shape, table.dtype),
                       pltpu.SemaphoreType.DMA((2,)), pltpu.SemaphoreType.DMA((2,)), pltpu.SemaphoreType.DMA((2,))])
    def gather_kernel(table_hbm, idx_hbm, out_hbm, idx_v, rows_v, isem, gsem, wsem):
        base = (lax.axis_index("s") * nc + lax.axis_index("c")) * per_w

        def rows_of(j):
            return pl.ds(pl.multiple_of(base + j * SC_CHUNK, 8), SC_CHUNK)

        def fetch_idx(j, b):
            return pltpu.make_async_copy(idx_hbm.at[rows_of(j)], idx_v.at[b], isem.at[b])

        def gather(b):
            return pltpu.make_async_copy(table_hbm.at[idx_v.at[b]], rows_v.at[b], gsem.at[b])

        def write(j, b):
            return pltpu.make_async_copy(rows_v.at[b], out_hbm.at[rows_of(j)], wsem.at[b])

        fetch_idx(0, 0).start()
        fetch_idx(1, 1).start()
        fetch_idx(0, 0).wait()
        gather(0).start()

        @pl.loop(0, nch, step=2)
        def _(j0):
            for b in (0, 1):
                j = j0 + b

                @pl.when(j >= 1)
                def _():
                    write(j - 1, 1 - b).wait()

                @pl.when(j + 1 < nch)
                def _():
                    fetch_idx(j + 1, 1 - b).wait()
                    gather(1 - b).start()

                gather(b).wait()

                @pl.when(j + 2 < nch)
                def _():
                    fetch_idx(j + 2, b).start()

                write(j, b).start()

        write(nch - 1, 1).wait()

    return gather_kernel(table, idx)


def _expert_dense_kernel(st_ref, h_ref, g_ref, x_ref, mod_ref, o_ref):
    d = h_ref.shape[-1]
    nct = d // LANE
    groups = PEER_SLOTS // 8
    rows_per_token = PEER_SLOTS * nct

    per_row = LANE // nct
    lane = lax.broadcasted_iota(i32, (nct, LANE), 1)
    sub = lax.broadcasted_iota(i32, (nct, LANE), 0)
    lane8 = lax.broadcasted_iota(i32, (8, LANE), 1)
    sub8 = lax.broadcasted_iota(i32, (8, LANE), 0)

    def u_stage(t):
        base = pl.multiple_of(t * rows_per_token, rows_per_token)
        hrow = h_ref[pl.ds(t, 1), :]
        hch = jnp.concatenate([hrow[:, c * LANE:(c + 1) * LANE] for c in range(nct)], axis=0).astype(bf16)
        words = st_ref[pl.ds(base, rows_per_token), :]
        u = lax.bitcast_convert_type(words & jnp.uint32(0xFFFF0000), f32).astype(bf16)
        q = lax.dot_general(hch, u, _NT, preferred_element_type=f32)
        x = jnp.concatenate([jnp.sum(jnp.where(lane % nct == sub, q[:, r * LANE:(r + 1) * LANE], 0.0), axis=0, keepdims=True)
                             for r in range(nct)], axis=0)
        shift = 1
        while shift < nct:
            x = x + pltpu.roll(x, LANE - shift, 1)
            shift *= 2
        return 0.5 * x * (1.0 + lax.erf(x * (2.0 ** -0.5)))

    def v_stage(t, act):
        base = t * rows_per_token
        gb = jnp.broadcast_to(g_ref[pl.ds(t, 1), :], (8, LANE))
        acc = [None] * nct
        for kg in range(groups):
            k0 = kg * 8
            ab = jnp.broadcast_to(act[k0 // per_row:k0 // per_row + 1, :], (8, LANE))
            wcol = (jnp.sum(jnp.where(lane8 == (k0 % per_row + sub8) * nct, ab, 0.0), axis=-1, keepdims=True)
                    * jnp.sum(jnp.where(lane8 == k0 + sub8, gb, 0.0), axis=-1, keepdims=True))
            for c in range(nct):
                words = st_ref[pl.ds(base + kg * 8 * nct + c, 8, stride=nct), :]
                term = lax.bitcast_convert_type(words << 16, f32) * wcol
                acc[c] = term if acc[c] is None else acc[c] + term
        o_ref[pl.ds(t, 1), :] = jnp.concatenate([jnp.sum(a, axis=0, keepdims=True) for a in acc], axis=1)

    def body(t, w_row):
        w_next = u_stage(t + 1)
        v_stage(t, w_row)
        return w_next

    w_row = lax.fori_loop(0, PEER_DENSE_TB - 1, body, u_stage(0), unroll=8)
    v_stage(PEER_DENSE_TB - 1, w_row)
    o_ref[...] = x_ref[...] + mod_ref[0, 0, 5:6, :] * o_ref[...]


def _experts_dense(staged, h, g, xflat, mod, ta, n_ctx, first_block):
    n, d = xflat.shape
    rows_per_token = PEER_SLOTS * (d // LANE)
    steps = staged.shape[0] // (PEER_DENSE_TB * rows_per_token)
    off = first_block * PEER_TB // PEER_DENSE_TB
    tokb = lambda width: pl.BlockSpec((PEER_DENSE_TB, width), lambda i: (i + off, 0))
    return pl.pallas_call(
        _expert_dense_kernel,
        grid=(steps,),
        in_specs=[pl.BlockSpec((PEER_DENSE_TB * rows_per_token, LANE), lambda i: (i, 0)),
                  tokb(d), tokb(PEER_SLOTS), tokb(d),
                  pl.BlockSpec((1, 1, 6, d), _mod_index(PEER_DENSE_TB, ta, n_ctx, off))],
        out_specs=tokb(d),
        out_shape=jax.ShapeDtypeStruct((n, d), f32),
        input_output_aliases={3: 0},
        compiler_params=pltpu.CompilerParams(dimension_semantics=("arbitrary",), vmem_limit_bytes=VMEM_LIMIT_BYTES),
        name="peer_experts_dense",
    )(staged, h, g, xflat, mod)


def _expert_table(u, v):
    half = lambda a: lax.bitcast_convert_type(a.astype(bf16), jnp.uint16).astype(jnp.uint32)
    return (half(u) << 16) | half(v)


def _final_norm_kernel(x_ref, w_ref, o_ref):
    x = x_ref[0]
    o_ref[0] = x * lax.rsqrt(jnp.mean(x * x, axis=-1, keepdims=True) + EPS) * w_ref[...]


def _final_norm(xall, w, n_ctx):
    nb, ta, d = xall.shape
    off = n_ctx // TM
    return pl.pallas_call(
        _final_norm_kernel,
        grid=(nb, (ta - n_ctx) // TM),
        in_specs=[pl.BlockSpec((1, TM, d), lambda b, i: (b, i + off, 0)), pl.BlockSpec((1, d), lambda b, i: (0, 0))],
        out_specs=pl.BlockSpec((1, TM, d), lambda b, i: (b, i, 0)),
        out_shape=jax.ShapeDtypeStruct((nb, ta - n_ctx, d), f32),
        compiler_params=pltpu.CompilerParams(dimension_semantics=("arbitrary", "arbitrary")),
        name="final_norm",
    )(xall, w.reshape(1, d))


def _rope_tables(n_ctx, seq):
    nf = HEAD_DIM // 4
    pos = np.arange(seq)
    inv = ROPE_BASE ** (-np.arange(nf, dtype=np.float32) / nf)
    ar = (pos // GRID_W).astype(np.float32)[:, None] * inv
    ac = (pos % GRID_W).astype(np.float32)[:, None] * inv
    cos = np.concatenate([np.cos(ar), np.cos(ar), np.cos(ac), np.cos(ac)], axis=-1)
    sin = np.concatenate([-np.sin(ar), np.sin(ar), -np.sin(ac), np.sin(ac)], axis=-1)
    cos = np.concatenate([np.ones((n_ctx, HEAD_DIM), np.float32), cos.astype(np.float32)], axis=0)
    sin = np.concatenate([np.zeros((n_ctx, HEAD_DIM), np.float32), sin.astype(np.float32)], axis=0)
    return jnp.asarray(np.stack([np.tile(cos, (1, 2)), np.tile(sin, (1, 2))]))


def _tri_tables():
    r = np.arange(TM)[:, None]
    c = np.arange(TM)[None, :]
    same = (r // HG_SUB) == (c // HG_SUB)
    return jnp.asarray(np.stack([same & (c <= r), same & (c >= r)]).astype(np.float32), dtype=bf16)


def _block_diag_ones():
    r = np.arange(HG_WIDTH)
    return jnp.asarray((r[:, None] // HEAD_DIM == r[None, :] // HEAD_DIM).astype(np.float32), dtype=bf16)


def _swa_perm():
    return np.concatenate([np.arange(h * HEAD_DIM, (h + 1) * HEAD_DIM) for h in SWA_HEAD_ORDER])


def kernel(x, c, ctx, c_ctx, w_mod, b_mod, norm_mix_w, norm_ffn_w, w_in, hgrn_lb_logits, hgrn_norm_w, swa_sink, nat_rpb,
           w_out, peer_wq, peer_subkeys, peer_u, peer_v, final_norm_w):
    nb, seq, d = x.shape
    n_ctx = ctx.shape[1]
    depth = w_in.shape[0]
    ta = n_ctx + seq
    assert n_ctx == TM and seq % TM == 0 and seq % (GRID_W * NAT_ROWS) == 0 and d % LANE == 0
    assert w_in.shape[-1] == PROJ_WIDTH and peer_wq.shape[-1] == PEER_HEADS * PEER_DK

    mods = _mods(c, c_ctx, w_mod, b_mod)
    cs = _rope_tables(n_ctx, seq)
    tri = _tri_tables()
    bd = _block_diag_ones()
    perm = _swa_perm()
    q0 = 5 * HG_WIDTH
    col_perm = np.concatenate([np.arange(q0), q0 + perm, np.arange(q0 + SWA_Q, PROJ_WIDTH)])
    w_in_b = w_in[:, :, col_perm].astype(bf16)
    w_out_b = w_out.astype(bf16)
    wa = w_out_b[:, :HG_WIDTH]
    wb = w_out_b[:, HG_WIDTH:HG_WIDTH + SWA_Q][:, perm]
    wn = w_out_b[:, HG_WIDTH + SWA_Q:]
    wq_b = peer_wq.astype(bf16)
    sub_b = peer_subkeys.astype(bf16)
    lb = jnp.cumsum(jax.nn.softmax(hgrn_lb_logits.astype(f32), axis=1), axis=1)
    lb = lb - lb[:, :1]
    gp = jnp.stack([jnp.log(lb[0]), jnp.log1p(-lb[0]), 1.0 - lb[0], jnp.log(lb[1]), jnp.log1p(-lb[1]), 1.0 - lb[1],
                    jnp.zeros_like(lb[0]), jnp.zeros_like(lb[0])], axis=1)
    sink_rows = jnp.zeros((depth, 8, LANE), f32).at[:, :SWA_HEADS].set(
        jnp.broadcast_to(swa_sink.astype(f32)[:, list(SWA_HEAD_ORDER), None], (depth, SWA_HEADS, LANE)))

    words = _expert_table(peer_u, peer_v)
    n_exp = words.shape[1]
    tab = words.reshape(depth, n_exp, d // LANE, 1, LANE)
    tab_sc = words.reshape(depth, n_exp, d // LANE, LANE)
    n_blocks = nb * ta // PEER_TB
    n_tc = n_blocks - int(n_blocks * PEER_SC_SHARE + 0.5)
    assert ta % PEER_TB == 0 and PEER_TB % PEER_NBUF == 0 and PEER_TB % PEER_DENSE_TB == 0 and 0 < n_tc < n_blocks

    xall = jnp.concatenate([ctx, x], axis=1)
    for l in range(depth):
        mod = mods[l]
        hq, cum, kk, hv, sg, sq, sk, sv, nq, nk, nv = _proj(xall, norm_mix_w[l].reshape(1, d), mod, w_in_b[l], cs, gp[l], tri)
        o = _hgrn_scan(hq, cum, kk, hv, bd)
        so = _swa(sq, sk, sv, sink_rows[l], n_ctx)
        no = _nat(nq, nk, nv, _nat_bias_tables(nat_rpb[l]), n_ctx)
        xall = _outproj(o, sg, hgrn_norm_w[l].reshape(1, HG_WIDTH), bd, so, no, wa[l], wb[l], wn[l], mod, xall)
        h2, idx, g = _route(xall, norm_ffn_w[l].reshape(1, d), mod, wq_b[l], sub_b[l])
        idx, h2, g = idx.reshape(nb * ta, PEER_SLOTS), h2.reshape(nb * ta, d), g.reshape(nb * ta, PEER_SLOTS)
        staged = _sc_gather(tab_sc[l], idx[n_tc * PEER_TB:].reshape(-1))
        xf = _experts(idx, h2, g, xall.reshape(nb * ta, d), mod, tab[l], ta, n_ctx, n_tc)
        xf = _experts_dense(staged.reshape(-1, LANE), h2, g, xf, mod, ta, n_ctx, n_tc)
        xall = xf.reshape(nb, ta, d)
    return _final_norm(xall, final_norm_w, n_ctx)
```
